```python
import math
import jax
import jax.numpy as jnp
from jax import lax
import numpy as np

D_MODEL = 2048
BATCH = 2
SEQ = 8192
DEPTH = 4
DEC_BATCH = 8
DEC_SEQ = 32
PAST_LEN = 4096

CHUNK = 64
QBLOCK = 128
NORM_EPS = 1e-6
N_EVEN = (DEPTH + 1) // 2
N_ODD = DEPTH // 2

A_HEADS = 16
A_HEAD_DIM = 64
A_WIDTH = A_HEADS * A_HEAD_DIM
A_DECAY_LORA = 64
A_ICLR_LORA = 64
A_GATE_LORA = 160
A_COLS = 3 * A_WIDTH + A_DECAY_LORA + A_ICLR_LORA + A_GATE_LORA
A_LN_EPS = 64e-5

B_WIDTH = 1024
B_BLOCKS = 16
B_BLOCK_DIM = B_WIDTH // B_BLOCKS
B_CONV = 4
B_C = 8.0
B_COLS = 2 * B_WIDTH

EV_COLS = A_COLS + B_COLS
EV_MIX = A_WIDTH + B_WIDTH

C_HEADS = 8
C_HEAD_DIM = 128
C_WIDTH = C_HEADS * C_HEAD_DIM
C_IDX_HEADS = 8
C_IDX_DIM = 64
TOPK_MAX = 256
C_COLS = 3 * C_WIDTH + C_IDX_HEADS * C_IDX_DIM + C_IDX_DIM + C_IDX_HEADS

D_HEADS = 8
D_NOPE = 128
D_ROPE = 64
D_V = 128
D_Q_RANK = 512
D_KV_RANK = 512
D_WIDTH = D_HEADS * D_V
D_COLS = D_Q_RANK + D_KV_RANK + D_ROPE
ROPE_BASE = 10000.0

OD_COLS = C_COLS + D_COLS
OD_MIX = C_WIDTH + D_WIDTH

N_BUCKETS = 32
MAX_DISTANCE = 128

D_FF = -(-8 * D_MODEL // (3 * 256)) * 256

kernel_name = 'hybrid_streaming_encoder_step'


def rmsnorm(x, g):
    x32 = x.astype(jnp.float32)
    y = x32 * lax.rsqrt(jnp.mean(x32 * x32, axis=-1, keepdims=True) + NORM_EPS)
    return (y * g.astype(jnp.float32)).astype(x.dtype)


def swiglu(x, w_gate, w_up, w_down):
    return (jax.nn.silu(x @ w_gate) * (x @ w_up)) @ w_down


def split_cols(p, sizes):
    outs, off = [], 0
    for s in sizes:
        outs.append(p[..., off:off + s])
        off += s
    return outs


def query_block(t):
    return QBLOCK if t % QBLOCK == 0 else t


def rope(x, pos):
    half = D_ROPE // 2
    inv = ROPE_BASE ** (-jnp.arange(0, D_ROPE, 2, dtype=jnp.float32) / D_ROPE)
    ang = pos.astype(jnp.float32)[:, None] * inv[None, :]
    shape = (1, pos.shape[0]) + (1,) * (x.ndim - 3) + (half,)
    cos = jnp.cos(ang).reshape(shape)
    sin = jnp.sin(ang).reshape(shape)
    x32 = x.astype(jnp.float32)
    x1, x2 = x32[..., :half], x32[..., half:]
    return jnp.concatenate([x1 * cos - x2 * sin, x2 * cos + x1 * sin], axis=-1).astype(x.dtype)


def t5_bucket(rel):
    nb = N_BUCKETS // 2
    max_exact = nb // 2
    n = jnp.abs(rel)
    big = jnp.maximum(n, max_exact).astype(jnp.float32)
    large = max_exact + (jnp.log(big / max_exact) / math.log(MAX_DISTANCE / max_exact)
                         * (nb - max_exact)).astype(jnp.int32)
    large = jnp.minimum(large, nb - 1)
    return jnp.where(rel > 0, nb, 0) + jnp.where(n < max_exact, n, large)


def rwkv7_mix(pa, pa_prev, S0, mu, w0, w2, a0, a2, g2, k_k, k_a, r_k, ln_w, ln_b):
    b_, t_, _ = pa.shape
    f32 = jnp.float32
    xm = pa + mu * (pa_prev - pa)
    r, k, v, wl, al, gl = split_cols(xm, [A_WIDTH, A_WIDTH, A_WIDTH, A_DECAY_LORA, A_ICLR_LORA, A_GATE_LORA])
    w = -jax.nn.softplus(-(w0 + jnp.tanh(wl) @ w2)) - 0.5
    decay = jnp.exp(-jnp.exp(w.astype(f32)))
    a = jax.nn.sigmoid(a0 + al @ a2)
    g = jax.nn.sigmoid(gl) @ g2

    def heads(z):
        return z.reshape(b_, t_, A_HEADS, A_HEAD_DIM).astype(f32)

    kk = heads(k * k_k)
    kk = kk / jnp.maximum(jnp.sqrt(jnp.sum(kk * kk, axis=-1, keepdims=True)), 1e-12)
    k = k * (1 + (a - 1) * k_a)
    rh, kh, vh, ah, dh = heads(r), heads(k), heads(v), heads(a), heads(decay)

    def step(S, inp):
        r_t, w_t, k_t, v_t, kk_t, a_t = inp
        sa = jnp.einsum('bhij,bhj->bhi', S, kk_t)
        S = (S * w_t[:, :, None, :] - sa[..., None] * (kk_t * a_t)[:, :, None, :]
             + v_t[..., None] * k_t[:, :, None, :])
        return S, jnp.einsum('bhij,bhj->bhi', S, r_t)

    xs = tuple(jnp.moveaxis(z, 1, 0) for z in (rh, dh, kh, vh, kk, ah))
    S, y = lax.scan(step, S0.astype(f32), xs)
    y = jnp.moveaxis(y, 0, 1)
    mean = jnp.mean(y, axis=-1, keepdims=True)
    var = jnp.var(y, axis=-1, keepdims=True)
    y = ((y - mean) * lax.rsqrt(var + A_LN_EPS)).reshape(b_, t_, A_WIDTH)
    y = y * ln_w.astype(f32) + ln_b.astype(f32)
    bonus = jnp.sum(rh * kh * r_k.astype(f32), axis=-1, keepdims=True) * vh
    y = (y + bonus.reshape(b_, t_, A_WIDTH)) * g.astype(f32)
    return y.astype(pa.dtype), S.astype(S0.dtype)


def rglru_mix(pb, p0, conv_buf, h0, conv_w, conv_b, wa, ba, wx, bx, lam):
    b_, t_, _ = pb.shape
    f32 = jnp.float32
    gate, xb = pb[..., :B_WIDTH], pb[..., B_WIDTH:]
    xpad = jnp.concatenate([conv_buf, xb], axis=1)
    xc = conv_b + sum(conv_w[j] * xpad[:, j:j + t_] for j in range(B_CONV))
    xblk = xc.reshape(b_, t_, B_BLOCKS, B_BLOCK_DIM)
    rg = jax.nn.sigmoid(jnp.einsum('bthi,hij->bthj', xblk, wa).reshape(b_, t_, B_WIDTH) + ba)
    ig = jax.nn.sigmoid(jnp.einsum('bthi,hij->bthj', xblk, wx).reshape(b_, t_, B_WIDTH) + bx)
    log_a = -B_C * rg.astype(f32) * jax.nn.softplus(-lam.astype(f32))
    a = jnp.exp(log_a)
    pos = p0 + jnp.arange(t_)
    mult = jnp.where((pos == 0)[None, :, None], 1.0, jnp.sqrt(-jnp.expm1(2.0 * log_a)))
    u = mult * (ig * xc).astype(f32)
    u = u.at[:, 0].add(a[:, 0] * h0.astype(f32))

    def combine(c1, c2):
        a1, b1 = c1
        a2, b2 = c2
        return a1 * a2, a2 * b1 + b2

    _, h = lax.associative_scan(combine, (a, u), axis=1)
    y = h * jax.nn.gelu(gate.astype(f32))
    return y.astype(pb.dtype), h[:, -1].astype(h0.dtype), xpad[:, -(B_CONV - 1):]


def mixer_even(xn, p0, shift_prev, S0, h0, conv_buf, W, e):
    p = xn @ W['ev_w_in'][e]
    pa, pb = p[..., :A_COLS], p[..., A_COLS:]
    pa_prev = jnp.concatenate([shift_prev[:, None], pa[:, :-1]], axis=1)
    ya, S = rwkv7_mix(pa, pa_prev, S0, W['rwkv_mu'][e], W['rwkv_w0'][e], W['rwkv_w2'][e],
                      W['rwkv_a0'][e], W['rwkv_a2'][e], W['rwkv_g2'][e], W['rwkv_k_k'][e],
                      W['rwkv_k_a'][e], W['rwkv_r_k'][e], W['rwkv_ln_w'][e], W['rwkv_ln_b'][e])
    yb, h, buf = rglru_mix(pb, p0, conv_buf, h0, W['lru_conv_w'][e], W['lru_conv_b'][e],
                           W['lru_wa'][e], W['lru_ba'][e], W['lru_wx'][e], W['lru_bx'][e],
                           W['lru_lambda'][e])
    out = jnp.concatenate([ya, yb], axis=-1) @ W['ev_w_out'][e]
    return out, pa[:, -1], S, h, buf


def dsa_attention(q, k_all, v_all, qi, ki_all, wi, p0, rel_bias):
    b_, t_ = q.shape[:2]
    l_ = k_all.shape[1]
    topk = min(TOPK_MAX, l_ // 4)
    qb = query_block(t_)
    kchunk = jnp.arange(l_) // CHUNK
    scale = C_HEAD_DIM ** -0.5
    iscale = (C_IDX_HEADS * C_IDX_DIM) ** -0.5
    gather = jax.vmap(lambda src, ix: src[ix])

    def block(n):
        start = n * qb
        q_b = lax.dynamic_slice_in_dim(q, start, qb, axis=1)
        qi_b = lax.dynamic_slice_in_dim(qi, start, qb, axis=1)
        wi_b = lax.dynamic_slice_in_dim(wi, start, qb, axis=1)
        qpos = p0 + start + jnp.arange(qb)
        qchunk = qpos // CHUNK
        allowed = kchunk[None, :] <= qchunk[:, None]
        score = jax.nn.relu(jnp.einsum('bqhd,bsd->bhqs', qi_b, ki_all))
        score = jnp.einsum('bhqs,bqh->bqs', score, wi_b).astype(jnp.float32) * iscale
        score = jnp.where(allowed[None], score, -jnp.inf)
        _, idx = lax.top_k(score, topk)
        valid = (idx // CHUNK) <= qchunk[None, :, None]
        k_sel = gather(k_all, idx)
        v_sel = gather(v_all, idx)
        bias = rel_bias[t5_bucket(idx - qpos[None, :, None])]
        logits = (jnp.einsum('bqhd,bqkhd->bqhk', q_b, k_sel).astype(jnp.float32) * scale
                  + jnp.swapaxes(bias, -1, -2).astype(jnp.float32))
        logits = jnp.where(valid[:, :, None, :], logits, -jnp.inf)
        p = jax.nn.softmax(logits, axis=-1).astype(v_all.dtype)
        return jnp.einsum('bqhk,bqkhd->bqhd', p, v_sel)

    out = lax.map(block, jnp.arange(t_ // qb))
    return jnp.moveaxis(out, 0, 1).reshape(b_, t_, C_WIDTH)


def mla_attention(q_nope, q_rope, lat_all, kr_all, w_ukv, p0):
    b_, t_ = q_nope.shape[:2]
    l_ = lat_all.shape[1]
    kv = jnp.einsum('bsr,rhd->bshd', lat_all, w_ukv.reshape(D_KV_RANK, D_HEADS, D_NOPE + D_V))
    k_nope, v = kv[..., :D_NOPE], kv[..., D_NOPE:]
    qb = query_block(t_)
    kchunk = jnp.arange(l_) // CHUNK
    scale = (D_NOPE + D_ROPE) ** -0.5

    def block(n):
        start = n * qb
        qn = lax.dynamic_slice_in_dim(q_nope, start, qb, axis=1)
        qr = lax.dynamic_slice_in_dim(q_rope, start, qb, axis=1)
        qchunk = (p0 + start + jnp.arange(qb)) // CHUNK
        allowed = kchunk[None, :] <= qchunk[:, None]
        logits = (jnp.einsum('bqhd,bshd->bhqs', qn, k_nope)
                  + jnp.einsum('bqhd,bsd->bhqs', qr, kr_all)).astype(jnp.float32) * scale
        logits = jnp.where(allowed[None, None], logits, -jnp.inf)
        p = jax.nn.softmax(logits, axis=-1).astype(v.dtype)
        return jnp.einsum('bhqs,bshd->bqhd', p, v)

    out = lax.map(block, jnp.arange(t_ // qb))
    return jnp.moveaxis(out, 0, 1).reshape(b_, t_, D_WIDTH)


def mixer_odd(xn, p0, ck, cv, cik, clat, ckr, W, o):
    b_, t_, _ = xn.shape
    p = xn @ W['od_w_in'][o]
    q, k, v, qi, ki, wi, qd, kvd, kr = split_cols(
        p, [C_WIDTH, C_WIDTH, C_WIDTH, C_IDX_HEADS * C_IDX_DIM, C_IDX_DIM, C_IDX_HEADS,
            D_Q_RANK, D_KV_RANK, D_ROPE])
    pos = p0 + jnp.arange(t_)
    q = q.reshape(b_, t_, C_HEADS, C_HEAD_DIM)
    k = k.reshape(b_, t_, C_HEADS, C_HEAD_DIM)
    v = v.reshape(b_, t_, C_HEADS, C_HEAD_DIM)
    qi = qi.reshape(b_, t_, C_IDX_HEADS, C_IDX_DIM)
    yc = dsa_attention(q, jnp.concatenate([ck, k], axis=1), jnp.concatenate([cv, v], axis=1), qi,
                       jnp.concatenate([cik, ki], axis=1), wi, p0, W['rel_bias'])
    qf = (rmsnorm(qd, W['mla_q_norm'][o]) @ W['mla_w_uq'][o]).reshape(b_, t_, D_HEADS, D_NOPE + D_ROPE)
    q_nope, q_rope = qf[..., :D_NOPE], rope(qf[..., D_NOPE:], pos)
    lat = rmsnorm(kvd, W['mla_kv_norm'][o])
    krope = rope(kr, pos)
    yd = mla_attention(q_nope, q_rope, jnp.concatenate([clat, lat], axis=1),
                       jnp.concatenate([ckr, krope], axis=1), W['mla_w_ukv'][o], p0)
    out = jnp.concatenate([yc, yd], axis=-1) @ W['od_w_out'][o]
    return out, k, v, ki, lat, krope


def trunk(x, p0, shift, rwkv_S, lru_h, lru_conv, dsa_k, dsa_v, dsa_ik, mla_lat, mla_kr, W):
    n_shift, n_S, n_h, n_conv = [], [], [], []
    n_k, n_v, n_ik, n_lat, n_kr = [], [], [], [], []
    for layer in range(DEPTH):
        xn = rmsnorm(x, W['norm_mix'][layer])
        if layer % 2 == 0:
            e = layer // 2
            out, s1, s2, s3, s4 = mixer_even(xn, p0, shift[e], rwkv_S[e], lru_h[e], lru_conv[e], W, e)
            n_shift.append(s1)
            n_S.append(s2)
            n_h.append(s3)
            n_conv.append(s4)
        else:
            o = layer // 2
            out, k, v, ik, lat, kr = mixer_odd(xn, p0, dsa_k[o], dsa_v[o], dsa_ik[o], mla_lat[o], mla_kr[o], W, o)
            n_k.append(k)
            n_v.append(v)
            n_ik.append(ik)
            n_lat.append(lat)
            n_kr.append(kr)
        x = x + out
        x = x + swiglu(rmsnorm(x, W['norm_ffn'][layer]), W['ffn_w_gate'][layer],
                       W['ffn_w_up'][layer], W['ffn_w_down'][layer])
    y = rmsnorm(x, W['final_norm'])
    return (y, jnp.stack(n_shift), jnp.stack(n_S), jnp.stack(n_h), jnp.stack(n_conv),
            jnp.stack(n_k), jnp.stack(n_v), jnp.stack(n_ik), jnp.stack(n_lat), jnp.stack(n_kr))


def setup_inputs(seed: int = 0) -> dict:
    key = jax.random.key(seed)
    ks = iter(jax.random.split(key, 64))
    f32 = jnp.float32

    def nrm(shape, scale=1.0):
        return jax.random.normal(next(ks), shape, f32) * scale

    def gain(shape):
        return 1.0 + nrm(shape, 0.05)

    u = jax.random.uniform(next(ks), (N_EVEN, B_WIDTH), f32, minval=0.9, maxval=0.999)
    a_base = u ** (1.0 / B_C)
    lru_lambda = jnp.log(a_base) - jnp.log1p(-a_base)
    return {
        'x_prompt': nrm((BATCH, SEQ, D_MODEL)),
        'x_sample': nrm((DEC_BATCH, DEC_SEQ, D_MODEL)),
        'state_rwkv_shift': nrm((N_EVEN, DEC_BATCH, A_COLS)),
        'state_rwkv': nrm((N_EVEN, DEC_BATCH, A_HEADS, A_HEAD_DIM, A_HEAD_DIM), 0.2),
        'state_lru': nrm((N_EVEN, DEC_BATCH, B_WIDTH), 0.5),
        'state_lru_conv': nrm((N_EVEN, DEC_BATCH, B_CONV - 1, B_WIDTH)),
        'cache_dsa_k': nrm((N_ODD, DEC_BATCH, PAST_LEN, C_HEADS, C_HEAD_DIM)),
        'cache_dsa_v': nrm((N_ODD, DEC_BATCH, PAST_LEN, C_HEADS, C_HEAD_DIM)),
        'cache_dsa_idx_k': nrm((N_ODD, DEC_BATCH, PAST_LEN, C_IDX_DIM)),
        'cache_mla_latent': nrm((N_ODD, DEC_BATCH, PAST_LEN, D_KV_RANK)),
        'cache_mla_krope': nrm((N_ODD, DEC_BATCH, PAST_LEN, D_ROPE)),
        'rel_bias': nrm((N_BUCKETS, C_HEADS), 0.5),
        'final_norm': gain((D_MODEL,)),
        'norm_mix': gain((DEPTH, D_MODEL)),
        'norm_ffn': gain((DEPTH, D_MODEL)),
        'ffn_w_gate': nrm((DEPTH, D_MODEL, D_FF), D_MODEL ** -0.5),
        'ffn_w_up': nrm((DEPTH, D_MODEL, D_FF), D_MODEL ** -0.5),
        'ffn_w_down': nrm((DEPTH, D_FF, D_MODEL), D_FF ** -0.5),
        'ev_w_in': nrm((N_EVEN, D_MODEL, EV_COLS), D_MODEL ** -0.5),
        'ev_w_out': nrm((N_EVEN, EV_MIX, D_MODEL), EV_MIX ** -0.5),
        'rwkv_mu': jax.random.uniform(next(ks), (N_EVEN, A_COLS), f32),
        'rwkv_w0': nrm((N_EVEN, A_WIDTH), 0.5),
        'rwkv_w2': nrm((N_EVEN, A_DECAY_LORA, A_WIDTH), 0.1),
        'rwkv_a0': nrm((N_EVEN, A_WIDTH), 0.1),
        'rwkv_a2': nrm((N_EVEN, A_ICLR_LORA, A_WIDTH), 0.1),
        'rwkv_g2': nrm((N_EVEN, A_GATE_LORA, A_WIDTH), A_GATE_LORA ** -0.5),
        'rwkv_k_k': 0.85 + nrm((N_EVEN, A_WIDTH), 0.05),
        'rwkv_k_a': gain((N_EVEN, A_WIDTH)),
        'rwkv_r_k': nrm((N_EVEN, A_HEADS, A_HEAD_DIM), 0.1),
        'rwkv_ln_w': gain((N_EVEN, A_WIDTH)),
        'rwkv_ln_b': nrm((N_EVEN, A_WIDTH), 0.02),
        'lru_conv_w': nrm((N_EVEN, B_CONV, B_WIDTH), 0.5),
        'lru_conv_b': nrm((N_EVEN, B_WIDTH), 0.02),
        'lru_wa': nrm((N_EVEN, B_BLOCKS, B_BLOCK_DIM, B_BLOCK_DIM), B_BLOCK_DIM ** -0.5),
        'lru_ba': nrm((N_EVEN, B_WIDTH), 0.02),
        'lru_wx': nrm((N_EVEN, B_BLOCKS, B_BLOCK_DIM, B_BLOCK_DIM), B_BLOCK_DIM ** -0.5),
        'lru_bx': nrm((N_EVEN, B_WIDTH), 0.02),
        'lru_lambda': lru_lambda,
        'od_w_in': nrm((N_ODD, D_MODEL, OD_COLS), D_MODEL ** -0.5),
        'od_w_out': nrm((N_ODD, OD_MIX, D_MODEL), OD_MIX ** -0.5),
        'mla_q_norm': gain((N_ODD, D_Q_RANK)),
        'mla_w_uq': nrm((N_ODD, D_Q_RANK, D_HEADS * (D_NOPE + D_ROPE)), D_Q_RANK ** -0.5),
        'mla_kv_norm': gain((N_ODD, D_KV_RANK)),
        'mla_w_ukv': nrm((N_ODD, D_KV_RANK, D_HEADS * (D_NOPE + D_V)), D_KV_RANK ** -0.5),
    }


def reference(x_prompt, x_sample, state_rwkv_shift, state_rwkv, state_lru, state_lru_conv,
              cache_dsa_k, cache_dsa_v, cache_dsa_idx_k, cache_mla_latent, cache_mla_krope,
              rel_bias, final_norm, norm_mix, norm_ffn, ffn_w_gate, ffn_w_up, ffn_w_down,
              ev_w_in, ev_w_out, rwkv_mu, rwkv_w0, rwkv_w2, rwkv_a0, rwkv_a2, rwkv_g2,
              rwkv_k_k, rwkv_k_a, rwkv_r_k, rwkv_ln_w, rwkv_ln_b, lru_conv_w, lru_conv_b,
              lru_wa, lru_ba, lru_wx, lru_bx, lru_lambda, od_w_in, od_w_out,
              mla_q_norm, mla_w_uq, mla_kv_norm, mla_w_ukv):
    W = dict(rel_bias=rel_bias, final_norm=final_norm, norm_mix=norm_mix, norm_ffn=norm_ffn,
             ffn_w_gate=ffn_w_gate, ffn_w_up=ffn_w_up, ffn_w_down=ffn_w_down,
             ev_w_in=ev_w_in, ev_w_out=ev_w_out, rwkv_mu=rwkv_mu, rwkv_w0=rwkv_w0,
             rwkv_w2=rwkv_w2, rwkv_a0=rwkv_a0, rwkv_a2=rwkv_a2, rwkv_g2=rwkv_g2,
             rwkv_k_k=rwkv_k_k, rwkv_k_a=rwkv_k_a, rwkv_r_k=rwkv_r_k, rwkv_ln_w=rwkv_ln_w,
             rwkv_ln_b=rwkv_ln_b, lru_conv_w=lru_conv_w, lru_conv_b=lru_conv_b, lru_wa=lru_wa,
             lru_ba=lru_ba, lru_wx=lru_wx, lru_bx=lru_bx, lru_lambda=lru_lambda,
             od_w_in=od_w_in, od_w_out=od_w_out, mla_q_norm=mla_q_norm, mla_w_uq=mla_w_uq,
             mla_kv_norm=mla_kv_norm, mla_w_ukv=mla_w_ukv)
    bp = x_prompt.shape[0]
    dt = x_prompt.dtype
    (y_prompt, p_rwkv_shift, p_rwkv, p_lru, p_lru_conv,
     p_dsa_k, p_dsa_v, p_dsa_idx_k, p_mla_latent, p_mla_krope) = trunk(
        x_prompt, 0,
        jnp.zeros((N_EVEN, bp, A_COLS), dt),
        jnp.zeros((N_EVEN, bp, A_HEADS, A_HEAD_DIM, A_HEAD_DIM), dt),
        jnp.zeros((N_EVEN, bp, B_WIDTH), dt),
        jnp.zeros((N_EVEN, bp, B_CONV - 1, B_WIDTH), dt),
        jnp.zeros((N_ODD, bp, 0, C_HEADS, C_HEAD_DIM), dt),
        jnp.zeros((N_ODD, bp, 0, C_HEADS, C_HEAD_DIM), dt),
        jnp.zeros((N_ODD, bp, 0, C_IDX_DIM), dt),
        jnp.zeros((N_ODD, bp, 0, D_KV_RANK), dt),
        jnp.zeros((N_ODD, bp, 0, D_ROPE), dt),
        W)
    past = cache_dsa_k.shape[2]
    (y_sample, s_rwkv_shift, s_rwkv, s_lru, s_lru_conv,
     s_dsa_k, s_dsa_v, s_dsa_idx_k, s_mla_latent, s_mla_krope) = trunk(
        x_sample, past, state_rwkv_shift, state_rwkv, state_lru, state_lru_conv,
        cache_dsa_k, cache_dsa_v, cache_dsa_idx_k, cache_mla_latent, cache_mla_krope, W)
    return (y_prompt, y_sample,
            p_rwkv_shift, p_rwkv, p_lru, p_lru_conv,
            p_dsa_k, p_dsa_v, p_dsa_idx_k, p_mla_latent, p_mla_krope,
            s_rwkv_shift, s_rwkv, s_lru, s_lru_conv,
            s_dsa_k, s_dsa_v, s_dsa_idx_k, s_mla_latent, s_mla_krope)
```

```python
import functools
import math

import jax
import jax.numpy as jnp
from jax import lax
from jax.experimental import pallas as pl
from jax.experimental.pallas import tpu as pltpu

F32 = jnp.float32
BF16 = jnp.bfloat16
I32 = jnp.int32

CHUNK = 64
NORM_EPS = 1e-6
A_HEADS = 16
A_HEAD_DIM = 64
A_WIDTH = A_HEADS * A_HEAD_DIM
A_DECAY_LORA = 64
A_ICLR_LORA = 64
A_GATE_LORA = 160
A_LORA = A_DECAY_LORA + A_ICLR_LORA + A_GATE_LORA
A_COLS = 3 * A_WIDTH + A_LORA
A_LN_EPS = 64e-5
B_WIDTH = 1024
B_BLOCKS = 16
B_CONV = 4
B_C = 8.0
C_HEADS = 8
C_HEAD_DIM = 128
C_WIDTH = C_HEADS * C_HEAD_DIM
C_IDX_HEADS = 8
C_IDX_DIM = 64
TOPK_MAX = 256
D_HEADS = 8
D_NOPE = 128
D_ROPE = 64
D_V = 128
D_Q_RANK = 512
D_KV_RANK = 512
ROPE_BASE = 10000.0
N_BUCKETS = 32
MAX_DISTANCE = 128

LANES = 128
SUBLANES = 8
VMEM_LIMIT = 56 * 1024 * 1024
LORA_PAD = 512
NEG_BIG = -1e30
INT_MIN = -2147483648

NN = (((1,), (0,)), ((), ()))
NT = (((1,), (1,)), ((), ()))
BNN = (((2,), (1,)), ((0,), (0,)))
BNT = (((2,), (2,)), ((0,), (0,)))
BTN = (((1,), (1,)), ((0,), (0,)))


def _cparams(sem):
    return pltpu.CompilerParams(dimension_semantics=sem, vmem_limit_bytes=VMEM_LIMIT)


def _rms(x, g, eps=NORM_EPS):
    ms = jnp.mean(x * x, axis=-1, keepdims=True)
    return x * lax.rsqrt(ms + eps) * g


def _softplus(x):
    return jnp.maximum(x, 0.0) + jnp.log1p(jnp.exp(-jnp.abs(x)))


def _dotp(a, b, dims, passes):
    if passes == 6:
        return lax.dot_general(a, b, dims, precision=lax.Precision.HIGHEST, preferred_element_type=F32)
    ah = a.astype(BF16)
    bh = b.astype(BF16)
    out = lax.dot_general(ah, bh, dims, preferred_element_type=F32)
    if passes == 3:
        al = (a - ah.astype(F32)).astype(BF16)
        bl = (b - bh.astype(F32)).astype(BF16)
        out = out + lax.dot_general(ah, bl, dims, preferred_element_type=F32)
        out = out + lax.dot_general(al, bh, dims, preferred_element_type=F32)
    return out


def _norm_matmul_body(x_ref, g_ref, w_ref, o_ref, xn_ref):
    @pl.when(pl.program_id(1) == 0)
    def _():
        xn_ref[...] = _rms(x_ref[...], g_ref[...]).astype(BF16)

    o_ref[...] = jnp.dot(xn_ref[...], w_ref[...], preferred_element_type=F32)


def _norm_matmul(x, g, w, *, tn=512):
    m, k = x.shape
    n = w.shape[1]
    tm = min(512, m)
    return pl.pallas_call(
        _norm_matmul_body,
        grid=(m // tm, n // tn),
        in_specs=[pl.BlockSpec((tm, k), lambda i, j: (i, 0)),
                  pl.BlockSpec((1, k), lambda i, j: (0, 0)),
                  pl.BlockSpec((k, tn), lambda i, j: (0, j))],
        out_specs=pl.BlockSpec((tm, tn), lambda i, j: (i, j)),
        out_shape=jax.ShapeDtypeStruct((m, n), F32),
        scratch_shapes=[pltpu.VMEM((tm, k), BF16)],
        compiler_params=_cparams(("parallel", "arbitrary")),
        name="norm_matmul",
    )(x, g.reshape(1, k), w)


def _mm_body(*refs, n_lhs, has_res):
    o_ref = refs[-1]
    acc = refs[2 * n_lhs][...] if has_res else None
    for a_ref, w_ref in zip(refs[:n_lhs], refs[n_lhs:2 * n_lhs]):
        d = jnp.dot(a_ref[...].astype(BF16), w_ref[...], preferred_element_type=F32)
        acc = d if acc is None else acc + d
    o_ref[...] = acc.astype(o_ref.dtype)


def _matmul(lhs_list, w_list, res=None, *, tn=512, out_dtype=F32):
    m = lhs_list[0].shape[0]
    n = w_list[0].shape[1]
    tm = min(512, m)
    in_specs = [pl.BlockSpec((tm, a.shape[1]), lambda i, j: (i, 0)) for a in lhs_list]
    in_specs += [pl.BlockSpec((w.shape[0], tn), lambda i, j: (0, j)) for w in w_list]
    args = list(lhs_list) + list(w_list)
    if res is not None:
        in_specs.append(pl.BlockSpec((tm, tn), lambda i, j: (i, j)))
        args.append(res)
    return pl.pallas_call(
        functools.partial(_mm_body, n_lhs=len(lhs_list), has_res=res is not None),
        grid=(m // tm, n // tn),
        in_specs=in_specs,
        out_specs=pl.BlockSpec((tm, tn), lambda i, j: (i, j)),
        out_shape=jax.ShapeDtypeStruct((m, n), out_dtype),
        compiler_params=_cparams(("parallel", "arbitrary")),
        name="matmul",
    )(*args)


def _ffn_body(x_ref, g_ref, wg_ref, wu_ref, wd_ref, gf_ref, o_ref, xn_ref, acc_ref, *, final_norm):
    f = pl.program_id(1)

    @pl.when(f == 0)
    def _():
        xn_ref[...] = _rms(x_ref[...], g_ref[...]).astype(BF16)
        acc_ref[...] = jnp.zeros_like(acc_ref)

    xn = xn_ref[...]
    hg = jnp.dot(xn, wg_ref[...], preferred_element_type=F32)
    hu = jnp.dot(xn, wu_ref[...], preferred_element_type=F32)
    h = hg * jax.nn.sigmoid(hg) * hu
    acc_ref[...] += jnp.dot(h.astype(BF16), wd_ref[...], preferred_element_type=F32)

    @pl.when(f == pl.num_programs(1) - 1)
    def _():
        y = x_ref[...] + acc_ref[...]
        if final_norm:
            y = _rms(y, gf_ref[...])
        o_ref[...] = y


def _ffn(x, g, wg, wu, wd, gf, *, final_norm, tf=512):
    m, k = x.shape
    dff = wg.shape[1]
    tm = min(512, m)
    return pl.pallas_call(
        functools.partial(_ffn_body, final_norm=final_norm),
        grid=(m // tm, dff // tf),
        in_specs=[pl.BlockSpec((tm, k), lambda i, f: (i, 0)),
                  pl.BlockSpec((1, k), lambda i, f: (0, 0)),
                  pl.BlockSpec((k, tf), lambda i, f: (0, f)),
                  pl.BlockSpec((k, tf), lambda i, f: (0, f)),
                  pl.BlockSpec((tf, k), lambda i, f: (f, 0)),
                  pl.BlockSpec((1, k), lambda i, f: (0, 0))],
        out_specs=pl.BlockSpec((tm, k), lambda i, f: (i, 0)),
        out_shape=jax.ShapeDtypeStruct((m, k), F32),
        scratch_shapes=[pltpu.VMEM((tm, k), BF16), pltpu.VMEM((tm, k), F32)],
        compiler_params=_cparams(("parallel", "arbitrary")),
        name="ffn",
    )(x, g.reshape(1, k), wg, wu, wd, gf.reshape(1, k))


def _lru_body(gate_ref, xb_ref, cw_ref, cb_ref, wg_ref, bg_ref, lam_ref, h0_ref, cbuf_ref,
              y_ref, hout_ref, cout_ref, xbuf, hcar, *, tt, p0):
    t = pl.program_id(1)
    w = B_WIDTH
    halo = SUBLANES

    @pl.when(t == 0)
    def _():
        xbuf[0:halo, :] = jnp.zeros((halo, w), F32)
        xbuf[halo - (B_CONV - 1):halo, :] = cbuf_ref[0]
        hcar[...] = h0_ref[0]

    xb = xb_ref[0]
    xbuf[halo:halo + tt, :] = xb
    xc = cb_ref[...] + cw_ref[B_CONV - 1:B_CONV, :] * xb
    for j in range(B_CONV - 1):
        off = halo - (B_CONV - 1) + j
        xc = xc + cw_ref[j:j + 1, :] * xbuf[off:off + tt, :]
    tail = xbuf[tt + halo - (B_CONV - 1):tt + halo, :]
    xbuf[halo - (B_CONV - 1):halo, :] = tail

    pre = jnp.dot(xc.astype(BF16), wg_ref[...], preferred_element_type=F32) + bg_ref[...]
    rg = jax.nn.sigmoid(pre[:, :w])
    ig = jax.nn.sigmoid(pre[:, w:])
    log_a = (-B_C) * rg * _softplus(-lam_ref[...])
    a = jnp.exp(log_a)
    row = lax.broadcasted_iota(I32, (tt, w), 0)
    th = jnp.tanh(log_a)
    mult = jnp.sqrt(-2.0 * th / (1.0 - th))
    mult = jnp.where(row + (p0 + t * tt) == 0, 1.0, mult)
    u = mult * (ig * xc)

    d = 1
    while d < tt:
        keep = row >= d
        a_sh = pltpu.roll(a, d, 0)
        u_sh = pltpu.roll(u, d, 0)
        u = u + jnp.where(keep, a * u_sh, 0.0)
        a = jnp.where(keep, a * a_sh, a)
        d *= 2
    h = u + a * hcar[...]
    hcar[...] = h[tt - 1:tt, :]
    y_ref[0] = (h * jax.nn.gelu(gate_ref[0])).astype(y_ref.dtype)

    @pl.when(t == pl.num_programs(1) - 1)
    def _():
        hout_ref[0] = h[tt - 1:tt, :]
        cout_ref[0] = tail


def _rglru(p3, gate_blk, xb_blk, cw, cb, wgate, bgate, lam, h0, cbuf, *, p0):
    b, t, _ = p3.shape
    w = B_WIDTH
    tt = min(256, t)
    row = lambda v: v.reshape(1, -1)
    return pl.pallas_call(
        functools.partial(_lru_body, tt=tt, p0=p0),
        grid=(b, t // tt),
        in_specs=[pl.BlockSpec((1, tt, w), lambda i, j: (i, j, gate_blk)),
                  pl.BlockSpec((1, tt, w), lambda i, j: (i, j, xb_blk)),
                  pl.BlockSpec((B_CONV, w), lambda i, j: (0, 0)),
                  pl.BlockSpec((1, w), lambda i, j: (0, 0)),
                  pl.BlockSpec((w, 2 * w), lambda i, j: (0, 0)),
                  pl.BlockSpec((1, 2 * w), lambda i, j: (0, 0)),
                  pl.BlockSpec((1, w), lambda i, j: (0, 0)),
                  pl.BlockSpec((1, 1, w), lambda i, j: (i, 0, 0)),
                  pl.BlockSpec((1, B_CONV - 1, w), lambda i, j: (i, 0, 0))],
        out_specs=[pl.BlockSpec((1, tt, w), lambda i, j: (i, j, 0)),
                   pl.BlockSpec((1, 1, w), lambda i, j: (i, 0, 0)),
                   pl.BlockSpec((1, B_CONV - 1, w), lambda i, j: (i, 0, 0))],
        out_shape=[jax.ShapeDtypeStruct((b, t, w), BF16),
                   jax.ShapeDtypeStruct((b, 1, w), F32),
                   jax.ShapeDtypeStruct((b, B_CONV - 1, w), F32)],
        scratch_shapes=[pltpu.VMEM((tt + SUBLANES, w), F32), pltpu.VMEM((1, w), F32)],
        compiler_params=_cparams(("parallel", "arbitrary")),
        name="rglru",
    )(p3, p3, cw, row(cb), wgate, row(bgate), row(lam), h0.reshape(b, 1, w), cbuf)


RWKV_PASSES = 6


def _rwkv_body(rkv_ref, lora_ref, shr_ref, shl_ref, s0_ref, mur_ref, mul_ref, w0_ref, w2_ref, a0_ref,
               a2_ref, g2_ref, kk_ref, ka_ref, rk_ref, lnw_ref, lnb_ref, tri_ref,
               y_ref, sout_ref, buf_r, buf_l, s_scr, st_r, st_k, st_v, st_q, st_a, st_l, st_d, *, c):
    ci = pl.program_id(1)
    halo = SUBLANES
    hn, n, wd = A_HEADS, A_HEAD_DIM, A_WIDTH
    mm = functools.partial(_dotp, passes=RWKV_PASSES)

    @pl.when(ci == 0)
    def _():
        buf_r[halo - 1:halo, :] = shr_ref[0]
        buf_l[halo - 1:halo, :] = shl_ref[0]
        s_scr[...] = s0_ref[0]

    cur_r = rkv_ref[0]
    cur_l = lora_ref[0]
    buf_r[halo:halo + c, :] = cur_r
    buf_l[halo:halo + c, :] = cur_l
    xm = cur_r + mur_ref[...] * (buf_r[halo - 1:halo - 1 + c, :] - cur_r)
    lo = cur_l + mul_ref[...] * (buf_l[halo - 1:halo - 1 + c, :] - cur_l)
    buf_r[halo - 1:halo, :] = cur_r[c - 1:c, :]
    buf_l[halo - 1:halo, :] = cur_l[c - 1:c, :]

    r = xm[:, :wd]
    k = xm[:, wd:2 * wd]
    v = xm[:, 2 * wd:]
    lo_a = lo[:, :LANES]
    w_pre = w0_ref[...] + jnp.dot(jnp.tanh(lo_a).astype(BF16), w2_ref[...], preferred_element_type=F32)
    w_log = -_softplus(-w_pre) - 0.5
    ld = -jnp.exp(w_log)
    a = jax.nn.sigmoid(a0_ref[...] + jnp.dot(lo_a.astype(BF16), a2_ref[...], preferred_element_type=F32))
    g = jnp.dot(jax.nn.sigmoid(lo[:, LANES:]).astype(BF16), g2_ref[...], preferred_element_type=F32)
    kq = k * kk_ref[...]
    k2 = k * (1.0 + (a - 1.0) * ka_ref[...])
    lc = lax.dot_general(tri_ref[...], ld, NN, precision=lax.Precision.HIGHEST,
                         preferred_element_type=F32)

    for h in range(hn):
        sl = slice(h * n, (h + 1) * n)
        st_r[h] = r[:, sl]
        st_k[h] = k2[:, sl]
        st_v[h] = v[:, sl]
        st_q[h] = kq[:, sl]
        st_a[h] = a[:, sl]
        st_l[h] = lc[:, sl]
        st_d[h] = ld[:, sl]

    rh_, k2h, vh, kqh, ah, lch, ldh = (st_r[...], st_k[...], st_v[...], st_q[...], st_a[...],
                                       st_l[...], st_d[...])
    nrm = jnp.sqrt(jnp.sum(kqh * kqh, axis=-1, keepdims=True))
    kk = kqh / jnp.maximum(nrm, 1e-12)
    kka = kk * ah
    e_neg = jnp.exp(-lch)
    am = jnp.exp(lch - ldh) * kk
    bm = kka * e_neg
    kh = k2h * e_neg
    rh = rh_ * jnp.exp(lch)
    l_end = lch[:, c - 1:c, :]
    e_c = jnp.exp(l_end - lch)
    bp = kka * e_c
    kp = k2h * e_c
    w_end = jnp.exp(l_end)

    x2 = jnp.concatenate([am, rh], axis=1)
    zb = mm(x2, bm, BNT)
    zk = mm(x2, kh, BNT)
    ti = lax.broadcasted_iota(I32, (hn, c, c), 1)
    si = lax.broadcasted_iota(I32, (hn, c, c), 2)
    strict = si < ti
    incl = si <= ti
    m1 = jnp.where(strict, zb[:, :c], 0.0)
    m4 = jnp.where(incl, zb[:, c:], 0.0)
    m2 = jnp.where(strict, zk[:, :c], 0.0)
    m3 = jnp.where(incl, zk[:, c:], 0.0)

    tm = jnp.where(si == ti, 1.0, 0.0) - m1
    npow = mm(m1, m1, BNN)
    span = 2
    while span < c:
        tm = tm + mm(tm, npow, BNN)
        span *= 2
        if span < c:
            npow = mm(npow, npow, BNN)

    s0 = s_scr[...]
    rhs = mm(am, s0, BNT) + mm(m2, vh, BNN)
    p = mm(tm, rhs, BNN)
    y = mm(rh, s0, BNT) + mm(m3, vh, BNN) - mm(m4, p, BNN)
    s_new = s0 * w_end + mm(vh, kp, BTN) - mm(p, bp, BTN)
    s_scr[...] = s_new

    mean = jnp.mean(y, axis=-1, keepdims=True)
    yc = y - mean
    var = jnp.mean(yc * yc, axis=-1, keepdims=True)
    yn = yc * lax.rsqrt(var + A_LN_EPS) * lnw_ref[...] + lnb_ref[...]
    bonus = jnp.sum(rh_ * k2h * rk_ref[...], axis=-1, keepdims=True) * vh
    yo = yn + bonus
    yo = jnp.concatenate([yo[h] for h in range(hn)], axis=-1)
    y_ref[0] = (yo * g).astype(y_ref.dtype)

    @pl.when(ci == pl.num_programs(1) - 1)
    def _():
        sout_ref[0] = s_new


def _rwkv(p3, lora_blk, shift_r, shift_l, s0, wts):
    b, t, _ = p3.shape
    c = min(CHUNK, t)
    hn, n, wd = A_HEADS, A_HEAD_DIM, A_WIDTH
    full = lambda shape: pl.BlockSpec(shape, lambda i, j: (0,) * len(shape))
    tri = (jnp.arange(c)[:, None] >= jnp.arange(c)[None, :]).astype(F32)
    st = pltpu.VMEM((hn, c, n), F32)
    return pl.pallas_call(
        functools.partial(_rwkv_body, c=c),
        grid=(b, t // c),
        in_specs=[pl.BlockSpec((1, c, 3 * wd), lambda i, j: (i, j, 0)),
                  pl.BlockSpec((1, c, LORA_PAD), lambda i, j: (i, j, lora_blk)),
                  pl.BlockSpec((1, 1, 3 * wd), lambda i, j: (i, 0, 0)),
                  pl.BlockSpec((1, 1, LORA_PAD), lambda i, j: (i, 0, 0)),
                  pl.BlockSpec((1, hn, n, n), lambda i, j: (i, 0, 0, 0)),
                  full((1, 3 * wd)), full((1, LORA_PAD)), full((1, wd)), full((LANES, wd)),
                  full((1, wd)), full((LANES, wd)), full((LORA_PAD - LANES, wd)),
                  full((1, wd)), full((1, wd)), full((hn, 1, n)), full((hn, 1, n)), full((hn, 1, n)),
                  full((c, c))],
        out_specs=[pl.BlockSpec((1, c, wd), lambda i, j: (i, j, 0)),
                   pl.BlockSpec((1, hn, n, n), lambda i, j: (i, 0, 0, 0))],
        out_shape=[jax.ShapeDtypeStruct((b, t, wd), BF16),
                   jax.ShapeDtypeStruct((b, hn, n, n), F32)],
        scratch_shapes=[pltpu.VMEM((c + SUBLANES, 3 * wd), F32), pltpu.VMEM((c + SUBLANES, LORA_PAD), F32),
                        pltpu.VMEM((hn, n, n), F32), st, st, st, st, st, st, st],
        compiler_params=_cparams(("parallel", "arbitrary")),
        name="rwkv7",
    )(p3, p3, shift_r, shift_l, s0, wts["mu_r"], wts["mu_l"], wts["w0"], wts["w2"], wts["a0"], wts["a2"],
      wts["g2"], wts["k_k"], wts["k_a"], wts["r_k"], wts["ln_w"], wts["ln_b"], tri)


def _rope(x, cos, sin):
    wdt = x.shape[-1]
    lane = lax.broadcasted_iota(I32, x.shape, 1)
    first = (lane % D_ROPE) < (D_ROPE // 2)
    rot = jnp.where(first, -pltpu.roll(x, wdt - D_ROPE // 2, 1), pltpu.roll(x, D_ROPE // 2, 1))
    return x * cos + rot * sin


def _mla_prep_body(qd_ref, kvd_ref, sm_ref, gq_ref, gkv_ref, wuq_ref, cos_ref, sin_ref,
                   qn_ref, qr_ref, lat_ref, kr_ref):
    nope_w = D_HEADS * D_NOPE
    qdn = _rms(qd_ref[0], gq_ref[...]).astype(BF16)
    qf = jnp.dot(qdn, wuq_ref[...], preferred_element_type=F32)
    qn_ref[0] = qf[:, :nope_w].astype(qn_ref.dtype)
    cos = cos_ref[...]
    sin = sin_ref[...]
    qr_ref[0] = _rope(qf[:, nope_w:], cos, sin).astype(qr_ref.dtype)
    lat_ref[0] = _rms(kvd_ref[0], gkv_ref[...])
    sm = sm_ref[0][:, :LANES]
    kr = _rope(sm, cos[:, :LANES], sin[:, :LANES])
    kr_ref[0] = kr[:, C_IDX_DIM:C_IDX_DIM + D_ROPE]


def _mla_prep(p3, qd_blk, kvd_blk, sm_blk, gq, gkv, wuq, cos, sin):
    b, t, _ = p3.shape
    tt = min(256, t)
    rw = D_HEADS * D_ROPE
    full = lambda shape: pl.BlockSpec(shape, lambda i, j: (0,) * len(shape))
    return pl.pallas_call(
        _mla_prep_body,
        grid=(b, t // tt),
        in_specs=[pl.BlockSpec((1, tt, D_Q_RANK), lambda i, j: (i, j, qd_blk)),
                  pl.BlockSpec((1, tt, D_KV_RANK), lambda i, j: (i, j, kvd_blk)),
                  pl.BlockSpec((1, tt, 512), lambda i, j: (i, j, sm_blk)),
                  full((1, D_Q_RANK)), full((1, D_KV_RANK)), full(wuq.shape),
                  pl.BlockSpec((tt, rw), lambda i, j: (j, 0)),
                  pl.BlockSpec((tt, rw), lambda i, j: (j, 0))],
        out_specs=[pl.BlockSpec((1, tt, D_HEADS * D_NOPE), lambda i, j: (i, j, 0)),
                   pl.BlockSpec((1, tt, rw), lambda i, j: (i, j, 0)),
                   pl.BlockSpec((1, tt, D_KV_RANK), lambda i, j: (i, j, 0)),
                   pl.BlockSpec((1, tt, D_ROPE), lambda i, j: (i, j, 0))],
        out_shape=[jax.ShapeDtypeStruct((b, t, D_HEADS * D_NOPE), BF16),
                   jax.ShapeDtypeStruct((b, t, rw), BF16),
                   jax.ShapeDtypeStruct((b, t, D_KV_RANK), F32),
                   jax.ShapeDtypeStruct((b, t, D_ROPE), F32)],
        compiler_params=_cparams(("parallel", "parallel")),
        name="mla_prep",
    )(p3, p3, p3, gq.reshape(1, -1), gkv.reshape(1, -1), wuq, cos, sin)


def _bias_table_body(rb_ref, o_ref):
    d = pl.program_id(0)
    h = pl.program_id(1)
    nb = N_BUCKETS // 2
    max_exact = nb // 2
    s = lax.broadcasted_iota(I32, (LANES, LANES), 0)
    q = lax.broadcasted_iota(I32, (LANES, LANES), 1)
    rel = s - q - d * LANES
    n = jnp.abs(rel)
    big = jnp.maximum(n, max_exact).astype(F32)
    large = max_exact + (jnp.log(big / max_exact) / math.log(MAX_DISTANCE / max_exact)
                         * (nb - max_exact)).astype(I32)
    large = jnp.minimum(large, nb - 1)
    bucket = jnp.where(rel > 0, nb, 0) + jnp.where(n < max_exact, n, large)
    out = jnp.zeros((LANES, LANES), F32)
    for bk in range(N_BUCKETS):
        out = jnp.where(bucket == bk, rb_ref[bk, h], out)
    o_ref[0, 0] = out


def _bias_tables(rel_bias):
    return pl.pallas_call(
        _bias_table_body,
        grid=(2, C_HEADS),
        in_specs=[pl.BlockSpec(memory_space=pltpu.SMEM)],
        out_specs=pl.BlockSpec((1, 1, LANES, LANES), lambda d, h: (d, h, 0, 0)),
        out_shape=jax.ShapeDtypeStruct((2, C_HEADS, LANES, LANES), F32),
        name="bias_tables",
    )(rel_bias)


def _attn_body(*refs, mode, p0, t_valid, l_valid, tk, topk, scale, nh):
    dsa = mode == "dsa"
    tq = LANES
    dh = LANES
    if dsa:
        (qT_ref, k_ref, vT_ref, qih_ref, qil_ref, kih_ref, kil_ref, wiT_ref, tab_ref, far_ref,
         o_ref, m_scr, l_scr, acc_scr, skey_scr, thr_scr) = refs
    else:
        (qT_ref, k_ref, vT_ref, qrT_ref, kr_ref, o_ref, m_scr, l_scr, acc_scr) = refs
    qt = pl.program_id(1)
    kt = pl.program_id(2)
    nk = pl.num_programs(2)
    q_lo = p0 + qt * tq
    q_hi = p0 + jnp.minimum(qt * tq + tq, t_valid) - 1
    n_allowed = jnp.minimum((q_hi // CHUNK + 1) * CHUNK, l_valid)
    last_kt = (n_allowed - 1) // tk
    qpos = q_lo + lax.broadcasted_iota(I32, (1, tq), 1)
    qchunk = qpos // CHUNK

    def allowed_mask(k0, rows):
        kidx = k0 + lax.broadcasted_iota(I32, (rows, tq), 0)
        return (kidx // CHUNK <= qchunk) & (kidx < l_valid)

    @pl.when(kt == 0)
    def _():
        m_scr[...] = jnp.full(m_scr.shape, NEG_BIG, F32)
        l_scr[...] = jnp.zeros(l_scr.shape, F32)
        acc_scr[...] = jnp.zeros(acc_scr.shape, F32)

    if dsa:
        iscale = (C_IDX_HEADS * C_IDX_DIM) ** -0.5

        @pl.when(kt == 0)
        def _():
            def score_tile(j, carry):
                k0 = pl.multiple_of(j * tk, tk)
                kh_ = kih_ref[0, pl.ds(k0, tk), :]
                kl_ = kil_ref[0, pl.ds(k0, tk), :]
                s = jnp.zeros((tk, tq), F32)
                for h in range(C_IDX_HEADS):
                    sl = slice(h * C_IDX_DIM, (h + 1) * C_IDX_DIM)
                    qh_ = qih_ref[0, sl, :]
                    ql_ = qil_ref[0, sl, :]
                    d = jnp.dot(kh_, qh_, preferred_element_type=F32)
                    d = d + jnp.dot(kh_, ql_, preferred_element_type=F32)
                    d = d + jnp.dot(kl_, qh_, preferred_element_type=F32)
                    s = s + jnp.maximum(d, 0.0) * wiT_ref[0, h:h + 1, :]
                s = s * iscale + 0.0
                bits = pltpu.bitcast(s, I32)
                key = bits ^ ((bits >> 31) & 0x7FFFFFFF)
                skey_scr[pl.ds(k0, tk), :] = jnp.where(allowed_mask(k0, tk), key, INT_MIN)
                return carry

            lax.fori_loop(0, last_kt + 1, score_tile, 0)
            nrows = (last_kt + 1) * tk
            blk = 256
            nblk = nrows // blk

            def count(pred):
                def body(i, acc):
                    r0 = pl.multiple_of(i * blk, blk)
                    x = skey_scr[pl.ds(r0, blk), :]
                    kidx = r0 + lax.broadcasted_iota(I32, (blk, tq), 0)
                    hit = jnp.where(pred(x, kidx), 1, 0)
                    return acc + jnp.sum(hit.reshape(blk // SUBLANES, SUBLANES, tq), axis=0)
                acc = lax.fori_loop(0, nblk, body, jnp.zeros((SUBLANES, tq), I32))
                return jnp.sum(acc, axis=0, keepdims=True)

            theta = jnp.full((1, tq), INT_MIN, I32)
            cnt = count(lambda x, kidx: x >= 0)
            theta = jnp.where(cnt >= topk, 0, theta)

            def bit_step(i, theta):
                cand = theta | (1 << (30 - i))
                cnt = count(lambda x, kidx: x >= cand)
                return jnp.where(cnt >= topk, cand, theta)

            theta = lax.fori_loop(0, 31, bit_step, theta)
            n_gt = count(lambda x, kidx: x > theta)
            need = topk - n_gt
            nbits = max(1, (l_valid - 1).bit_length())

            def idx_step(i, jb):
                cand = jb - (1 << (nbits - 1 - i))
                cnt = count(lambda x, kidx: (x == theta) & (kidx <= cand))
                return jnp.where(cnt >= need, cand, jb)

            jb = lax.fori_loop(0, nbits, idx_step, jnp.full((1, tq), (1 << nbits) - 1, I32))
            thr_scr[0:1, :] = theta
            thr_scr[1:2, :] = jb

    def tile(near):
        k0 = pl.multiple_of(kt * tk, tk)
        mask = allowed_mask(k0, tk) if near else None
        if dsa:
            theta = thr_scr[0:1, :]
            jb = thr_scr[1:2, :]
            x = skey_scr[pl.ds(k0, tk), :]
            kidx = k0 + lax.broadcasted_iota(I32, (tk, tq), 0)
            sel = (x > theta) | ((x == theta) & (kidx <= jb))
            mask = sel & mask if near else sel
        else:
            kr = kr_ref[0]
        for h in range(nh):
            sl = slice(h * dh, (h + 1) * dh)
            s = jnp.dot(k_ref[0, :, sl], qT_ref[0, sl, :], preferred_element_type=F32)
            if not dsa:
                rs = slice(h * D_ROPE, (h + 1) * D_ROPE)
                s = s + jnp.dot(kr, qrT_ref[0, rs, :], preferred_element_type=F32)
            s = s * scale
            if dsa:
                far = far_ref[0:1, h:h + 1]
                if near:
                    parts = []
                    for sb in range(tk // LANES):
                        delta = k0 + sb * LANES - q_lo
                        parts.append(jnp.where(delta == 0, tab_ref[0, h],
                                               jnp.where(delta == -LANES, tab_ref[1, h], far)))
                    s = s + (jnp.concatenate(parts, axis=0) if len(parts) > 1 else parts[0])
                else:
                    s = s + far
            if mask is not None:
                s = jnp.where(mask, s, NEG_BIG)
            m_prev = m_scr[h:h + 1, :]
            m_new = jnp.maximum(m_prev, jnp.max(s, axis=0, keepdims=True))
            pr = jnp.exp(s - m_new)
            alpha = jnp.exp(m_prev - m_new)
            l_scr[h:h + 1, :] = alpha * l_scr[h:h + 1, :] + jnp.sum(pr, axis=0, keepdims=True)
            pv = jnp.dot(vT_ref[0, sl, :], pr.astype(BF16), preferred_element_type=F32)
            acc_scr[sl, :] = alpha * acc_scr[sl, :] + pv
            m_scr[h:h + 1, :] = m_new

    is_far = kt * tk + tk - 1 <= q_lo - LANES

    @pl.when((kt <= last_kt) & is_far)
    def _():
        tile(False)

    @pl.when((kt <= last_kt) & jnp.logical_not(is_far))
    def _():
        tile(True)

    @pl.when(kt == nk - 1)
    def _():
        for h in range(nh):
            sl = slice(h * dh, (h + 1) * dh)
            o = acc_scr[sl, :] / l_scr[h:h + 1, :]
            o_ref[0, :, sl] = o.T.astype(o_ref.dtype)


def _attention(mode, qT, k, vT, extra, *, p0, t_valid, l_valid, tk):
    b, hd, tpad = qT.shape
    lpad = k.shape[1]
    nh = hd // LANES
    tq = LANES
    nq, nk = tpad // tq, lpad // tk
    topk = min(TOPK_MAX, l_valid // 4)
    assert p0 % LANES == 0 and lpad % tk == 0 and tk % 256 == 0 and l_valid >= topk >= 1

    def last_kt(qt):
        q_hi = p0 + jnp.minimum(qt * tq + tq, t_valid) - 1
        return (jnp.minimum((q_hi // CHUNK + 1) * CHUNK, l_valid) - 1) // tk

    kmap = lambda i, q, kk: (i, jnp.minimum(kk, last_kt(q)), 0)
    vmap_ = lambda i, q, kk: (i, 0, jnp.minimum(kk, last_kt(q)))
    qmap = lambda i, q, kk: (i, 0, q)
    in_specs = [pl.BlockSpec((1, hd, tq), qmap),
                pl.BlockSpec((1, tk, hd), kmap),
                pl.BlockSpec((1, hd, tk), vmap_)]
    scratch = [pltpu.VMEM((nh, tq), F32), pltpu.VMEM((nh, tq), F32), pltpu.VMEM((hd, tq), F32)]
    if mode == "dsa":
        qih, qil, kih, kil, wiT, tabs, far = extra
        in_specs += [pl.BlockSpec((1, qih.shape[1], tq), qmap),
                     pl.BlockSpec((1, qil.shape[1], tq), qmap),
                     pl.BlockSpec((1, lpad, C_IDX_DIM), lambda i, q, kk: (i, 0, 0)),
                     pl.BlockSpec((1, lpad, C_IDX_DIM), lambda i, q, kk: (i, 0, 0)),
                     pl.BlockSpec((1, C_IDX_HEADS, tq), qmap),
                     pl.BlockSpec(tabs.shape, lambda i, q, kk: (0, 0, 0, 0)),
                     pl.BlockSpec(far.shape, lambda i, q, kk: (0, 0))]
        scratch += [pltpu.VMEM((lpad, tq), I32), pltpu.VMEM((SUBLANES, tq), I32)]
        scale = C_HEAD_DIM ** -0.5
    else:
        qrT, kr = extra
        in_specs += [pl.BlockSpec((1, qrT.shape[1], tq), qmap),
                     pl.BlockSpec((1, tk, D_ROPE), kmap)]
        scale = (D_NOPE + D_ROPE) ** -0.5
    return pl.pallas_call(
        functools.partial(_attn_body, mode=mode, p0=p0, t_valid=t_valid, l_valid=l_valid, tk=tk,
                          topk=topk, scale=scale, nh=nh),
        grid=(b, nq, nk),
        in_specs=in_specs,
        out_specs=pl.BlockSpec((1, tq, hd), lambda i, q, kk: (i, q, 0)),
        out_shape=jax.ShapeDtypeStruct((b, tpad, hd), BF16),
        scratch_shapes=scratch,
        compiler_params=_cparams(("parallel", "parallel", "arbitrary")),
        name="attn_" + mode,
    )(qT, k, vT, *extra)


def _split_hi_lo(x):
    hi = x.astype(BF16)
    lo = (x - hi.astype(F32)).astype(BF16)
    return hi, lo


def _pad_axis(x, axis, size):
    if x.shape[axis] == size:
        return x
    pad = [(0, 0)] * x.ndim
    pad[axis] = (0, size - x.shape[axis])
    return jnp.pad(x, pad)


def _key_tile(l_valid):
    lpad = -(-l_valid // 256) * 256
    for tk in (512, 256):
        if lpad % tk == 0:
            return lpad, tk
    return lpad, 256


def _prep_weights(W, d_model):
    f = {}
    wd = A_WIDTH
    n_even = W["ev_w_in"].shape[0]
    n_odd = W["od_w_in"].shape[0]
    f["even"] = []
    for e in range(n_even):
        wi = W["ev_w_in"][e]
        w_in = jnp.concatenate([wi[:, :3 * wd], wi[:, A_COLS:], wi[:, 3 * wd:A_COLS],
                                jnp.zeros((d_model, LORA_PAD - A_LORA), F32)], axis=1).astype(BF16)
        mu = W["rwkv_mu"][e]
        w2 = jnp.zeros((LANES, wd), F32).at[:A_DECAY_LORA].set(W["rwkv_w2"][e])
        a2 = jnp.zeros((LANES, wd), F32).at[A_DECAY_LORA:A_DECAY_LORA + A_ICLR_LORA].set(W["rwkv_a2"][e])
        g2 = jnp.zeros((LORA_PAD - LANES, wd), F32).at[:A_GATE_LORA].set(W["rwkv_g2"][e])
        hm = lambda v: v.reshape(A_HEADS, 1, A_HEAD_DIM)
        rw = dict(mu_r=mu[:3 * wd].reshape(1, -1),
                  mu_l=_pad_axis(mu[3 * wd:], 0, LORA_PAD).reshape(1, -1),
                  w0=W["rwkv_w0"][e].reshape(1, -1), w2=w2.astype(BF16),
                  a0=W["rwkv_a0"][e].reshape(1, -1), a2=a2.astype(BF16), g2=g2.astype(BF16),
                  k_k=W["rwkv_k_k"][e].reshape(1, -1), k_a=W["rwkv_k_a"][e].reshape(1, -1),
                  r_k=hm(W["rwkv_r_k"][e]), ln_w=hm(W["rwkv_ln_w"][e]), ln_b=hm(W["rwkv_ln_b"][e]))
        eye = jnp.eye(B_BLOCKS, dtype=F32)
        blockdiag = lambda w: (eye[:, None, :, None] * w[:, :, None, :]).reshape(B_WIDTH, B_WIDTH)
        wgate = jnp.concatenate([blockdiag(W["lru_wa"][e]), blockdiag(W["lru_wx"][e])], axis=1).astype(BF16)
        bgate = jnp.concatenate([W["lru_ba"][e], W["lru_bx"][e]])
        wo = W["ev_w_out"][e].astype(BF16)
        f["even"].append(dict(w_in=w_in, rw=rw, wgate=wgate, bgate=bgate, wo_a=wo[:wd], wo_b=wo[wd:],
                              cw=W["lru_conv_w"][e], cb=W["lru_conv_b"][e], lam=W["lru_lambda"][e]))
    f["odd"] = []
    cw_ = C_WIDTH
    qi_w = C_IDX_HEADS * C_IDX_DIM
    for o in range(n_odd):
        wi = W["od_w_in"][o]
        offs = [0]
        for s in (cw_, cw_, cw_, qi_w, C_IDX_DIM, C_IDX_HEADS, D_Q_RANK, D_KV_RANK, D_ROPE):
            offs.append(offs[-1] + s)
        q, k, v, qi, ki, wi_, qd, kvd, kr = [wi[:, offs[i]:offs[i + 1]] for i in range(9)]
        small = jnp.concatenate([ki, kr, wi_], axis=1)
        w_in = jnp.concatenate([q, k, v, qi, qd, kvd, _pad_axis(small, 1, 512)], axis=1).astype(BF16)
        wuq = W["mla_w_uq"][o].reshape(D_Q_RANK, D_HEADS, D_NOPE + D_ROPE)
        wuq = jnp.concatenate([wuq[:, :, :D_NOPE].reshape(D_Q_RANK, -1),
                               wuq[:, :, D_NOPE:].reshape(D_Q_RANK, -1)], axis=1).astype(BF16)
        wukv = W["mla_w_ukv"][o].reshape(D_KV_RANK, D_HEADS, D_NOPE + D_V)
        wukv = jnp.concatenate([wukv[:, :, :D_NOPE].reshape(D_KV_RANK, -1),
                                wukv[:, :, D_NOPE:].reshape(D_KV_RANK, -1)], axis=1).astype(BF16)
        wo = W["od_w_out"][o].astype(BF16)
        f["odd"].append(dict(w_in=w_in, wuq=wuq, wukv=wukv, wo_c=wo[:cw_], wo_d=wo[cw_:],
                             gq=W["mla_q_norm"][o], gkv=W["mla_kv_norm"][o]))
    f["ffn"] = [dict(wg=W["ffn_w_gate"][l].astype(BF16), wu=W["ffn_w_up"][l].astype(BF16),
                     wd=W["ffn_w_down"][l].astype(BF16)) for l in range(W["ffn_w_gate"].shape[0])]
    f["tabs"] = _bias_tables(W["rel_bias"])
    nb = N_BUCKETS // 2
    f["far"] = _pad_axis(W["rel_bias"][nb - 1:nb, :], 1, LANES)
    return f


def _mixer_even(x2, b, t, p0, shift, s0, h0, cbuf, fe, norm_g):
    wd = A_WIDTH
    p = _norm_matmul(x2, norm_g, fe["w_in"])
    p3 = p.reshape(b, t, -1)
    lora_col = 3 * wd + 2 * B_WIDTH
    shift_r = shift[:, None, :3 * wd]
    shift_l = _pad_axis(shift[:, None, 3 * wd:], 2, LORA_PAD)
    ya, s_new = _rwkv(p3, lora_col // LORA_PAD, shift_r, shift_l, s0, fe["rw"])
    yb, h_new, c_new = _rglru(p3, 3, 4, fe["cw"], fe["cb"], fe["wgate"], fe["bgate"], fe["lam"],
                              h0, cbuf, p0=p0)
    x2 = _matmul([ya.reshape(b * t, wd), yb.reshape(b * t, B_WIDTH)], [fe["wo_a"], fe["wo_b"]], res=x2)
    last = p3[:, t - 1]
    new_shift = jnp.concatenate([last[:, :3 * wd], last[:, lora_col:lora_col + A_LORA]], axis=-1)
    return x2, new_shift, s_new, h_new[:, 0], c_new


def _mixer_odd(x2, b, t, p0, ck, cv, cik, clat, ckr, fo, f, norm_g, cos, sin):
    cw_ = C_WIDTH
    p = _norm_matmul(x2, norm_g, fo["w_in"])
    p3 = p.reshape(b, t, -1)
    past = ck.shape[1]
    l_valid = past + t
    lpad, tk = _key_tile(l_valid)
    tpad = -(-t // LANES) * LANES
    q = p3[..., :cw_]
    k_new = p3[..., cw_:2 * cw_]
    v_new = p3[..., 2 * cw_:3 * cw_]
    qi = p3[..., 3 * cw_:3 * cw_ + 512]
    small = p3[..., 3 * cw_ + 1536:]
    ki_new = small[..., :C_IDX_DIM]
    wi = small[..., 2 * C_IDX_DIM:2 * C_IDX_DIM + C_IDX_HEADS]

    def keys(cache, new):
        allk = jnp.concatenate([cache.reshape(b, past, -1), new], axis=1) if past else new
        return _pad_axis(allk, 1, lpad)

    tq_ = lambda z: _pad_axis(jnp.swapaxes(z, 1, 2), 2, tpad)
    k_all = keys(ck, k_new).astype(BF16)
    vT_all = jnp.swapaxes(keys(cv, v_new), 1, 2).astype(BF16)
    ki_all = keys(cik, ki_new)
    kih, kil = _split_hi_lo(ki_all)
    qih, qil = _split_hi_lo(tq_(qi))
    yc = _attention("dsa", tq_(q).astype(BF16), k_all, vT_all,
                    (qih, qil, kih, kil, tq_(wi), f["tabs"], f["far"]),
                    p0=p0, t_valid=t, l_valid=l_valid, tk=tk)[:, :t]
    qn, qr, lat, krope = _mla_prep(p3, 7, 8, 9, fo["gq"], fo["gkv"], fo["wuq"], cos, sin)
    lat_all = keys(clat, lat)
    kr_all = keys(ckr, krope).astype(BF16)
    kv = _matmul([lat_all.reshape(b * lpad, -1)], [fo["wukv"]], out_dtype=BF16).reshape(b, lpad, -1)
    nw = D_HEADS * D_NOPE
    yd = _attention("mla", tq_(qn), kv[..., :nw], jnp.swapaxes(kv[..., nw:], 1, 2), (tq_(qr), kr_all),
                    p0=p0, t_valid=t, l_valid=l_valid, tk=tk)[:, :t]
    x2 = _matmul([yc.reshape(b * t, cw_), yd.reshape(b * t, -1)], [fo["wo_c"], fo["wo_d"]], res=x2)
    return (x2, k_new.reshape(b, t, C_HEADS, C_HEAD_DIM), v_new.reshape(b, t, C_HEADS, C_HEAD_DIM),
            ki_new, lat, krope)


def _trunk(x, p0, shift, rwkv_s, lru_h, lru_conv, dsa_k, dsa_v, dsa_ik, mla_lat, mla_kr, W, f):
    b, t, d = x.shape
    depth = W["norm_mix"].shape[0]
    pos = (p0 + jnp.arange(t)).astype(F32)
    inv = ROPE_BASE ** (-jnp.arange(0, D_ROPE, 2, dtype=F32) / D_ROPE)
    ang = pos[:, None] * inv[None, :]
    cos = jnp.tile(jnp.cos(ang), (1, 2 * D_HEADS))
    sin = jnp.tile(jnp.sin(ang), (1, 2 * D_HEADS))
    x2 = x.reshape(b * t, d)
    ev = [[] for _ in range(4)]
    od = [[] for _ in range(5)]
    for layer in range(depth):
        if layer % 2 == 0:
            e = layer // 2
            x2, *outs = _mixer_even(x2, b, t, p0, shift[e], rwkv_s[e], lru_h[e], lru_conv[e],
                                    f["even"][e], W["norm_mix"][layer])
            for lst, o_ in zip(ev, outs):
                lst.append(o_)
        else:
            o = layer // 2
            x2, *outs = _mixer_odd(x2, b, t, p0, dsa_k[o], dsa_v[o], dsa_ik[o], mla_lat[o], mla_kr[o],
                                   f["odd"][o], f, W["norm_mix"][layer], cos, sin)
            for lst, o_ in zip(od, outs):
                lst.append(o_)
        ff = f["ffn"][layer]
        x2 = _ffn(x2, W["norm_ffn"][layer], ff["wg"], ff["wu"], ff["wd"], W["final_norm"],
                  final_norm=layer == depth - 1)
    return (x2.reshape(b, t, d),) + tuple(jnp.stack(v) for v in ev) + tuple(jnp.stack(v) for v in od)


def kernel(x_prompt, x_sample, state_rwkv_shift, state_rwkv, state_lru, state_lru_conv, cache_dsa_k, cache_dsa_v, cache_dsa_idx_k, cache_mla_latent, cache_mla_krope, rel_bias, final_norm, norm_mix, norm_ffn, ffn_w_gate, ffn_w_up, ffn_w_down, ev_w_in, ev_w_out, rwkv_mu, rwkv_w0, rwkv_w2, rwkv_a0, rwkv_a2, rwkv_g2, rwkv_k_k, rwkv_k_a, rwkv_r_k, rwkv_ln_w, rwkv_ln_b, lru_conv_w, lru_conv_b, lru_wa, lru_ba, lru_wx, lru_bx, lru_lambda, od_w_in, od_w_out, mla_q_norm, mla_w_uq, mla_kv_norm, mla_w_ukv):
    W = dict(rel_bias=rel_bias, final_norm=final_norm, norm_mix=norm_mix, norm_ffn=norm_ffn,
             ffn_w_gate=ffn_w_gate, ffn_w_up=ffn_w_up, ffn_w_down=ffn_w_down,
             ev_w_in=ev_w_in, ev_w_out=ev_w_out, rwkv_mu=rwkv_mu, rwkv_w0=rwkv_w0,
             rwkv_w2=rwkv_w2, rwkv_a0=rwkv_a0, rwkv_a2=rwkv_a2, rwkv_g2=rwkv_g2,
             rwkv_k_k=rwkv_k_k, rwkv_k_a=rwkv_k_a, rwkv_r_k=rwkv_r_k, rwkv_ln_w=rwkv_ln_w,
             rwkv_ln_b=rwkv_ln_b, lru_conv_w=lru_conv_w, lru_conv_b=lru_conv_b, lru_wa=lru_wa,
             lru_ba=lru_ba, lru_wx=lru_wx, lru_bx=lru_bx, lru_lambda=lru_lambda,
             od_w_in=od_w_in, od_w_out=od_w_out, mla_q_norm=mla_q_norm, mla_w_uq=mla_w_uq,
             mla_kv_norm=mla_kv_norm, mla_w_ukv=mla_w_ukv)
    d_model = x_prompt.shape[-1]
    f = _prep_weights(W, d_model)
    bp = x_prompt.shape[0]
    n_even, n_odd = ev_w_in.shape[0], od_w_in.shape[0]
    dt = x_prompt.dtype
    z = lambda *shape: jnp.zeros(shape, dt)
    outs_p = _trunk(
        x_prompt, 0,
        z(n_even, bp, A_COLS), z(n_even, bp, A_HEADS, A_HEAD_DIM, A_HEAD_DIM),
        z(n_even, bp, B_WIDTH), z(n_even, bp, B_CONV - 1, B_WIDTH),
        z(n_odd, bp, 0, C_HEADS, C_HEAD_DIM), z(n_odd, bp, 0, C_HEADS, C_HEAD_DIM),
        z(n_odd, bp, 0, C_IDX_DIM), z(n_odd, bp, 0, D_KV_RANK), z(n_odd, bp, 0, D_ROPE), W, f)
    past = cache_dsa_k.shape[2]
    outs_s = _trunk(x_sample, past, state_rwkv_shift, state_rwkv, state_lru, state_lru_conv,
                    cache_dsa_k, cache_dsa_v, cache_dsa_idx_k, cache_mla_latent, cache_mla_krope, W, f)
    return (outs_p[0], outs_s[0]) + tuple(outs_p[1:]) + tuple(outs_s[1:])
```

```python
import functools
import math

import jax
import jax.numpy as jnp
from jax import lax
from jax.experimental import pallas as pl
from jax.experimental.pallas import tpu as pltpu

F32 = jnp.float32
BF16 = jnp.bfloat16
I32 = jnp.int32

CHUNK = 64
NORM_EPS = 1e-6
A_HEADS = 16
A_HEAD_DIM = 64
A_WIDTH = A_HEADS * A_HEAD_DIM
A_DECAY_LORA = 64
A_ICLR_LORA = 64
A_GATE_LORA = 160
A_LORA = A_DECAY_LORA + A_ICLR_LORA + A_GATE_LORA
A_COLS = 3 * A_WIDTH + A_LORA
A_LN_EPS = 64e-5
B_WIDTH = 1024
B_BLOCKS = 16
B_CONV = 4
B_C = 8.0
C_HEADS = 8
C_HEAD_DIM = 128
C_WIDTH = C_HEADS * C_HEAD_DIM
C_IDX_HEADS = 8
C_IDX_DIM = 64
TOPK_MAX = 256
D_HEADS = 8
D_NOPE = 128
D_ROPE = 64
D_V = 128
D_Q_RANK = 512
D_KV_RANK = 512
ROPE_BASE = 10000.0
N_BUCKETS = 32
MAX_DISTANCE = 128

LANES = 128
SUBLANES = 8
VMEM_LIMIT = 56 * 1024 * 1024
LORA_PAD = 512
NEG_BIG = -1e30
INT_MIN = -2147483648
MLA_DK = 256

NN = (((1,), (0,)), ((), ()))
NT = (((1,), (1,)), ((), ()))
BNN = (((2,), (1,)), ((0,), (0,)))
BNT = (((2,), (2,)), ((0,), (0,)))
BTN = (((1,), (1,)), ((0,), (0,)))


def _cparams(sem):
    return pltpu.CompilerParams(dimension_semantics=sem, vmem_limit_bytes=VMEM_LIMIT)


def _rms(x, g, eps=NORM_EPS):
    ms = jnp.mean(x * x, axis=-1, keepdims=True)
    return x * lax.rsqrt(ms + eps) * g


def _softplus(x):
    return jnp.maximum(x, 0.0) + jnp.log1p(jnp.exp(-jnp.abs(x)))


def _dotp(a, b, dims, passes):
    if passes == 6:
        return lax.dot_general(a, b, dims, precision=lax.Precision.HIGHEST, preferred_element_type=F32)
    ah = a.astype(BF16)
    bh = b.astype(BF16)
    out = lax.dot_general(ah, bh, dims, preferred_element_type=F32)
    if passes == 3:
        al = (a - ah.astype(F32)).astype(BF16)
        bl = (b - bh.astype(F32)).astype(BF16)
        out = out + lax.dot_general(ah, bl, dims, preferred_element_type=F32)
        out = out + lax.dot_general(al, bh, dims, preferred_element_type=F32)
    return out


def _norm_matmul_body(x_ref, g_ref, w_ref, o_ref, xn_ref):
    @pl.when(pl.program_id(1) == 0)
    def _():
        xn_ref[...] = _rms(x_ref[...], g_ref[...]).astype(BF16)

    o_ref[...] = jnp.dot(xn_ref[...], w_ref[...], preferred_element_type=F32)


def _row_tile(m, cap):
    tm = cap
    while m % tm:
        tm //= 2
    return tm


def _norm_matmul(x, g, w, *, tn=512):
    m, k = x.shape
    n = w.shape[1]
    tm = _row_tile(m, 1024)
    return pl.pallas_call(
        _norm_matmul_body,
        grid=(m // tm, n // tn),
        in_specs=[pl.BlockSpec((tm, k), lambda i, j: (i, 0)),
                  pl.BlockSpec((1, k), lambda i, j: (0, 0)),
                  pl.BlockSpec((k, tn), lambda i, j: (0, j))],
        out_specs=pl.BlockSpec((tm, tn), lambda i, j: (i, j)),
        out_shape=jax.ShapeDtypeStruct((m, n), F32),
        scratch_shapes=[pltpu.VMEM((tm, k), BF16)],
        compiler_params=_cparams(("parallel", "arbitrary")),
        name="norm_matmul",
    )(x, g.reshape(1, k), w)


def _mm_body(*refs, n_lhs, has_res):
    o_ref = refs[-1]
    acc = refs[2 * n_lhs][...] if has_res else None
    for a_ref, w_ref in zip(refs[:n_lhs], refs[n_lhs:2 * n_lhs]):
        d = jnp.dot(a_ref[...].astype(BF16), w_ref[...], preferred_element_type=F32)
        acc = d if acc is None else acc + d
    o_ref[...] = acc.astype(o_ref.dtype)


def _matmul(lhs_list, w_list, res=None, *, tn=512, out_dtype=F32):
    m = lhs_list[0].shape[0]
    n = w_list[0].shape[1]
    tm = _row_tile(m, 1024)
    in_specs = [pl.BlockSpec((tm, a.shape[1]), lambda i, j: (i, 0)) for a in lhs_list]
    in_specs += [pl.BlockSpec((w.shape[0], tn), lambda i, j: (0, j)) for w in w_list]
    args = list(lhs_list) + list(w_list)
    if res is not None:
        in_specs.append(pl.BlockSpec((tm, tn), lambda i, j: (i, j)))
        args.append(res)
    return pl.pallas_call(
        functools.partial(_mm_body, n_lhs=len(lhs_list), has_res=res is not None),
        grid=(m // tm, n // tn),
        in_specs=in_specs,
        out_specs=pl.BlockSpec((tm, tn), lambda i, j: (i, j)),
        out_shape=jax.ShapeDtypeStruct((m, n), out_dtype),
        compiler_params=_cparams(("parallel", "arbitrary")),
        name="matmul",
    )(*args)


def _ffn_body(x_ref, g_ref, wg_ref, wu_ref, wd_ref, gf_ref, o_ref, xn_ref, acc_ref, *, final_norm):
    f = pl.program_id(1)

    @pl.when(f == 0)
    def _():
        xn_ref[...] = _rms(x_ref[...], g_ref[...]).astype(BF16)
        acc_ref[...] = jnp.zeros_like(acc_ref)

    xn = xn_ref[...]
    hg = jnp.dot(xn, wg_ref[...], preferred_element_type=F32)
    hu = jnp.dot(xn, wu_ref[...], preferred_element_type=F32)
    h = hg * jax.nn.sigmoid(hg) * hu
    acc_ref[...] += jnp.dot(h.astype(BF16), wd_ref[...], preferred_element_type=F32)

    @pl.when(f == pl.num_programs(1) - 1)
    def _():
        y = x_ref[...] + acc_ref[...]
        if final_norm:
            y = _rms(y, gf_ref[...])
        o_ref[...] = y


def _ffn(x, g, wg, wu, wd, gf, *, final_norm, tf=512):
    m, k = x.shape
    dff = wg.shape[1]
    tm = min(512, m)
    return pl.pallas_call(
        functools.partial(_ffn_body, final_norm=final_norm),
        grid=(m // tm, dff // tf),
        in_specs=[pl.BlockSpec((tm, k), lambda i, f: (i, 0)),
                  pl.BlockSpec((1, k), lambda i, f: (0, 0)),
                  pl.BlockSpec((k, tf), lambda i, f: (0, f)),
                  pl.BlockSpec((k, tf), lambda i, f: (0, f)),
                  pl.BlockSpec((tf, k), lambda i, f: (f, 0)),
                  pl.BlockSpec((1, k), lambda i, f: (0, 0))],
        out_specs=pl.BlockSpec((tm, k), lambda i, f: (i, 0)),
        out_shape=jax.ShapeDtypeStruct((m, k), F32),
        scratch_shapes=[pltpu.VMEM((tm, k), BF16), pltpu.VMEM((tm, k), F32)],
        compiler_params=_cparams(("parallel", "arbitrary")),
        name="ffn",
    )(x, g.reshape(1, k), wg, wu, wd, gf.reshape(1, k))


def _lru_body(gate_ref, xb_ref, cw_ref, cb_ref, wg_ref, bg_ref, lam_ref, h0_ref, cbuf_ref,
              y_ref, hout_ref, cout_ref, xbuf, hcar, *, tt, p0):
    t = pl.program_id(1)
    w = B_WIDTH
    halo = SUBLANES

    @pl.when(t == 0)
    def _():
        xbuf[0:halo, :] = jnp.zeros((halo, w), F32)
        xbuf[halo - (B_CONV - 1):halo, :] = cbuf_ref[0]
        hcar[...] = h0_ref[0]

    xb = xb_ref[0]
    xbuf[halo:halo + tt, :] = xb
    xc = cb_ref[...] + cw_ref[B_CONV - 1:B_CONV, :] * xb
    for j in range(B_CONV - 1):
        off = halo - (B_CONV - 1) + j
        xc = xc + cw_ref[j:j + 1, :] * xbuf[off:off + tt, :]
    tail = xbuf[tt + halo - (B_CONV - 1):tt + halo, :]
    xbuf[halo - (B_CONV - 1):halo, :] = tail

    pre = jnp.dot(xc.astype(BF16), wg_ref[...], preferred_element_type=F32) + bg_ref[...]
    rg = jax.nn.sigmoid(pre[:, :w])
    ig = jax.nn.sigmoid(pre[:, w:])
    log_a = (-B_C) * rg * _softplus(-lam_ref[...])
    a = jnp.exp(log_a)
    row = lax.broadcasted_iota(I32, (tt, w), 0)
    th = jnp.tanh(log_a)
    mult = jnp.sqrt(-2.0 * th / (1.0 - th))
    mult = jnp.where(row + (p0 + t * tt) == 0, 1.0, mult)
    u = mult * (ig * xc)

    d = 1
    while d < tt:
        keep = row >= d
        a_sh = pltpu.roll(a, d, 0)
        u_sh = pltpu.roll(u, d, 0)
        u = u + jnp.where(keep, a * u_sh, 0.0)
        a = jnp.where(keep, a * a_sh, a)
        d *= 2
    h = u + a * hcar[...]
    hcar[...] = h[tt - 1:tt, :]
    y_ref[0] = (h * jax.nn.gelu(gate_ref[0])).astype(y_ref.dtype)

    @pl.when(t == pl.num_programs(1) - 1)
    def _():
        hout_ref[0] = h[tt - 1:tt, :]
        cout_ref[0] = tail


def _rglru(p3, gate_blk, xb_blk, cw, cb, wgate, bgate, lam, h0, cbuf, *, p0):
    b, t, _ = p3.shape
    w = B_WIDTH
    tt = min(256, t)
    row = lambda v: v.reshape(1, -1)
    return pl.pallas_call(
        functools.partial(_lru_body, tt=tt, p0=p0),
        grid=(b, t // tt),
        in_specs=[pl.BlockSpec((1, tt, w), lambda i, j: (i, j, gate_blk)),
                  pl.BlockSpec((1, tt, w), lambda i, j: (i, j, xb_blk)),
                  pl.BlockSpec((B_CONV, w), lambda i, j: (0, 0)),
                  pl.BlockSpec((1, w), lambda i, j: (0, 0)),
                  pl.BlockSpec((w, 2 * w), lambda i, j: (0, 0)),
                  pl.BlockSpec((1, 2 * w), lambda i, j: (0, 0)),
                  pl.BlockSpec((1, w), lambda i, j: (0, 0)),
                  pl.BlockSpec((1, 1, w), lambda i, j: (i, 0, 0)),
                  pl.BlockSpec((1, B_CONV - 1, w), lambda i, j: (i, 0, 0))],
        out_specs=[pl.BlockSpec((1, tt, w), lambda i, j: (i, j, 0)),
                   pl.BlockSpec((1, 1, w), lambda i, j: (i, 0, 0)),
                   pl.BlockSpec((1, B_CONV - 1, w), lambda i, j: (i, 0, 0))],
        out_shape=[jax.ShapeDtypeStruct((b, t, w), BF16),
                   jax.ShapeDtypeStruct((b, 1, w), F32),
                   jax.ShapeDtypeStruct((b, B_CONV - 1, w), F32)],
        scratch_shapes=[pltpu.VMEM((tt + SUBLANES, w), F32), pltpu.VMEM((1, w), F32)],
        compiler_params=_cparams(("parallel", "arbitrary")),
        name="rglru",
    )(p3, p3, cw, row(cb), wgate, row(bgate), row(lam), h0.reshape(b, 1, w), cbuf)


RWKV_PASSES = 1
RWKV_PASSES_SOLVE = 3


def _rwkv_body(rkv_ref, lora_ref, shr_ref, shl_ref, s0_ref, mur_ref, mul_ref, w0_ref, w2_ref, a0_ref,
               a2_ref, g2_ref, kk_ref, ka_ref, rk_ref, lnw_ref, lnb_ref, tri_ref,
               y_ref, sout_ref, buf_r, buf_l, s_scr, st_r, st_k, st_v, st_q, st_a, st_l, st_d, *, c):
    ci = pl.program_id(1)
    halo = SUBLANES
    hn, n, wd = A_HEADS, A_HEAD_DIM, A_WIDTH
    mm = functools.partial(_dotp, passes=RWKV_PASSES)

    @pl.when(ci == 0)
    def _():
        buf_r[halo - 1:halo, :] = shr_ref[0]
        buf_l[halo - 1:halo, :] = shl_ref[0]
        s_scr[...] = s0_ref[0]

    cur_r = rkv_ref[0]
    cur_l = lora_ref[0]
    buf_r[halo:halo + c, :] = cur_r
    buf_l[halo:halo + c, :] = cur_l
    xm = cur_r + mur_ref[...] * (buf_r[halo - 1:halo - 1 + c, :] - cur_r)
    lo = cur_l + mul_ref[...] * (buf_l[halo - 1:halo - 1 + c, :] - cur_l)
    buf_r[halo - 1:halo, :] = cur_r[c - 1:c, :]
    buf_l[halo - 1:halo, :] = cur_l[c - 1:c, :]

    r = xm[:, :wd]
    k = xm[:, wd:2 * wd]
    v = xm[:, 2 * wd:]
    lo_a = lo[:, :LANES]
    w_pre = w0_ref[...] + jnp.dot(jnp.tanh(lo_a).astype(BF16), w2_ref[...], preferred_element_type=F32)
    w_log = -_softplus(-w_pre) - 0.5
    ld = -jnp.exp(w_log)
    a = jax.nn.sigmoid(a0_ref[...] + jnp.dot(lo_a.astype(BF16), a2_ref[...], preferred_element_type=F32))
    g = jnp.dot(jax.nn.sigmoid(lo[:, LANES:]).astype(BF16), g2_ref[...], preferred_element_type=F32)
    kq = k * kk_ref[...]
    k2 = k * (1.0 + (a - 1.0) * ka_ref[...])
    lc = lax.dot_general(tri_ref[...], ld, NN, precision=lax.Precision.HIGHEST,
                         preferred_element_type=F32)

    for h in range(hn):
        sl = slice(h * n, (h + 1) * n)
        st_r[h] = r[:, sl]
        st_k[h] = k2[:, sl]
        st_v[h] = v[:, sl]
        st_q[h] = kq[:, sl]
        st_a[h] = a[:, sl]
        st_l[h] = lc[:, sl]
        st_d[h] = ld[:, sl]

    rh_, k2h, vh, kqh, ah, lch, ldh = (st_r[...], st_k[...], st_v[...], st_q[...], st_a[...],
                                       st_l[...], st_d[...])
    nrm = jnp.sqrt(jnp.sum(kqh * kqh, axis=-1, keepdims=True))
    kk = kqh / jnp.maximum(nrm, 1e-12)
    kka = kk * ah
    e_neg = jnp.exp(-lch)
    am = jnp.exp(lch - ldh) * kk
    bm = kka * e_neg
    kh = k2h * e_neg
    rh = rh_ * jnp.exp(lch)
    l_end = lch[:, c - 1:c, :]
    e_c = jnp.exp(l_end - lch)
    bp = kka * e_c
    kp = k2h * e_c
    w_end = jnp.exp(l_end)

    x2 = jnp.concatenate([am, rh], axis=1)
    zb = _dotp(x2, bm, BNT, RWKV_PASSES_SOLVE)
    zk = mm(x2, kh, BNT)
    ti = lax.broadcasted_iota(I32, (hn, c, c), 1)
    si = lax.broadcasted_iota(I32, (hn, c, c), 2)
    strict = si < ti
    incl = si <= ti
    m1 = jnp.where(strict, zb[:, :c], 0.0)
    m4 = jnp.where(incl, zb[:, c:], 0.0)
    m2 = jnp.where(strict, zk[:, :c], 0.0)
    m3 = jnp.where(incl, zk[:, c:], 0.0)

    tm = jnp.where(si == ti, 1.0, 0.0) - m1
    npow = mm(m1, m1, BNN)
    span = 2
    while span < c:
        tm = tm + mm(tm, npow, BNN)
        span *= 2
        if span < c:
            npow = mm(npow, npow, BNN)

    s0 = s_scr[...]
    rhs = mm(am, s0, BNT) + mm(m2, vh, BNN)
    p = mm(tm, rhs, BNN)
    y = mm(rh, s0, BNT) + mm(m3, vh, BNN) - mm(m4, p, BNN)
    s_new = s0 * w_end + mm(vh, kp, BTN) - mm(p, bp, BTN)
    s_scr[...] = s_new

    mean = jnp.mean(y, axis=-1, keepdims=True)
    yc = y - mean
    var = jnp.mean(yc * yc, axis=-1, keepdims=True)
    yn = yc * lax.rsqrt(var + A_LN_EPS) * lnw_ref[...] + lnb_ref[...]
    bonus = jnp.sum(rh_ * k2h * rk_ref[...], axis=-1, keepdims=True) * vh
    yo = yn + bonus
    yo = jnp.concatenate([yo[h] for h in range(hn)], axis=-1)
    y_ref[0] = (yo * g).astype(y_ref.dtype)

    @pl.when(ci == pl.num_programs(1) - 1)
    def _():
        sout_ref[0] = s_new


def _rwkv(p3, lora_blk, shift_r, shift_l, s0, wts):
    b, t, _ = p3.shape
    c = min(CHUNK, t)
    hn, n, wd = A_HEADS, A_HEAD_DIM, A_WIDTH
    full = lambda shape: pl.BlockSpec(shape, lambda i, j: (0,) * len(shape))
    tri = (jnp.arange(c)[:, None] >= jnp.arange(c)[None, :]).astype(F32)
    st = pltpu.VMEM((hn, c, n), F32)
    return pl.pallas_call(
        functools.partial(_rwkv_body, c=c),
        grid=(b, t // c),
        in_specs=[pl.BlockSpec((1, c, 3 * wd), lambda i, j: (i, j, 0)),
                  pl.BlockSpec((1, c, LORA_PAD), lambda i, j: (i, j, lora_blk)),
                  pl.BlockSpec((1, 1, 3 * wd), lambda i, j: (i, 0, 0)),
                  pl.BlockSpec((1, 1, LORA_PAD), lambda i, j: (i, 0, 0)),
                  pl.BlockSpec((1, hn, n, n), lambda i, j: (i, 0, 0, 0)),
                  full((1, 3 * wd)), full((1, LORA_PAD)), full((1, wd)), full((LANES, wd)),
                  full((1, wd)), full((LANES, wd)), full((LORA_PAD - LANES, wd)),
                  full((1, wd)), full((1, wd)), full((hn, 1, n)), full((hn, 1, n)), full((hn, 1, n)),
                  full((c, c))],
        out_specs=[pl.BlockSpec((1, c, wd), lambda i, j: (i, j, 0)),
                   pl.BlockSpec((1, hn, n, n), lambda i, j: (i, 0, 0, 0))],
        out_shape=[jax.ShapeDtypeStruct((b, t, wd), BF16),
                   jax.ShapeDtypeStruct((b, hn, n, n), F32)],
        scratch_shapes=[pltpu.VMEM((c + SUBLANES, 3 * wd), F32), pltpu.VMEM((c + SUBLANES, LORA_PAD), F32),
                        pltpu.VMEM((hn, n, n), F32), st, st, st, st, st, st, st],
        compiler_params=_cparams(("parallel", "arbitrary")),
        name="rwkv7",
    )(p3, p3, shift_r, shift_l, s0, wts["mu_r"], wts["mu_l"], wts["w0"], wts["w2"], wts["a0"], wts["a2"],
      wts["g2"], wts["k_k"], wts["k_a"], wts["r_k"], wts["ln_w"], wts["ln_b"], tri)


def _rope(x, cos, sin):
    wdt = x.shape[-1]
    lane = lax.broadcasted_iota(I32, x.shape, 1)
    first = (lane % D_ROPE) < (D_ROPE // 2)
    rot = jnp.where(first, -pltpu.roll(x, wdt - D_ROPE // 2, 1), pltpu.roll(x, D_ROPE // 2, 1))
    return x * cos + rot * sin


def _mla_prep_body(qd_ref, kvd_ref, sm_ref, gq_ref, gkv_ref, wuq_ref, cos_ref, sin_ref,
                   qn_ref, qr_ref, lat_ref, kr_ref):
    nope_w = D_HEADS * D_NOPE
    qdn = _rms(qd_ref[0], gq_ref[...]).astype(BF16)
    qf = jnp.dot(qdn, wuq_ref[...], preferred_element_type=F32)
    qn_ref[0] = qf[:, :nope_w].astype(qn_ref.dtype)
    cos = cos_ref[...]
    sin = sin_ref[...]
    qr_ref[0] = _rope(qf[:, nope_w:], cos, sin).astype(qr_ref.dtype)
    lat_ref[0] = _rms(kvd_ref[0], gkv_ref[...])
    sm = sm_ref[0][:, :LANES]
    kr = _rope(sm, cos[:, :LANES], sin[:, :LANES])
    kr_ref[0] = kr[:, C_IDX_DIM:C_IDX_DIM + D_ROPE]


def _mla_prep(p3, qd_blk, kvd_blk, sm_blk, gq, gkv, wuq, cos, sin):
    b, t, _ = p3.shape
    tt = min(256, t)
    rw = D_HEADS * D_ROPE
    full = lambda shape: pl.BlockSpec(shape, lambda i, j: (0,) * len(shape))
    return pl.pallas_call(
        _mla_prep_body,
        grid=(b, t // tt),
        in_specs=[pl.BlockSpec((1, tt, D_Q_RANK), lambda i, j: (i, j, qd_blk)),
                  pl.BlockSpec((1, tt, D_KV_RANK), lambda i, j: (i, j, kvd_blk)),
                  pl.BlockSpec((1, tt, 512), lambda i, j: (i, j, sm_blk)),
                  full((1, D_Q_RANK)), full((1, D_KV_RANK)), full(wuq.shape),
                  pl.BlockSpec((tt, rw), lambda i, j: (j, 0)),
                  pl.BlockSpec((tt, rw), lambda i, j: (j, 0))],
        out_specs=[pl.BlockSpec((1, tt, D_HEADS * D_NOPE), lambda i, j: (i, j, 0)),
                   pl.BlockSpec((1, tt, rw), lambda i, j: (i, j, 0)),
                   pl.BlockSpec((1, tt, D_KV_RANK), lambda i, j: (i, j, 0)),
                   pl.BlockSpec((1, tt, D_ROPE), lambda i, j: (i, j, 0))],
        out_shape=[jax.ShapeDtypeStruct((b, t, D_HEADS * D_NOPE), BF16),
                   jax.ShapeDtypeStruct((b, t, rw), BF16),
                   jax.ShapeDtypeStruct((b, t, D_KV_RANK), F32),
                   jax.ShapeDtypeStruct((b, t, D_ROPE), F32)],
        compiler_params=_cparams(("parallel", "parallel")),
        name="mla_prep",
    )(p3, p3, p3, gq.reshape(1, -1), gkv.reshape(1, -1), wuq, cos, sin)


def _bias_table_body(rb_ref, o_ref):
    d = pl.program_id(0)
    h = pl.program_id(1)
    nb = N_BUCKETS // 2
    max_exact = nb // 2
    s = lax.broadcasted_iota(I32, (LANES, LANES), 0)
    q = lax.broadcasted_iota(I32, (LANES, LANES), 1)
    rel = s - q - d * LANES
    n = jnp.abs(rel)
    big = jnp.maximum(n, max_exact).astype(F32)
    large = max_exact + (jnp.log(big / max_exact) / math.log(MAX_DISTANCE / max_exact)
                         * (nb - max_exact)).astype(I32)
    large = jnp.minimum(large, nb - 1)
    bucket = jnp.where(rel > 0, nb, 0) + jnp.where(n < max_exact, n, large)
    out = jnp.zeros((LANES, LANES), F32)
    for bk in range(N_BUCKETS):
        out = jnp.where(bucket == bk, rb_ref[bk, h], out)
    o_ref[0, 0] = out


def _bias_tables(rel_bias):
    return pl.pallas_call(
        _bias_table_body,
        grid=(2, C_HEADS),
        in_specs=[pl.BlockSpec(memory_space=pltpu.SMEM)],
        out_specs=pl.BlockSpec((1, 1, LANES, LANES), lambda d, h: (d, h, 0, 0)),
        out_shape=jax.ShapeDtypeStruct((2, C_HEADS, LANES, LANES), F32),
        name="bias_tables",
    )(rel_bias)


def _transpose32(x):
    x = list(x)
    for s, msk in ((16, 0x0000FFFF), (8, 0x00FF00FF), (4, 0x0F0F0F0F), (2, 0x33333333), (1, 0x55555555)):
        sh = jnp.full(x[0].shape, s, I32)
        for i in range(32):
            if i & s == 0:
                t = (lax.shift_right_logical(x[i], sh) ^ x[i + s]) & msk
                x[i + s] = x[i + s] ^ t
                x[i] = x[i] ^ lax.shift_left(t, sh)
    return x


def _attn_body(*refs, mode, p0, t_valid, l_valid, tq, tk, dk, topk, scale, nh):
    dsa = mode == "dsa"
    dh = LANES
    if dsa:
        (qT_ref, k_ref, vT_ref, q3_ref, ki3_ref, wiT_ref, tab_ref, far_ref,
         o_ref, m_scr, l_scr, acc_scr, skey_scr, thr_scr, planes_scr, e_scr) = refs
    else:
        (qT_ref, k_ref, vT_ref, o_ref, m_scr, l_scr, acc_scr) = refs
    qt = pl.program_id(1)
    kt = pl.program_id(2)
    nk = pl.num_programs(2)
    q_lo = p0 + qt * tq
    q_hi = p0 + jnp.minimum(qt * tq + tq, t_valid) - 1
    n_allowed = jnp.minimum((q_hi // CHUNK + 1) * CHUNK, l_valid)
    last_kt = (n_allowed - 1) // tk
    qpos = q_lo + lax.broadcasted_iota(I32, (1, tq), 1)
    qchunk = qpos // CHUNK

    def allowed_mask(k0, rows):
        kidx = k0 + lax.broadcasted_iota(I32, (rows, tq), 0)
        return (kidx // CHUNK <= qchunk) & (kidx < l_valid)

    @pl.when(kt == 0)
    def _():
        m_scr[...] = jnp.full(m_scr.shape, NEG_BIG, F32)
        l_scr[...] = jnp.zeros(l_scr.shape, F32)
        acc_scr[...] = jnp.zeros(acc_scr.shape, F32)

    if dsa:
        iscale = (C_IDX_HEADS * C_IDX_DIM) ** -0.5

        @pl.when(kt == 0)
        def _():
            wpt = tk // 32
            kw = 3 * C_IDX_DIM

            def score_tile(j, carry):
                k0 = pl.multiple_of(j * tk, tk)
                ki3 = ki3_ref[0, pl.ds(k0, tk), :]
                s = jnp.zeros((tk, tq), F32)
                for h in range(C_IDX_HEADS):
                    d = jnp.dot(ki3, q3_ref[0, h * kw:(h + 1) * kw, :], preferred_element_type=F32)
                    s = s + jnp.maximum(d, 0.0) * wiT_ref[0, h:h + 1, :]
                s = s * iscale + 0.0
                bits = pltpu.bitcast(s, I32)
                key = bits ^ ((bits >> 31) & 0x7FFFFFFF)
                key = jnp.where(allowed_mask(k0, tk), key, INT_MIN)
                skey_scr[pl.ds(k0, tk), :] = key
                ukey = key ^ INT_MIN
                w0 = pl.multiple_of(j * wpt, SUBLANES)
                for g in range(tk // 256):
                    rows = [ukey[g * 256 + 8 * i:g * 256 + 8 * i + 8, :] for i in range(32)]
                    for bi, plane in enumerate(_transpose32(rows)):
                        planes_scr[bi, pl.ds(w0 + g * SUBLANES, SUBLANES), :] = plane
                e_scr[pl.ds(w0, wpt), :] = jnp.full((wpt, tq), -1, I32)
                return carry

            ntile = last_kt + 1
            lax.fori_loop(0, ntile, score_tile, 0)

            def popsum(fn):
                def body(j, acc):
                    w0 = pl.multiple_of(j * wpt, SUBLANES)
                    pc = lax.population_count(fn(w0))
                    return acc + jnp.sum(pc.reshape(wpt // SUBLANES, SUBLANES, tq), axis=0)
                acc = lax.fori_loop(0, ntile, body, jnp.zeros((SUBLANES, tq), I32))
                return jnp.sum(acc, axis=0, keepdims=True)

            def bit_step(i, st):
                c_gt, th = st
                bi = 31 - i
                c1 = popsum(lambda w0: e_scr[pl.ds(w0, wpt), :] & planes_scr[bi, pl.ds(w0, wpt), :])
                take = (c_gt + c1) >= topk

                def upd(j, carry):
                    w0 = pl.multiple_of(j * wpt, SUBLANES)
                    e = e_scr[pl.ds(w0, wpt), :]
                    t = e & planes_scr[bi, pl.ds(w0, wpt), :]
                    e_scr[pl.ds(w0, wpt), :] = jnp.where(take, t, e ^ t)
                    return carry

                lax.fori_loop(0, ntile, upd, 0)
                return (jnp.where(take, c_gt, c_gt + c1),
                        jnp.where(take, th | lax.shift_left(jnp.int32(1), bi), th))

            zero = jnp.zeros((1, tq), I32)
            c_gt, th_u = lax.fori_loop(0, 32, bit_step, (zero, zero))
            need = topk - c_gt
            nbits = max(1, (l_valid - 1).bit_length())

            def idx_step(i, jb):
                cand = jb - lax.shift_left(jnp.int32(1), nbits - 1 - i)

                def hits(w0):
                    wr = w0 + lax.broadcasted_iota(I32, (wpt, tq), 0)
                    base = (wr >> 3) * 256 + (wr & 7)
                    mx = (cand - base) >> 3
                    low = jnp.left_shift(2, jnp.clip(mx, 0, 30)) - 1
                    msk = jnp.where(mx < 0, 0, jnp.where(mx >= 31, -1, low))
                    return e_scr[pl.ds(w0, wpt), :] & msk

                return jnp.where(popsum(hits) >= need, cand, jb)

            jb = lax.fori_loop(0, nbits, idx_step, jnp.full((1, tq), (1 << nbits) - 1, I32))
            thr_scr[0:1, :] = th_u ^ INT_MIN
            thr_scr[1:2, :] = jb

    def tile(near):
        k0 = pl.multiple_of(kt * tk, tk)
        mask = allowed_mask(k0, tk) if near else None
        if dsa:
            theta = thr_scr[0:1, :]
            jb = thr_scr[1:2, :]
            x = skey_scr[pl.ds(k0, tk), :]
            kidx = k0 + lax.broadcasted_iota(I32, (tk, tq), 0)
            sel = (x > theta) | ((x == theta) & (kidx <= jb))
            mask = sel & mask if near else sel
        for h in range(nh):
            sl = slice(h * dh, (h + 1) * dh)
            ks = slice(h * dk, (h + 1) * dk)
            s = jnp.dot(k_ref[0, :, ks], qT_ref[0, ks, :], preferred_element_type=F32) * scale
            if dsa:
                far = far_ref[0:1, h:h + 1]
                if near:
                    cols = []
                    for qb in range(tq // LANES):
                        parts = []
                        for sb in range(tk // LANES):
                            delta = k0 + sb * LANES - (q_lo + qb * LANES)
                            parts.append(jnp.where(delta == 0, tab_ref[0, h],
                                                   jnp.where(delta == -LANES, tab_ref[1, h], far)))
                        cols.append(jnp.concatenate(parts, axis=0))
                    s = s + (jnp.concatenate(cols, axis=1) if len(cols) > 1 else cols[0])
                else:
                    s = s + far
            if mask is not None:
                s = jnp.where(mask, s, NEG_BIG)
            m_prev = m_scr[h:h + 1, :]
            m_new = jnp.maximum(m_prev, jnp.max(s, axis=0, keepdims=True))
            pr = jnp.exp(s - m_new)
            alpha = jnp.exp(m_prev - m_new)
            l_scr[h:h + 1, :] = alpha * l_scr[h:h + 1, :] + jnp.sum(pr, axis=0, keepdims=True)
            pv = jnp.dot(vT_ref[0, sl, :], pr.astype(BF16), preferred_element_type=F32)
            acc_scr[sl, :] = alpha * acc_scr[sl, :] + pv
            m_scr[h:h + 1, :] = m_new

    is_far = kt * tk + tk - 1 <= q_lo - LANES

    @pl.when((kt <= last_kt) & is_far)
    def _():
        tile(False)

    @pl.when((kt <= last_kt) & jnp.logical_not(is_far))
    def _():
        tile(True)

    @pl.when(kt == nk - 1)
    def _():
        for h in range(nh):
            sl = slice(h * dh, (h + 1) * dh)
            o = acc_scr[sl, :] / l_scr[h:h + 1, :]
            o_ref[0, :, sl] = o.T.astype(o_ref.dtype)


def _attention(mode, qT, k, vT, extra, *, p0, t_valid, l_valid, tk):
    b, hdk, tpad = qT.shape
    lpad = k.shape[1]
    hd = vT.shape[1]
    nh = hd // LANES
    dk = hdk // nh
    tq = 2 * LANES if tpad % (2 * LANES) == 0 else LANES
    nq, nk = tpad // tq, lpad // tk
    topk = min(TOPK_MAX, l_valid // 4)
    assert p0 % LANES == 0 and lpad % tk == 0 and tk % 256 == 0 and l_valid >= topk >= 1 and tk >= topk

    def last_kt(qt):
        q_hi = p0 + jnp.minimum(qt * tq + tq, t_valid) - 1
        return (jnp.minimum((q_hi // CHUNK + 1) * CHUNK, l_valid) - 1) // tk

    kmap = lambda i, q, kk: (i, jnp.minimum(kk, last_kt(q)), 0)
    vmap_ = lambda i, q, kk: (i, 0, jnp.minimum(kk, last_kt(q)))
    qmap = lambda i, q, kk: (i, 0, q)
    in_specs = [pl.BlockSpec((1, hdk, tq), qmap),
                pl.BlockSpec((1, tk, hdk), kmap),
                pl.BlockSpec((1, hd, tk), vmap_)]
    scratch = [pltpu.VMEM((nh, tq), F32), pltpu.VMEM((nh, tq), F32), pltpu.VMEM((hd, tq), F32)]
    if mode == "dsa":
        q3, ki3, wiT, tabs, far = extra
        in_specs += [pl.BlockSpec((1, q3.shape[1], tq), qmap),
                     pl.BlockSpec((1, lpad, ki3.shape[2]), lambda i, q, kk: (i, 0, 0)),
                     pl.BlockSpec((1, C_IDX_HEADS, tq), qmap),
                     pl.BlockSpec(tabs.shape, lambda i, q, kk: (0, 0, 0, 0)),
                     pl.BlockSpec(far.shape, lambda i, q, kk: (0, 0))]
        scratch += [pltpu.VMEM((lpad, tq), I32), pltpu.VMEM((SUBLANES, tq), I32),
                    pltpu.VMEM((32, lpad // 32, tq), I32), pltpu.VMEM((lpad // 32, tq), I32)]
        scale = C_HEAD_DIM ** -0.5
    else:
        scale = (D_NOPE + D_ROPE) ** -0.5
    return pl.pallas_call(
        functools.partial(_attn_body, mode=mode, p0=p0, t_valid=t_valid, l_valid=l_valid, tq=tq, tk=tk,
                          dk=dk, topk=topk, scale=scale, nh=nh),
        grid=(b, nq, nk),
        in_specs=in_specs,
        out_specs=pl.BlockSpec((1, tq, hd), lambda i, q, kk: (i, q, 0)),
        out_shape=jax.ShapeDtypeStruct((b, tpad, hd), BF16),
        scratch_shapes=scratch,
        compiler_params=_cparams(("parallel", "parallel", "arbitrary")),
        name="attn_" + mode,
    )(qT, k, vT, *extra)


def _split_hi_lo(x):
    hi = x.astype(BF16)
    lo = (x - hi.astype(F32)).astype(BF16)
    return hi, lo


def _pad_axis(x, axis, size):
    if x.shape[axis] == size:
        return x
    pad = [(0, 0)] * x.ndim
    pad[axis] = (0, size - x.shape[axis])
    return jnp.pad(x, pad)


def _key_tile(l_valid):
    lpad = -(-l_valid // 256) * 256
    for tk in (512, 256):
        if lpad % tk == 0:
            return lpad, tk
    return lpad, 256


def _prep_weights(W, d_model):
    f = {}
    wd = A_WIDTH
    n_even = W["ev_w_in"].shape[0]
    n_odd = W["od_w_in"].shape[0]
    f["even"] = []
    for e in range(n_even):
        wi = W["ev_w_in"][e]
        w_in = jnp.concatenate([wi[:, :3 * wd], wi[:, A_COLS:], wi[:, 3 * wd:A_COLS],
                                jnp.zeros((d_model, LORA_PAD - A_LORA), F32)], axis=1).astype(BF16)
        mu = W["rwkv_mu"][e]
        w2 = jnp.zeros((LANES, wd), F32).at[:A_DECAY_LORA].set(W["rwkv_w2"][e])
        a2 = jnp.zeros((LANES, wd), F32).at[A_DECAY_LORA:A_DECAY_LORA + A_ICLR_LORA].set(W["rwkv_a2"][e])
        g2 = jnp.zeros((LORA_PAD - LANES, wd), F32).at[:A_GATE_LORA].set(W["rwkv_g2"][e])
        hm = lambda v: v.reshape(A_HEADS, 1, A_HEAD_DIM)
        rw = dict(mu_r=mu[:3 * wd].reshape(1, -1),
                  mu_l=_pad_axis(mu[3 * wd:], 0, LORA_PAD).reshape(1, -1),
                  w0=W["rwkv_w0"][e].reshape(1, -1), w2=w2.astype(BF16),
                  a0=W["rwkv_a0"][e].reshape(1, -1), a2=a2.astype(BF16), g2=g2.astype(BF16),
                  k_k=W["rwkv_k_k"][e].reshape(1, -1), k_a=W["rwkv_k_a"][e].reshape(1, -1),
                  r_k=hm(W["rwkv_r_k"][e]), ln_w=hm(W["rwkv_ln_w"][e]), ln_b=hm(W["rwkv_ln_b"][e]))
        eye = jnp.eye(B_BLOCKS, dtype=F32)
        blockdiag = lambda w: (eye[:, None, :, None] * w[:, :, None, :]).reshape(B_WIDTH, B_WIDTH)
        wgate = jnp.concatenate([blockdiag(W["lru_wa"][e]), blockdiag(W["lru_wx"][e])], axis=1).astype(BF16)
        bgate = jnp.concatenate([W["lru_ba"][e], W["lru_bx"][e]])
        wo = W["ev_w_out"][e].astype(BF16)
        f["even"].append(dict(w_in=w_in, rw=rw, wgate=wgate, bgate=bgate, wo_a=wo[:wd], wo_b=wo[wd:],
                              cw=W["lru_conv_w"][e], cb=W["lru_conv_b"][e], lam=W["lru_lambda"][e]))
    f["odd"] = []
    cw_ = C_WIDTH
    qi_w = C_IDX_HEADS * C_IDX_DIM
    for o in range(n_odd):
        wi = W["od_w_in"][o]
        offs = [0]
        for s in (cw_, cw_, cw_, qi_w, C_IDX_DIM, C_IDX_HEADS, D_Q_RANK, D_KV_RANK, D_ROPE):
            offs.append(offs[-1] + s)
        q, k, v, qi, ki, wi_, qd, kvd, kr = [wi[:, offs[i]:offs[i + 1]] for i in range(9)]
        small = jnp.concatenate([ki, kr, wi_], axis=1)
        w_in = jnp.concatenate([q, k, v, qi, qd, kvd, _pad_axis(small, 1, 512)], axis=1).astype(BF16)
        wuq = W["mla_w_uq"][o].reshape(D_Q_RANK, D_HEADS, D_NOPE + D_ROPE)
        wuq = jnp.concatenate([wuq[:, :, :D_NOPE].reshape(D_Q_RANK, -1),
                               wuq[:, :, D_NOPE:].reshape(D_Q_RANK, -1)], axis=1).astype(BF16)
        wukv = W["mla_w_ukv"][o].reshape(D_KV_RANK, D_HEADS, D_NOPE + D_V)
        wuk = _pad_axis(wukv[:, :, :D_NOPE], 2, MLA_DK).reshape(D_KV_RANK, -1).astype(BF16)
        wuv = wukv[:, :, D_NOPE:].reshape(D_KV_RANK, -1).astype(BF16)
        wo = W["od_w_out"][o].astype(BF16)
        f["odd"].append(dict(w_in=w_in, wuq=wuq, wuk=wuk, wuv=wuv, wo_c=wo[:cw_], wo_d=wo[cw_:],
                             gq=W["mla_q_norm"][o], gkv=W["mla_kv_norm"][o]))
    f["ffn"] = [dict(wg=W["ffn_w_gate"][l].astype(BF16), wu=W["ffn_w_up"][l].astype(BF16),
                     wd=W["ffn_w_down"][l].astype(BF16)) for l in range(W["ffn_w_gate"].shape[0])]
    f["tabs"] = _bias_tables(W["rel_bias"])
    nb = N_BUCKETS // 2
    f["far"] = _pad_axis(W["rel_bias"][nb - 1:nb, :], 1, LANES)
    return f


def _mixer_even(x2, b, t, p0, shift, s0, h0, cbuf, fe, norm_g):
    wd = A_WIDTH
    p = _norm_matmul(x2, norm_g, fe["w_in"])
    p3 = p.reshape(b, t, -1)
    lora_col = 3 * wd + 2 * B_WIDTH
    shift_r = shift[:, None, :3 * wd]
    shift_l = _pad_axis(shift[:, None, 3 * wd:], 2, LORA_PAD)
    ya, s_new = _rwkv(p3, lora_col // LORA_PAD, shift_r, shift_l, s0, fe["rw"])
    yb, h_new, c_new = _rglru(p3, 3, 4, fe["cw"], fe["cb"], fe["wgate"], fe["bgate"], fe["lam"],
                              h0, cbuf, p0=p0)
    x2 = _matmul([ya.reshape(b * t, wd), yb.reshape(b * t, B_WIDTH)], [fe["wo_a"], fe["wo_b"]], res=x2)
    last = p3[:, t - 1]
    new_shift = jnp.concatenate([last[:, :3 * wd], last[:, lora_col:lora_col + A_LORA]], axis=-1)
    return x2, new_shift, s_new, h_new[:, 0], c_new


def _mixer_odd(x2, b, t, p0, ck, cv, cik, clat, ckr, fo, f, norm_g, cos, sin):
    cw_ = C_WIDTH
    p = _norm_matmul(x2, norm_g, fo["w_in"])
    p3 = p.reshape(b, t, -1)
    past = ck.shape[1]
    l_valid = past + t
    lpad, tk = _key_tile(l_valid)
    tpad = -(-t // LANES) * LANES
    q = p3[..., :cw_]
    k_new = p3[..., cw_:2 * cw_]
    v_new = p3[..., 2 * cw_:3 * cw_]
    qi = p3[..., 3 * cw_:3 * cw_ + 512]
    small = p3[..., 3 * cw_ + 1536:]
    ki_new = small[..., :C_IDX_DIM]
    wi = small[..., 2 * C_IDX_DIM:2 * C_IDX_DIM + C_IDX_HEADS]

    def keys(cache, new):
        allk = jnp.concatenate([cache.reshape(b, past, -1), new], axis=1) if past else new
        return _pad_axis(allk, 1, lpad)

    tq_ = lambda z: _pad_axis(jnp.swapaxes(z, 1, 2), 2, tpad)
    k_all = keys(ck, k_new).astype(BF16)
    vT_all = jnp.swapaxes(keys(cv, v_new), 1, 2).astype(BF16)
    kih, kil = _split_hi_lo(keys(cik, ki_new))
    ki3 = jnp.concatenate([kih, kih, kil], axis=-1)
    qih, qil = _split_hi_lo(tq_(qi).reshape(b, C_IDX_HEADS, C_IDX_DIM, tpad))
    q3 = jnp.concatenate([qih, qil, qih], axis=2).reshape(b, 3 * C_IDX_HEADS * C_IDX_DIM, tpad)
    yc = _attention("dsa", tq_(q).astype(BF16), k_all, vT_all,
                    (q3, ki3, tq_(wi), f["tabs"], f["far"]),
                    p0=p0, t_valid=t, l_valid=l_valid, tk=tk)[:, :t]
    qn, qr, lat, krope = _mla_prep(p3, 7, 8, 9, fo["gq"], fo["gkv"], fo["wuq"], cos, sin)
    lat_all = keys(clat, lat).reshape(b * lpad, -1)
    kr_all = keys(ckr, krope).reshape(b * lpad, -1)
    eye = jnp.eye(D_ROPE, dtype=BF16)
    ident = jnp.tile(jnp.pad(eye, ((0, 0), (D_NOPE, MLA_DK - D_NOPE - D_ROPE))), (1, D_HEADS))
    k_full = _matmul([lat_all, kr_all], [fo["wuk"], ident], out_dtype=BF16).reshape(b, lpad, -1)
    v_all = _matmul([lat_all], [fo["wuv"]], out_dtype=BF16).reshape(b, lpad, -1)
    q_full = jnp.concatenate([qn.reshape(b, t, D_HEADS, D_NOPE), qr.reshape(b, t, D_HEADS, D_ROPE),
                              jnp.zeros((b, t, D_HEADS, MLA_DK - D_NOPE - D_ROPE), BF16)], axis=-1)
    yd = _attention("mla", tq_(q_full.reshape(b, t, -1)), k_full, jnp.swapaxes(v_all, 1, 2), (),
                    p0=p0, t_valid=t, l_valid=l_valid, tk=tk)[:, :t]
    x2 = _matmul([yc.reshape(b * t, cw_), yd.reshape(b * t, -1)], [fo["wo_c"], fo["wo_d"]], res=x2)
    return (x2, k_new.reshape(b, t, C_HEADS, C_HEAD_DIM), v_new.reshape(b, t, C_HEADS, C_HEAD_DIM),
            ki_new, lat, krope)


def _trunk(x, p0, shift, rwkv_s, lru_h, lru_conv, dsa_k, dsa_v, dsa_ik, mla_lat, mla_kr, W, f):
    b, t, d = x.shape
    depth = W["norm_mix"].shape[0]
    pos = (p0 + jnp.arange(t)).astype(F32)
    inv = ROPE_BASE ** (-jnp.arange(0, D_ROPE, 2, dtype=F32) / D_ROPE)
    ang = pos[:, None] * inv[None, :]
    cos = jnp.tile(jnp.cos(ang), (1, 2 * D_HEADS))
    sin = jnp.tile(jnp.sin(ang), (1, 2 * D_HEADS))
    x2 = x.reshape(b * t, d)
    ev = [[] for _ in range(4)]
    od = [[] for _ in range(5)]
    for layer in range(depth):
        if layer % 2 == 0:
            e = layer // 2
            x2, *outs = _mixer_even(x2, b, t, p0, shift[e], rwkv_s[e], lru_h[e], lru_conv[e],
                                    f["even"][e], W["norm_mix"][layer])
            for lst, o_ in zip(ev, outs):
                lst.append(o_)
        else:
            o = layer // 2
            x2, *outs = _mixer_odd(x2, b, t, p0, dsa_k[o], dsa_v[o], dsa_ik[o], mla_lat[o], mla_kr[o],
                                   f["odd"][o], f, W["norm_mix"][layer], cos, sin)
            for lst, o_ in zip(od, outs):
                lst.append(o_)
        ff = f["ffn"][layer]
        x2 = _ffn(x2, W["norm_ffn"][layer], ff["wg"], ff["wu"], ff["wd"], W["final_norm"],
                  final_norm=layer == depth - 1)
    return (x2.reshape(b, t, d),) + tuple(jnp.stack(v) for v in ev) + tuple(jnp.stack(v) for v in od)


def kernel(x_prompt, x_sample, state_rwkv_shift, state_rwkv, state_lru, state_lru_conv, cache_dsa_k, cache_dsa_v, cache_dsa_idx_k, cache_mla_latent, cache_mla_krope, rel_bias, final_norm, norm_mix, norm_ffn, ffn_w_gate, ffn_w_up, ffn_w_down, ev_w_in, ev_w_out, rwkv_mu, rwkv_w0, rwkv_w2, rwkv_a0, rwkv_a2, rwkv_g2, rwkv_k_k, rwkv_k_a, rwkv_r_k, rwkv_ln_w, rwkv_ln_b, lru_conv_w, lru_conv_b, lru_wa, lru_ba, lru_wx, lru_bx, lru_lambda, od_w_in, od_w_out, mla_q_norm, mla_w_uq, mla_kv_norm, mla_w_ukv):
    W = dict(rel_bias=rel_bias, final_norm=final_norm, norm_mix=norm_mix, norm_ffn=norm_ffn,
             ffn_w_gate=ffn_w_gate, ffn_w_up=ffn_w_up, ffn_w_down=ffn_w_down,
             ev_w_in=ev_w_in, ev_w_out=ev_w_out, rwkv_mu=rwkv_mu, rwkv_w0=rwkv_w0,
             rwkv_w2=rwkv_w2, rwkv_a0=rwkv_a0, rwkv_a2=rwkv_a2, rwkv_g2=rwkv_g2,
             rwkv_k_k=rwkv_k_k, rwkv_k_a=rwkv_k_a, rwkv_r_k=rwkv_r_k, rwkv_ln_w=rwkv_ln_w,
             rwkv_ln_b=rwkv_ln_b, lru_conv_w=lru_conv_w, lru_conv_b=lru_conv_b, lru_wa=lru_wa,
             lru_ba=lru_ba, lru_wx=lru_wx, lru_bx=lru_bx, lru_lambda=lru_lambda,
             od_w_in=od_w_in, od_w_out=od_w_out, mla_q_norm=mla_q_norm, mla_w_uq=mla_w_uq,
             mla_kv_norm=mla_kv_norm, mla_w_ukv=mla_w_ukv)
    d_model = x_prompt.shape[-1]
    f = _prep_weights(W, d_model)
    bp = x_prompt.shape[0]
    n_even, n_odd = ev_w_in.shape[0], od_w_in.shape[0]
    dt = x_prompt.dtype
    z = lambda *shape: jnp.zeros(shape, dt)
    outs_p = _trunk(
        x_prompt, 0,
        z(n_even, bp, A_COLS), z(n_even, bp, A_HEADS, A_HEAD_DIM, A_HEAD_DIM),
        z(n_even, bp, B_WIDTH), z(n_even, bp, B_CONV - 1, B_WIDTH),
        z(n_odd, bp, 0, C_HEADS, C_HEAD_DIM), z(n_odd, bp, 0, C_HEADS, C_HEAD_DIM),
        z(n_odd, bp, 0, C_IDX_DIM), z(n_odd, bp, 0, D_KV_RANK), z(n_odd, bp, 0, D_ROPE), W, f)
    past = cache_dsa_k.shape[2]
    outs_s = _trunk(x_sample, past, state_rwkv_shift, state_rwkv, state_lru, state_lru_conv,
                    cache_dsa_k, cache_dsa_v, cache_dsa_idx_k, cache_mla_latent, cache_mla_krope, W, f)
    return (outs_p[0], outs_s[0]) + tuple(outs_p[1:]) + tuple(outs_s[1:])
```

```python
import functools
import math

import jax
import jax.numpy as jnp
from jax import lax
from jax.experimental import pallas as pl
from jax.experimental.pallas import tpu as pltpu

F32 = jnp.float32
BF16 = jnp.bfloat16
I32 = jnp.int32

CHUNK = 64
NORM_EPS = 1e-6
A_HEADS = 16
A_HEAD_DIM = 64
A_WIDTH = A_HEADS * A_HEAD_DIM
A_DECAY_LORA = 64
A_ICLR_LORA = 64
A_GATE_LORA = 160
A_LORA = A_DECAY_LORA + A_ICLR_LORA + A_GATE_LORA
A_COLS = 3 * A_WIDTH + A_LORA
A_LN_EPS = 64e-5
B_WIDTH = 1024
B_BLOCKS = 16
B_CONV = 4
B_C = 8.0
C_HEADS = 8
C_HEAD_DIM = 128
C_WIDTH = C_HEADS * C_HEAD_DIM
C_IDX_HEADS = 8
C_IDX_DIM = 64
TOPK_MAX = 256
D_HEADS = 8
D_NOPE = 128
D_ROPE = 64
D_V = 128
D_Q_RANK = 512
D_KV_RANK = 512
ROPE_BASE = 10000.0
N_BUCKETS = 32
MAX_DISTANCE = 128

LANES = 128
SUBLANES = 8
VMEM_LIMIT = 56 * 1024 * 1024
LORA_PAD = 512
NEG_BIG = -1e30
INT_MIN = -2147483648
RC = 64
LOG2E = 1.4426950408889634
MLA_DK = 256

NN = (((1,), (0,)), ((), ()))
NT = (((1,), (1,)), ((), ()))
BNN = (((2,), (1,)), ((0,), (0,)))
BNT = (((2,), (2,)), ((0,), (0,)))
BTN = (((1,), (1,)), ((0,), (0,)))


def _cparams(sem):
    return pltpu.CompilerParams(dimension_semantics=sem, vmem_limit_bytes=VMEM_LIMIT)


def _rms(x, g, eps=NORM_EPS):
    ms = jnp.mean(x * x, axis=-1, keepdims=True)
    return x * lax.rsqrt(ms + eps) * g


def _softplus(x):
    return jnp.maximum(x, 0.0) + jnp.log1p(jnp.exp(-jnp.abs(x)))


def _dotp(a, b, dims, passes):
    if passes == 6:
        return lax.dot_general(a, b, dims, precision=lax.Precision.HIGHEST, preferred_element_type=F32)
    ah = a.astype(BF16)
    bh = b.astype(BF16)
    out = lax.dot_general(ah, bh, dims, preferred_element_type=F32)
    if passes == 3:
        al = (a - ah.astype(F32)).astype(BF16)
        bl = (b - bh.astype(F32)).astype(BF16)
        out = out + lax.dot_general(ah, bl, dims, preferred_element_type=F32)
        out = out + lax.dot_general(al, bh, dims, preferred_element_type=F32)
    return out


def _norm_matmul_body(x_ref, g_ref, w_ref, o_ref, xn_ref):
    @pl.when(pl.program_id(1) == 0)
    def _():
        xn_ref[...] = _rms(x_ref[...], g_ref[...]).astype(BF16)

    o_ref[...] = jnp.dot(xn_ref[...], w_ref[...], preferred_element_type=F32)


def _row_tile(m, cap):
    tm = cap
    while m % tm:
        tm //= 2
    return tm


def _norm_matmul(x, g, w, *, tn=512):
    m, k = x.shape
    n = w.shape[1]
    tm = _row_tile(m, 1024)
    return pl.pallas_call(
        _norm_matmul_body,
        grid=(m // tm, n // tn),
        in_specs=[pl.BlockSpec((tm, k), lambda i, j: (i, 0)),
                  pl.BlockSpec((1, k), lambda i, j: (0, 0)),
                  pl.BlockSpec((k, tn), lambda i, j: (0, j))],
        out_specs=pl.BlockSpec((tm, tn), lambda i, j: (i, j)),
        out_shape=jax.ShapeDtypeStruct((m, n), F32),
        scratch_shapes=[pltpu.VMEM((tm, k), BF16)],
        compiler_params=_cparams(("parallel", "arbitrary")),
        name="norm_matmul",
    )(x, g.reshape(1, k), w)


def _mm_body(*refs, n_lhs, has_res):
    o_ref = refs[-1]
    acc = refs[2 * n_lhs][...] if has_res else None
    for a_ref, w_ref in zip(refs[:n_lhs], refs[n_lhs:2 * n_lhs]):
        d = jnp.dot(a_ref[...].astype(BF16), w_ref[...], preferred_element_type=F32)
        acc = d if acc is None else acc + d
    o_ref[...] = acc.astype(o_ref.dtype)


def _matmul(lhs_list, w_list, res=None, *, tn=512, out_dtype=F32):
    m = lhs_list[0].shape[0]
    n = w_list[0].shape[1]
    tm = _row_tile(m, 1024)
    in_specs = [pl.BlockSpec((tm, a.shape[1]), lambda i, j: (i, 0)) for a in lhs_list]
    in_specs += [pl.BlockSpec((w.shape[0], tn), lambda i, j: (0, j)) for w in w_list]
    args = list(lhs_list) + list(w_list)
    if res is not None:
        in_specs.append(pl.BlockSpec((tm, tn), lambda i, j: (i, j)))
        args.append(res)
    return pl.pallas_call(
        functools.partial(_mm_body, n_lhs=len(lhs_list), has_res=res is not None),
        grid=(m // tm, n // tn),
        in_specs=in_specs,
        out_specs=pl.BlockSpec((tm, tn), lambda i, j: (i, j)),
        out_shape=jax.ShapeDtypeStruct((m, n), out_dtype),
        compiler_params=_cparams(("parallel", "arbitrary")),
        name="matmul",
    )(*args)


def _ffn_body(x_ref, g_ref, wg_ref, wu_ref, wd_ref, gf_ref, o_ref, xn_ref, acc_ref, *, final_norm):
    f = pl.program_id(1)

    @pl.when(f == 0)
    def _():
        xn_ref[...] = _rms(x_ref[...], g_ref[...]).astype(BF16)
        acc_ref[...] = jnp.zeros_like(acc_ref)

    xn = xn_ref[...]
    hg = jnp.dot(xn, wg_ref[...], preferred_element_type=F32)
    hu = jnp.dot(xn, wu_ref[...], preferred_element_type=F32)
    h = hg * jax.nn.sigmoid(hg) * hu
    acc_ref[...] += jnp.dot(h.astype(BF16), wd_ref[...], preferred_element_type=F32)

    @pl.when(f == pl.num_programs(1) - 1)
    def _():
        y = x_ref[...] + acc_ref[...]
        if final_norm:
            y = _rms(y, gf_ref[...])
        o_ref[...] = y


def _ffn(x, g, wg, wu, wd, gf, *, final_norm, tf=512):
    m, k = x.shape
    dff = wg.shape[1]
    tm = min(512, m)
    return pl.pallas_call(
        functools.partial(_ffn_body, final_norm=final_norm),
        grid=(m // tm, dff // tf),
        in_specs=[pl.BlockSpec((tm, k), lambda i, f: (i, 0)),
                  pl.BlockSpec((1, k), lambda i, f: (0, 0)),
                  pl.BlockSpec((k, tf), lambda i, f: (0, f)),
                  pl.BlockSpec((k, tf), lambda i, f: (0, f)),
                  pl.BlockSpec((tf, k), lambda i, f: (f, 0)),
                  pl.BlockSpec((1, k), lambda i, f: (0, 0))],
        out_specs=pl.BlockSpec((tm, k), lambda i, f: (i, 0)),
        out_shape=jax.ShapeDtypeStruct((m, k), F32),
        scratch_shapes=[pltpu.VMEM((tm, k), BF16), pltpu.VMEM((tm, k), F32)],
        compiler_params=_cparams(("parallel", "arbitrary")),
        name="ffn",
    )(x, g.reshape(1, k), wg, wu, wd, gf.reshape(1, k))


def _lru_body(gate_ref, xb_ref, cw_ref, cb_ref, wg_ref, bg_ref, lam_ref, h0_ref, cbuf_ref,
              y_ref, hout_ref, cout_ref, xbuf, hcar, *, tt, p0):
    t = pl.program_id(1)
    w = B_WIDTH
    halo = SUBLANES

    @pl.when(t == 0)
    def _():
        xbuf[0:halo, :] = jnp.zeros((halo, w), F32)
        xbuf[halo - (B_CONV - 1):halo, :] = cbuf_ref[0]
        hcar[...] = h0_ref[0]

    xb = xb_ref[0]
    xbuf[halo:halo + tt, :] = xb
    xc = cb_ref[...] + cw_ref[B_CONV - 1:B_CONV, :] * xb
    for j in range(B_CONV - 1):
        off = halo - (B_CONV - 1) + j
        xc = xc + cw_ref[j:j + 1, :] * xbuf[off:off + tt, :]
    tail = xbuf[tt + halo - (B_CONV - 1):tt + halo, :]
    xbuf[halo - (B_CONV - 1):halo, :] = tail

    pre = jnp.dot(xc.astype(BF16), wg_ref[...], preferred_element_type=F32) + bg_ref[...]
    rg = jax.nn.sigmoid(pre[:, :w])
    ig = jax.nn.sigmoid(pre[:, w:])
    log_a = (-B_C) * rg * _softplus(-lam_ref[...])
    a = jnp.exp(log_a)
    row = lax.broadcasted_iota(I32, (tt, w), 0)
    th = jnp.tanh(log_a)
    mult = jnp.sqrt(-2.0 * th / (1.0 - th))
    mult = jnp.where(row + (p0 + t * tt) == 0, 1.0, mult)
    u = mult * (ig * xc)

    d = 1
    while d < tt:
        keep = row >= d
        a_sh = pltpu.roll(a, d, 0)
        u_sh = pltpu.roll(u, d, 0)
        u = u + jnp.where(keep, a * u_sh, 0.0)
        a = jnp.where(keep, a * a_sh, a)
        d *= 2
    h = u + a * hcar[...]
    hcar[...] = h[tt - 1:tt, :]
    y_ref[0] = (h * jax.nn.gelu(gate_ref[0])).astype(y_ref.dtype)

    @pl.when(t == pl.num_programs(1) - 1)
    def _():
        hout_ref[0] = h[tt - 1:tt, :]
        cout_ref[0] = tail


def _rglru(p3, gate_blk, xb_blk, cw, cb, wgate, bgate, lam, h0, cbuf, *, p0):
    b, t, _ = p3.shape
    w = B_WIDTH
    tt = min(256, t)
    row = lambda v: v.reshape(1, -1)
    return pl.pallas_call(
        functools.partial(_lru_body, tt=tt, p0=p0),
        grid=(b, t // tt),
        in_specs=[pl.BlockSpec((1, tt, w), lambda i, j: (i, j, gate_blk)),
                  pl.BlockSpec((1, tt, w), lambda i, j: (i, j, xb_blk)),
                  pl.BlockSpec((B_CONV, w), lambda i, j: (0, 0)),
                  pl.BlockSpec((1, w), lambda i, j: (0, 0)),
                  pl.BlockSpec((w, 2 * w), lambda i, j: (0, 0)),
                  pl.BlockSpec((1, 2 * w), lambda i, j: (0, 0)),
                  pl.BlockSpec((1, w), lambda i, j: (0, 0)),
                  pl.BlockSpec((1, 1, w), lambda i, j: (i, 0, 0)),
                  pl.BlockSpec((1, B_CONV - 1, w), lambda i, j: (i, 0, 0))],
        out_specs=[pl.BlockSpec((1, tt, w), lambda i, j: (i, j, 0)),
                   pl.BlockSpec((1, 1, w), lambda i, j: (i, 0, 0)),
                   pl.BlockSpec((1, B_CONV - 1, w), lambda i, j: (i, 0, 0))],
        out_shape=[jax.ShapeDtypeStruct((b, t, w), BF16),
                   jax.ShapeDtypeStruct((b, 1, w), F32),
                   jax.ShapeDtypeStruct((b, B_CONV - 1, w), F32)],
        scratch_shapes=[pltpu.VMEM((tt + SUBLANES, w), F32), pltpu.VMEM((1, w), F32)],
        compiler_params=_cparams(("parallel", "arbitrary")),
        name="rglru",
    )(p3, p3, cw, row(cb), wgate, row(bgate), row(lam), h0.reshape(b, 1, w), cbuf)


RWKV_PASSES = 1
RWKV_PASSES_SOLVE = 3


def _rwkv_body(rkv_ref, lora_ref, shr_ref, shl_ref, s0_ref, mur_ref, mul_ref, w0_ref, w2_ref, a0_ref,
               a2_ref, g2_ref, kk_ref, ka_ref, rk_ref, lnw_ref, lnb_ref, tri_ref,
               y_ref, sout_ref, buf_r, buf_l, s_scr, st_r, st_k, st_v, st_q, st_a, st_l, st_d, *, c):
    ci = pl.program_id(1)
    halo = SUBLANES
    hn, n, wd = A_HEADS, A_HEAD_DIM, A_WIDTH
    mm = functools.partial(_dotp, passes=RWKV_PASSES)

    @pl.when(ci == 0)
    def _():
        buf_r[halo - 1:halo, :] = shr_ref[0]
        buf_l[halo - 1:halo, :] = shl_ref[0]
        s_scr[...] = s0_ref[0]

    cur_r = rkv_ref[0]
    cur_l = lora_ref[0]
    buf_r[halo:halo + c, :] = cur_r
    buf_l[halo:halo + c, :] = cur_l
    xm = cur_r + mur_ref[...] * (buf_r[halo - 1:halo - 1 + c, :] - cur_r)
    lo = cur_l + mul_ref[...] * (buf_l[halo - 1:halo - 1 + c, :] - cur_l)
    buf_r[halo - 1:halo, :] = cur_r[c - 1:c, :]
    buf_l[halo - 1:halo, :] = cur_l[c - 1:c, :]

    r = xm[:, :wd]
    k = xm[:, wd:2 * wd]
    v = xm[:, 2 * wd:]
    lo_a = lo[:, :LANES]
    w_pre = w0_ref[...] + jnp.dot(jnp.tanh(lo_a).astype(BF16), w2_ref[...], preferred_element_type=F32)
    w_log = -_softplus(-w_pre) - 0.5
    ld = -jnp.exp(w_log)
    a = jax.nn.sigmoid(a0_ref[...] + jnp.dot(lo_a.astype(BF16), a2_ref[...], preferred_element_type=F32))
    g = jnp.dot(jax.nn.sigmoid(lo[:, LANES:]).astype(BF16), g2_ref[...], preferred_element_type=F32)
    kq = k * kk_ref[...]
    k2 = k * (1.0 + (a - 1.0) * ka_ref[...])
    lc = lax.dot_general(tri_ref[...], ld, NN, precision=lax.Precision.HIGHEST,
                         preferred_element_type=F32)

    for h in range(hn):
        sl = slice(h * n, (h + 1) * n)
        st_r[h] = r[:, sl]
        st_k[h] = k2[:, sl]
        st_v[h] = v[:, sl]
        st_q[h] = kq[:, sl]
        st_a[h] = a[:, sl]
        st_l[h] = lc[:, sl]
        st_d[h] = ld[:, sl]

    rh_, k2h, vh, kqh, ah, lch, ldh = (st_r[...], st_k[...], st_v[...], st_q[...], st_a[...],
                                       st_l[...], st_d[...])
    nrm = jnp.sqrt(jnp.sum(kqh * kqh, axis=-1, keepdims=True))
    kk = kqh / jnp.maximum(nrm, 1e-12)
    kka = kk * ah
    e_neg = jnp.exp(-lch)
    am = jnp.exp(lch - ldh) * kk
    bm = kka * e_neg
    kh = k2h * e_neg
    rh = rh_ * jnp.exp(lch)
    l_end = lch[:, c - 1:c, :]
    e_c = jnp.exp(l_end - lch)
    bp = kka * e_c
    kp = k2h * e_c
    w_end = jnp.exp(l_end)

    x2 = jnp.concatenate([am, rh], axis=1)
    zb = _dotp(x2, bm, BNT, RWKV_PASSES_SOLVE)
    zk = mm(x2, kh, BNT)
    ti = lax.broadcasted_iota(I32, (hn, c, c), 1)
    si = lax.broadcasted_iota(I32, (hn, c, c), 2)
    strict = si < ti
    incl = si <= ti
    m1 = jnp.where(strict, zb[:, :c], 0.0)
    m4 = jnp.where(incl, zb[:, c:], 0.0)
    m2 = jnp.where(strict, zk[:, :c], 0.0)
    m3 = jnp.where(incl, zk[:, c:], 0.0)

    tm = jnp.where(si == ti, 1.0, 0.0) - m1
    npow = mm(m1, m1, BNN)
    span = 2
    while span < c:
        tm = tm + mm(tm, npow, BNN)
        span *= 2
        if span < c:
            npow = mm(npow, npow, BNN)

    s0 = s_scr[...]
    rhs = mm(am, s0, BNT) + mm(m2, vh, BNN)
    p = mm(tm, rhs, BNN)
    y = mm(rh, s0, BNT) + mm(m3, vh, BNN) - mm(m4, p, BNN)
    s_new = s0 * w_end + mm(vh, kp, BTN) - mm(p, bp, BTN)
    s_scr[...] = s_new

    mean = jnp.mean(y, axis=-1, keepdims=True)
    yc = y - mean
    var = jnp.mean(yc * yc, axis=-1, keepdims=True)
    yn = yc * lax.rsqrt(var + A_LN_EPS) * lnw_ref[...] + lnb_ref[...]
    bonus = jnp.sum(rh_ * k2h * rk_ref[...], axis=-1, keepdims=True) * vh
    yo = yn + bonus
    yo = jnp.concatenate([yo[h] for h in range(hn)], axis=-1)
    y_ref[0] = (yo * g).astype(y_ref.dtype)

    @pl.when(ci == pl.num_programs(1) - 1)
    def _():
        sout_ref[0] = s_new


def _rwkv(p3, lora_blk, shift_r, shift_l, s0, wts):
    b, t, _ = p3.shape
    c = min(CHUNK, t)
    hn, n, wd = A_HEADS, A_HEAD_DIM, A_WIDTH
    full = lambda shape: pl.BlockSpec(shape, lambda i, j: (0,) * len(shape))
    tri = (jnp.arange(c)[:, None] >= jnp.arange(c)[None, :]).astype(F32)
    st = pltpu.VMEM((hn, c, n), F32)
    return pl.pallas_call(
        functools.partial(_rwkv_body, c=c),
        grid=(b, t // c),
        in_specs=[pl.BlockSpec((1, c, 3 * wd), lambda i, j: (i, j, 0)),
                  pl.BlockSpec((1, c, LORA_PAD), lambda i, j: (i, j, lora_blk)),
                  pl.BlockSpec((1, 1, 3 * wd), lambda i, j: (i, 0, 0)),
                  pl.BlockSpec((1, 1, LORA_PAD), lambda i, j: (i, 0, 0)),
                  pl.BlockSpec((1, hn, n, n), lambda i, j: (i, 0, 0, 0)),
                  full((1, 3 * wd)), full((1, LORA_PAD)), full((1, wd)), full((LANES, wd)),
                  full((1, wd)), full((LANES, wd)), full((LORA_PAD - LANES, wd)),
                  full((1, wd)), full((1, wd)), full((hn, 1, n)), full((hn, 1, n)), full((hn, 1, n)),
                  full((c, c))],
        out_specs=[pl.BlockSpec((1, c, wd), lambda i, j: (i, j, 0)),
                   pl.BlockSpec((1, hn, n, n), lambda i, j: (i, 0, 0, 0))],
        out_shape=[jax.ShapeDtypeStruct((b, t, wd), BF16),
                   jax.ShapeDtypeStruct((b, hn, n, n), F32)],
        scratch_shapes=[pltpu.VMEM((c + SUBLANES, 3 * wd), F32), pltpu.VMEM((c + SUBLANES, LORA_PAD), F32),
                        pltpu.VMEM((hn, n, n), F32), st, st, st, st, st, st, st],
        compiler_params=_cparams(("parallel", "arbitrary")),
        name="rwkv7",
    )(p3, p3, shift_r, shift_l, s0, wts["mu_r"], wts["mu_l"], wts["w0"], wts["w2"], wts["a0"], wts["a2"],
      wts["g2"], wts["k_k"], wts["k_a"], wts["r_k"], wts["ln_w"], wts["ln_b"], tri)


def _rope(x, cos, sin):
    wdt = x.shape[-1]
    lane = lax.broadcasted_iota(I32, x.shape, 1)
    first = (lane % D_ROPE) < (D_ROPE // 2)
    rot = jnp.where(first, -pltpu.roll(x, wdt - D_ROPE // 2, 1), pltpu.roll(x, D_ROPE // 2, 1))
    return x * cos + rot * sin


def _mla_prep_body(qd_ref, kvd_ref, sm_ref, gq_ref, gkv_ref, wuq_ref, cos_ref, sin_ref,
                   qn_ref, qr_ref, lat_ref, kr_ref):
    nope_w = D_HEADS * D_NOPE
    qdn = _rms(qd_ref[0], gq_ref[...]).astype(BF16)
    qf = jnp.dot(qdn, wuq_ref[...], preferred_element_type=F32) * ((D_NOPE + D_ROPE) ** -0.5 * LOG2E)
    qn_ref[0] = qf[:, :nope_w].astype(qn_ref.dtype)
    cos = cos_ref[...]
    sin = sin_ref[...]
    qr_ref[0] = _rope(qf[:, nope_w:], cos, sin).astype(qr_ref.dtype)
    lat_ref[0] = _rms(kvd_ref[0], gkv_ref[...])
    sm = sm_ref[0][:, :LANES]
    kr = _rope(sm, cos[:, :LANES], sin[:, :LANES])
    kr_ref[0] = kr[:, C_IDX_DIM:C_IDX_DIM + D_ROPE]


def _mla_prep(p3, qd_blk, kvd_blk, sm_blk, gq, gkv, wuq, cos, sin):
    b, t, _ = p3.shape
    tt = min(256, t)
    rw = D_HEADS * D_ROPE
    full = lambda shape: pl.BlockSpec(shape, lambda i, j: (0,) * len(shape))
    return pl.pallas_call(
        _mla_prep_body,
        grid=(b, t // tt),
        in_specs=[pl.BlockSpec((1, tt, D_Q_RANK), lambda i, j: (i, j, qd_blk)),
                  pl.BlockSpec((1, tt, D_KV_RANK), lambda i, j: (i, j, kvd_blk)),
                  pl.BlockSpec((1, tt, 512), lambda i, j: (i, j, sm_blk)),
                  full((1, D_Q_RANK)), full((1, D_KV_RANK)), full(wuq.shape),
                  pl.BlockSpec((tt, rw), lambda i, j: (j, 0)),
                  pl.BlockSpec((tt, rw), lambda i, j: (j, 0))],
        out_specs=[pl.BlockSpec((1, tt, D_HEADS * D_NOPE), lambda i, j: (i, j, 0)),
                   pl.BlockSpec((1, tt, rw), lambda i, j: (i, j, 0)),
                   pl.BlockSpec((1, tt, D_KV_RANK), lambda i, j: (i, j, 0)),
                   pl.BlockSpec((1, tt, D_ROPE), lambda i, j: (i, j, 0))],
        out_shape=[jax.ShapeDtypeStruct((b, t, D_HEADS * D_NOPE), BF16),
                   jax.ShapeDtypeStruct((b, t, rw), BF16),
                   jax.ShapeDtypeStruct((b, t, D_KV_RANK), F32),
                   jax.ShapeDtypeStruct((b, t, D_ROPE), F32)],
        compiler_params=_cparams(("parallel", "parallel")),
        name="mla_prep",
    )(p3, p3, p3, gq.reshape(1, -1), gkv.reshape(1, -1), wuq, cos, sin)


def _bias_table_body(rb_ref, o_ref):
    d = pl.program_id(0)
    h = pl.program_id(1)
    nb = N_BUCKETS // 2
    max_exact = nb // 2
    s = lax.broadcasted_iota(I32, (LANES, LANES), 0)
    q = lax.broadcasted_iota(I32, (LANES, LANES), 1)
    rel = s - q - d * LANES
    n = jnp.abs(rel)
    big = jnp.maximum(n, max_exact).astype(F32)
    large = max_exact + (jnp.log(big / max_exact) / math.log(MAX_DISTANCE / max_exact)
                         * (nb - max_exact)).astype(I32)
    large = jnp.minimum(large, nb - 1)
    bucket = jnp.where(rel > 0, nb, 0) + jnp.where(n < max_exact, n, large)
    out = jnp.zeros((LANES, LANES), F32)
    for bk in range(N_BUCKETS):
        out = jnp.where(bucket == bk, rb_ref[bk, h], out)
    o_ref[0, 0] = out * LOG2E


def _bias_tables(rel_bias):
    return pl.pallas_call(
        _bias_table_body,
        grid=(2, C_HEADS),
        in_specs=[pl.BlockSpec(memory_space=pltpu.SMEM)],
        out_specs=pl.BlockSpec((1, 1, LANES, LANES), lambda d, h: (d, h, 0, 0)),
        out_shape=jax.ShapeDtypeStruct((2, C_HEADS, LANES, LANES), F32),
        name="bias_tables",
    )(rel_bias)


def _transpose32(x):
    x = list(x)
    for s, msk in ((16, 0x0000FFFF), (8, 0x00FF00FF), (4, 0x0F0F0F0F), (2, 0x33333333), (1, 0x55555555)):
        sh = jnp.full(x[0].shape, s, I32)
        for i in range(32):
            if i & s == 0:
                t = (lax.shift_right_logical(x[i], sh) ^ x[i + s]) & msk
                x[i + s] = x[i + s] ^ t
                x[i] = x[i] ^ lax.shift_left(t, sh)
    return x


def _attn_body(*refs, mode, p0, t_valid, l_valid, tq, tk, dk, topk, nh):
    dsa = mode == "dsa"
    dh = LANES
    if dsa:
        (qT_ref, k_ref, vT_ref, q3_ref, ki3_ref, wiT_ref, tab_ref, far_ref,
         o_ref, m_scr, l_scr, acc_scr, s_scr, p_scr, mb_scr, skey_scr, thr_scr, planes_scr, e_scr) = refs
    else:
        (qT_ref, k_ref, vT_ref, o_ref, m_scr, l_scr, acc_scr, s_scr, p_scr, mb_scr) = refs
    qt = pl.program_id(1)
    kt = pl.program_id(2)
    nk = pl.num_programs(2)
    q_lo = p0 + qt * tq
    q_hi = p0 + jnp.minimum(qt * tq + tq, t_valid) - 1
    n_allowed = jnp.minimum((q_hi // CHUNK + 1) * CHUNK, l_valid)
    last_kt = (n_allowed - 1) // tk
    qpos = q_lo + lax.broadcasted_iota(I32, (1, tq), 1)
    qchunk = qpos // CHUNK

    def allowed_mask(k0, rows):
        kidx = k0 + lax.broadcasted_iota(I32, (rows, tq), 0)
        return (kidx // CHUNK <= qchunk) & (kidx < l_valid)

    @pl.when(kt == 0)
    def _():
        m_scr[...] = jnp.full(m_scr.shape, NEG_BIG, F32)
        l_scr[...] = jnp.zeros(l_scr.shape, F32)
        acc_scr[...] = jnp.zeros(acc_scr.shape, F32)

    if dsa:
        iscale = (C_IDX_HEADS * C_IDX_DIM) ** -0.5

        @pl.when(kt == 0)
        def _():
            wpt = tk // 32
            kw = 3 * C_IDX_DIM

            def score_tile(j, carry):
                k0 = pl.multiple_of(j * tk, tk)
                ki3 = ki3_ref[0, pl.ds(k0, tk), :]
                s = jnp.zeros((tk, tq), F32)
                for h in range(C_IDX_HEADS):
                    d = jnp.dot(ki3, q3_ref[0, h * kw:(h + 1) * kw, :], preferred_element_type=F32)
                    s = s + jnp.maximum(d, 0.0) * wiT_ref[0, h:h + 1, :]
                s = s * iscale + 0.0
                bits = pltpu.bitcast(s, I32)
                key = bits ^ ((bits >> 31) & 0x7FFFFFFF)
                key = jnp.where(allowed_mask(k0, tk), key, INT_MIN)
                skey_scr[pl.ds(k0, tk), :] = key
                ukey = key ^ INT_MIN
                w0 = pl.multiple_of(j * wpt, SUBLANES)
                for g in range(tk // 256):
                    rows = [ukey[g * 256 + 8 * i:g * 256 + 8 * i + 8, :] for i in range(32)]
                    for bi, plane in enumerate(_transpose32(rows)):
                        planes_scr[bi, pl.ds(w0 + g * SUBLANES, SUBLANES), :] = plane
                e_scr[pl.ds(w0, wpt), :] = jnp.full((wpt, tq), -1, I32)
                return carry

            ntile = last_kt + 1
            lax.fori_loop(0, ntile, score_tile, 0)

            def popsum(fn):
                def body(j, acc):
                    w0 = pl.multiple_of(j * wpt, SUBLANES)
                    pc = lax.population_count(fn(w0))
                    return acc + jnp.sum(pc.reshape(wpt // SUBLANES, SUBLANES, tq), axis=0)
                acc = lax.fori_loop(0, ntile, body, jnp.zeros((SUBLANES, tq), I32))
                return jnp.sum(acc, axis=0, keepdims=True)

            def bit_step(i, st):
                c_gt, th = st
                bi = 31 - i
                c1 = popsum(lambda w0: e_scr[pl.ds(w0, wpt), :] & planes_scr[bi, pl.ds(w0, wpt), :])
                take = (c_gt + c1) >= topk

                def upd(j, carry):
                    w0 = pl.multiple_of(j * wpt, SUBLANES)
                    e = e_scr[pl.ds(w0, wpt), :]
                    t = e & planes_scr[bi, pl.ds(w0, wpt), :]
                    e_scr[pl.ds(w0, wpt), :] = jnp.where(take, t, e ^ t)
                    return carry

                lax.fori_loop(0, ntile, upd, 0)
                return (jnp.where(take, c_gt, c_gt + c1),
                        jnp.where(take, th | lax.shift_left(jnp.int32(1), bi), th))

            zero = jnp.zeros((1, tq), I32)
            c_gt, th_u = lax.fori_loop(0, 32, bit_step, (zero, zero))
            need = topk - c_gt
            nbits = max(1, (l_valid - 1).bit_length())

            def idx_step(i, jb):
                cand = jb - lax.shift_left(jnp.int32(1), nbits - 1 - i)

                def hits(w0):
                    wr = w0 + lax.broadcasted_iota(I32, (wpt, tq), 0)
                    base = (wr >> 3) * 256 + (wr & 7)
                    mx = (cand - base) >> 3
                    low = jnp.left_shift(2, jnp.clip(mx, 0, 30)) - 1
                    msk = jnp.where(mx < 0, 0, jnp.where(mx >= 31, -1, low))
                    return e_scr[pl.ds(w0, wpt), :] & msk

                return jnp.where(popsum(hits) >= need, cand, jb)

            jb = lax.fori_loop(0, nbits, idx_step, jnp.full((1, tq), (1 << nbits) - 1, I32))
            thr_scr[0:1, :] = th_u ^ INT_MIN
            thr_scr[1:2, :] = jb

    def tile(near):
        k0 = pl.multiple_of(kt * tk, tk)
        nchunk = tk // RC
        masked = dsa or near
        if masked:
            for c in range(nchunk):
                r0 = k0 + c * RC
                mask = allowed_mask(r0, RC) if near else None
                if dsa:
                    x = skey_scr[pl.ds(r0, RC), :]
                    kidx = r0 + lax.broadcasted_iota(I32, (RC, tq), 0)
                    sel = (x > thr_scr[0:1, :]) | ((x == thr_scr[0:1, :]) & (kidx <= thr_scr[1:2, :]))
                    mask = sel & mask if near else sel
                mb_scr[c * RC:(c + 1) * RC, :] = jnp.where(mask, 0.0, NEG_BIG)

        def chunk_bias(h, c):
            far = far_ref[0:1, h:h + 1]
            if not near:
                return far
            sb, off = (c * RC) // LANES, (c * RC) % LANES
            cols = []
            for qb in range(tq // LANES):
                delta = k0 + sb * LANES - (q_lo + qb * LANES)
                cols.append(jnp.where(delta == 0, tab_ref[0, h, off:off + RC, :],
                                      jnp.where(delta == -LANES, tab_ref[1, h, off:off + RC, :], far)))
            return jnp.concatenate(cols, axis=1) if len(cols) > 1 else cols[0]

        for h in range(nh):
            ks = slice(h * dk, (h + 1) * dk)
            s_scr[h] = jnp.dot(k_ref[0, :, ks], qT_ref[0, ks, :], preferred_element_type=F32)
        m_news, alphas = [], []
        for h in range(nh):
            mx = jnp.full((SUBLANES, tq), NEG_BIG, F32)
            for c in range(nchunk):
                rows = slice(c * RC, (c + 1) * RC)
                blk = s_scr[h, rows, :]
                if masked:
                    blk = blk + mb_scr[rows, :]
                    if dsa:
                        blk = blk + chunk_bias(h, c)
                    s_scr[h, rows, :] = blk
                mx = jnp.maximum(mx, jnp.max(blk.reshape(RC // SUBLANES, SUBLANES, tq), axis=0))
            m_prev = m_scr[h:h + 1, :]
            m_new = jnp.maximum(m_prev, jnp.max(mx, axis=0, keepdims=True))
            m_scr[h:h + 1, :] = m_new
            m_news.append(m_new)
            alphas.append(jnp.exp2(m_prev - m_new))
        for h in range(nh):
            lsum = jnp.zeros((SUBLANES, tq), F32)
            for c in range(nchunk):
                rows = slice(c * RC, (c + 1) * RC)
                pr = jnp.exp2(s_scr[h, rows, :] - m_news[h])
                lsum = lsum + jnp.sum(pr.reshape(RC // SUBLANES, SUBLANES, tq), axis=0)
                p_scr[h, rows, :] = pr.astype(BF16)
            l_scr[h:h + 1, :] = alphas[h] * l_scr[h:h + 1, :] + jnp.sum(lsum, axis=0, keepdims=True)
        for h in range(nh):
            sl = slice(h * dh, (h + 1) * dh)
            pv = jnp.dot(vT_ref[0, sl, :], p_scr[h], preferred_element_type=F32)
            acc_scr[sl, :] = alphas[h] * acc_scr[sl, :] + pv

    is_far = kt * tk + tk - 1 <= q_lo - LANES

    @pl.when((kt <= last_kt) & is_far)
    def _():
        tile(False)

    @pl.when((kt <= last_kt) & jnp.logical_not(is_far))
    def _():
        tile(True)

    @pl.when(kt == nk - 1)
    def _():
        for h in range(nh):
            sl = slice(h * dh, (h + 1) * dh)
            o = acc_scr[sl, :] / l_scr[h:h + 1, :]
            o_ref[0, :, sl] = o.T.astype(o_ref.dtype)


def _attention(mode, qT, k, vT, extra, *, p0, t_valid, l_valid, tk):
    b, hdk, tpad = qT.shape
    lpad = k.shape[1]
    hd = vT.shape[1]
    nh = hd // LANES
    dk = hdk // nh
    tq = 2 * LANES if tpad % (2 * LANES) == 0 else LANES
    nq, nk = tpad // tq, lpad // tk
    topk = min(TOPK_MAX, l_valid // 4)
    assert p0 % LANES == 0 and lpad % tk == 0 and tk % 256 == 0 and l_valid >= topk >= 1 and tk >= topk

    def last_kt(qt):
        q_hi = p0 + jnp.minimum(qt * tq + tq, t_valid) - 1
        return (jnp.minimum((q_hi // CHUNK + 1) * CHUNK, l_valid) - 1) // tk

    kmap = lambda i, q, kk: (i, jnp.minimum(kk, last_kt(q)), 0)
    vmap_ = lambda i, q, kk: (i, 0, jnp.minimum(kk, last_kt(q)))
    qmap = lambda i, q, kk: (i, 0, q)
    in_specs = [pl.BlockSpec((1, hdk, tq), qmap),
                pl.BlockSpec((1, tk, hdk), kmap),
                pl.BlockSpec((1, hd, tk), vmap_)]
    scratch = [pltpu.VMEM((nh, tq), F32), pltpu.VMEM((nh, tq), F32), pltpu.VMEM((hd, tq), F32),
               pltpu.VMEM((nh, tk, tq), F32), pltpu.VMEM((nh, tk, tq), BF16), pltpu.VMEM((tk, tq), F32)]
    if mode == "dsa":
        q3, ki3, wiT, tabs, far = extra
        in_specs += [pl.BlockSpec((1, q3.shape[1], tq), qmap),
                     pl.BlockSpec((1, lpad, ki3.shape[2]), lambda i, q, kk: (i, 0, 0)),
                     pl.BlockSpec((1, C_IDX_HEADS, tq), qmap),
                     pl.BlockSpec(tabs.shape, lambda i, q, kk: (0, 0, 0, 0)),
                     pl.BlockSpec(far.shape, lambda i, q, kk: (0, 0))]
        scratch += [pltpu.VMEM((lpad, tq), I32), pltpu.VMEM((SUBLANES, tq), I32),
                    pltpu.VMEM((32, lpad // 32, tq), I32), pltpu.VMEM((lpad // 32, tq), I32)]
    return pl.pallas_call(
        functools.partial(_attn_body, mode=mode, p0=p0, t_valid=t_valid, l_valid=l_valid, tq=tq, tk=tk,
                          dk=dk, topk=topk, nh=nh),
        grid=(b, nq, nk),
        in_specs=in_specs,
        out_specs=pl.BlockSpec((1, tq, hd), lambda i, q, kk: (i, q, 0)),
        out_shape=jax.ShapeDtypeStruct((b, tpad, hd), BF16),
        scratch_shapes=scratch,
        compiler_params=_cparams(("parallel", "parallel", "arbitrary")),
        name="attn_" + mode,
    )(qT, k, vT, *extra)


PACK_TILE = 256


def _pack_body(cache_ref, new_ref, o_ref, *, n_cache, transpose, nh):
    j = pl.program_id(1)

    def emit(head):
        for h in range(nh):
            sl = slice(h * LANES, (h + 1) * LANES)
            x = head(h, sl)
            if transpose:
                o_ref[0, sl, :] = x.T.astype(o_ref.dtype)
            else:
                o_ref[0, :, sl] = x.astype(o_ref.dtype)

    @pl.when(j < n_cache)
    def _():
        emit(lambda h, sl: cache_ref[0, :, h, :])

    @pl.when(j >= n_cache)
    def _():
        emit(lambda h, sl: new_ref[0, :, sl])


def _pack_keys(cache, new, lpad, *, transpose):
    b, past, nh, dh = cache.shape
    tp = PACK_TILE
    assert dh == LANES and past % tp == 0 and lpad % tp == 0
    n_cache = past // tp
    hd = nh * dh
    new = _pad_axis(new, 1, lpad - past)
    out_shape = (b, hd, lpad) if transpose else (b, lpad, hd)
    out_spec = (pl.BlockSpec((1, hd, tp), lambda i, j: (i, 0, j)) if transpose
                else pl.BlockSpec((1, tp, hd), lambda i, j: (i, j, 0)))
    return pl.pallas_call(
        functools.partial(_pack_body, n_cache=n_cache, transpose=transpose, nh=nh),
        grid=(b, lpad // tp),
        in_specs=[pl.BlockSpec((1, tp, nh, dh), lambda i, j: (i, jnp.minimum(j, n_cache - 1), 0, 0)),
                  pl.BlockSpec((1, tp, hd), lambda i, j: (i, jnp.maximum(j - n_cache, 0), 0))],
        out_specs=out_spec,
        out_shape=jax.ShapeDtypeStruct(out_shape, BF16),
        compiler_params=_cparams(("parallel", "parallel")),
        name="pack_keys",
    )(cache, new)


def _split_hi_lo(x):
    hi = x.astype(BF16)
    lo = (x - hi.astype(F32)).astype(BF16)
    return hi, lo


def _pad_axis(x, axis, size):
    if x.shape[axis] == size:
        return x
    pad = [(0, 0)] * x.ndim
    pad[axis] = (0, size - x.shape[axis])
    return jnp.pad(x, pad)


def _key_tile(l_valid):
    lpad = -(-l_valid // 256) * 256
    for tk in (512, 256):
        if lpad % tk == 0:
            return lpad, tk
    return lpad, 256


def _prep_weights(W, d_model):
    f = {}
    wd = A_WIDTH
    n_even = W["ev_w_in"].shape[0]
    n_odd = W["od_w_in"].shape[0]
    f["even"] = []
    for e in range(n_even):
        wi = W["ev_w_in"][e]
        w_in = jnp.concatenate([wi[:, :3 * wd], wi[:, A_COLS:], wi[:, 3 * wd:A_COLS],
                                jnp.zeros((d_model, LORA_PAD - A_LORA), F32)], axis=1).astype(BF16)
        mu = W["rwkv_mu"][e]
        w2 = jnp.zeros((LANES, wd), F32).at[:A_DECAY_LORA].set(W["rwkv_w2"][e])
        a2 = jnp.zeros((LANES, wd), F32).at[A_DECAY_LORA:A_DECAY_LORA + A_ICLR_LORA].set(W["rwkv_a2"][e])
        g2 = jnp.zeros((LORA_PAD - LANES, wd), F32).at[:A_GATE_LORA].set(W["rwkv_g2"][e])
        hm = lambda v: v.reshape(A_HEADS, 1, A_HEAD_DIM)
        rw = dict(mu_r=mu[:3 * wd].reshape(1, -1),
                  mu_l=_pad_axis(mu[3 * wd:], 0, LORA_PAD).reshape(1, -1),
                  w0=W["rwkv_w0"][e].reshape(1, -1), w2=w2.astype(BF16),
                  a0=W["rwkv_a0"][e].reshape(1, -1), a2=a2.astype(BF16), g2=g2.astype(BF16),
                  k_k=W["rwkv_k_k"][e].reshape(1, -1), k_a=W["rwkv_k_a"][e].reshape(1, -1),
                  r_k=hm(W["rwkv_r_k"][e]), ln_w=hm(W["rwkv_ln_w"][e]), ln_b=hm(W["rwkv_ln_b"][e]))
        eye = jnp.eye(B_BLOCKS, dtype=F32)
        blockdiag = lambda w: (eye[:, None, :, None] * w[:, :, None, :]).reshape(B_WIDTH, B_WIDTH)
        wgate = jnp.concatenate([blockdiag(W["lru_wa"][e]), blockdiag(W["lru_wx"][e])], axis=1).astype(BF16)
        bgate = jnp.concatenate([W["lru_ba"][e], W["lru_bx"][e]])
        wo = W["ev_w_out"][e].astype(BF16)
        f["even"].append(dict(w_in=w_in, rw=rw, wgate=wgate, bgate=bgate, wo_a=wo[:wd], wo_b=wo[wd:],
                              cw=W["lru_conv_w"][e], cb=W["lru_conv_b"][e], lam=W["lru_lambda"][e]))
    f["odd"] = []
    cw_ = C_WIDTH
    qi_w = C_IDX_HEADS * C_IDX_DIM
    for o in range(n_odd):
        wi = W["od_w_in"][o]
        offs = [0]
        for s in (cw_, cw_, cw_, qi_w, C_IDX_DIM, C_IDX_HEADS, D_Q_RANK, D_KV_RANK, D_ROPE):
            offs.append(offs[-1] + s)
        q, k, v, qi, ki, wi_, qd, kvd, kr = [wi[:, offs[i]:offs[i + 1]] for i in range(9)]
        small = jnp.concatenate([ki, kr, wi_], axis=1)
        w_in = jnp.concatenate([q, k, v, qi, qd, kvd, _pad_axis(small, 1, 512)], axis=1).astype(BF16)
        wuq = W["mla_w_uq"][o].reshape(D_Q_RANK, D_HEADS, D_NOPE + D_ROPE)
        wuq = jnp.concatenate([wuq[:, :, :D_NOPE].reshape(D_Q_RANK, -1),
                               wuq[:, :, D_NOPE:].reshape(D_Q_RANK, -1)], axis=1).astype(BF16)
        wukv = W["mla_w_ukv"][o].reshape(D_KV_RANK, D_HEADS, D_NOPE + D_V)
        wuk = _pad_axis(wukv[:, :, :D_NOPE], 2, MLA_DK).reshape(D_KV_RANK, -1).astype(BF16)
        wuv = wukv[:, :, D_NOPE:].reshape(D_KV_RANK, -1).astype(BF16)
        wo = W["od_w_out"][o].astype(BF16)
        f["odd"].append(dict(w_in=w_in, wuq=wuq, wuk=wuk, wuv=wuv, wo_c=wo[:cw_], wo_d=wo[cw_:],
                             gq=W["mla_q_norm"][o], gkv=W["mla_kv_norm"][o]))
    f["ffn"] = [dict(wg=W["ffn_w_gate"][l].astype(BF16), wu=W["ffn_w_up"][l].astype(BF16),
                     wd=W["ffn_w_down"][l].astype(BF16)) for l in range(W["ffn_w_gate"].shape[0])]
    f["tabs"] = _bias_tables(W["rel_bias"])
    nb = N_BUCKETS // 2
    f["far"] = _pad_axis(W["rel_bias"][nb - 1:nb, :] * LOG2E, 1, LANES)
    return f


def _mixer_even(x2, b, t, p0, shift, s0, h0, cbuf, fe, norm_g):
    wd = A_WIDTH
    p = _norm_matmul(x2, norm_g, fe["w_in"])
    p3 = p.reshape(b, t, -1)
    lora_col = 3 * wd + 2 * B_WIDTH
    shift_r = shift[:, None, :3 * wd]
    shift_l = _pad_axis(shift[:, None, 3 * wd:], 2, LORA_PAD)
    ya, s_new = _rwkv(p3, lora_col // LORA_PAD, shift_r, shift_l, s0, fe["rw"])
    yb, h_new, c_new = _rglru(p3, 3, 4, fe["cw"], fe["cb"], fe["wgate"], fe["bgate"], fe["lam"],
                              h0, cbuf, p0=p0)
    x2 = _matmul([ya.reshape(b * t, wd), yb.reshape(b * t, B_WIDTH)], [fe["wo_a"], fe["wo_b"]], res=x2)
    last = p3[:, t - 1]
    new_shift = jnp.concatenate([last[:, :3 * wd], last[:, lora_col:lora_col + A_LORA]], axis=-1)
    return x2, new_shift, s_new, h_new[:, 0], c_new


def _mixer_odd(x2, b, t, p0, ck, cv, cik, clat, ckr, fo, f, norm_g, cos, sin):
    cw_ = C_WIDTH
    p = _norm_matmul(x2, norm_g, fo["w_in"])
    p3 = p.reshape(b, t, -1)
    past = ck.shape[1]
    l_valid = past + t
    lpad, tk = _key_tile(l_valid)
    tpad = -(-t // LANES) * LANES
    q = p3[..., :cw_]
    k_new = p3[..., cw_:2 * cw_]
    v_new = p3[..., 2 * cw_:3 * cw_]
    qi = p3[..., 3 * cw_:3 * cw_ + 512]
    small = p3[..., 3 * cw_ + 1536:]
    ki_new = small[..., :C_IDX_DIM]
    wi = small[..., 2 * C_IDX_DIM:2 * C_IDX_DIM + C_IDX_HEADS]

    def keys(cache, new):
        allk = jnp.concatenate([cache.reshape(b, past, -1), new], axis=1) if past else new
        return _pad_axis(allk, 1, lpad)

    tq_ = lambda z: _pad_axis(jnp.swapaxes(z, 1, 2), 2, tpad)
    if past and past % PACK_TILE == 0:
        k_all = _pack_keys(ck, k_new, lpad, transpose=False)
        vT_all = _pack_keys(cv, v_new, lpad, transpose=True)
    else:
        k_all = keys(ck, k_new).astype(BF16)
        vT_all = jnp.swapaxes(keys(cv, v_new), 1, 2).astype(BF16)
    kih, kil = _split_hi_lo(keys(cik, ki_new))
    ki3 = jnp.concatenate([kih, kih, kil], axis=-1)
    qih, qil = _split_hi_lo(tq_(qi).reshape(b, C_IDX_HEADS, C_IDX_DIM, tpad))
    q3 = jnp.concatenate([qih, qil, qih], axis=2).reshape(b, 3 * C_IDX_HEADS * C_IDX_DIM, tpad)
    yc = _attention("dsa", tq_(q * (C_HEAD_DIM ** -0.5 * LOG2E)).astype(BF16), k_all, vT_all,
                    (q3, ki3, tq_(wi), f["tabs"], f["far"]),
                    p0=p0, t_valid=t, l_valid=l_valid, tk=tk)[:, :t]
    qn, qr, lat, krope = _mla_prep(p3, 7, 8, 9, fo["gq"], fo["gkv"], fo["wuq"], cos, sin)
    lat_all = keys(clat, lat).reshape(b * lpad, -1)
    kr_all = keys(ckr, krope).reshape(b * lpad, -1)
    eye = jnp.eye(D_ROPE, dtype=BF16)
    ident = jnp.tile(jnp.pad(eye, ((0, 0), (D_NOPE, MLA_DK - D_NOPE - D_ROPE))), (1, D_HEADS))
    k_full = _matmul([lat_all, kr_all], [fo["wuk"], ident], out_dtype=BF16).reshape(b, lpad, -1)
    v_all = _matmul([lat_all], [fo["wuv"]], out_dtype=BF16).reshape(b, lpad, -1)
    q_full = jnp.concatenate([qn.reshape(b, t, D_HEADS, D_NOPE), qr.reshape(b, t, D_HEADS, D_ROPE),
                              jnp.zeros((b, t, D_HEADS, MLA_DK - D_NOPE - D_ROPE), BF16)], axis=-1)
    yd = _attention("mla", tq_(q_full.reshape(b, t, -1)), k_full, jnp.swapaxes(v_all, 1, 2), (),
                    p0=p0, t_valid=t, l_valid=l_valid, tk=tk)[:, :t]
    x2 = _matmul([yc.reshape(b * t, cw_), yd.reshape(b * t, -1)], [fo["wo_c"], fo["wo_d"]], res=x2)
    return (x2, k_new.reshape(b, t, C_HEADS, C_HEAD_DIM), v_new.reshape(b, t, C_HEADS, C_HEAD_DIM),
            ki_new, lat, krope)


def _trunk(x, p0, shift, rwkv_s, lru_h, lru_conv, dsa_k, dsa_v, dsa_ik, mla_lat, mla_kr, W, f):
    b, t, d = x.shape
    depth = W["norm_mix"].shape[0]
    pos = (p0 + jnp.arange(t)).astype(F32)
    inv = ROPE_BASE ** (-jnp.arange(0, D_ROPE, 2, dtype=F32) / D_ROPE)
    ang = pos[:, None] * inv[None, :]
    cos = jnp.tile(jnp.cos(ang), (1, 2 * D_HEADS))
    sin = jnp.tile(jnp.sin(ang), (1, 2 * D_HEADS))
    x2 = x.reshape(b * t, d)
    ev = [[] for _ in range(4)]
    od = [[] for _ in range(5)]
    for layer in range(depth):
        if layer % 2 == 0:
            e = layer // 2
            x2, *outs = _mixer_even(x2, b, t, p0, shift[e], rwkv_s[e], lru_h[e], lru_conv[e],
                                    f["even"][e], W["norm_mix"][layer])
            for lst, o_ in zip(ev, outs):
                lst.append(o_)
        else:
            o = layer // 2
            x2, *outs = _mixer_odd(x2, b, t, p0, dsa_k[o], dsa_v[o], dsa_ik[o], mla_lat[o], mla_kr[o],
                                   f["odd"][o], f, W["norm_mix"][layer], cos, sin)
            for lst, o_ in zip(od, outs):
                lst.append(o_)
        ff = f["ffn"][layer]
        x2 = _ffn(x2, W["norm_ffn"][layer], ff["wg"], ff["wu"], ff["wd"], W["final_norm"],
                  final_norm=layer == depth - 1)
    return (x2.reshape(b, t, d),) + tuple(jnp.stack(v) for v in ev) + tuple(jnp.stack(v) for v in od)


def kernel(x_prompt, x_sample, state_rwkv_shift, state_rwkv, state_lru, state_lru_conv, cache_dsa_k, cache_dsa_v, cache_dsa_idx_k, cache_mla_latent, cache_mla_krope, rel_bias, final_norm, norm_mix, norm_ffn, ffn_w_gate, ffn_w_up, ffn_w_down, ev_w_in, ev_w_out, rwkv_mu, rwkv_w0, rwkv_w2, rwkv_a0, rwkv_a2, rwkv_g2, rwkv_k_k, rwkv_k_a, rwkv_r_k, rwkv_ln_w, rwkv_ln_b, lru_conv_w, lru_conv_b, lru_wa, lru_ba, lru_wx, lru_bx, lru_lambda, od_w_in, od_w_out, mla_q_norm, mla_w_uq, mla_kv_norm, mla_w_ukv):
    W = dict(rel_bias=rel_bias, final_norm=final_norm, norm_mix=norm_mix, norm_ffn=norm_ffn,
             ffn_w_gate=ffn_w_gate, ffn_w_up=ffn_w_up, ffn_w_down=ffn_w_down,
             ev_w_in=ev_w_in, ev_w_out=ev_w_out, rwkv_mu=rwkv_mu, rwkv_w0=rwkv_w0,
             rwkv_w2=rwkv_w2, rwkv_a0=rwkv_a0, rwkv_a2=rwkv_a2, rwkv_g2=rwkv_g2,
             rwkv_k_k=rwkv_k_k, rwkv_k_a=rwkv_k_a, rwkv_r_k=rwkv_r_k, rwkv_ln_w=rwkv_ln_w,
             rwkv_ln_b=rwkv_ln_b, lru_conv_w=lru_conv_w, lru_conv_b=lru_conv_b, lru_wa=lru_wa,
             lru_ba=lru_ba, lru_wx=lru_wx, lru_bx=lru_bx, lru_lambda=lru_lambda,
             od_w_in=od_w_in, od_w_out=od_w_out, mla_q_norm=mla_q_norm, mla_w_uq=mla_w_uq,
             mla_kv_norm=mla_kv_norm, mla_w_ukv=mla_w_ukv)
    d_model = x_prompt.shape[-1]
    f = _prep_weights(W, d_model)
    bp = x_prompt.shape[0]
    n_even, n_odd = ev_w_in.shape[0], od_w_in.shape[0]
    dt = x_prompt.dtype
    z = lambda *shape: jnp.zeros(shape, dt)
    outs_p = _trunk(
        x_prompt, 0,
        z(n_even, bp, A_COLS), z(n_even, bp, A_HEADS, A_HEAD_DIM, A_HEAD_DIM),
        z(n_even, bp, B_WIDTH), z(n_even, bp, B_CONV - 1, B_WIDTH),
        z(n_odd, bp, 0, C_HEADS, C_HEAD_DIM), z(n_odd, bp, 0, C_HEADS, C_HEAD_DIM),
        z(n_odd, bp, 0, C_IDX_DIM), z(n_odd, bp, 0, D_KV_RANK), z(n_odd, bp, 0, D_ROPE), W, f)
    past = cache_dsa_k.shape[2]
    outs_s = _trunk(x_sample, past, state_rwkv_shift, state_rwkv, state_lru, state_lru_conv,
                    cache_dsa_k, cache_dsa_v, cache_dsa_idx_k, cache_mla_latent, cache_mla_krope, W, f)
    return (outs_p[0], outs_s[0]) + tuple(outs_p[1:]) + tuple(outs_s[1:])
```

```python
import functools
import math

import jax
import jax.numpy as jnp
from jax import lax
from jax.experimental import pallas as pl
from jax.experimental.pallas import tpu as pltpu

F32 = jnp.float32
BF16 = jnp.bfloat16
I32 = jnp.int32

CHUNK = 64
NORM_EPS = 1e-6
A_HEADS = 16
A_HEAD_DIM = 64
A_WIDTH = A_HEADS * A_HEAD_DIM
A_DECAY_LORA = 64
A_ICLR_LORA = 64
A_GATE_LORA = 160
A_LORA = A_DECAY_LORA + A_ICLR_LORA + A_GATE_LORA
A_COLS = 3 * A_WIDTH + A_LORA
A_LN_EPS = 64e-5
B_WIDTH = 1024
B_BLOCKS = 16
B_CONV = 4
B_C = 8.0
C_HEADS = 8
C_HEAD_DIM = 128
C_WIDTH = C_HEADS * C_HEAD_DIM
C_IDX_HEADS = 8
C_IDX_DIM = 64
TOPK_MAX = 256
D_HEADS = 8
D_NOPE = 128
D_ROPE = 64
D_V = 128
D_Q_RANK = 512
D_KV_RANK = 512
ROPE_BASE = 10000.0
N_BUCKETS = 32
MAX_DISTANCE = 128

LANES = 128
SUBLANES = 8
VMEM_LIMIT = 56 * 1024 * 1024
LORA_PAD = 512
NEG_BIG = -1e30
INT_MIN = -2147483648
RC = 64
SEARCH_TILES = 4
LOG2E = 1.4426950408889634
MLA_DK = 256

NN = (((1,), (0,)), ((), ()))
NT = (((1,), (1,)), ((), ()))
BNN = (((2,), (1,)), ((0,), (0,)))
BNT = (((2,), (2,)), ((0,), (0,)))
BTN = (((1,), (1,)), ((0,), (0,)))


def _cparams(sem):
    return pltpu.CompilerParams(dimension_semantics=sem, vmem_limit_bytes=VMEM_LIMIT)


def _rms(x, g, eps=NORM_EPS):
    ms = jnp.mean(x * x, axis=-1, keepdims=True)
    return x * lax.rsqrt(ms + eps) * g


def _softplus(x):
    return jnp.maximum(x, 0.0) + jnp.log1p(jnp.exp(-jnp.abs(x)))


def _dotp(a, b, dims, passes):
    if passes == 6:
        return lax.dot_general(a, b, dims, precision=lax.Precision.HIGHEST, preferred_element_type=F32)
    ah = a.astype(BF16)
    bh = b.astype(BF16)
    out = lax.dot_general(ah, bh, dims, preferred_element_type=F32)
    if passes == 3:
        al = (a - ah.astype(F32)).astype(BF16)
        bl = (b - bh.astype(F32)).astype(BF16)
        out = out + lax.dot_general(ah, bl, dims, preferred_element_type=F32)
        out = out + lax.dot_general(al, bh, dims, preferred_element_type=F32)
    return out


def _norm_matmul_body(x_ref, g_ref, w_ref, o_ref, xn_ref):
    @pl.when(pl.program_id(1) == 0)
    def _():
        xn_ref[...] = _rms(x_ref[...], g_ref[...]).astype(BF16)

    o_ref[...] = jnp.dot(xn_ref[...], w_ref[...], preferred_element_type=F32)


def _row_tile(m, cap):
    tm = cap
    while m % tm:
        tm //= 2
    return tm


def _norm_matmul(x, g, w, *, tn=512):
    m, k = x.shape
    n = w.shape[1]
    tm = _row_tile(m, 1024)
    return pl.pallas_call(
        _norm_matmul_body,
        grid=(m // tm, n // tn),
        in_specs=[pl.BlockSpec((tm, k), lambda i, j: (i, 0)),
                  pl.BlockSpec((1, k), lambda i, j: (0, 0)),
                  pl.BlockSpec((k, tn), lambda i, j: (0, j))],
        out_specs=pl.BlockSpec((tm, tn), lambda i, j: (i, j)),
        out_shape=jax.ShapeDtypeStruct((m, n), F32),
        scratch_shapes=[pltpu.VMEM((tm, k), BF16)],
        compiler_params=_cparams(("parallel", "arbitrary")),
        name="norm_matmul",
    )(x, g.reshape(1, k), w)


def _mm_body(*refs, n_lhs, has_res):
    o_ref = refs[-1]
    acc = refs[2 * n_lhs][...] if has_res else None
    for a_ref, w_ref in zip(refs[:n_lhs], refs[n_lhs:2 * n_lhs]):
        d = jnp.dot(a_ref[...].astype(BF16), w_ref[...], preferred_element_type=F32)
        acc = d if acc is None else acc + d
    o_ref[...] = acc.astype(o_ref.dtype)


def _matmul(lhs_list, w_list, res=None, *, tn=512, out_dtype=F32):
    m = lhs_list[0].shape[0]
    n = w_list[0].shape[1]
    tm = _row_tile(m, 1024)
    in_specs = [pl.BlockSpec((tm, a.shape[1]), lambda i, j: (i, 0)) for a in lhs_list]
    in_specs += [pl.BlockSpec((w.shape[0], tn), lambda i, j: (0, j)) for w in w_list]
    args = list(lhs_list) + list(w_list)
    if res is not None:
        in_specs.append(pl.BlockSpec((tm, tn), lambda i, j: (i, j)))
        args.append(res)
    return pl.pallas_call(
        functools.partial(_mm_body, n_lhs=len(lhs_list), has_res=res is not None),
        grid=(m // tm, n // tn),
        in_specs=in_specs,
        out_specs=pl.BlockSpec((tm, tn), lambda i, j: (i, j)),
        out_shape=jax.ShapeDtypeStruct((m, n), out_dtype),
        compiler_params=_cparams(("parallel", "arbitrary")),
        name="matmul",
    )(*args)


def _ffn_body(x_ref, g_ref, wg_ref, wu_ref, wd_ref, gf_ref, o_ref, xn_ref, acc_ref, *, final_norm):
    f = pl.program_id(1)

    @pl.when(f == 0)
    def _():
        xn_ref[...] = _rms(x_ref[...], g_ref[...]).astype(BF16)
        acc_ref[...] = jnp.zeros_like(acc_ref)

    xn = xn_ref[...]
    hg = jnp.dot(xn, wg_ref[0], preferred_element_type=F32)
    hu = jnp.dot(xn, wu_ref[0], preferred_element_type=F32)
    h = hg * jax.nn.sigmoid(hg) * hu
    acc_ref[...] += jnp.dot(h.astype(BF16), wd_ref[0], preferred_element_type=F32)

    @pl.when(f == pl.num_programs(1) - 1)
    def _():
        y = x_ref[...] + acc_ref[...]
        if final_norm:
            y = _rms(y, gf_ref[...])
        o_ref[...] = y


def _ffn(x, g, wg, wu, wd, layer, gf, *, final_norm, tf=512):
    m, k = x.shape
    dff = wg.shape[2]
    tm = min(512, m)
    return pl.pallas_call(
        functools.partial(_ffn_body, final_norm=final_norm),
        grid=(m // tm, dff // tf),
        in_specs=[pl.BlockSpec((tm, k), lambda i, f: (i, 0)),
                  pl.BlockSpec((1, k), lambda i, f: (0, 0)),
                  pl.BlockSpec((1, k, tf), lambda i, f: (layer, 0, f)),
                  pl.BlockSpec((1, k, tf), lambda i, f: (layer, 0, f)),
                  pl.BlockSpec((1, tf, k), lambda i, f: (layer, f, 0)),
                  pl.BlockSpec((1, k), lambda i, f: (0, 0))],
        out_specs=pl.BlockSpec((tm, k), lambda i, f: (i, 0)),
        out_shape=jax.ShapeDtypeStruct((m, k), F32),
        scratch_shapes=[pltpu.VMEM((tm, k), BF16), pltpu.VMEM((tm, k), F32)],
        compiler_params=_cparams(("parallel", "arbitrary")),
        name="ffn",
    )(x, g.reshape(1, k), wg, wu, wd, gf.reshape(1, k))


def _lru_body(gate_ref, xb_ref, cw_ref, cb_ref, wg_ref, bg_ref, lam_ref, h0_ref, cbuf_ref,
              y_ref, hout_ref, cout_ref, xbuf, hcar, *, tt, p0):
    t = pl.program_id(1)
    w = B_WIDTH
    halo = SUBLANES

    @pl.when(t == 0)
    def _():
        xbuf[0:halo, :] = jnp.zeros((halo, w), F32)
        xbuf[halo - (B_CONV - 1):halo, :] = cbuf_ref[0]
        hcar[...] = h0_ref[0]

    xb = xb_ref[0]
    xbuf[halo:halo + tt, :] = xb
    xc = cb_ref[...] + cw_ref[B_CONV - 1:B_CONV, :] * xb
    for j in range(B_CONV - 1):
        off = halo - (B_CONV - 1) + j
        xc = xc + cw_ref[j:j + 1, :] * xbuf[off:off + tt, :]
    tail = xbuf[tt + halo - (B_CONV - 1):tt + halo, :]
    xbuf[halo - (B_CONV - 1):halo, :] = tail

    pre = jnp.dot(xc.astype(BF16), wg_ref[...], preferred_element_type=F32) + bg_ref[...]
    rg = jax.nn.sigmoid(pre[:, :w])
    ig = jax.nn.sigmoid(pre[:, w:])
    log_a = (-B_C) * rg * _softplus(-lam_ref[...])
    a = jnp.exp(log_a)
    row = lax.broadcasted_iota(I32, (tt, w), 0)
    th = jnp.tanh(log_a)
    mult = jnp.sqrt(-2.0 * th / (1.0 - th))
    mult = jnp.where(row + (p0 + t * tt) == 0, 1.0, mult)
    u = mult * (ig * xc)

    d = 1
    while d < tt:
        keep = row >= d
        a_sh = pltpu.roll(a, d, 0)
        u_sh = pltpu.roll(u, d, 0)
        u = u + jnp.where(keep, a * u_sh, 0.0)
        a = jnp.where(keep, a * a_sh, a)
        d *= 2
    h = u + a * hcar[...]
    hcar[...] = h[tt - 1:tt, :]
    y_ref[0] = (h * jax.nn.gelu(gate_ref[0])).astype(y_ref.dtype)

    @pl.when(t == pl.num_programs(1) - 1)
    def _():
        hout_ref[0] = h[tt - 1:tt, :]
        cout_ref[0] = tail


def _rglru(p3, gate_blk, xb_blk, cw, cb, wgate, bgate, lam, h0, cbuf, *, p0):
    b, t, _ = p3.shape
    w = B_WIDTH
    tt = min(256, t)
    row = lambda v: v.reshape(1, -1)
    return pl.pallas_call(
        functools.partial(_lru_body, tt=tt, p0=p0),
        grid=(b, t // tt),
        in_specs=[pl.BlockSpec((1, tt, w), lambda i, j: (i, j, gate_blk)),
                  pl.BlockSpec((1, tt, w), lambda i, j: (i, j, xb_blk)),
                  pl.BlockSpec((B_CONV, w), lambda i, j: (0, 0)),
                  pl.BlockSpec((1, w), lambda i, j: (0, 0)),
                  pl.BlockSpec((w, 2 * w), lambda i, j: (0, 0)),
                  pl.BlockSpec((1, 2 * w), lambda i, j: (0, 0)),
                  pl.BlockSpec((1, w), lambda i, j: (0, 0)),
                  pl.BlockSpec((1, 1, w), lambda i, j: (i, 0, 0)),
                  pl.BlockSpec((1, B_CONV - 1, w), lambda i, j: (i, 0, 0))],
        out_specs=[pl.BlockSpec((1, tt, w), lambda i, j: (i, j, 0)),
                   pl.BlockSpec((1, 1, w), lambda i, j: (i, 0, 0)),
                   pl.BlockSpec((1, B_CONV - 1, w), lambda i, j: (i, 0, 0))],
        out_shape=[jax.ShapeDtypeStruct((b, t, w), BF16),
                   jax.ShapeDtypeStruct((b, 1, w), F32),
                   jax.ShapeDtypeStruct((b, B_CONV - 1, w), F32)],
        scratch_shapes=[pltpu.VMEM((tt + SUBLANES, w), F32), pltpu.VMEM((1, w), F32)],
        compiler_params=_cparams(("parallel", "arbitrary")),
        name="rglru",
    )(p3, p3, cw, row(cb), wgate, row(bgate), row(lam), h0.reshape(b, 1, w), cbuf)


RWKV_PASSES = 1
RWKV_PASSES_SOLVE = 3


def _rwkv_body(rkv_ref, lora_ref, shr_ref, shl_ref, s0_ref, mur_ref, mul_ref, w0_ref, w2_ref, a0_ref,
               a2_ref, g2_ref, kk_ref, ka_ref, rk_ref, lnw_ref, lnb_ref, tri_ref,
               y_ref, sout_ref, buf_r, buf_l, s_scr, st_r, st_k, st_v, st_q, st_a, st_l, st_d, *, c):
    ci = pl.program_id(1)
    halo = SUBLANES
    hn, n, wd = A_HEADS, A_HEAD_DIM, A_WIDTH
    mm = functools.partial(_dotp, passes=RWKV_PASSES)

    @pl.when(ci == 0)
    def _():
        buf_r[halo - 1:halo, :] = shr_ref[0]
        buf_l[halo - 1:halo, :] = shl_ref[0]
        s_scr[...] = s0_ref[0]

    cur_r = rkv_ref[0]
    cur_l = lora_ref[0]
    buf_r[halo:halo + c, :] = cur_r
    buf_l[halo:halo + c, :] = cur_l
    xm = cur_r + mur_ref[...] * (buf_r[halo - 1:halo - 1 + c, :] - cur_r)
    lo = cur_l + mul_ref[...] * (buf_l[halo - 1:halo - 1 + c, :] - cur_l)
    buf_r[halo - 1:halo, :] = cur_r[c - 1:c, :]
    buf_l[halo - 1:halo, :] = cur_l[c - 1:c, :]

    r = xm[:, :wd]
    k = xm[:, wd:2 * wd]
    v = xm[:, 2 * wd:]
    lo_a = lo[:, :LANES]
    w_pre = w0_ref[...] + jnp.dot(jnp.tanh(lo_a).astype(BF16), w2_ref[...], preferred_element_type=F32)
    w_log = -_softplus(-w_pre) - 0.5
    ld = -jnp.exp(w_log)
    a = jax.nn.sigmoid(a0_ref[...] + jnp.dot(lo_a.astype(BF16), a2_ref[...], preferred_element_type=F32))
    g = jnp.dot(jax.nn.sigmoid(lo[:, LANES:]).astype(BF16), g2_ref[...], preferred_element_type=F32)
    kq = k * kk_ref[...]
    k2 = k * (1.0 + (a - 1.0) * ka_ref[...])
    lc = lax.dot_general(tri_ref[...], ld, NN, precision=lax.Precision.HIGHEST,
                         preferred_element_type=F32)

    for h in range(hn):
        sl = slice(h * n, (h + 1) * n)
        st_r[h] = r[:, sl]
        st_k[h] = k2[:, sl]
        st_v[h] = v[:, sl]
        st_q[h] = kq[:, sl]
        st_a[h] = a[:, sl]
        st_l[h] = lc[:, sl]
        st_d[h] = ld[:, sl]

    rh_, k2h, vh, kqh, ah, lch, ldh = (st_r[...], st_k[...], st_v[...], st_q[...], st_a[...],
                                       st_l[...], st_d[...])
    nrm = jnp.sqrt(jnp.sum(kqh * kqh, axis=-1, keepdims=True))
    kk = kqh / jnp.maximum(nrm, 1e-12)
    kka = kk * ah
    e_neg = jnp.exp(-lch)
    am = jnp.exp(lch - ldh) * kk
    bm = kka * e_neg
    kh = k2h * e_neg
    rh = rh_ * jnp.exp(lch)
    l_end = lch[:, c - 1:c, :]
    e_c = jnp.exp(l_end - lch)
    bp = kka * e_c
    kp = k2h * e_c
    w_end = jnp.exp(l_end)

    x2 = jnp.concatenate([am, rh], axis=1)
    zb = _dotp(x2, bm, BNT, RWKV_PASSES_SOLVE)
    zk = mm(x2, kh, BNT)
    ti = lax.broadcasted_iota(I32, (hn, c, c), 1)
    si = lax.broadcasted_iota(I32, (hn, c, c), 2)
    strict = si < ti
    incl = si <= ti
    m1 = jnp.where(strict, zb[:, :c], 0.0)
    m4 = jnp.where(incl, zb[:, c:], 0.0)
    m2 = jnp.where(strict, zk[:, :c], 0.0)
    m3 = jnp.where(incl, zk[:, c:], 0.0)

    tm = jnp.where(si == ti, 1.0, 0.0) - m1
    npow = mm(m1, m1, BNN)
    span = 2
    while span < c:
        tm = tm + mm(tm, npow, BNN)
        span *= 2
        if span < c:
            npow = mm(npow, npow, BNN)

    s0 = s_scr[...]
    rhs = mm(am, s0, BNT) + mm(m2, vh, BNN)
    p = mm(tm, rhs, BNN)
    y = mm(rh, s0, BNT) + mm(m3, vh, BNN) - mm(m4, p, BNN)
    s_new = s0 * w_end + mm(vh, kp, BTN) - mm(p, bp, BTN)
    s_scr[...] = s_new

    mean = jnp.mean(y, axis=-1, keepdims=True)
    yc = y - mean
    var = jnp.mean(yc * yc, axis=-1, keepdims=True)
    yn = yc * lax.rsqrt(var + A_LN_EPS) * lnw_ref[...] + lnb_ref[...]
    bonus = jnp.sum(rh_ * k2h * rk_ref[...], axis=-1, keepdims=True) * vh
    yo = yn + bonus
    yo = jnp.concatenate([yo[h] for h in range(hn)], axis=-1)
    y_ref[0] = (yo * g).astype(y_ref.dtype)

    @pl.when(ci == pl.num_programs(1) - 1)
    def _():
        sout_ref[0] = s_new


def _rwkv(p3, lora_blk, shift_r, shift_l, s0, wts):
    b, t, _ = p3.shape
    c = min(CHUNK, t)
    hn, n, wd = A_HEADS, A_HEAD_DIM, A_WIDTH
    full = lambda shape: pl.BlockSpec(shape, lambda i, j: (0,) * len(shape))
    tri = (jnp.arange(c)[:, None] >= jnp.arange(c)[None, :]).astype(F32)
    st = pltpu.VMEM((hn, c, n), F32)
    return pl.pallas_call(
        functools.partial(_rwkv_body, c=c),
        grid=(b, t // c),
        in_specs=[pl.BlockSpec((1, c, 3 * wd), lambda i, j: (i, j, 0)),
                  pl.BlockSpec((1, c, LORA_PAD), lambda i, j: (i, j, lora_blk)),
                  pl.BlockSpec((1, 1, 3 * wd), lambda i, j: (i, 0, 0)),
                  pl.BlockSpec((1, 1, LORA_PAD), lambda i, j: (i, 0, 0)),
                  pl.BlockSpec((1, hn, n, n), lambda i, j: (i, 0, 0, 0)),
                  full((1, 3 * wd)), full((1, LORA_PAD)), full((1, wd)), full((LANES, wd)),
                  full((1, wd)), full((LANES, wd)), full((LORA_PAD - LANES, wd)),
                  full((1, wd)), full((1, wd)), full((hn, 1, n)), full((hn, 1, n)), full((hn, 1, n)),
                  full((c, c))],
        out_specs=[pl.BlockSpec((1, c, wd), lambda i, j: (i, j, 0)),
                   pl.BlockSpec((1, hn, n, n), lambda i, j: (i, 0, 0, 0))],
        out_shape=[jax.ShapeDtypeStruct((b, t, wd), BF16),
                   jax.ShapeDtypeStruct((b, hn, n, n), F32)],
        scratch_shapes=[pltpu.VMEM((c + SUBLANES, 3 * wd), F32), pltpu.VMEM((c + SUBLANES, LORA_PAD), F32),
                        pltpu.VMEM((hn, n, n), F32), st, st, st, st, st, st, st],
        compiler_params=_cparams(("parallel", "arbitrary")),
        name="rwkv7",
    )(p3, p3, shift_r, shift_l, s0, wts["mu_r"], wts["mu_l"], wts["w0"], wts["w2"], wts["a0"], wts["a2"],
      wts["g2"], wts["k_k"], wts["k_a"], wts["r_k"], wts["ln_w"], wts["ln_b"], tri)


def _rope(x, cos, sin):
    wdt = x.shape[-1]
    lane = lax.broadcasted_iota(I32, x.shape, 1)
    first = (lane % D_ROPE) < (D_ROPE // 2)
    rot = jnp.where(first, -pltpu.roll(x, wdt - D_ROPE // 2, 1), pltpu.roll(x, D_ROPE // 2, 1))
    return x * cos + rot * sin


def _mla_prep_body(qd_ref, kvd_ref, sm_ref, gq_ref, gkv_ref, wuq_ref, cos_ref, sin_ref,
                   qn_ref, qr_ref, lat_ref, kr_ref):
    nope_w = D_HEADS * D_NOPE
    qdn = _rms(qd_ref[0], gq_ref[...]).astype(BF16)
    qf = jnp.dot(qdn, wuq_ref[...], preferred_element_type=F32) * ((D_NOPE + D_ROPE) ** -0.5 * LOG2E)
    qn_ref[0] = qf[:, :nope_w].astype(qn_ref.dtype)
    cos = cos_ref[...]
    sin = sin_ref[...]
    qr_ref[0] = _rope(qf[:, nope_w:], cos, sin).astype(qr_ref.dtype)
    lat_ref[0] = _rms(kvd_ref[0], gkv_ref[...])
    sm = sm_ref[0][:, :LANES]
    kr = _rope(sm, cos[:, :LANES], sin[:, :LANES])
    kr_ref[0] = kr[:, C_IDX_DIM:C_IDX_DIM + D_ROPE]


def _mla_prep(p3, qd_blk, kvd_blk, sm_blk, gq, gkv, wuq, cos, sin):
    b, t, _ = p3.shape
    tt = min(256, t)
    rw = D_HEADS * D_ROPE
    full = lambda shape: pl.BlockSpec(shape, lambda i, j: (0,) * len(shape))
    return pl.pallas_call(
        _mla_prep_body,
        grid=(b, t // tt),
        in_specs=[pl.BlockSpec((1, tt, D_Q_RANK), lambda i, j: (i, j, qd_blk)),
                  pl.BlockSpec((1, tt, D_KV_RANK), lambda i, j: (i, j, kvd_blk)),
                  pl.BlockSpec((1, tt, 512), lambda i, j: (i, j, sm_blk)),
                  full((1, D_Q_RANK)), full((1, D_KV_RANK)), full(wuq.shape),
                  pl.BlockSpec((tt, rw), lambda i, j: (j, 0)),
                  pl.BlockSpec((tt, rw), lambda i, j: (j, 0))],
        out_specs=[pl.BlockSpec((1, tt, D_HEADS * D_NOPE), lambda i, j: (i, j, 0)),
                   pl.BlockSpec((1, tt, rw), lambda i, j: (i, j, 0)),
                   pl.BlockSpec((1, tt, D_KV_RANK), lambda i, j: (i, j, 0)),
                   pl.BlockSpec((1, tt, D_ROPE), lambda i, j: (i, j, 0))],
        out_shape=[jax.ShapeDtypeStruct((b, t, D_HEADS * D_NOPE), BF16),
                   jax.ShapeDtypeStruct((b, t, rw), BF16),
                   jax.ShapeDtypeStruct((b, t, D_KV_RANK), F32),
                   jax.ShapeDtypeStruct((b, t, D_ROPE), F32)],
        compiler_params=_cparams(("parallel", "parallel")),
        name="mla_prep",
    )(p3, p3, p3, gq.reshape(1, -1), gkv.reshape(1, -1), wuq, cos, sin)


def _bias_table_body(rb_ref, o_ref):
    d = pl.program_id(0)
    h = pl.program_id(1)
    nb = N_BUCKETS // 2
    max_exact = nb // 2
    s = lax.broadcasted_iota(I32, (LANES, LANES), 0)
    q = lax.broadcasted_iota(I32, (LANES, LANES), 1)
    rel = s - q - d * LANES
    n = jnp.abs(rel)
    big = jnp.maximum(n, max_exact).astype(F32)
    large = max_exact + (jnp.log(big / max_exact) / math.log(MAX_DISTANCE / max_exact)
                         * (nb - max_exact)).astype(I32)
    large = jnp.minimum(large, nb - 1)
    bucket = jnp.where(rel > 0, nb, 0) + jnp.where(n < max_exact, n, large)
    out = jnp.zeros((LANES, LANES), F32)
    for bk in range(N_BUCKETS):
        out = jnp.where(bucket == bk, rb_ref[bk, h], out)
    o_ref[0, 0] = out * LOG2E


def _bias_tables(rel_bias):
    return pl.pallas_call(
        _bias_table_body,
        grid=(2, C_HEADS),
        in_specs=[pl.BlockSpec(memory_space=pltpu.SMEM)],
        out_specs=pl.BlockSpec((1, 1, LANES, LANES), lambda d, h: (d, h, 0, 0)),
        out_shape=jax.ShapeDtypeStruct((2, C_HEADS, LANES, LANES), F32),
        name="bias_tables",
    )(rel_bias)


def _transpose32(x):
    x = list(x)
    for s, msk in ((16, 0x0000FFFF), (8, 0x00FF00FF), (4, 0x0F0F0F0F), (2, 0x33333333), (1, 0x55555555)):
        sh = jnp.full(x[0].shape, s, I32)
        for i in range(32):
            if i & s == 0:
                t = (lax.shift_right_logical(x[i], sh) ^ x[i + s]) & msk
                x[i + s] = x[i + s] ^ t
                x[i] = x[i] ^ lax.shift_left(t, sh)
    return x


def _attn_body(*refs, mode, p0, t_valid, l_valid, tq, tk, dk, topk, nh):
    dsa = mode == "dsa"
    dh = LANES
    if dsa:
        (qT_ref, k_ref, vT_ref, q3_ref, ki3_ref, wiT_ref, tab_ref, far_ref,
         o_ref, m_scr, l_scr, acc_scr, s_scr, p_scr, mb_scr, skey_scr, thr_scr, planes_scr, e_scr) = refs
    else:
        (qT_ref, k_ref, vT_ref, o_ref, m_scr, l_scr, acc_scr, s_scr, p_scr, mb_scr) = refs
    qt = pl.program_id(1)
    kt = pl.program_id(2)
    nk = pl.num_programs(2)
    q_lo = p0 + qt * tq
    q_hi = p0 + jnp.minimum(qt * tq + tq, t_valid) - 1
    n_allowed = jnp.minimum((q_hi // CHUNK + 1) * CHUNK, l_valid)
    last_kt = (n_allowed - 1) // tk
    qpos = q_lo + lax.broadcasted_iota(I32, (1, tq), 1)
    qchunk = qpos // CHUNK

    def allowed_mask(k0, rows):
        kidx = k0 + lax.broadcasted_iota(I32, (rows, tq), 0)
        return (kidx // CHUNK <= qchunk) & (kidx < l_valid)

    @pl.when(kt == 0)
    def _():
        m_scr[...] = jnp.full(m_scr.shape, NEG_BIG, F32)
        l_scr[...] = jnp.zeros(l_scr.shape, F32)
        acc_scr[...] = jnp.zeros(acc_scr.shape, F32)

    if dsa:
        iscale = (C_IDX_HEADS * C_IDX_DIM) ** -0.5

        @pl.when(kt == 0)
        def _():
            wpt = tk // 32
            kw = 3 * C_IDX_DIM

            def score_tile(j, carry):
                k0 = pl.multiple_of(j * tk, tk)
                ki3 = ki3_ref[0, pl.ds(k0, tk), :]
                s = jnp.zeros((tk, tq), F32)
                for h in range(C_IDX_HEADS):
                    d = jnp.dot(ki3, q3_ref[0, h * kw:(h + 1) * kw, :], preferred_element_type=F32)
                    s = s + jnp.maximum(d, 0.0) * wiT_ref[0, h:h + 1, :]
                s = s * iscale + 0.0
                bits = pltpu.bitcast(s, I32)
                key = bits ^ ((bits >> 31) & 0x7FFFFFFF)
                key = jnp.where(allowed_mask(k0, tk), key, INT_MIN)
                skey_scr[pl.ds(k0, tk), :] = key
                ukey = key ^ INT_MIN
                w0 = pl.multiple_of(j * wpt, SUBLANES)
                for g in range(tk // 256):
                    rows = [ukey[g * 256 + 8 * i:g * 256 + 8 * i + 8, :] for i in range(32)]
                    for bi, plane in enumerate(_transpose32(rows)):
                        planes_scr[bi, pl.ds(w0 + g * SUBLANES, SUBLANES), :] = plane
                ones = jnp.full((wpt, tq), -1, I32)
                planes_scr[32, pl.ds(w0, wpt), :] = ones
                e_scr[pl.ds(w0, wpt), :] = ones
                return carry

            def blank_tile(j, carry):
                w0 = pl.multiple_of(j * wpt, SUBLANES)
                for bi in range(33):
                    planes_scr[bi, pl.ds(w0, wpt), :] = jnp.zeros((wpt, tq), I32)
                e_scr[pl.ds(w0, wpt), :] = jnp.zeros((wpt, tq), I32)
                return carry

            ntile = last_kt + 1
            lax.fori_loop(0, ntile, score_tile, 0)
            nblk = (ntile + SEARCH_TILES - 1) // SEARCH_TILES
            lax.fori_loop(ntile, nblk * SEARCH_TILES, blank_tile, 0)
            wpb = SEARCH_TILES * wpt

            def lanesum(acc):
                return jnp.sum(acc, axis=0, keepdims=True)

            def fold(pc):
                return jnp.sum(pc.reshape(wpb // SUBLANES, SUBLANES, tq), axis=0)

            def settle(e, plane, take):
                t = e & plane
                return jnp.where(take != 0, t, e ^ t)

            def bit_step(i, st):
                c_gt, th, take_prev = st
                bi = 31 - i

                def body(j, acc):
                    w0 = pl.multiple_of(j * wpb, SUBLANES)
                    e = settle(e_scr[pl.ds(w0, wpb), :], planes_scr[bi + 1, pl.ds(w0, wpb), :], take_prev)
                    e_scr[pl.ds(w0, wpb), :] = e
                    return acc + fold(lax.population_count(e & planes_scr[bi, pl.ds(w0, wpb), :]))

                c1 = lanesum(lax.fori_loop(0, nblk, body, jnp.zeros((SUBLANES, tq), I32)))
                take = (c_gt + c1) >= topk
                return (jnp.where(take, c_gt, c_gt + c1),
                        jnp.where(take, th | lax.shift_left(jnp.int32(1), bi), th), jnp.where(take, 1, 0))

            zero = jnp.zeros((1, tq), I32)
            c_gt, th_u, take0 = lax.fori_loop(0, 32, bit_step, (zero, zero, zero + 1))

            def last_body(j, acc):
                w0 = pl.multiple_of(j * wpb, SUBLANES)
                e = settle(e_scr[pl.ds(w0, wpb), :], planes_scr[0, pl.ds(w0, wpb), :], take0)
                e_scr[pl.ds(w0, wpb), :] = e
                return acc + fold(lax.population_count(e))

            n_eq = lanesum(lax.fori_loop(0, nblk, last_body, jnp.zeros((SUBLANES, tq), I32)))
            need = topk - c_gt
            nbits = max(1, (l_valid - 1).bit_length())
            thr_scr[0:1, :] = th_u ^ INT_MIN
            thr_scr[1:2, :] = jnp.full((1, tq), (1 << nbits) - 1, I32)

            @pl.when(jnp.max((n_eq - need).astype(F32)) > 0.0)
            def _():
                def idx_step(i, jb):
                    cand = jb - lax.shift_left(jnp.int32(1), nbits - 1 - i)

                    def body(j, acc):
                        w0 = pl.multiple_of(j * wpb, SUBLANES)
                        wr = w0 + lax.broadcasted_iota(I32, (wpb, tq), 0)
                        base = (wr >> 3) * 256 + (wr & 7)
                        mx = (cand - base) >> 3
                        low = jnp.left_shift(2, jnp.clip(mx, 0, 30)) - 1
                        msk = jnp.where(mx < 0, 0, jnp.where(mx >= 31, -1, low))
                        return acc + fold(lax.population_count(e_scr[pl.ds(w0, wpb), :] & msk))

                    cnt = lanesum(lax.fori_loop(0, nblk, body, jnp.zeros((SUBLANES, tq), I32)))
                    return jnp.where(cnt >= need, cand, jb)

                thr_scr[1:2, :] = lax.fori_loop(0, nbits, idx_step, jnp.full((1, tq), (1 << nbits) - 1, I32))

    def tile(near):
        k0 = pl.multiple_of(kt * tk, tk)
        nchunk = tk // RC
        masked = dsa or near
        if masked:
            for c in range(nchunk):
                r0 = k0 + c * RC
                mask = allowed_mask(r0, RC) if near else None
                if dsa:
                    x = skey_scr[pl.ds(r0, RC), :]
                    kidx = r0 + lax.broadcasted_iota(I32, (RC, tq), 0)
                    sel = (x > thr_scr[0:1, :]) | ((x == thr_scr[0:1, :]) & (kidx <= thr_scr[1:2, :]))
                    mask = sel & mask if near else sel
                mb_scr[c * RC:(c + 1) * RC, :] = jnp.where(mask, 0.0, NEG_BIG)

        def chunk_bias(h, c):
            far = far_ref[0:1, h:h + 1]
            if not near:
                return far
            sb, off = (c * RC) // LANES, (c * RC) % LANES
            cols = []
            for qb in range(tq // LANES):
                delta = k0 + sb * LANES - (q_lo + qb * LANES)
                cols.append(jnp.where(delta == 0, tab_ref[0, h, off:off + RC, :],
                                      jnp.where(delta == -LANES, tab_ref[1, h, off:off + RC, :], far)))
            return jnp.concatenate(cols, axis=1) if len(cols) > 1 else cols[0]

        for h in range(nh):
            ks = slice(h * dk, (h + 1) * dk)
            s_scr[h] = jnp.dot(k_ref[0, :, ks], qT_ref[0, ks, :], preferred_element_type=F32)
        m_news, alphas = [], []
        for h in range(nh):
            mx = jnp.full((SUBLANES, tq), NEG_BIG, F32)
            for c in range(nchunk):
                rows = slice(c * RC, (c + 1) * RC)
                blk = s_scr[h, rows, :]
                if masked:
                    blk = blk + mb_scr[rows, :]
                    if dsa:
                        blk = blk + chunk_bias(h, c)
                    s_scr[h, rows, :] = blk
                mx = jnp.maximum(mx, jnp.max(blk.reshape(RC // SUBLANES, SUBLANES, tq), axis=0))
            m_prev = m_scr[h:h + 1, :]
            m_new = jnp.maximum(m_prev, jnp.max(mx, axis=0, keepdims=True))
            m_scr[h:h + 1, :] = m_new
            m_news.append(m_new)
            alphas.append(jnp.exp2(m_prev - m_new))
        for h in range(nh):
            lsum = jnp.zeros((SUBLANES, tq), F32)
            for c in range(nchunk):
                rows = slice(c * RC, (c + 1) * RC)
                pr = jnp.exp2(s_scr[h, rows, :] - m_news[h])
                lsum = lsum + jnp.sum(pr.reshape(RC // SUBLANES, SUBLANES, tq), axis=0)
                p_scr[h, rows, :] = pr.astype(BF16)
            l_scr[h:h + 1, :] = alphas[h] * l_scr[h:h + 1, :] + jnp.sum(lsum, axis=0, keepdims=True)
        for h in range(nh):
            sl = slice(h * dh, (h + 1) * dh)
            pv = jnp.dot(vT_ref[0, sl, :], p_scr[h], preferred_element_type=F32)
            acc_scr[sl, :] = alphas[h] * acc_scr[sl, :] + pv

    is_far = kt * tk + tk - 1 <= q_lo - LANES

    @pl.when((kt <= last_kt) & is_far)
    def _():
        tile(False)

    @pl.when((kt <= last_kt) & jnp.logical_not(is_far))
    def _():
        tile(True)

    @pl.when(kt == nk - 1)
    def _():
        for h in range(nh):
            sl = slice(h * dh, (h + 1) * dh)
            o = acc_scr[sl, :] / l_scr[h:h + 1, :]
            o_ref[0, :, sl] = o.T.astype(o_ref.dtype)


def _attention(mode, qT, k, vT, extra, *, p0, t_valid, l_valid, tk):
    b, hdk, tpad = qT.shape
    lpad = k.shape[1]
    hd = vT.shape[1]
    nh = hd // LANES
    dk = hdk // nh
    tq = 2 * LANES if tpad % (2 * LANES) == 0 else LANES
    nq, nk = tpad // tq, lpad // tk
    topk = min(TOPK_MAX, l_valid // 4)
    assert p0 % LANES == 0 and lpad % tk == 0 and tk % 256 == 0 and l_valid >= topk >= 1 and tk >= topk

    def last_kt(qt):
        q_hi = p0 + jnp.minimum(qt * tq + tq, t_valid) - 1
        return (jnp.minimum((q_hi // CHUNK + 1) * CHUNK, l_valid) - 1) // tk

    kmap = lambda i, q, kk: (i, jnp.minimum(kk, last_kt(q)), 0)
    vmap_ = lambda i, q, kk: (i, 0, jnp.minimum(kk, last_kt(q)))
    qmap = lambda i, q, kk: (i, 0, q)
    in_specs = [pl.BlockSpec((1, hdk, tq), qmap),
                pl.BlockSpec((1, tk, hdk), kmap),
                pl.BlockSpec((1, hd, tk), vmap_)]
    scratch = [pltpu.VMEM((nh, tq), F32), pltpu.VMEM((nh, tq), F32), pltpu.VMEM((hd, tq), F32),
               pltpu.VMEM((nh, tk, tq), F32), pltpu.VMEM((nh, tk, tq), BF16), pltpu.VMEM((tk, tq), F32)]
    if mode == "dsa":
        q3, ki3, wiT, tabs, far = extra
        wrows = -(-nk // SEARCH_TILES) * SEARCH_TILES * (tk // 32)
        in_specs += [pl.BlockSpec((1, q3.shape[1], tq), qmap),
                     pl.BlockSpec((1, lpad, ki3.shape[2]), lambda i, q, kk: (i, 0, 0)),
                     pl.BlockSpec((1, C_IDX_HEADS, tq), qmap),
                     pl.BlockSpec(tabs.shape, lambda i, q, kk: (0, 0, 0, 0)),
                     pl.BlockSpec(far.shape, lambda i, q, kk: (0, 0))]
        scratch += [pltpu.VMEM((lpad, tq), I32), pltpu.VMEM((SUBLANES, tq), I32),
                    pltpu.VMEM((33, wrows, tq), I32), pltpu.VMEM((wrows, tq), I32)]
    return pl.pallas_call(
        functools.partial(_attn_body, mode=mode, p0=p0, t_valid=t_valid, l_valid=l_valid, tq=tq, tk=tk,
                          dk=dk, topk=topk, nh=nh),
        grid=(b, nq, nk),
        in_specs=in_specs,
        out_specs=pl.BlockSpec((1, tq, hd), lambda i, q, kk: (i, q, 0)),
        out_shape=jax.ShapeDtypeStruct((b, tpad, hd), BF16),
        scratch_shapes=scratch,
        compiler_params=_cparams(("parallel", "parallel", "arbitrary")),
        name="attn_" + mode,
    )(qT, k, vT, *extra)


PACK_TILE = 256


def _pack_body(cache_ref, new_ref, o_ref, *, n_cache, transpose, nh):
    j = pl.program_id(1)

    def emit(head):
        for h in range(nh):
            sl = slice(h * LANES, (h + 1) * LANES)
            x = head(h, sl)
            if transpose:
                o_ref[0, sl, :] = x.T.astype(o_ref.dtype)
            else:
                o_ref[0, :, sl] = x.astype(o_ref.dtype)

    @pl.when(j < n_cache)
    def _():
        emit(lambda h, sl: cache_ref[0, 0, :, h, :])

    @pl.when(j >= n_cache)
    def _():
        emit(lambda h, sl: new_ref[0, :, sl])


def _pack_keys(cache, layer, new, lpad, *, transpose):
    _, b, past, nh, dh = cache.shape
    tp = PACK_TILE
    assert dh == LANES and past % tp == 0 and lpad % tp == 0
    n_cache = past // tp
    hd = nh * dh
    new = _pad_axis(new, 1, lpad - past)
    out_shape = (b, hd, lpad) if transpose else (b, lpad, hd)
    out_spec = (pl.BlockSpec((1, hd, tp), lambda i, j: (i, 0, j)) if transpose
                else pl.BlockSpec((1, tp, hd), lambda i, j: (i, j, 0)))
    return pl.pallas_call(
        functools.partial(_pack_body, n_cache=n_cache, transpose=transpose, nh=nh),
        grid=(b, lpad // tp),
        in_specs=[pl.BlockSpec((1, 1, tp, nh, dh),
                               lambda i, j: (layer, i, jnp.minimum(j, n_cache - 1), 0, 0)),
                  pl.BlockSpec((1, tp, hd), lambda i, j: (i, jnp.maximum(j - n_cache, 0), 0))],
        out_specs=out_spec,
        out_shape=jax.ShapeDtypeStruct(out_shape, BF16),
        compiler_params=_cparams(("parallel", "parallel")),
        name="pack_keys",
    )(cache, new)


def _split_hi_lo(x):
    hi = x.astype(BF16)
    lo = (x - hi.astype(F32)).astype(BF16)
    return hi, lo


def _pad_axis(x, axis, size):
    if x.shape[axis] == size:
        return x
    pad = [(0, 0)] * x.ndim
    pad[axis] = (0, size - x.shape[axis])
    return jnp.pad(x, pad)


def _key_tile(l_valid):
    lpad = -(-l_valid // 256) * 256
    for tk in (512, 256):
        if lpad % tk == 0:
            return lpad, tk
    return lpad, 256


def _prep_weights(W, d_model):
    f = {}
    wd = A_WIDTH
    n_even = W["ev_w_in"].shape[0]
    n_odd = W["od_w_in"].shape[0]
    f["even"] = []
    for e in range(n_even):
        wi = W["ev_w_in"][e]
        w_in = jnp.concatenate([wi[:, :3 * wd], wi[:, A_COLS:], wi[:, 3 * wd:A_COLS],
                                jnp.zeros((d_model, LORA_PAD - A_LORA), F32)], axis=1).astype(BF16)
        mu = W["rwkv_mu"][e]
        w2 = jnp.zeros((LANES, wd), F32).at[:A_DECAY_LORA].set(W["rwkv_w2"][e])
        a2 = jnp.zeros((LANES, wd), F32).at[A_DECAY_LORA:A_DECAY_LORA + A_ICLR_LORA].set(W["rwkv_a2"][e])
        g2 = jnp.zeros((LORA_PAD - LANES, wd), F32).at[:A_GATE_LORA].set(W["rwkv_g2"][e])
        hm = lambda v: v.reshape(A_HEADS, 1, A_HEAD_DIM)
        rw = dict(mu_r=mu[:3 * wd].reshape(1, -1),
                  mu_l=_pad_axis(mu[3 * wd:], 0, LORA_PAD).reshape(1, -1),
                  w0=W["rwkv_w0"][e].reshape(1, -1), w2=w2.astype(BF16),
                  a0=W["rwkv_a0"][e].reshape(1, -1), a2=a2.astype(BF16), g2=g2.astype(BF16),
                  k_k=W["rwkv_k_k"][e].reshape(1, -1), k_a=W["rwkv_k_a"][e].reshape(1, -1),
                  r_k=hm(W["rwkv_r_k"][e]), ln_w=hm(W["rwkv_ln_w"][e]), ln_b=hm(W["rwkv_ln_b"][e]))
        eye = jnp.eye(B_BLOCKS, dtype=F32)
        blockdiag = lambda w: (eye[:, None, :, None] * w[:, :, None, :]).reshape(B_WIDTH, B_WIDTH)
        wgate = jnp.concatenate([blockdiag(W["lru_wa"][e]), blockdiag(W["lru_wx"][e])], axis=1).astype(BF16)
        bgate = jnp.concatenate([W["lru_ba"][e], W["lru_bx"][e]])
        wo = W["ev_w_out"][e].astype(BF16)
        f["even"].append(dict(w_in=w_in, rw=rw, wgate=wgate, bgate=bgate, wo_a=wo[:wd], wo_b=wo[wd:],
                              cw=W["lru_conv_w"][e], cb=W["lru_conv_b"][e], lam=W["lru_lambda"][e]))
    f["odd"] = []
    cw_ = C_WIDTH
    qi_w = C_IDX_HEADS * C_IDX_DIM
    for o in range(n_odd):
        wi = W["od_w_in"][o]
        offs = [0]
        for s in (cw_, cw_, cw_, qi_w, C_IDX_DIM, C_IDX_HEADS, D_Q_RANK, D_KV_RANK, D_ROPE):
            offs.append(offs[-1] + s)
        q, k, v, qi, ki, wi_, qd, kvd, kr = [wi[:, offs[i]:offs[i + 1]] for i in range(9)]
        small = jnp.concatenate([ki, kr, wi_], axis=1)
        w_in = jnp.concatenate([q, k, v, qi, qd, kvd, _pad_axis(small, 1, 512)], axis=1).astype(BF16)
        wuq = W["mla_w_uq"][o].reshape(D_Q_RANK, D_HEADS, D_NOPE + D_ROPE)
        wuq = jnp.concatenate([wuq[:, :, :D_NOPE].reshape(D_Q_RANK, -1),
                               wuq[:, :, D_NOPE:].reshape(D_Q_RANK, -1)], axis=1).astype(BF16)
        wukv = W["mla_w_ukv"][o].reshape(D_KV_RANK, D_HEADS, D_NOPE + D_V)
        wuk = _pad_axis(wukv[:, :, :D_NOPE], 2, MLA_DK).reshape(D_KV_RANK, -1).astype(BF16)
        wuv = wukv[:, :, D_NOPE:].reshape(D_KV_RANK, -1).astype(BF16)
        wo = W["od_w_out"][o].astype(BF16)
        f["odd"].append(dict(w_in=w_in, wuq=wuq, wuk=wuk, wuv=wuv, wo_c=wo[:cw_], wo_d=wo[cw_:],
                             gq=W["mla_q_norm"][o], gkv=W["mla_kv_norm"][o]))
    f["ffn"] = dict(wg=W["ffn_w_gate"].astype(BF16), wu=W["ffn_w_up"].astype(BF16),
                    wd=W["ffn_w_down"].astype(BF16))
    f["tabs"] = _bias_tables(W["rel_bias"])
    nb = N_BUCKETS // 2
    f["far"] = _pad_axis(W["rel_bias"][nb - 1:nb, :] * LOG2E, 1, LANES)
    return f


def _mixer_even(x2, b, t, p0, shift, s0, h0, cbuf, fe, norm_g):
    wd = A_WIDTH
    p = _norm_matmul(x2, norm_g, fe["w_in"])
    p3 = p.reshape(b, t, -1)
    lora_col = 3 * wd + 2 * B_WIDTH
    shift_r = shift[:, None, :3 * wd]
    shift_l = _pad_axis(shift[:, None, 3 * wd:], 2, LORA_PAD)
    ya, s_new = _rwkv(p3, lora_col // LORA_PAD, shift_r, shift_l, s0, fe["rw"])
    yb, h_new, c_new = _rglru(p3, 3, 4, fe["cw"], fe["cb"], fe["wgate"], fe["bgate"], fe["lam"],
                              h0, cbuf, p0=p0)
    x2 = _matmul([ya.reshape(b * t, wd), yb.reshape(b * t, B_WIDTH)], [fe["wo_a"], fe["wo_b"]], res=x2)
    last = p3[:, t - 1]
    new_shift = jnp.concatenate([last[:, :3 * wd], last[:, lora_col:lora_col + A_LORA]], axis=-1)
    return x2, new_shift, s_new, h_new[:, 0], c_new


def _mixer_odd(x2, b, t, p0, o, dsa_k, dsa_v, cik, clat, ckr, fo, f, norm_g, cos, sin):
    cw_ = C_WIDTH
    p = _norm_matmul(x2, norm_g, fo["w_in"])
    p3 = p.reshape(b, t, -1)
    past = dsa_k.shape[2]
    l_valid = past + t
    lpad, tk = _key_tile(l_valid)
    tpad = -(-t // LANES) * LANES
    q = p3[..., :cw_]
    k_new = p3[..., cw_:2 * cw_]
    v_new = p3[..., 2 * cw_:3 * cw_]
    qi = p3[..., 3 * cw_:3 * cw_ + 512]
    small = p3[..., 3 * cw_ + 1536:]
    ki_new = small[..., :C_IDX_DIM]
    wi = small[..., 2 * C_IDX_DIM:2 * C_IDX_DIM + C_IDX_HEADS]

    def keys(cache, new):
        allk = jnp.concatenate([cache.reshape(b, past, -1), new], axis=1) if past else new
        return _pad_axis(allk, 1, lpad)

    tq_ = lambda z: _pad_axis(jnp.swapaxes(z, 1, 2), 2, tpad)
    if past and past % PACK_TILE == 0:
        k_all = _pack_keys(dsa_k, o, k_new, lpad, transpose=False)
        vT_all = _pack_keys(dsa_v, o, v_new, lpad, transpose=True)
    else:
        k_all = keys(dsa_k[o], k_new).astype(BF16)
        vT_all = jnp.swapaxes(keys(dsa_v[o], v_new), 1, 2).astype(BF16)
    kih, kil = _split_hi_lo(keys(cik, ki_new))
    ki3 = jnp.concatenate([kih, kih, kil], axis=-1)
    qih, qil = _split_hi_lo(tq_(qi).reshape(b, C_IDX_HEADS, C_IDX_DIM, tpad))
    q3 = jnp.concatenate([qih, qil, qih], axis=2).reshape(b, 3 * C_IDX_HEADS * C_IDX_DIM, tpad)
    yc = _attention("dsa", tq_(q * (C_HEAD_DIM ** -0.5 * LOG2E)).astype(BF16), k_all, vT_all,
                    (q3, ki3, tq_(wi), f["tabs"], f["far"]),
                    p0=p0, t_valid=t, l_valid=l_valid, tk=tk)[:, :t]
    qn, qr, lat, krope = _mla_prep(p3, 7, 8, 9, fo["gq"], fo["gkv"], fo["wuq"], cos, sin)
    lat_all = keys(clat, lat).reshape(b * lpad, -1)
    kr_all = keys(ckr, krope).reshape(b * lpad, -1)
    eye = jnp.eye(D_ROPE, dtype=BF16)
    ident = jnp.tile(jnp.pad(eye, ((0, 0), (D_NOPE, MLA_DK - D_NOPE - D_ROPE))), (1, D_HEADS))
    k_full = _matmul([lat_all, kr_all], [fo["wuk"], ident], out_dtype=BF16).reshape(b, lpad, -1)
    v_all = _matmul([lat_all], [fo["wuv"]], out_dtype=BF16).reshape(b, lpad, -1)
    q_full = jnp.concatenate([qn.reshape(b, t, D_HEADS, D_NOPE), qr.reshape(b, t, D_HEADS, D_ROPE),
                              jnp.zeros((b, t, D_HEADS, MLA_DK - D_NOPE - D_ROPE), BF16)], axis=-1)
    yd = _attention("mla", tq_(q_full.reshape(b, t, -1)), k_full, jnp.swapaxes(v_all, 1, 2), (),
                    p0=p0, t_valid=t, l_valid=l_valid, tk=tk)[:, :t]
    x2 = _matmul([yc.reshape(b * t, cw_), yd.reshape(b * t, -1)], [fo["wo_c"], fo["wo_d"]], res=x2)
    return (x2, k_new.reshape(b, t, C_HEADS, C_HEAD_DIM), v_new.reshape(b, t, C_HEADS, C_HEAD_DIM),
            ki_new, lat, krope)


def _trunk(x, p0, shift, rwkv_s, lru_h, lru_conv, dsa_k, dsa_v, dsa_ik, mla_lat, mla_kr, W, f):
    b, t, d = x.shape
    depth = W["norm_mix"].shape[0]
    pos = (p0 + jnp.arange(t)).astype(F32)
    inv = ROPE_BASE ** (-jnp.arange(0, D_ROPE, 2, dtype=F32) / D_ROPE)
    ang = pos[:, None] * inv[None, :]
    cos = jnp.tile(jnp.cos(ang), (1, 2 * D_HEADS))
    sin = jnp.tile(jnp.sin(ang), (1, 2 * D_HEADS))
    x2 = x.reshape(b * t, d)
    ev = [[] for _ in range(4)]
    od = [[] for _ in range(5)]
    for layer in range(depth):
        if layer % 2 == 0:
            e = layer // 2
            x2, *outs = _mixer_even(x2, b, t, p0, shift[e], rwkv_s[e], lru_h[e], lru_conv[e],
                                    f["even"][e], W["norm_mix"][layer])
            for lst, o_ in zip(ev, outs):
                lst.append(o_)
        else:
            o = layer // 2
            x2, *outs = _mixer_odd(x2, b, t, p0, o, dsa_k, dsa_v, dsa_ik[o], mla_lat[o], mla_kr[o],
                                   f["odd"][o], f, W["norm_mix"][layer], cos, sin)
            for lst, o_ in zip(od, outs):
                lst.append(o_)
        ff = f["ffn"]
        x2 = _ffn(x2, W["norm_ffn"][layer], ff["wg"], ff["wu"], ff["wd"], layer, W["final_norm"],
                  final_norm=layer == depth - 1)
    return (x2.reshape(b, t, d),) + tuple(jnp.stack(v) for v in ev) + tuple(jnp.stack(v) for v in od)


def kernel(x_prompt, x_sample, state_rwkv_shift, state_rwkv, state_lru, state_lru_conv, cache_dsa_k, cache_dsa_v, cache_dsa_idx_k, cache_mla_latent, cache_mla_krope, rel_bias, final_norm, norm_mix, norm_ffn, ffn_w_gate, ffn_w_up, ffn_w_down, ev_w_in, ev_w_out, rwkv_mu, rwkv_w0, rwkv_w2, rwkv_a0, rwkv_a2, rwkv_g2, rwkv_k_k, rwkv_k_a, rwkv_r_k, rwkv_ln_w, rwkv_ln_b, lru_conv_w, lru_conv_b, lru_wa, lru_ba, lru_wx, lru_bx, lru_lambda, od_w_in, od_w_out, mla_q_norm, mla_w_uq, mla_kv_norm, mla_w_ukv):
    W = dict(rel_bias=rel_bias, final_norm=final_norm, norm_mix=norm_mix, norm_ffn=norm_ffn,
             ffn_w_gate=ffn_w_gate, ffn_w_up=ffn_w_up, ffn_w_down=ffn_w_down,
             ev_w_in=ev_w_in, ev_w_out=ev_w_out, rwkv_mu=rwkv_mu, rwkv_w0=rwkv_w0,
             rwkv_w2=rwkv_w2, rwkv_a0=rwkv_a0, rwkv_a2=rwkv_a2, rwkv_g2=rwkv_g2,
             rwkv_k_k=rwkv_k_k, rwkv_k_a=rwkv_k_a, rwkv_r_k=rwkv_r_k, rwkv_ln_w=rwkv_ln_w,
             rwkv_ln_b=rwkv_ln_b, lru_conv_w=lru_conv_w, lru_conv_b=lru_conv_b, lru_wa=lru_wa,
             lru_ba=lru_ba, lru_wx=lru_wx, lru_bx=lru_bx, lru_lambda=lru_lambda,
             od_w_in=od_w_in, od_w_out=od_w_out, mla_q_norm=mla_q_norm, mla_w_uq=mla_w_uq,
             mla_kv_norm=mla_kv_norm, mla_w_ukv=mla_w_ukv)
    d_model = x_prompt.shape[-1]
    f = _prep_weights(W, d_model)
    bp = x_prompt.shape[0]
    n_even, n_odd = ev_w_in.shape[0], od_w_in.shape[0]
    dt = x_prompt.dtype
    z = lambda *shape: jnp.zeros(shape, dt)
    outs_p = _trunk(
        x_prompt, 0,
        z(n_even, bp, A_COLS), z(n_even, bp, A_HEADS, A_HEAD_DIM, A_HEAD_DIM),
        z(n_even, bp, B_WIDTH), z(n_even, bp, B_CONV - 1, B_WIDTH),
        z(n_odd, bp, 0, C_HEADS, C_HEAD_DIM), z(n_odd, bp, 0, C_HEADS, C_HEAD_DIM),
        z(n_odd, bp, 0, C_IDX_DIM), z(n_odd, bp, 0, D_KV_RANK), z(n_odd, bp, 0, D_ROPE), W, f)
    past = cache_dsa_k.shape[2]
    outs_s = _trunk(x_sample, past, state_rwkv_shift, state_rwkv, state_lru, state_lru_conv,
                    cache_dsa_k, cache_dsa_v, cache_dsa_idx_k, cache_mla_latent, cache_mla_krope, W, f)
    return (outs_p[0], outs_s[0]) + tuple(outs_p[1:]) + tuple(outs_s[1:])
```

```python
import functools
import math

import jax
import jax.numpy as jnp
from jax import lax
from jax.experimental import pallas as pl
from jax.experimental.pallas import tpu as pltpu

F32 = jnp.float32
BF16 = jnp.bfloat16
I32 = jnp.int32

CHUNK = 64
NORM_EPS = 1e-6
A_HEADS = 16
A_HEAD_DIM = 64
A_WIDTH = A_HEADS * A_HEAD_DIM
A_DECAY_LORA = 64
A_ICLR_LORA = 64
A_GATE_LORA = 160
A_LORA = A_DECAY_LORA + A_ICLR_LORA + A_GATE_LORA
A_COLS = 3 * A_WIDTH + A_LORA
A_LN_EPS = 64e-5
B_WIDTH = 1024
B_BLOCKS = 16
B_CONV = 4
B_C = 8.0
C_HEADS = 8
C_HEAD_DIM = 128
C_WIDTH = C_HEADS * C_HEAD_DIM
C_IDX_HEADS = 8
C_IDX_DIM = 64
TOPK_MAX = 256
D_HEADS = 8
D_NOPE = 128
D_ROPE = 64
D_V = 128
D_Q_RANK = 512
D_KV_RANK = 512
ROPE_BASE = 10000.0
N_BUCKETS = 32
MAX_DISTANCE = 128

LANES = 128
SUBLANES = 8
VMEM_LIMIT = 56 * 1024 * 1024
LORA_PAD = 512
NEG_BIG = -1e30
INT_MIN = -2147483648
RC = 64
SEARCH_TILES = 4
VT_ONES = 16
VT_ROWS = LANES + VT_ONES
LOG2E = 1.4426950408889634
MLA_DK = 256

NN = (((1,), (0,)), ((), ()))
NT = (((1,), (1,)), ((), ()))
BNN = (((2,), (1,)), ((0,), (0,)))
BNT = (((2,), (2,)), ((0,), (0,)))
BTN = (((1,), (1,)), ((0,), (0,)))


def _cparams(sem):
    return pltpu.CompilerParams(dimension_semantics=sem, vmem_limit_bytes=VMEM_LIMIT)


def _rms(x, g, eps=NORM_EPS):
    ms = jnp.mean(x * x, axis=-1, keepdims=True)
    return x * lax.rsqrt(ms + eps) * g


def _softplus(x):
    return jnp.maximum(x, 0.0) + jnp.log1p(jnp.exp(-jnp.abs(x)))


def _dotp(a, b, dims, passes):
    if passes == 6:
        return lax.dot_general(a, b, dims, precision=lax.Precision.HIGHEST, preferred_element_type=F32)
    ah = a.astype(BF16)
    bh = b.astype(BF16)
    out = lax.dot_general(ah, bh, dims, preferred_element_type=F32)
    if passes == 3:
        al = (a - ah.astype(F32)).astype(BF16)
        bl = (b - bh.astype(F32)).astype(BF16)
        out = out + lax.dot_general(ah, bl, dims, preferred_element_type=F32)
        out = out + lax.dot_general(al, bh, dims, preferred_element_type=F32)
    return out


def _norm_matmul_body(x_ref, g_ref, w_ref, o_ref, xn_ref):
    @pl.when(pl.program_id(1) == 0)
    def _():
        xn_ref[...] = _rms(x_ref[...], g_ref[...]).astype(BF16)

    o_ref[...] = jnp.dot(xn_ref[...], w_ref[...], preferred_element_type=F32)


def _row_tile(m, cap):
    tm = cap
    while m % tm:
        tm //= 2
    return tm


def _norm_matmul(x, g, w, *, tn=512):
    m, k = x.shape
    n = w.shape[1]
    tm = _row_tile(m, 1024)
    return pl.pallas_call(
        _norm_matmul_body,
        grid=(m // tm, n // tn),
        in_specs=[pl.BlockSpec((tm, k), lambda i, j: (i, 0)),
                  pl.BlockSpec((1, k), lambda i, j: (0, 0)),
                  pl.BlockSpec((k, tn), lambda i, j: (0, j))],
        out_specs=pl.BlockSpec((tm, tn), lambda i, j: (i, j)),
        out_shape=jax.ShapeDtypeStruct((m, n), F32),
        scratch_shapes=[pltpu.VMEM((tm, k), BF16)],
        compiler_params=_cparams(("parallel", "arbitrary")),
        name="norm_matmul",
    )(x, g.reshape(1, k), w)


def _mm_body(*refs, n_lhs, has_res):
    o_ref = refs[-1]
    acc = refs[2 * n_lhs][...] if has_res else None
    for a_ref, w_ref in zip(refs[:n_lhs], refs[n_lhs:2 * n_lhs]):
        d = jnp.dot(a_ref[...].astype(BF16), w_ref[...], preferred_element_type=F32)
        acc = d if acc is None else acc + d
    o_ref[...] = acc.astype(o_ref.dtype)


def _matmul(lhs_list, w_list, res=None, *, tn=512, out_dtype=F32):
    m = lhs_list[0].shape[0]
    n = w_list[0].shape[1]
    tm = _row_tile(m, 1024)
    in_specs = [pl.BlockSpec((tm, a.shape[1]), lambda i, j: (i, 0)) for a in lhs_list]
    in_specs += [pl.BlockSpec((w.shape[0], tn), lambda i, j: (0, j)) for w in w_list]
    args = list(lhs_list) + list(w_list)
    if res is not None:
        in_specs.append(pl.BlockSpec((tm, tn), lambda i, j: (i, j)))
        args.append(res)
    return pl.pallas_call(
        functools.partial(_mm_body, n_lhs=len(lhs_list), has_res=res is not None),
        grid=(m // tm, n // tn),
        in_specs=in_specs,
        out_specs=pl.BlockSpec((tm, tn), lambda i, j: (i, j)),
        out_shape=jax.ShapeDtypeStruct((m, n), out_dtype),
        compiler_params=_cparams(("parallel", "arbitrary")),
        name="matmul",
    )(*args)


def _ffn_body(x_ref, g_ref, wg_ref, wu_ref, wd_ref, gf_ref, o_ref, xn_ref, *, final_norm):
    f = pl.program_id(1)

    @pl.when(f == 0)
    def _():
        x = x_ref[...]
        xn_ref[...] = _rms(x, g_ref[...]).astype(BF16)
        o_ref[...] = x

    xn = xn_ref[...]
    hg = jnp.dot(xn, wg_ref[0], preferred_element_type=F32)
    hu = jnp.dot(xn, wu_ref[0], preferred_element_type=F32)
    h = hg * jax.nn.sigmoid(hg) * hu
    o_ref[...] += jnp.dot(h.astype(BF16), wd_ref[0], preferred_element_type=F32)

    if final_norm:
        @pl.when(f == pl.num_programs(1) - 1)
        def _():
            o_ref[...] = _rms(o_ref[...], gf_ref[...])


def _ffn(x, g, wg, wu, wd, layer, gf, *, final_norm, tf=512):
    m, k = x.shape
    dff = wg.shape[2]
    tm = _row_tile(m, 1024)
    return pl.pallas_call(
        functools.partial(_ffn_body, final_norm=final_norm),
        grid=(m // tm, dff // tf),
        in_specs=[pl.BlockSpec((tm, k), lambda i, f: (i, 0), pipeline_mode=pl.Buffered(1)),
                  pl.BlockSpec((1, k), lambda i, f: (0, 0)),
                  pl.BlockSpec((1, k, tf), lambda i, f: (layer, 0, f)),
                  pl.BlockSpec((1, k, tf), lambda i, f: (layer, 0, f)),
                  pl.BlockSpec((1, tf, k), lambda i, f: (layer, f, 0)),
                  pl.BlockSpec((1, k), lambda i, f: (0, 0))],
        out_specs=pl.BlockSpec((tm, k), lambda i, f: (i, 0)),
        out_shape=jax.ShapeDtypeStruct((m, k), F32),
        scratch_shapes=[pltpu.VMEM((tm, k), BF16)],
        compiler_params=_cparams(("parallel", "arbitrary")),
        name="ffn",
    )(x, g.reshape(1, k), wg, wu, wd, gf.reshape(1, k))


def _lru_body(gate_ref, xb_ref, cw_ref, cb_ref, wg_ref, bg_ref, lam_ref, h0_ref, cbuf_ref,
              y_ref, hout_ref, cout_ref, xbuf, hcar, *, tt, p0):
    t = pl.program_id(1)
    w = B_WIDTH
    halo = SUBLANES

    @pl.when(t == 0)
    def _():
        xbuf[0:halo, :] = jnp.zeros((halo, w), F32)
        xbuf[halo - (B_CONV - 1):halo, :] = cbuf_ref[0]
        hcar[...] = h0_ref[0]

    xb = xb_ref[0]
    xbuf[halo:halo + tt, :] = xb
    xc = cb_ref[...] + cw_ref[B_CONV - 1:B_CONV, :] * xb
    for j in range(B_CONV - 1):
        off = halo - (B_CONV - 1) + j
        xc = xc + cw_ref[j:j + 1, :] * xbuf[off:off + tt, :]
    tail = xbuf[tt + halo - (B_CONV - 1):tt + halo, :]
    xbuf[halo - (B_CONV - 1):halo, :] = tail

    pre = jnp.dot(xc.astype(BF16), wg_ref[...], preferred_element_type=F32) + bg_ref[...]
    rg = jax.nn.sigmoid(pre[:, :w])
    ig = jax.nn.sigmoid(pre[:, w:])
    log_a = (-B_C) * rg * _softplus(-lam_ref[...])
    a = jnp.exp(log_a)
    row = lax.broadcasted_iota(I32, (tt, w), 0)
    th = jnp.tanh(log_a)
    mult = jnp.sqrt(-2.0 * th / (1.0 - th))
    mult = jnp.where(row + (p0 + t * tt) == 0, 1.0, mult)
    u = mult * (ig * xc)

    d = 1
    while d < tt:
        keep = row >= d
        a_sh = pltpu.roll(a, d, 0)
        u_sh = pltpu.roll(u, d, 0)
        u = u + jnp.where(keep, a * u_sh, 0.0)
        a = jnp.where(keep, a * a_sh, a)
        d *= 2
    h = u + a * hcar[...]
    hcar[...] = h[tt - 1:tt, :]
    y_ref[0] = (h * jax.nn.gelu(gate_ref[0])).astype(y_ref.dtype)

    @pl.when(t == pl.num_programs(1) - 1)
    def _():
        hout_ref[0] = h[tt - 1:tt, :]
        cout_ref[0] = tail


def _rglru(p3, gate_blk, xb_blk, cw, cb, wgate, bgate, lam, h0, cbuf, *, p0):
    b, t, _ = p3.shape
    w = B_WIDTH
    tt = min(256, t)
    row = lambda v: v.reshape(1, -1)
    return pl.pallas_call(
        functools.partial(_lru_body, tt=tt, p0=p0),
        grid=(b, t // tt),
        in_specs=[pl.BlockSpec((1, tt, w), lambda i, j: (i, j, gate_blk)),
                  pl.BlockSpec((1, tt, w), lambda i, j: (i, j, xb_blk)),
                  pl.BlockSpec((B_CONV, w), lambda i, j: (0, 0)),
                  pl.BlockSpec((1, w), lambda i, j: (0, 0)),
                  pl.BlockSpec((w, 2 * w), lambda i, j: (0, 0)),
                  pl.BlockSpec((1, 2 * w), lambda i, j: (0, 0)),
                  pl.BlockSpec((1, w), lambda i, j: (0, 0)),
                  pl.BlockSpec((1, 1, w), lambda i, j: (i, 0, 0)),
                  pl.BlockSpec((1, B_CONV - 1, w), lambda i, j: (i, 0, 0))],
        out_specs=[pl.BlockSpec((1, tt, w), lambda i, j: (i, j, 0)),
                   pl.BlockSpec((1, 1, w), lambda i, j: (i, 0, 0)),
                   pl.BlockSpec((1, B_CONV - 1, w), lambda i, j: (i, 0, 0))],
        out_shape=[jax.ShapeDtypeStruct((b, t, w), BF16),
                   jax.ShapeDtypeStruct((b, 1, w), F32),
                   jax.ShapeDtypeStruct((b, B_CONV - 1, w), F32)],
        scratch_shapes=[pltpu.VMEM((tt + SUBLANES, w), F32), pltpu.VMEM((1, w), F32)],
        compiler_params=_cparams(("parallel", "arbitrary")),
        name="rglru",
    )(p3, p3, cw, row(cb), wgate, row(bgate), row(lam), h0.reshape(b, 1, w), cbuf)


RWKV_PASSES = 1
RWKV_PASSES_SOLVE = 3


def _rwkv_body(rkv_ref, lora_ref, shr_ref, shl_ref, s0_ref, mur_ref, mul_ref, w0_ref, w2_ref, a0_ref,
               a2_ref, g2_ref, kk_ref, ka_ref, rk_ref, lnw_ref, lnb_ref, tri_ref,
               y_ref, sout_ref, buf_r, buf_l, s_scr, st_r, st_k, st_v, st_q, st_a, st_l, st_d, *, c):
    ci = pl.program_id(1)
    halo = SUBLANES
    hn, n, wd = A_HEADS, A_HEAD_DIM, A_WIDTH
    mm = functools.partial(_dotp, passes=RWKV_PASSES)

    @pl.when(ci == 0)
    def _():
        buf_r[halo - 1:halo, :] = shr_ref[0]
        buf_l[halo - 1:halo, :] = shl_ref[0]
        s_scr[...] = s0_ref[0]

    cur_r = rkv_ref[0]
    cur_l = lora_ref[0]
    buf_r[halo:halo + c, :] = cur_r
    buf_l[halo:halo + c, :] = cur_l
    xm = cur_r + mur_ref[...] * (buf_r[halo - 1:halo - 1 + c, :] - cur_r)
    lo = cur_l + mul_ref[...] * (buf_l[halo - 1:halo - 1 + c, :] - cur_l)
    buf_r[halo - 1:halo, :] = cur_r[c - 1:c, :]
    buf_l[halo - 1:halo, :] = cur_l[c - 1:c, :]

    r = xm[:, :wd]
    k = xm[:, wd:2 * wd]
    v = xm[:, 2 * wd:]
    lo_a = lo[:, :LANES]
    w_pre = w0_ref[...] + jnp.dot(jnp.tanh(lo_a).astype(BF16), w2_ref[...], preferred_element_type=F32)
    w_log = -_softplus(-w_pre) - 0.5
    ld = -jnp.exp(w_log)
    a = jax.nn.sigmoid(a0_ref[...] + jnp.dot(lo_a.astype(BF16), a2_ref[...], preferred_element_type=F32))
    g = jnp.dot(jax.nn.sigmoid(lo[:, LANES:]).astype(BF16), g2_ref[...], preferred_element_type=F32)
    kq = k * kk_ref[...]
    k2 = k * (1.0 + (a - 1.0) * ka_ref[...])
    lc = lax.dot_general(tri_ref[...], ld, NN, precision=lax.Precision.HIGHEST,
                         preferred_element_type=F32)

    for h in range(hn):
        sl = slice(h * n, (h + 1) * n)
        st_r[h] = r[:, sl]
        st_k[h] = k2[:, sl]
        st_v[h] = v[:, sl]
        st_q[h] = kq[:, sl]
        st_a[h] = a[:, sl]
        st_l[h] = lc[:, sl]
        st_d[h] = ld[:, sl]

    rh_, k2h, vh, kqh, ah, lch, ldh = (st_r[...], st_k[...], st_v[...], st_q[...], st_a[...],
                                       st_l[...], st_d[...])
    nrm = jnp.sqrt(jnp.sum(kqh * kqh, axis=-1, keepdims=True))
    kk = kqh / jnp.maximum(nrm, 1e-12)
    kka = kk * ah
    e_neg = jnp.exp(-lch)
    am = jnp.exp(lch - ldh) * kk
    bm = kka * e_neg
    kh = k2h * e_neg
    rh = rh_ * jnp.exp(lch)
    l_end = lch[:, c - 1:c, :]
    e_c = jnp.exp(l_end - lch)
    bp = kka * e_c
    kp = k2h * e_c
    w_end = jnp.exp(l_end)

    x2 = jnp.concatenate([am, rh], axis=1)
    zb = _dotp(x2, bm, BNT, RWKV_PASSES_SOLVE)
    zk = mm(x2, kh, BNT)
    ti = lax.broadcasted_iota(I32, (hn, c, c), 1)
    si = lax.broadcasted_iota(I32, (hn, c, c), 2)
    strict = si < ti
    incl = si <= ti
    m1 = jnp.where(strict, zb[:, :c], 0.0)
    m4 = jnp.where(incl, zb[:, c:], 0.0)
    m2 = jnp.where(strict, zk[:, :c], 0.0)
    m3 = jnp.where(incl, zk[:, c:], 0.0)

    tm = jnp.where(si == ti, 1.0, 0.0) - m1
    npow = mm(m1, m1, BNN)
    span = 2
    while span < c:
        tm = tm + mm(tm, npow, BNN)
        span *= 2
        if span < c:
            npow = mm(npow, npow, BNN)

    s0 = s_scr[...]
    rhs = mm(am, s0, BNT) + mm(m2, vh, BNN)
    p = mm(tm, rhs, BNN)
    y = mm(rh, s0, BNT) + mm(m3, vh, BNN) - mm(m4, p, BNN)
    s_new = s0 * w_end + mm(vh, kp, BTN) - mm(p, bp, BTN)
    s_scr[...] = s_new

    mean = jnp.mean(y, axis=-1, keepdims=True)
    yc = y - mean
    var = jnp.mean(yc * yc, axis=-1, keepdims=True)
    yn = yc * lax.rsqrt(var + A_LN_EPS) * lnw_ref[...] + lnb_ref[...]
    bonus = jnp.sum(rh_ * k2h * rk_ref[...], axis=-1, keepdims=True) * vh
    yo = yn + bonus
    yo = jnp.concatenate([yo[h] for h in range(hn)], axis=-1)
    y_ref[0] = (yo * g).astype(y_ref.dtype)

    @pl.when(ci == pl.num_programs(1) - 1)
    def _():
        sout_ref[0] = s_new


def _rwkv(p3, lora_blk, shift_r, shift_l, s0, wts):
    b, t, _ = p3.shape
    c = min(CHUNK, t)
    hn, n, wd = A_HEADS, A_HEAD_DIM, A_WIDTH
    full = lambda shape: pl.BlockSpec(shape, lambda i, j: (0,) * len(shape))
    tri = (jnp.arange(c)[:, None] >= jnp.arange(c)[None, :]).astype(F32)
    st = pltpu.VMEM((hn, c, n), F32)
    return pl.pallas_call(
        functools.partial(_rwkv_body, c=c),
        grid=(b, t // c),
        in_specs=[pl.BlockSpec((1, c, 3 * wd), lambda i, j: (i, j, 0)),
                  pl.BlockSpec((1, c, LORA_PAD), lambda i, j: (i, j, lora_blk)),
                  pl.BlockSpec((1, 1, 3 * wd), lambda i, j: (i, 0, 0)),
                  pl.BlockSpec((1, 1, LORA_PAD), lambda i, j: (i, 0, 0)),
                  pl.BlockSpec((1, hn, n, n), lambda i, j: (i, 0, 0, 0)),
                  full((1, 3 * wd)), full((1, LORA_PAD)), full((1, wd)), full((LANES, wd)),
                  full((1, wd)), full((LANES, wd)), full((LORA_PAD - LANES, wd)),
                  full((1, wd)), full((1, wd)), full((hn, 1, n)), full((hn, 1, n)), full((hn, 1, n)),
                  full((c, c))],
        out_specs=[pl.BlockSpec((1, c, wd), lambda i, j: (i, j, 0)),
                   pl.BlockSpec((1, hn, n, n), lambda i, j: (i, 0, 0, 0))],
        out_shape=[jax.ShapeDtypeStruct((b, t, wd), BF16),
                   jax.ShapeDtypeStruct((b, hn, n, n), F32)],
        scratch_shapes=[pltpu.VMEM((c + SUBLANES, 3 * wd), F32), pltpu.VMEM((c + SUBLANES, LORA_PAD), F32),
                        pltpu.VMEM((hn, n, n), F32), st, st, st, st, st, st, st],
        compiler_params=_cparams(("parallel", "arbitrary")),
        name="rwkv7",
    )(p3, p3, shift_r, shift_l, s0, wts["mu_r"], wts["mu_l"], wts["w0"], wts["w2"], wts["a0"], wts["a2"],
      wts["g2"], wts["k_k"], wts["k_a"], wts["r_k"], wts["ln_w"], wts["ln_b"], tri)


def _rope(x, cos, sin):
    wdt = x.shape[-1]
    lane = lax.broadcasted_iota(I32, x.shape, 1)
    first = (lane % D_ROPE) < (D_ROPE // 2)
    rot = jnp.where(first, -pltpu.roll(x, wdt - D_ROPE // 2, 1), pltpu.roll(x, D_ROPE // 2, 1))
    return x * cos + rot * sin


def _mla_prep_body(qd_ref, kvd_ref, sm_ref, gq_ref, gkv_ref, wuq_ref, cos_ref, sin_ref,
                   qn_ref, qr_ref, lat_ref, kr_ref):
    nope_w = D_HEADS * D_NOPE
    qdn = _rms(qd_ref[0], gq_ref[...]).astype(BF16)
    qf = jnp.dot(qdn, wuq_ref[...], preferred_element_type=F32) * ((D_NOPE + D_ROPE) ** -0.5 * LOG2E)
    qn_ref[0] = qf[:, :nope_w].astype(qn_ref.dtype)
    cos = cos_ref[...]
    sin = sin_ref[...]
    qr_ref[0] = _rope(qf[:, nope_w:], cos, sin).astype(qr_ref.dtype)
    lat_ref[0] = _rms(kvd_ref[0], gkv_ref[...])
    sm = sm_ref[0][:, :LANES]
    kr = _rope(sm, cos[:, :LANES], sin[:, :LANES])
    kr_ref[0] = kr[:, C_IDX_DIM:C_IDX_DIM + D_ROPE]


def _mla_prep(p3, qd_blk, kvd_blk, sm_blk, gq, gkv, wuq, cos, sin):
    b, t, _ = p3.shape
    tt = min(256, t)
    rw = D_HEADS * D_ROPE
    full = lambda shape: pl.BlockSpec(shape, lambda i, j: (0,) * len(shape))
    return pl.pallas_call(
        _mla_prep_body,
        grid=(b, t // tt),
        in_specs=[pl.BlockSpec((1, tt, D_Q_RANK), lambda i, j: (i, j, qd_blk)),
                  pl.BlockSpec((1, tt, D_KV_RANK), lambda i, j: (i, j, kvd_blk)),
                  pl.BlockSpec((1, tt, 512), lambda i, j: (i, j, sm_blk)),
                  full((1, D_Q_RANK)), full((1, D_KV_RANK)), full(wuq.shape),
                  pl.BlockSpec((tt, rw), lambda i, j: (j, 0)),
                  pl.BlockSpec((tt, rw), lambda i, j: (j, 0))],
        out_specs=[pl.BlockSpec((1, tt, D_HEADS * D_NOPE), lambda i, j: (i, j, 0)),
                   pl.BlockSpec((1, tt, rw), lambda i, j: (i, j, 0)),
                   pl.BlockSpec((1, tt, D_KV_RANK), lambda i, j: (i, j, 0)),
                   pl.BlockSpec((1, tt, D_ROPE), lambda i, j: (i, j, 0))],
        out_shape=[jax.ShapeDtypeStruct((b, t, D_HEADS * D_NOPE), BF16),
                   jax.ShapeDtypeStruct((b, t, rw), BF16),
                   jax.ShapeDtypeStruct((b, t, D_KV_RANK), F32),
                   jax.ShapeDtypeStruct((b, t, D_ROPE), F32)],
        compiler_params=_cparams(("parallel", "parallel")),
        name="mla_prep",
    )(p3, p3, p3, gq.reshape(1, -1), gkv.reshape(1, -1), wuq, cos, sin)


def _bias_table_body(rb_ref, o_ref):
    d = pl.program_id(0)
    h = pl.program_id(1)
    nb = N_BUCKETS // 2
    max_exact = nb // 2
    s = lax.broadcasted_iota(I32, (LANES, LANES), 0)
    q = lax.broadcasted_iota(I32, (LANES, LANES), 1)
    rel = s - q - d * LANES
    n = jnp.abs(rel)
    big = jnp.maximum(n, max_exact).astype(F32)
    large = max_exact + (jnp.log(big / max_exact) / math.log(MAX_DISTANCE / max_exact)
                         * (nb - max_exact)).astype(I32)
    large = jnp.minimum(large, nb - 1)
    bucket = jnp.where(rel > 0, nb, 0) + jnp.where(n < max_exact, n, large)
    out = jnp.zeros((LANES, LANES), F32)
    for bk in range(N_BUCKETS):
        out = jnp.where(bucket == bk, rb_ref[bk, h], out)
    o_ref[0, 0] = out * LOG2E


def _bias_tables(rel_bias):
    return pl.pallas_call(
        _bias_table_body,
        grid=(2, C_HEADS),
        in_specs=[pl.BlockSpec(memory_space=pltpu.SMEM)],
        out_specs=pl.BlockSpec((1, 1, LANES, LANES), lambda d, h: (d, h, 0, 0)),
        out_shape=jax.ShapeDtypeStruct((2, C_HEADS, LANES, LANES), F32),
        name="bias_tables",
    )(rel_bias)


def _transpose32(x):
    x = list(x)
    for s, msk in ((16, 0x0000FFFF), (8, 0x00FF00FF), (4, 0x0F0F0F0F), (2, 0x33333333), (1, 0x55555555)):
        sh = jnp.full(x[0].shape, s, I32)
        for i in range(32):
            if i & s == 0:
                t = (lax.shift_right_logical(x[i], sh) ^ x[i + s]) & msk
                x[i + s] = x[i + s] ^ t
                x[i] = x[i] ^ lax.shift_left(t, sh)
    return x


def _attn_body(*refs, mode, p0, t_valid, l_valid, tq, tk, dk, topk, nh):
    dsa = mode == "dsa"
    dh = LANES
    if dsa:
        (qT_ref, k_ref, vT_ref, q3_ref, ki3_ref, wiT_ref, tab_ref, far_ref,
         o_ref, m_scr, l_scr, acc_scr, s_scr, p_scr, mb_scr, skey_scr, thr_scr, planes_scr, e_scr) = refs
    else:
        (qT_ref, k_ref, vT_ref, o_ref, m_scr, l_scr, acc_scr, s_scr, p_scr, mb_scr) = refs
    qt = pl.program_id(1)
    kt = pl.program_id(2)
    nk = pl.num_programs(2)
    q_lo = p0 + qt * tq
    q_hi = p0 + jnp.minimum(qt * tq + tq, t_valid) - 1
    n_allowed = jnp.minimum((q_hi // CHUNK + 1) * CHUNK, l_valid)
    last_kt = (n_allowed - 1) // tk
    qpos = q_lo + lax.broadcasted_iota(I32, (1, tq), 1)
    qchunk = qpos // CHUNK

    def allowed_mask(k0, rows):
        kidx = k0 + lax.broadcasted_iota(I32, (rows, tq), 0)
        return (kidx // CHUNK <= qchunk) & (kidx < l_valid)

    @pl.when(kt == 0)
    def _():
        m_scr[...] = jnp.full(m_scr.shape, NEG_BIG, F32)
        l_scr[...] = jnp.zeros(l_scr.shape, F32)
        acc_scr[...] = jnp.zeros(acc_scr.shape, F32)

    if dsa:
        iscale = (C_IDX_HEADS * C_IDX_DIM) ** -0.5

        @pl.when(kt == 0)
        def _():
            wpt = tk // 32
            kw = 3 * C_IDX_DIM

            def score_tile(j, carry):
                k0 = pl.multiple_of(j * tk, tk)
                ki3 = ki3_ref[0, pl.ds(k0, tk), :]
                s = jnp.zeros((tk, tq), F32)
                for h in range(C_IDX_HEADS):
                    d = jnp.dot(ki3, q3_ref[0, h * kw:(h + 1) * kw, :], preferred_element_type=F32)
                    s = s + jnp.maximum(d, 0.0) * wiT_ref[0, h:h + 1, :]
                s = s * iscale + 0.0
                bits = pltpu.bitcast(s, I32)
                key = bits ^ ((bits >> 31) & 0x7FFFFFFF)
                key = jnp.where(allowed_mask(k0, tk), key, INT_MIN)
                skey_scr[pl.ds(k0, tk), :] = key
                ukey = key ^ INT_MIN
                w0 = pl.multiple_of(j * wpt, SUBLANES)
                for g in range(tk // 256):
                    rows = [ukey[g * 256 + 8 * i:g * 256 + 8 * i + 8, :] for i in range(32)]
                    for bi, plane in enumerate(_transpose32(rows)):
                        planes_scr[bi, pl.ds(w0 + g * SUBLANES, SUBLANES), :] = plane
                ones = jnp.full((wpt, tq), -1, I32)
                planes_scr[32, pl.ds(w0, wpt), :] = ones
                e_scr[pl.ds(w0, wpt), :] = ones
                return carry

            def blank_tile(j, carry):
                w0 = pl.multiple_of(j * wpt, SUBLANES)
                for bi in range(33):
                    planes_scr[bi, pl.ds(w0, wpt), :] = jnp.zeros((wpt, tq), I32)
                e_scr[pl.ds(w0, wpt), :] = jnp.zeros((wpt, tq), I32)
                return carry

            ntile = last_kt + 1
            lax.fori_loop(0, ntile, score_tile, 0)
            nblk = (ntile + SEARCH_TILES - 1) // SEARCH_TILES
            lax.fori_loop(ntile, nblk * SEARCH_TILES, blank_tile, 0)
            wpb = SEARCH_TILES * wpt

            def lanesum(acc):
                return jnp.sum(acc, axis=0, keepdims=True)

            def fold(pc):
                return jnp.sum(pc.reshape(wpb // SUBLANES, SUBLANES, tq), axis=0)

            def settle(e, plane, take):
                t = e & plane
                return jnp.where(take != 0, t, e ^ t)

            def bit_step(i, st):
                c_gt, th, take_prev = st
                bi = 31 - i

                def body(j, acc):
                    w0 = pl.multiple_of(j * wpb, SUBLANES)
                    e = settle(e_scr[pl.ds(w0, wpb), :], planes_scr[bi + 1, pl.ds(w0, wpb), :], take_prev)
                    e_scr[pl.ds(w0, wpb), :] = e
                    return acc + fold(lax.population_count(e & planes_scr[bi, pl.ds(w0, wpb), :]))

                c1 = lanesum(lax.fori_loop(0, nblk, body, jnp.zeros((SUBLANES, tq), I32)))
                take = (c_gt + c1) >= topk
                return (jnp.where(take, c_gt, c_gt + c1),
                        jnp.where(take, th | lax.shift_left(jnp.int32(1), bi), th), jnp.where(take, 1, 0))

            zero = jnp.zeros((1, tq), I32)
            c_gt, th_u, take0 = lax.fori_loop(0, 32, bit_step, (zero, zero, zero + 1))

            def last_body(j, acc):
                w0 = pl.multiple_of(j * wpb, SUBLANES)
                e = settle(e_scr[pl.ds(w0, wpb), :], planes_scr[0, pl.ds(w0, wpb), :], take0)
                e_scr[pl.ds(w0, wpb), :] = e
                return acc + fold(lax.population_count(e))

            n_eq = lanesum(lax.fori_loop(0, nblk, last_body, jnp.zeros((SUBLANES, tq), I32)))
            need = topk - c_gt
            nbits = max(1, (l_valid - 1).bit_length())
            thr_scr[0:1, :] = th_u ^ INT_MIN
            thr_scr[1:2, :] = jnp.full((1, tq), (1 << nbits) - 1, I32)

            @pl.when(jnp.max((n_eq - need).astype(F32)) > 0.0)
            def _():
                def idx_step(i, jb):
                    cand = jb - lax.shift_left(jnp.int32(1), nbits - 1 - i)

                    def body(j, acc):
                        w0 = pl.multiple_of(j * wpb, SUBLANES)
                        wr = w0 + lax.broadcasted_iota(I32, (wpb, tq), 0)
                        base = (wr >> 3) * 256 + (wr & 7)
                        mx = (cand - base) >> 3
                        low = jnp.left_shift(2, jnp.clip(mx, 0, 30)) - 1
                        msk = jnp.where(mx < 0, 0, jnp.where(mx >= 31, -1, low))
                        return acc + fold(lax.population_count(e_scr[pl.ds(w0, wpb), :] & msk))

                    cnt = lanesum(lax.fori_loop(0, nblk, body, jnp.zeros((SUBLANES, tq), I32)))
                    return jnp.where(cnt >= need, cand, jb)

                thr_scr[1:2, :] = lax.fori_loop(0, nbits, idx_step, jnp.full((1, tq), (1 << nbits) - 1, I32))

    def tile(near):
        k0 = pl.multiple_of(kt * tk, tk)
        nchunk = tk // RC
        masked = dsa or near
        if masked:
            for c in range(nchunk):
                r0 = k0 + c * RC
                mask = allowed_mask(r0, RC) if near else None
                if dsa:
                    x = skey_scr[pl.ds(r0, RC), :]
                    kidx = r0 + lax.broadcasted_iota(I32, (RC, tq), 0)
                    sel = (x > thr_scr[0:1, :]) | ((x == thr_scr[0:1, :]) & (kidx <= thr_scr[1:2, :]))
                    mask = sel & mask if near else sel
                mb_scr[c * RC:(c + 1) * RC, :] = jnp.where(mask, 0.0, NEG_BIG)

        def chunk_bias(h, c):
            far = far_ref[0:1, h:h + 1]
            if not near:
                return far
            sb, off = (c * RC) // LANES, (c * RC) % LANES
            cols = []
            for qb in range(tq // LANES):
                delta = k0 + sb * LANES - (q_lo + qb * LANES)
                cols.append(jnp.where(delta == 0, tab_ref[0, h, off:off + RC, :],
                                      jnp.where(delta == -LANES, tab_ref[1, h, off:off + RC, :], far)))
            return jnp.concatenate(cols, axis=1) if len(cols) > 1 else cols[0]

        for h in range(nh):
            ks = slice(h * dk, (h + 1) * dk)
            s_scr[h] = jnp.dot(k_ref[0, :, ks], qT_ref[0, ks, :], preferred_element_type=F32)
        m_news, alphas = [], []
        for h in range(nh):
            mx = jnp.full((SUBLANES, tq), NEG_BIG, F32)
            for c in range(nchunk):
                rows = slice(c * RC, (c + 1) * RC)
                blk = s_scr[h, rows, :]
                if masked:
                    blk = blk + mb_scr[rows, :]
                    if dsa:
                        blk = blk + chunk_bias(h, c)
                    s_scr[h, rows, :] = blk
                mx = jnp.maximum(mx, jnp.max(blk.reshape(RC // SUBLANES, SUBLANES, tq), axis=0))
            m_prev = m_scr[h:h + 1, :]
            m_new = jnp.maximum(m_prev, jnp.max(mx, axis=0, keepdims=True))
            m_scr[h:h + 1, :] = m_new
            m_news.append(m_new)
            alphas.append(jnp.exp2(m_prev - m_new))
        for h in range(nh):
            for c in range(nchunk):
                rows = slice(c * RC, (c + 1) * RC)
                p_scr[h, rows, :] = jnp.exp2(s_scr[h, rows, :] - m_news[h]).astype(BF16)
        for h in range(nh):
            sl = slice(h * dh, (h + 1) * dh)
            pv = jnp.dot(vT_ref[0, h * VT_ROWS:(h + 1) * VT_ROWS, :], p_scr[h], preferred_element_type=F32)
            acc_scr[sl, :] = alphas[h] * acc_scr[sl, :] + pv[:dh]
            l_scr[h:h + 1, :] = alphas[h] * l_scr[h:h + 1, :] + pv[dh:dh + 1]

    is_far = kt * tk + tk - 1 <= q_lo - LANES

    @pl.when((kt <= last_kt) & is_far)
    def _():
        tile(False)

    @pl.when((kt <= last_kt) & jnp.logical_not(is_far))
    def _():
        tile(True)

    @pl.when(kt == nk - 1)
    def _():
        for h in range(nh):
            sl = slice(h * dh, (h + 1) * dh)
            o = acc_scr[sl, :] / l_scr[h:h + 1, :]
            o_ref[0, :, sl] = o.T.astype(o_ref.dtype)


def _attention(mode, qT, k, vT, extra, *, p0, t_valid, l_valid, tk):
    b, hdk, tpad = qT.shape
    lpad = k.shape[1]
    nh = vT.shape[1] // VT_ROWS
    hd = nh * LANES
    dk = hdk // nh
    tq = 2 * LANES if tpad % (2 * LANES) == 0 else LANES
    nq, nk = tpad // tq, lpad // tk
    topk = min(TOPK_MAX, l_valid // 4)
    assert p0 % LANES == 0 and lpad % tk == 0 and tk % 256 == 0 and l_valid >= topk >= 1 and tk >= topk

    def last_kt(qt):
        q_hi = p0 + jnp.minimum(qt * tq + tq, t_valid) - 1
        return (jnp.minimum((q_hi // CHUNK + 1) * CHUNK, l_valid) - 1) // tk

    kmap = lambda i, q, kk: (i, jnp.minimum(kk, last_kt(q)), 0)
    vmap_ = lambda i, q, kk: (i, 0, jnp.minimum(kk, last_kt(q)))
    qmap = lambda i, q, kk: (i, 0, q)
    in_specs = [pl.BlockSpec((1, hdk, tq), qmap),
                pl.BlockSpec((1, tk, hdk), kmap),
                pl.BlockSpec((1, nh * VT_ROWS, tk), vmap_)]
    scratch = [pltpu.VMEM((nh, tq), F32), pltpu.VMEM((nh, tq), F32), pltpu.VMEM((hd, tq), F32),
               pltpu.VMEM((nh, tk, tq), F32), pltpu.VMEM((nh, tk, tq), BF16), pltpu.VMEM((tk, tq), F32)]
    if mode == "dsa":
        q3, ki3, wiT, tabs, far = extra
        wrows = -(-nk // SEARCH_TILES) * SEARCH_TILES * (tk // 32)
        in_specs += [pl.BlockSpec((1, q3.shape[1], tq), qmap),
                     pl.BlockSpec((1, lpad, ki3.shape[2]), lambda i, q, kk: (i, 0, 0)),
                     pl.BlockSpec((1, C_IDX_HEADS, tq), qmap),
                     pl.BlockSpec(tabs.shape, lambda i, q, kk: (0, 0, 0, 0)),
                     pl.BlockSpec(far.shape, lambda i, q, kk: (0, 0))]
        scratch += [pltpu.VMEM((lpad, tq), I32), pltpu.VMEM((SUBLANES, tq), I32),
                    pltpu.VMEM((33, wrows, tq), I32), pltpu.VMEM((wrows, tq), I32)]
    return pl.pallas_call(
        functools.partial(_attn_body, mode=mode, p0=p0, t_valid=t_valid, l_valid=l_valid, tq=tq, tk=tk,
                          dk=dk, topk=topk, nh=nh),
        grid=(b, nq, nk),
        in_specs=in_specs,
        out_specs=pl.BlockSpec((1, tq, hd), lambda i, q, kk: (i, q, 0)),
        out_shape=jax.ShapeDtypeStruct((b, tpad, hd), BF16),
        scratch_shapes=scratch,
        compiler_params=_cparams(("parallel", "parallel", "arbitrary")),
        name="attn_" + mode,
    )(qT, k, vT, *extra)


PACK_TILE = 256


def _pack_body(cache_ref, new_ref, o_ref, *, n_cache, transpose, nh):
    j = pl.program_id(1)

    def emit(head):
        for h in range(nh):
            sl = slice(h * LANES, (h + 1) * LANES)
            x = head(h, sl)
            if transpose:
                r0 = h * VT_ROWS
                o_ref[0, r0:r0 + LANES, :] = x.T.astype(o_ref.dtype)
                o_ref[0, r0 + LANES:r0 + VT_ROWS, :] = jnp.ones((VT_ONES, x.shape[0]), o_ref.dtype)
            else:
                o_ref[0, :, sl] = x.astype(o_ref.dtype)

    @pl.when(j < n_cache)
    def _():
        emit(lambda h, sl: cache_ref[0, 0, :, h, :])

    @pl.when(j >= n_cache)
    def _():
        emit(lambda h, sl: new_ref[0, :, sl])


def _pack_keys(cache, layer, new, lpad, *, transpose):
    _, b, past, nh, dh = cache.shape
    tp = PACK_TILE
    assert dh == LANES and past % tp == 0 and lpad % tp == 0
    n_cache = past // tp
    hd = nh * dh
    new = _pad_axis(new, 1, lpad - past)
    out_shape = (b, nh * VT_ROWS, lpad) if transpose else (b, lpad, hd)
    out_spec = (pl.BlockSpec((1, nh * VT_ROWS, tp), lambda i, j: (i, 0, j)) if transpose
                else pl.BlockSpec((1, tp, hd), lambda i, j: (i, j, 0)))
    return pl.pallas_call(
        functools.partial(_pack_body, n_cache=n_cache, transpose=transpose, nh=nh),
        grid=(b, lpad // tp),
        in_specs=[pl.BlockSpec((1, 1, tp, nh, dh),
                               lambda i, j: (layer, i, jnp.minimum(j, n_cache - 1), 0, 0)),
                  pl.BlockSpec((1, tp, hd), lambda i, j: (i, jnp.maximum(j - n_cache, 0), 0))],
        out_specs=out_spec,
        out_shape=jax.ShapeDtypeStruct(out_shape, BF16),
        compiler_params=_cparams(("parallel", "parallel")),
        name="pack_keys",
    )(cache, new)


def _split_hi_lo(x):
    hi = x.astype(BF16)
    lo = (x - hi.astype(F32)).astype(BF16)
    return hi, lo


def _pad_axis(x, axis, size):
    if x.shape[axis] == size:
        return x
    pad = [(0, 0)] * x.ndim
    pad[axis] = (0, size - x.shape[axis])
    return jnp.pad(x, pad)


def _vt_ones(v):
    b, l, hd = v.shape
    vt = jnp.swapaxes(v, 1, 2).reshape(b, hd // LANES, LANES, l)
    ones = jnp.ones((b, hd // LANES, VT_ONES, l), v.dtype)
    return jnp.concatenate([vt, ones], axis=2).reshape(b, -1, l)


def _key_tile(l_valid):
    lpad = -(-l_valid // 256) * 256
    for tk in (512, 256):
        if lpad % tk == 0:
            return lpad, tk
    return lpad, 256


def _prep_weights(W, d_model):
    f = {}
    wd = A_WIDTH
    n_even = W["ev_w_in"].shape[0]
    n_odd = W["od_w_in"].shape[0]
    f["even"] = []
    for e in range(n_even):
        wi = W["ev_w_in"][e]
        w_in = jnp.concatenate([wi[:, :3 * wd], wi[:, A_COLS:], wi[:, 3 * wd:A_COLS],
                                jnp.zeros((d_model, LORA_PAD - A_LORA), F32)], axis=1).astype(BF16)
        mu = W["rwkv_mu"][e]
        w2 = jnp.zeros((LANES, wd), F32).at[:A_DECAY_LORA].set(W["rwkv_w2"][e])
        a2 = jnp.zeros((LANES, wd), F32).at[A_DECAY_LORA:A_DECAY_LORA + A_ICLR_LORA].set(W["rwkv_a2"][e])
        g2 = jnp.zeros((LORA_PAD - LANES, wd), F32).at[:A_GATE_LORA].set(W["rwkv_g2"][e])
        hm = lambda v: v.reshape(A_HEADS, 1, A_HEAD_DIM)
        rw = dict(mu_r=mu[:3 * wd].reshape(1, -1),
                  mu_l=_pad_axis(mu[3 * wd:], 0, LORA_PAD).reshape(1, -1),
                  w0=W["rwkv_w0"][e].reshape(1, -1), w2=w2.astype(BF16),
                  a0=W["rwkv_a0"][e].reshape(1, -1), a2=a2.astype(BF16), g2=g2.astype(BF16),
                  k_k=W["rwkv_k_k"][e].reshape(1, -1), k_a=W["rwkv_k_a"][e].reshape(1, -1),
                  r_k=hm(W["rwkv_r_k"][e]), ln_w=hm(W["rwkv_ln_w"][e]), ln_b=hm(W["rwkv_ln_b"][e]))
        eye = jnp.eye(B_BLOCKS, dtype=F32)
        blockdiag = lambda w: (eye[:, None, :, None] * w[:, :, None, :]).reshape(B_WIDTH, B_WIDTH)
        wgate = jnp.concatenate([blockdiag(W["lru_wa"][e]), blockdiag(W["lru_wx"][e])], axis=1).astype(BF16)
        bgate = jnp.concatenate([W["lru_ba"][e], W["lru_bx"][e]])
        wo = W["ev_w_out"][e].astype(BF16)
        f["even"].append(dict(w_in=w_in, rw=rw, wgate=wgate, bgate=bgate, wo_a=wo[:wd], wo_b=wo[wd:],
                              cw=W["lru_conv_w"][e], cb=W["lru_conv_b"][e], lam=W["lru_lambda"][e]))
    f["odd"] = []
    cw_ = C_WIDTH
    qi_w = C_IDX_HEADS * C_IDX_DIM
    for o in range(n_odd):
        wi = W["od_w_in"][o]
        offs = [0]
        for s in (cw_, cw_, cw_, qi_w, C_IDX_DIM, C_IDX_HEADS, D_Q_RANK, D_KV_RANK, D_ROPE):
            offs.append(offs[-1] + s)
        q, k, v, qi, ki, wi_, qd, kvd, kr = [wi[:, offs[i]:offs[i + 1]] for i in range(9)]
        small = jnp.concatenate([ki, kr, wi_], axis=1)
        w_in = jnp.concatenate([q, k, v, qi, qd, kvd, _pad_axis(small, 1, 512)], axis=1).astype(BF16)
        wuq = W["mla_w_uq"][o].reshape(D_Q_RANK, D_HEADS, D_NOPE + D_ROPE)
        wuq = jnp.concatenate([wuq[:, :, :D_NOPE].reshape(D_Q_RANK, -1),
                               wuq[:, :, D_NOPE:].reshape(D_Q_RANK, -1)], axis=1).astype(BF16)
        wukv = W["mla_w_ukv"][o].reshape(D_KV_RANK, D_HEADS, D_NOPE + D_V)
        wuk = _pad_axis(wukv[:, :, :D_NOPE], 2, MLA_DK).reshape(D_KV_RANK, -1).astype(BF16)
        wuv = wukv[:, :, D_NOPE:].reshape(D_KV_RANK, -1).astype(BF16)
        wo = W["od_w_out"][o].astype(BF16)
        f["odd"].append(dict(w_in=w_in, wuq=wuq, wuk=wuk, wuv=wuv, wo_c=wo[:cw_], wo_d=wo[cw_:],
                             gq=W["mla_q_norm"][o], gkv=W["mla_kv_norm"][o]))
    f["ffn"] = dict(wg=W["ffn_w_gate"].astype(BF16), wu=W["ffn_w_up"].astype(BF16),
                    wd=W["ffn_w_down"].astype(BF16))
    f["tabs"] = _bias_tables(W["rel_bias"])
    nb = N_BUCKETS // 2
    f["far"] = _pad_axis(W["rel_bias"][nb - 1:nb, :] * LOG2E, 1, LANES)
    return f


def _mixer_even(x2, b, t, p0, shift, s0, h0, cbuf, fe, norm_g):
    wd = A_WIDTH
    p = _norm_matmul(x2, norm_g, fe["w_in"])
    p3 = p.reshape(b, t, -1)
    lora_col = 3 * wd + 2 * B_WIDTH
    shift_r = shift[:, None, :3 * wd]
    shift_l = _pad_axis(shift[:, None, 3 * wd:], 2, LORA_PAD)
    ya, s_new = _rwkv(p3, lora_col // LORA_PAD, shift_r, shift_l, s0, fe["rw"])
    yb, h_new, c_new = _rglru(p3, 3, 4, fe["cw"], fe["cb"], fe["wgate"], fe["bgate"], fe["lam"],
                              h0, cbuf, p0=p0)
    x2 = _matmul([ya.reshape(b * t, wd), yb.reshape(b * t, B_WIDTH)], [fe["wo_a"], fe["wo_b"]], res=x2)
    last = p3[:, t - 1]
    new_shift = jnp.concatenate([last[:, :3 * wd], last[:, lora_col:lora_col + A_LORA]], axis=-1)
    return x2, new_shift, s_new, h_new[:, 0], c_new


def _mixer_odd(x2, b, t, p0, o, dsa_k, dsa_v, cik, clat, ckr, fo, f, norm_g, cos, sin):
    cw_ = C_WIDTH
    p = _norm_matmul(x2, norm_g, fo["w_in"])
    p3 = p.reshape(b, t, -1)
    past = dsa_k.shape[2]
    l_valid = past + t
    lpad, tk = _key_tile(l_valid)
    tpad = -(-t // LANES) * LANES
    q = p3[..., :cw_]
    k_new = p3[..., cw_:2 * cw_]
    v_new = p3[..., 2 * cw_:3 * cw_]
    qi = p3[..., 3 * cw_:3 * cw_ + 512]
    small = p3[..., 3 * cw_ + 1536:]
    ki_new = small[..., :C_IDX_DIM]
    wi = small[..., 2 * C_IDX_DIM:2 * C_IDX_DIM + C_IDX_HEADS]

    def keys(cache, new):
        allk = jnp.concatenate([cache.reshape(b, past, -1), new], axis=1) if past else new
        return _pad_axis(allk, 1, lpad)

    tq_ = lambda z: _pad_axis(jnp.swapaxes(z, 1, 2), 2, tpad)
    if past and past % PACK_TILE == 0:
        k_all = _pack_keys(dsa_k, o, k_new, lpad, transpose=False)
        vT_all = _pack_keys(dsa_v, o, v_new, lpad, transpose=True)
    else:
        k_all = keys(dsa_k[o], k_new).astype(BF16)
        vT_all = _vt_ones(keys(dsa_v[o], v_new).astype(BF16))
    kih, kil = _split_hi_lo(keys(cik, ki_new))
    ki3 = jnp.concatenate([kih, kih, kil], axis=-1)
    qih, qil = _split_hi_lo(tq_(qi).reshape(b, C_IDX_HEADS, C_IDX_DIM, tpad))
    q3 = jnp.concatenate([qih, qil, qih], axis=2).reshape(b, 3 * C_IDX_HEADS * C_IDX_DIM, tpad)
    yc = _attention("dsa", tq_(q * (C_HEAD_DIM ** -0.5 * LOG2E)).astype(BF16), k_all, vT_all,
                    (q3, ki3, tq_(wi), f["tabs"], f["far"]),
                    p0=p0, t_valid=t, l_valid=l_valid, tk=tk)[:, :t]
    qn, qr, lat, krope = _mla_prep(p3, 7, 8, 9, fo["gq"], fo["gkv"], fo["wuq"], cos, sin)
    lat_all = keys(clat, lat).reshape(b * lpad, -1)
    kr_all = keys(ckr, krope).reshape(b * lpad, -1)
    eye = jnp.eye(D_ROPE, dtype=BF16)
    ident = jnp.tile(jnp.pad(eye, ((0, 0), (D_NOPE, MLA_DK - D_NOPE - D_ROPE))), (1, D_HEADS))
    k_full = _matmul([lat_all, kr_all], [fo["wuk"], ident], out_dtype=BF16).reshape(b, lpad, -1)
    v_all = _matmul([lat_all], [fo["wuv"]], out_dtype=BF16).reshape(b, lpad, -1)
    q_full = jnp.concatenate([qn.reshape(b, t, D_HEADS, D_NOPE), qr.reshape(b, t, D_HEADS, D_ROPE),
                              jnp.zeros((b, t, D_HEADS, MLA_DK - D_NOPE - D_ROPE), BF16)], axis=-1)
    yd = _attention("mla", tq_(q_full.reshape(b, t, -1)), k_full, _vt_ones(v_all), (),
                    p0=p0, t_valid=t, l_valid=l_valid, tk=tk)[:, :t]
    x2 = _matmul([yc.reshape(b * t, cw_), yd.reshape(b * t, -1)], [fo["wo_c"], fo["wo_d"]], res=x2)
    return (x2, k_new.reshape(b, t, C_HEADS, C_HEAD_DIM), v_new.reshape(b, t, C_HEADS, C_HEAD_DIM),
            ki_new, lat, krope)


def _trunk(x, p0, shift, rwkv_s, lru_h, lru_conv, dsa_k, dsa_v, dsa_ik, mla_lat, mla_kr, W, f):
    b, t, d = x.shape
    depth = W["norm_mix"].shape[0]
    pos = (p0 + jnp.arange(t)).astype(F32)
    inv = ROPE_BASE ** (-jnp.arange(0, D_ROPE, 2, dtype=F32) / D_ROPE)
    ang = pos[:, None] * inv[None, :]
    cos = jnp.tile(jnp.cos(ang), (1, 2 * D_HEADS))
    sin = jnp.tile(jnp.sin(ang), (1, 2 * D_HEADS))
    x2 = x.reshape(b * t, d)
    ev = [[] for _ in range(4)]
    od = [[] for _ in range(5)]
    for layer in range(depth):
        if layer % 2 == 0:
            e = layer // 2
            x2, *outs = _mixer_even(x2, b, t, p0, shift[e], rwkv_s[e], lru_h[e], lru_conv[e],
                                    f["even"][e], W["norm_mix"][layer])
            for lst, o_ in zip(ev, outs):
                lst.append(o_)
        else:
            o = layer // 2
            x2, *outs = _mixer_odd(x2, b, t, p0, o, dsa_k, dsa_v, dsa_ik[o], mla_lat[o], mla_kr[o],
                                   f["odd"][o], f, W["norm_mix"][layer], cos, sin)
            for lst, o_ in zip(od, outs):
                lst.append(o_)
        ff = f["ffn"]
        x2 = _ffn(x2, W["norm_ffn"][layer], ff["wg"], ff["wu"], ff["wd"], layer, W["final_norm"],
                  final_norm=layer == depth - 1)
    return (x2.reshape(b, t, d),) + tuple(jnp.stack(v) for v in ev) + tuple(jnp.stack(v) for v in od)


def kernel(x_prompt, x_sample, state_rwkv_shift, state_rwkv, state_lru, state_lru_conv, cache_dsa_k, cache_dsa_v, cache_dsa_idx_k, cache_mla_latent, cache_mla_krope, rel_bias, final_norm, norm_mix, norm_ffn, ffn_w_gate, ffn_w_up, ffn_w_down, ev_w_in, ev_w_out, rwkv_mu, rwkv_w0, rwkv_w2, rwkv_a0, rwkv_a2, rwkv_g2, rwkv_k_k, rwkv_k_a, rwkv_r_k, rwkv_ln_w, rwkv_ln_b, lru_conv_w, lru_conv_b, lru_wa, lru_ba, lru_wx, lru_bx, lru_lambda, od_w_in, od_w_out, mla_q_norm, mla_w_uq, mla_kv_norm, mla_w_ukv):
    W = dict(rel_bias=rel_bias, final_norm=final_norm, norm_mix=norm_mix, norm_ffn=norm_ffn,
             ffn_w_gate=ffn_w_gate, ffn_w_up=ffn_w_up, ffn_w_down=ffn_w_down,
             ev_w_in=ev_w_in, ev_w_out=ev_w_out, rwkv_mu=rwkv_mu, rwkv_w0=rwkv_w0,
             rwkv_w2=rwkv_w2, rwkv_a0=rwkv_a0, rwkv_a2=rwkv_a2, rwkv_g2=rwkv_g2,
             rwkv_k_k=rwkv_k_k, rwkv_k_a=rwkv_k_a, rwkv_r_k=rwkv_r_k, rwkv_ln_w=rwkv_ln_w,
             rwkv_ln_b=rwkv_ln_b, lru_conv_w=lru_conv_w, lru_conv_b=lru_conv_b, lru_wa=lru_wa,
             lru_ba=lru_ba, lru_wx=lru_wx, lru_bx=lru_bx, lru_lambda=lru_lambda,
             od_w_in=od_w_in, od_w_out=od_w_out, mla_q_norm=mla_q_norm, mla_w_uq=mla_w_uq,
             mla_kv_norm=mla_kv_norm, mla_w_ukv=mla_w_ukv)
    d_model = x_prompt.shape[-1]
    f = _prep_weights(W, d_model)
    bp = x_prompt.shape[0]
    n_even, n_odd = ev_w_in.shape[0], od_w_in.shape[0]
    dt = x_prompt.dtype
    z = lambda *shape: jnp.zeros(shape, dt)
    outs_p = _trunk(
        x_prompt, 0,
        z(n_even, bp, A_COLS), z(n_even, bp, A_HEADS, A_HEAD_DIM, A_HEAD_DIM),
        z(n_even, bp, B_WIDTH), z(n_even, bp, B_CONV - 1, B_WIDTH),
        z(n_odd, bp, 0, C_HEADS, C_HEAD_DIM), z(n_odd, bp, 0, C_HEADS, C_HEAD_DIM),
        z(n_odd, bp, 0, C_IDX_DIM), z(n_odd, bp, 0, D_KV_RANK), z(n_odd, bp, 0, D_ROPE), W, f)
    past = cache_dsa_k.shape[2]
    outs_s = _trunk(x_sample, past, state_rwkv_shift, state_rwkv, state_lru, state_lru_conv,
                    cache_dsa_k, cache_dsa_v, cache_dsa_idx_k, cache_mla_latent, cache_mla_krope, W, f)
    return (outs_p[0], outs_s[0]) + tuple(outs_p[1:]) + tuple(outs_s[1:])
```

```python
import functools
import math

import jax
import jax.numpy as jnp
from jax import lax
from jax.experimental import pallas as pl
from jax.experimental.pallas import tpu as pltpu

F32 = jnp.float32
BF16 = jnp.bfloat16
I32 = jnp.int32

CHUNK = 64
NORM_EPS = 1e-6
A_HEADS = 16
A_HEAD_DIM = 64
A_WIDTH = A_HEADS * A_HEAD_DIM
A_DECAY_LORA = 64
A_ICLR_LORA = 64
A_GATE_LORA = 160
A_LORA = A_DECAY_LORA + A_ICLR_LORA + A_GATE_LORA
A_COLS = 3 * A_WIDTH + A_LORA
A_LN_EPS = 64e-5
B_WIDTH = 1024
B_BLOCKS = 16
B_CONV = 4
B_C = 8.0
C_HEADS = 8
C_HEAD_DIM = 128
C_WIDTH = C_HEADS * C_HEAD_DIM
C_IDX_HEADS = 8
C_IDX_DIM = 64
TOPK_MAX = 256
D_HEADS = 8
D_NOPE = 128
D_ROPE = 64
D_V = 128
D_Q_RANK = 512
D_KV_RANK = 512
ROPE_BASE = 10000.0
N_BUCKETS = 32
MAX_DISTANCE = 128

LANES = 128
SUBLANES = 8
VMEM_LIMIT = 56 * 1024 * 1024
LORA_PAD = 512
NEG_BIG = -1e30
INT_MIN = -2147483648
RC = 64
SEARCH_TILES = 4
VT_ONES = 16
VT_ROWS = LANES + VT_ONES
LOG2E = 1.4426950408889634
MLA_DK = 256

NN = (((1,), (0,)), ((), ()))
NT = (((1,), (1,)), ((), ()))
BNN = (((2,), (1,)), ((0,), (0,)))
BNT = (((2,), (2,)), ((0,), (0,)))
BTN = (((1,), (1,)), ((0,), (0,)))


def _cparams(sem):
    return pltpu.CompilerParams(dimension_semantics=sem, vmem_limit_bytes=VMEM_LIMIT)


def _rms(x, g, eps=NORM_EPS):
    ms = jnp.mean(x * x, axis=-1, keepdims=True)
    return x * lax.rsqrt(ms + eps) * g


def _softplus(x):
    return jnp.maximum(x, 0.0) + jnp.log1p(jnp.exp(-jnp.abs(x)))


def _dotp(a, b, dims, passes):
    if passes == 6:
        return lax.dot_general(a, b, dims, precision=lax.Precision.HIGHEST, preferred_element_type=F32)
    ah = a.astype(BF16)
    bh = b.astype(BF16)
    out = lax.dot_general(ah, bh, dims, preferred_element_type=F32)
    if passes == 3:
        al = (a - ah.astype(F32)).astype(BF16)
        bl = (b - bh.astype(F32)).astype(BF16)
        out = out + lax.dot_general(ah, bl, dims, preferred_element_type=F32)
        out = out + lax.dot_general(al, bh, dims, preferred_element_type=F32)
    return out


def _norm_matmul_body(x_ref, g_ref, w_ref, o_ref, xn_ref):
    @pl.when(pl.program_id(1) == 0)
    def _():
        xn_ref[...] = _rms(x_ref[...], g_ref[...]).astype(BF16)

    o_ref[...] = jnp.dot(xn_ref[...], w_ref[...], preferred_element_type=F32)


def _row_tile(m, cap):
    tm = cap
    while m % tm:
        tm //= 2
    return tm


def _norm_matmul(x, g, w, *, tn=512):
    m, k = x.shape
    n = w.shape[1]
    tm = _row_tile(m, 1024)
    return pl.pallas_call(
        _norm_matmul_body,
        grid=(m // tm, n // tn),
        in_specs=[pl.BlockSpec((tm, k), lambda i, j: (i, 0)),
                  pl.BlockSpec((1, k), lambda i, j: (0, 0)),
                  pl.BlockSpec((k, tn), lambda i, j: (0, j))],
        out_specs=pl.BlockSpec((tm, tn), lambda i, j: (i, j)),
        out_shape=jax.ShapeDtypeStruct((m, n), F32),
        scratch_shapes=[pltpu.VMEM((tm, k), BF16)],
        compiler_params=_cparams(("parallel", "arbitrary")),
        name="norm_matmul",
    )(x, g.reshape(1, k), w)


def _mm_body(*refs, n_lhs, has_res):
    o_ref = refs[-1]
    acc = refs[2 * n_lhs][...] if has_res else None
    for a_ref, w_ref in zip(refs[:n_lhs], refs[n_lhs:2 * n_lhs]):
        d = jnp.dot(a_ref[...].astype(BF16), w_ref[...], preferred_element_type=F32)
        acc = d if acc is None else acc + d
    o_ref[...] = acc.astype(o_ref.dtype)


def _matmul(lhs_list, w_list, res=None, *, tn=512, out_dtype=F32):
    m = lhs_list[0].shape[0]
    n = w_list[0].shape[1]
    tm = _row_tile(m, 1024)
    in_specs = [pl.BlockSpec((tm, a.shape[1]), lambda i, j: (i, 0)) for a in lhs_list]
    in_specs += [pl.BlockSpec((w.shape[0], tn), lambda i, j: (0, j)) for w in w_list]
    args = list(lhs_list) + list(w_list)
    if res is not None:
        in_specs.append(pl.BlockSpec((tm, tn), lambda i, j: (i, j)))
        args.append(res)
    return pl.pallas_call(
        functools.partial(_mm_body, n_lhs=len(lhs_list), has_res=res is not None),
        grid=(m // tm, n // tn),
        in_specs=in_specs,
        out_specs=pl.BlockSpec((tm, tn), lambda i, j: (i, j)),
        out_shape=jax.ShapeDtypeStruct((m, n), out_dtype),
        compiler_params=_cparams(("parallel", "arbitrary")),
        name="matmul",
    )(*args)


def _ffn_body(x_ref, g_ref, wg_ref, wu_ref, wd_ref, gf_ref, o_ref, xn_ref, acc_ref, *, final_norm):
    f = pl.program_id(1)

    @pl.when(f == 0)
    def _():
        xn_ref[...] = _rms(x_ref[...], g_ref[...]).astype(BF16)
        acc_ref[...] = jnp.zeros_like(acc_ref)

    xn = xn_ref[...]
    hg = jnp.dot(xn, wg_ref[0], preferred_element_type=F32)
    hu = jnp.dot(xn, wu_ref[0], preferred_element_type=F32)
    h = hg * jax.nn.sigmoid(hg) * hu
    acc_ref[...] += jnp.dot(h.astype(BF16), wd_ref[0], preferred_element_type=F32)

    @pl.when(f == pl.num_programs(1) - 1)
    def _():
        y = x_ref[...] + acc_ref[...]
        if final_norm:
            y = _rms(y, gf_ref[...])
        o_ref[...] = y


def _ffn(x, g, wg, wu, wd, layer, gf, *, final_norm, tf=512):
    m, k = x.shape
    dff = wg.shape[2]
    tm = _row_tile(m, 512)
    return pl.pallas_call(
        functools.partial(_ffn_body, final_norm=final_norm),
        grid=(m // tm, dff // tf),
        in_specs=[pl.BlockSpec((tm, k), lambda i, f: (i, 0)),
                  pl.BlockSpec((1, k), lambda i, f: (0, 0)),
                  pl.BlockSpec((1, k, tf), lambda i, f: (layer, 0, f)),
                  pl.BlockSpec((1, k, tf), lambda i, f: (layer, 0, f)),
                  pl.BlockSpec((1, tf, k), lambda i, f: (layer, f, 0)),
                  pl.BlockSpec((1, k), lambda i, f: (0, 0))],
        out_specs=pl.BlockSpec((tm, k), lambda i, f: (i, 0)),
        out_shape=jax.ShapeDtypeStruct((m, k), F32),
        scratch_shapes=[pltpu.VMEM((tm, k), BF16), pltpu.VMEM((tm, k), F32)],
        compiler_params=_cparams(("parallel", "arbitrary")),
        name="ffn",
    )(x, g.reshape(1, k), wg, wu, wd, gf.reshape(1, k))


def _lru_body(gate_ref, xb_ref, cw_ref, cb_ref, wg_ref, bg_ref, lam_ref, h0_ref, cbuf_ref,
              y_ref, hout_ref, cout_ref, xbuf, hcar, *, tt, p0):
    t = pl.program_id(1)
    w = B_WIDTH
    halo = SUBLANES

    @pl.when(t == 0)
    def _():
        xbuf[0:halo, :] = jnp.zeros((halo, w), F32)
        xbuf[halo - (B_CONV - 1):halo, :] = cbuf_ref[0]
        hcar[...] = h0_ref[0]

    xb = xb_ref[0]
    xbuf[halo:halo + tt, :] = xb
    xc = cb_ref[...] + cw_ref[B_CONV - 1:B_CONV, :] * xb
    for j in range(B_CONV - 1):
        off = halo - (B_CONV - 1) + j
        xc = xc + cw_ref[j:j + 1, :] * xbuf[off:off + tt, :]
    tail = xbuf[tt + halo - (B_CONV - 1):tt + halo, :]
    xbuf[halo - (B_CONV - 1):halo, :] = tail

    pre = jnp.dot(xc.astype(BF16), wg_ref[...], preferred_element_type=F32) + bg_ref[...]
    rg = jax.nn.sigmoid(pre[:, :w])
    ig = jax.nn.sigmoid(pre[:, w:])
    log_a = (-B_C) * rg * _softplus(-lam_ref[...])
    a = jnp.exp(log_a)
    row = lax.broadcasted_iota(I32, (tt, w), 0)
    th = jnp.tanh(log_a)
    mult = jnp.sqrt(-2.0 * th / (1.0 - th))
    mult = jnp.where(row + (p0 + t * tt) == 0, 1.0, mult)
    u = mult * (ig * xc)

    d = 1
    while d < tt:
        keep = row >= d
        a_sh = pltpu.roll(a, d, 0)
        u_sh = pltpu.roll(u, d, 0)
        u = u + jnp.where(keep, a * u_sh, 0.0)
        a = jnp.where(keep, a * a_sh, a)
        d *= 2
    h = u + a * hcar[...]
    hcar[...] = h[tt - 1:tt, :]
    y_ref[0] = (h * jax.nn.gelu(gate_ref[0])).astype(y_ref.dtype)

    @pl.when(t == pl.num_programs(1) - 1)
    def _():
        hout_ref[0] = h[tt - 1:tt, :]
        cout_ref[0] = tail


def _rglru(p3, gate_blk, xb_blk, cw, cb, wgate, bgate, lam, h0, cbuf, *, p0):
    b, t, _ = p3.shape
    w = B_WIDTH
    tt = min(256, t)
    row = lambda v: v.reshape(1, -1)
    return pl.pallas_call(
        functools.partial(_lru_body, tt=tt, p0=p0),
        grid=(b, t // tt),
        in_specs=[pl.BlockSpec((1, tt, w), lambda i, j: (i, j, gate_blk)),
                  pl.BlockSpec((1, tt, w), lambda i, j: (i, j, xb_blk)),
                  pl.BlockSpec((B_CONV, w), lambda i, j: (0, 0)),
                  pl.BlockSpec((1, w), lambda i, j: (0, 0)),
                  pl.BlockSpec((w, 2 * w), lambda i, j: (0, 0)),
                  pl.BlockSpec((1, 2 * w), lambda i, j: (0, 0)),
                  pl.BlockSpec((1, w), lambda i, j: (0, 0)),
                  pl.BlockSpec((1, 1, w), lambda i, j: (i, 0, 0)),
                  pl.BlockSpec((1, B_CONV - 1, w), lambda i, j: (i, 0, 0))],
        out_specs=[pl.BlockSpec((1, tt, w), lambda i, j: (i, j, 0)),
                   pl.BlockSpec((1, 1, w), lambda i, j: (i, 0, 0)),
                   pl.BlockSpec((1, B_CONV - 1, w), lambda i, j: (i, 0, 0))],
        out_shape=[jax.ShapeDtypeStruct((b, t, w), BF16),
                   jax.ShapeDtypeStruct((b, 1, w), F32),
                   jax.ShapeDtypeStruct((b, B_CONV - 1, w), F32)],
        scratch_shapes=[pltpu.VMEM((tt + SUBLANES, w), F32), pltpu.VMEM((1, w), F32)],
        compiler_params=_cparams(("parallel", "arbitrary")),
        name="rglru",
    )(p3, p3, cw, row(cb), wgate, row(bgate), row(lam), h0.reshape(b, 1, w), cbuf)


RWKV_PASSES = 1
RWKV_PASSES_SOLVE = 3


def _rwkv_body(rkv_ref, lora_ref, shr_ref, shl_ref, s0_ref, mur_ref, mul_ref, w0_ref, w2_ref, a0_ref,
               a2_ref, g2_ref, kk_ref, ka_ref, rk_ref, lnw_ref, lnb_ref, tri_ref,
               y_ref, sout_ref, buf_r, buf_l, s_scr, st_r, st_k, st_v, st_q, st_a, st_l, st_d, *, c):
    ci = pl.program_id(1)
    halo = SUBLANES
    hn, n, wd = A_HEADS, A_HEAD_DIM, A_WIDTH
    mm = functools.partial(_dotp, passes=RWKV_PASSES)

    @pl.when(ci == 0)
    def _():
        buf_r[halo - 1:halo, :] = shr_ref[0]
        buf_l[halo - 1:halo, :] = shl_ref[0]
        s_scr[...] = s0_ref[0]

    cur_r = rkv_ref[0]
    cur_l = lora_ref[0]
    buf_r[halo:halo + c, :] = cur_r
    buf_l[halo:halo + c, :] = cur_l
    xm = cur_r + mur_ref[...] * (buf_r[halo - 1:halo - 1 + c, :] - cur_r)
    lo = cur_l + mul_ref[...] * (buf_l[halo - 1:halo - 1 + c, :] - cur_l)
    buf_r[halo - 1:halo, :] = cur_r[c - 1:c, :]
    buf_l[halo - 1:halo, :] = cur_l[c - 1:c, :]

    r = xm[:, :wd]
    k = xm[:, wd:2 * wd]
    v = xm[:, 2 * wd:]
    lo_a = lo[:, :LANES]
    w_pre = w0_ref[...] + jnp.dot(jnp.tanh(lo_a).astype(BF16), w2_ref[...], preferred_element_type=F32)
    w_log = -_softplus(-w_pre) - 0.5
    ld = -jnp.exp(w_log)
    a = jax.nn.sigmoid(a0_ref[...] + jnp.dot(lo_a.astype(BF16), a2_ref[...], preferred_element_type=F32))
    g = jnp.dot(jax.nn.sigmoid(lo[:, LANES:]).astype(BF16), g2_ref[...], preferred_element_type=F32)
    kq = k * kk_ref[...]
    k2 = k * (1.0 + (a - 1.0) * ka_ref[...])
    lc = lax.dot_general(tri_ref[...], ld, NN, precision=lax.Precision.HIGHEST,
                         preferred_element_type=F32)

    for h in range(hn):
        sl = slice(h * n, (h + 1) * n)
        st_r[h] = r[:, sl]
        st_k[h] = k2[:, sl]
        st_v[h] = v[:, sl]
        st_q[h] = kq[:, sl]
        st_a[h] = a[:, sl]
        st_l[h] = lc[:, sl]
        st_d[h] = ld[:, sl]

    rh_, k2h, vh, kqh, ah, lch, ldh = (st_r[...], st_k[...], st_v[...], st_q[...], st_a[...],
                                       st_l[...], st_d[...])
    nrm = jnp.sqrt(jnp.sum(kqh * kqh, axis=-1, keepdims=True))
    kk = kqh / jnp.maximum(nrm, 1e-12)
    kka = kk * ah
    e_neg = jnp.exp(-lch)
    am = jnp.exp(lch - ldh) * kk
    bm = kka * e_neg
    kh = k2h * e_neg
    rh = rh_ * jnp.exp(lch)
    l_end = lch[:, c - 1:c, :]
    e_c = jnp.exp(l_end - lch)
    bp = kka * e_c
    kp = k2h * e_c
    w_end = jnp.exp(l_end)

    x2 = jnp.concatenate([am, rh], axis=1)
    zb = _dotp(x2, bm, BNT, RWKV_PASSES_SOLVE)
    zk = mm(x2, kh, BNT)
    ti = lax.broadcasted_iota(I32, (hn, c, c), 1)
    si = lax.broadcasted_iota(I32, (hn, c, c), 2)
    strict = si < ti
    incl = si <= ti
    m1 = jnp.where(strict, zb[:, :c], 0.0)
    m4 = jnp.where(incl, zb[:, c:], 0.0)
    m2 = jnp.where(strict, zk[:, :c], 0.0)
    m3 = jnp.where(incl, zk[:, c:], 0.0)

    tm = jnp.where(si == ti, 1.0, 0.0) - m1
    npow = mm(m1, m1, BNN)
    span = 2
    while span < c:
        tm = tm + mm(tm, npow, BNN)
        span *= 2
        if span < c:
            npow = mm(npow, npow, BNN)

    s0 = s_scr[...]
    rhs = mm(am, s0, BNT) + mm(m2, vh, BNN)
    p = mm(tm, rhs, BNN)
    y = mm(rh, s0, BNT) + mm(m3, vh, BNN) - mm(m4, p, BNN)
    s_new = s0 * w_end + mm(vh, kp, BTN) - mm(p, bp, BTN)
    s_scr[...] = s_new

    mean = jnp.mean(y, axis=-1, keepdims=True)
    yc = y - mean
    var = jnp.mean(yc * yc, axis=-1, keepdims=True)
    yn = yc * lax.rsqrt(var + A_LN_EPS) * lnw_ref[...] + lnb_ref[...]
    bonus = jnp.sum(rh_ * k2h * rk_ref[...], axis=-1, keepdims=True) * vh
    yo = yn + bonus
    yo = jnp.concatenate([yo[h] for h in range(hn)], axis=-1)
    y_ref[0] = (yo * g).astype(y_ref.dtype)

    @pl.when(ci == pl.num_programs(1) - 1)
    def _():
        sout_ref[0] = s_new


def _rwkv(p3, lora_blk, shift_r, shift_l, s0, wts):
    b, t, _ = p3.shape
    c = min(CHUNK, t)
    hn, n, wd = A_HEADS, A_HEAD_DIM, A_WIDTH
    full = lambda shape: pl.BlockSpec(shape, lambda i, j: (0,) * len(shape))
    tri = (jnp.arange(c)[:, None] >= jnp.arange(c)[None, :]).astype(F32)
    st = pltpu.VMEM((hn, c, n), F32)
    return pl.pallas_call(
        functools.partial(_rwkv_body, c=c),
        grid=(b, t // c),
        in_specs=[pl.BlockSpec((1, c, 3 * wd), lambda i, j: (i, j, 0)),
                  pl.BlockSpec((1, c, LORA_PAD), lambda i, j: (i, j, lora_blk)),
                  pl.BlockSpec((1, 1, 3 * wd), lambda i, j: (i, 0, 0)),
                  pl.BlockSpec((1, 1, LORA_PAD), lambda i, j: (i, 0, 0)),
                  pl.BlockSpec((1, hn, n, n), lambda i, j: (i, 0, 0, 0)),
                  full((1, 3 * wd)), full((1, LORA_PAD)), full((1, wd)), full((LANES, wd)),
                  full((1, wd)), full((LANES, wd)), full((LORA_PAD - LANES, wd)),
                  full((1, wd)), full((1, wd)), full((hn, 1, n)), full((hn, 1, n)), full((hn, 1, n)),
                  full((c, c))],
        out_specs=[pl.BlockSpec((1, c, wd), lambda i, j: (i, j, 0)),
                   pl.BlockSpec((1, hn, n, n), lambda i, j: (i, 0, 0, 0))],
        out_shape=[jax.ShapeDtypeStruct((b, t, wd), BF16),
                   jax.ShapeDtypeStruct((b, hn, n, n), F32)],
        scratch_shapes=[pltpu.VMEM((c + SUBLANES, 3 * wd), F32), pltpu.VMEM((c + SUBLANES, LORA_PAD), F32),
                        pltpu.VMEM((hn, n, n), F32), st, st, st, st, st, st, st],
        compiler_params=_cparams(("parallel", "arbitrary")),
        name="rwkv7",
    )(p3, p3, shift_r, shift_l, s0, wts["mu_r"], wts["mu_l"], wts["w0"], wts["w2"], wts["a0"], wts["a2"],
      wts["g2"], wts["k_k"], wts["k_a"], wts["r_k"], wts["ln_w"], wts["ln_b"], tri)


def _rope(x, cos, sin):
    wdt = x.shape[-1]
    lane = lax.broadcasted_iota(I32, x.shape, 1)
    first = (lane % D_ROPE) < (D_ROPE // 2)
    rot = jnp.where(first, -pltpu.roll(x, wdt - D_ROPE // 2, 1), pltpu.roll(x, D_ROPE // 2, 1))
    return x * cos + rot * sin


def _mla_prep_body(qd_ref, kvd_ref, sm_ref, gq_ref, gkv_ref, wuq_ref, cos_ref, sin_ref,
                   qn_ref, qr_ref, lat_ref, kr_ref):
    nope_w = D_HEADS * D_NOPE
    qdn = _rms(qd_ref[0], gq_ref[...]).astype(BF16)
    qf = jnp.dot(qdn, wuq_ref[...], preferred_element_type=F32) * ((D_NOPE + D_ROPE) ** -0.5 * LOG2E)
    qn_ref[0] = qf[:, :nope_w].astype(qn_ref.dtype)
    cos = cos_ref[...]
    sin = sin_ref[...]
    qr_ref[0] = _rope(qf[:, nope_w:], cos, sin).astype(qr_ref.dtype)
    lat_ref[0] = _rms(kvd_ref[0], gkv_ref[...])
    sm = sm_ref[0][:, :LANES]
    kr = _rope(sm, cos[:, :LANES], sin[:, :LANES])
    kr_ref[0] = kr[:, C_IDX_DIM:C_IDX_DIM + D_ROPE]


def _mla_prep(p3, qd_blk, kvd_blk, sm_blk, gq, gkv, wuq, cos, sin):
    b, t, _ = p3.shape
    tt = min(256, t)
    rw = D_HEADS * D_ROPE
    full = lambda shape: pl.BlockSpec(shape, lambda i, j: (0,) * len(shape))
    return pl.pallas_call(
        _mla_prep_body,
        grid=(b, t // tt),
        in_specs=[pl.BlockSpec((1, tt, D_Q_RANK), lambda i, j: (i, j, qd_blk)),
                  pl.BlockSpec((1, tt, D_KV_RANK), lambda i, j: (i, j, kvd_blk)),
                  pl.BlockSpec((1, tt, 512), lambda i, j: (i, j, sm_blk)),
                  full((1, D_Q_RANK)), full((1, D_KV_RANK)), full(wuq.shape),
                  pl.BlockSpec((tt, rw), lambda i, j: (j, 0)),
                  pl.BlockSpec((tt, rw), lambda i, j: (j, 0))],
        out_specs=[pl.BlockSpec((1, tt, D_HEADS * D_NOPE), lambda i, j: (i, j, 0)),
                   pl.BlockSpec((1, tt, rw), lambda i, j: (i, j, 0)),
                   pl.BlockSpec((1, tt, D_KV_RANK), lambda i, j: (i, j, 0)),
                   pl.BlockSpec((1, tt, D_ROPE), lambda i, j: (i, j, 0))],
        out_shape=[jax.ShapeDtypeStruct((b, t, D_HEADS * D_NOPE), BF16),
                   jax.ShapeDtypeStruct((b, t, rw), BF16),
                   jax.ShapeDtypeStruct((b, t, D_KV_RANK), F32),
                   jax.ShapeDtypeStruct((b, t, D_ROPE), F32)],
        compiler_params=_cparams(("parallel", "parallel")),
        name="mla_prep",
    )(p3, p3, p3, gq.reshape(1, -1), gkv.reshape(1, -1), wuq, cos, sin)


def _bias_table_body(rb_ref, o_ref):
    d = pl.program_id(0)
    h = pl.program_id(1)
    nb = N_BUCKETS // 2
    max_exact = nb // 2
    s = lax.broadcasted_iota(I32, (LANES, LANES), 0)
    q = lax.broadcasted_iota(I32, (LANES, LANES), 1)
    rel = s - q - d * LANES
    n = jnp.abs(rel)
    big = jnp.maximum(n, max_exact).astype(F32)
    large = max_exact + (jnp.log(big / max_exact) / math.log(MAX_DISTANCE / max_exact)
                         * (nb - max_exact)).astype(I32)
    large = jnp.minimum(large, nb - 1)
    bucket = jnp.where(rel > 0, nb, 0) + jnp.where(n < max_exact, n, large)
    out = jnp.zeros((LANES, LANES), F32)
    for bk in range(N_BUCKETS):
        out = jnp.where(bucket == bk, rb_ref[bk, h], out)
    o_ref[0, 0] = out * LOG2E


def _bias_tables(rel_bias):
    return pl.pallas_call(
        _bias_table_body,
        grid=(2, C_HEADS),
        in_specs=[pl.BlockSpec(memory_space=pltpu.SMEM)],
        out_specs=pl.BlockSpec((1, 1, LANES, LANES), lambda d, h: (d, h, 0, 0)),
        out_shape=jax.ShapeDtypeStruct((2, C_HEADS, LANES, LANES), F32),
        name="bias_tables",
    )(rel_bias)


def _transpose32(x):
    x = list(x)
    for s, msk in ((16, 0x0000FFFF), (8, 0x00FF00FF), (4, 0x0F0F0F0F), (2, 0x33333333), (1, 0x55555555)):
        sh = jnp.full(x[0].shape, s, I32)
        for i in range(32):
            if i & s == 0:
                t = (lax.shift_right_logical(x[i], sh) ^ x[i + s]) & msk
                x[i + s] = x[i + s] ^ t
                x[i] = x[i] ^ lax.shift_left(t, sh)
    return x


def _attn_body(qt_tab, kt_tab, *refs, mode, p0, t_valid, l_valid, tq, tk, dk, topk, nh):
    dsa = mode == "dsa"
    dh = LANES
    if dsa:
        (qT_ref, k_ref, vT_ref, q3_ref, ki3_ref, wiT_ref, tab_ref, far_ref,
         o_ref, m_scr, l_scr, acc_scr, s_scr, p_scr, mb_scr, skey_scr, thr_scr, planes_scr, e_scr) = refs
    else:
        (qT_ref, k_ref, vT_ref, o_ref, m_scr, l_scr, acc_scr, s_scr, p_scr, mb_scr) = refs
    qt = qt_tab[pl.program_id(1)]
    kt = kt_tab[pl.program_id(1)]
    q_lo = p0 + qt * tq
    q_hi = p0 + jnp.minimum(qt * tq + tq, t_valid) - 1
    n_allowed = jnp.minimum((q_hi // CHUNK + 1) * CHUNK, l_valid)
    last_kt = (n_allowed - 1) // tk
    qpos = q_lo + lax.broadcasted_iota(I32, (1, tq), 1)
    qchunk = qpos // CHUNK

    def allowed_mask(k0, rows):
        kidx = k0 + lax.broadcasted_iota(I32, (rows, tq), 0)
        return (kidx // CHUNK <= qchunk) & (kidx < l_valid)

    @pl.when(kt == 0)
    def _():
        m_scr[...] = jnp.full(m_scr.shape, NEG_BIG, F32)
        l_scr[...] = jnp.zeros(l_scr.shape, F32)
        acc_scr[...] = jnp.zeros(acc_scr.shape, F32)

    if dsa:
        iscale = (C_IDX_HEADS * C_IDX_DIM) ** -0.5

        @pl.when(kt == 0)
        def _():
            wpt = tk // 32
            kw = 3 * C_IDX_DIM

            def score_tile(j, carry):
                k0 = pl.multiple_of(j * tk, tk)
                ki3 = ki3_ref[0, pl.ds(k0, tk), :]
                s = jnp.zeros((tk, tq), F32)
                for h in range(C_IDX_HEADS):
                    d = jnp.dot(ki3, q3_ref[0, h * kw:(h + 1) * kw, :], preferred_element_type=F32)
                    s = s + jnp.maximum(d, 0.0) * wiT_ref[0, h:h + 1, :]
                s = s * iscale + 0.0
                bits = pltpu.bitcast(s, I32)
                key = bits ^ ((bits >> 31) & 0x7FFFFFFF)
                key = jnp.where(allowed_mask(k0, tk), key, INT_MIN)
                skey_scr[pl.ds(k0, tk), :] = key
                ukey = key ^ INT_MIN
                w0 = pl.multiple_of(j * wpt, SUBLANES)
                for g in range(tk // 256):
                    rows = [ukey[g * 256 + 8 * i:g * 256 + 8 * i + 8, :] for i in range(32)]
                    for bi, plane in enumerate(_transpose32(rows)):
                        planes_scr[bi, pl.ds(w0 + g * SUBLANES, SUBLANES), :] = plane
                ones = jnp.full((wpt, tq), -1, I32)
                planes_scr[32, pl.ds(w0, wpt), :] = ones
                e_scr[pl.ds(w0, wpt), :] = ones
                return carry

            def blank_tile(j, carry):
                w0 = pl.multiple_of(j * wpt, SUBLANES)
                for bi in range(33):
                    planes_scr[bi, pl.ds(w0, wpt), :] = jnp.zeros((wpt, tq), I32)
                e_scr[pl.ds(w0, wpt), :] = jnp.zeros((wpt, tq), I32)
                return carry

            ntile = last_kt + 1
            lax.fori_loop(0, ntile, score_tile, 0)
            nblk = (ntile + SEARCH_TILES - 1) // SEARCH_TILES
            lax.fori_loop(ntile, nblk * SEARCH_TILES, blank_tile, 0)
            wpb = SEARCH_TILES * wpt

            def lanesum(acc):
                return jnp.sum(acc, axis=0, keepdims=True)

            def fold(pc):
                return jnp.sum(pc.reshape(wpb // SUBLANES, SUBLANES, tq), axis=0)

            def settle(e, plane, take):
                t = e & plane
                return jnp.where(take != 0, t, e ^ t)

            def bit_step(i, st):
                c_gt, th, take_prev = st
                bi = 31 - i

                def body(j, acc):
                    w0 = pl.multiple_of(j * wpb, SUBLANES)
                    e = settle(e_scr[pl.ds(w0, wpb), :], planes_scr[bi + 1, pl.ds(w0, wpb), :], take_prev)
                    e_scr[pl.ds(w0, wpb), :] = e
                    return acc + fold(lax.population_count(e & planes_scr[bi, pl.ds(w0, wpb), :]))

                c1 = lanesum(lax.fori_loop(0, nblk, body, jnp.zeros((SUBLANES, tq), I32)))
                take = (c_gt + c1) >= topk
                return (jnp.where(take, c_gt, c_gt + c1),
                        jnp.where(take, th | lax.shift_left(jnp.int32(1), bi), th), jnp.where(take, 1, 0))

            zero = jnp.zeros((1, tq), I32)
            c_gt, th_u, take0 = lax.fori_loop(0, 32, bit_step, (zero, zero, zero + 1))

            def last_body(j, acc):
                w0 = pl.multiple_of(j * wpb, SUBLANES)
                e = settle(e_scr[pl.ds(w0, wpb), :], planes_scr[0, pl.ds(w0, wpb), :], take0)
                e_scr[pl.ds(w0, wpb), :] = e
                return acc + fold(lax.population_count(e))

            n_eq = lanesum(lax.fori_loop(0, nblk, last_body, jnp.zeros((SUBLANES, tq), I32)))
            need = topk - c_gt
            nbits = max(1, (l_valid - 1).bit_length())
            thr_scr[0:1, :] = th_u ^ INT_MIN
            thr_scr[1:2, :] = jnp.full((1, tq), (1 << nbits) - 1, I32)

            @pl.when(jnp.max((n_eq - need).astype(F32)) > 0.0)
            def _():
                def idx_step(i, jb):
                    cand = jb - lax.shift_left(jnp.int32(1), nbits - 1 - i)

                    def body(j, acc):
                        w0 = pl.multiple_of(j * wpb, SUBLANES)
                        wr = w0 + lax.broadcasted_iota(I32, (wpb, tq), 0)
                        base = (wr >> 3) * 256 + (wr & 7)
                        mx = (cand - base) >> 3
                        low = jnp.left_shift(2, jnp.clip(mx, 0, 30)) - 1
                        msk = jnp.where(mx < 0, 0, jnp.where(mx >= 31, -1, low))
                        return acc + fold(lax.population_count(e_scr[pl.ds(w0, wpb), :] & msk))

                    cnt = lanesum(lax.fori_loop(0, nblk, body, jnp.zeros((SUBLANES, tq), I32)))
                    return jnp.where(cnt >= need, cand, jb)

                thr_scr[1:2, :] = lax.fori_loop(0, nbits, idx_step, jnp.full((1, tq), (1 << nbits) - 1, I32))

    def tile(near):
        k0 = pl.multiple_of(kt * tk, tk)
        nchunk = tk // RC
        masked = dsa or near
        if masked:
            for c in range(nchunk):
                r0 = k0 + c * RC
                mask = allowed_mask(r0, RC) if near else None
                if dsa:
                    x = skey_scr[pl.ds(r0, RC), :]
                    kidx = r0 + lax.broadcasted_iota(I32, (RC, tq), 0)
                    sel = (x > thr_scr[0:1, :]) | ((x == thr_scr[0:1, :]) & (kidx <= thr_scr[1:2, :]))
                    mask = sel & mask if near else sel
                mb_scr[c * RC:(c + 1) * RC, :] = jnp.where(mask, 0.0, NEG_BIG)

        def chunk_bias(h, c):
            far = far_ref[0:1, h:h + 1]
            if not near:
                return far
            sb, off = (c * RC) // LANES, (c * RC) % LANES
            cols = []
            for qb in range(tq // LANES):
                delta = k0 + sb * LANES - (q_lo + qb * LANES)
                cols.append(jnp.where(delta == 0, tab_ref[0, h, off:off + RC, :],
                                      jnp.where(delta == -LANES, tab_ref[1, h, off:off + RC, :], far)))
            return jnp.concatenate(cols, axis=1) if len(cols) > 1 else cols[0]

        for h in range(nh):
            ks = slice(h * dk, (h + 1) * dk)
            s_scr[h] = jnp.dot(k_ref[0, :, ks], qT_ref[0, ks, :], preferred_element_type=F32)
        m_news, alphas = [], []
        for h in range(nh):
            mx = jnp.full((SUBLANES, tq), NEG_BIG, F32)
            for c in range(nchunk):
                rows = slice(c * RC, (c + 1) * RC)
                blk = s_scr[h, rows, :]
                if masked:
                    blk = blk + mb_scr[rows, :]
                    if dsa:
                        blk = blk + chunk_bias(h, c)
                    s_scr[h, rows, :] = blk
                mx = jnp.maximum(mx, jnp.max(blk.reshape(RC // SUBLANES, SUBLANES, tq), axis=0))
            m_prev = m_scr[h:h + 1, :]
            m_new = jnp.maximum(m_prev, jnp.max(mx, axis=0, keepdims=True))
            m_scr[h:h + 1, :] = m_new
            m_news.append(m_new)
            alphas.append(jnp.exp2(m_prev - m_new))
        for h in range(nh):
            for c in range(nchunk):
                rows = slice(c * RC, (c + 1) * RC)
                p_scr[h, rows, :] = jnp.exp2(s_scr[h, rows, :] - m_news[h]).astype(BF16)
        for h in range(nh):
            sl = slice(h * dh, (h + 1) * dh)
            pv = jnp.dot(vT_ref[0, h * VT_ROWS:(h + 1) * VT_ROWS, :], p_scr[h], preferred_element_type=F32)
            acc_scr[sl, :] = alphas[h] * acc_scr[sl, :] + pv[:dh]
            l_scr[h:h + 1, :] = alphas[h] * l_scr[h:h + 1, :] + pv[dh:dh + 1]

    is_far = kt * tk + tk - 1 <= q_lo - LANES

    @pl.when(is_far)
    def _():
        tile(False)

    @pl.when(jnp.logical_not(is_far))
    def _():
        tile(True)

    @pl.when(kt == last_kt)
    def _():
        for h in range(nh):
            sl = slice(h * dh, (h + 1) * dh)
            o = acc_scr[sl, :] / l_scr[h:h + 1, :]
            o_ref[0, :, sl] = o.T.astype(o_ref.dtype)


def _attention(mode, qT, k, vT, extra, *, p0, t_valid, l_valid, tk):
    b, hdk, tpad = qT.shape
    lpad = k.shape[1]
    nh = vT.shape[1] // VT_ROWS
    hd = nh * LANES
    dk = hdk // nh
    tq = 2 * LANES if tpad % (2 * LANES) == 0 else LANES
    nq, nk = tpad // tq, lpad // tk
    topk = min(TOPK_MAX, l_valid // 4)
    assert p0 % LANES == 0 and lpad % tk == 0 and tk % 256 == 0 and l_valid >= topk >= 1 and tk >= topk

    pairs = []
    for qt in range(nq):
        q_hi = p0 + min(qt * tq + tq, t_valid) - 1
        last = (min((q_hi // CHUNK + 1) * CHUNK, l_valid) - 1) // tk
        pairs += [(qt, kt) for kt in range(last + 1)]
    qt_tab = jnp.asarray([p[0] for p in pairs], I32)
    kt_tab = jnp.asarray([p[1] for p in pairs], I32)

    kmap = lambda i, s, qtt, ktt: (i, ktt[s], 0)
    vmap_ = lambda i, s, qtt, ktt: (i, 0, ktt[s])
    qmap = lambda i, s, qtt, ktt: (i, 0, qtt[s])
    cmap = lambda n: (lambda i, s, qtt, ktt: (0,) * n)
    in_specs = [pl.BlockSpec((1, hdk, tq), qmap),
                pl.BlockSpec((1, tk, hdk), kmap),
                pl.BlockSpec((1, nh * VT_ROWS, tk), vmap_)]
    scratch = [pltpu.VMEM((nh, tq), F32), pltpu.VMEM((nh, tq), F32), pltpu.VMEM((hd, tq), F32),
               pltpu.VMEM((nh, tk, tq), F32), pltpu.VMEM((nh, tk, tq), BF16), pltpu.VMEM((tk, tq), F32)]
    if mode == "dsa":
        q3, ki3, wiT, tabs, far = extra
        wrows = -(-nk // SEARCH_TILES) * SEARCH_TILES * (tk // 32)
        in_specs += [pl.BlockSpec((1, q3.shape[1], tq), qmap),
                     pl.BlockSpec((1, lpad, ki3.shape[2]), lambda i, s, qtt, ktt: (i, 0, 0)),
                     pl.BlockSpec((1, C_IDX_HEADS, tq), qmap),
                     pl.BlockSpec(tabs.shape, cmap(4)),
                     pl.BlockSpec(far.shape, cmap(2))]
        scratch += [pltpu.VMEM((lpad, tq), I32), pltpu.VMEM((SUBLANES, tq), I32),
                    pltpu.VMEM((33, wrows, tq), I32), pltpu.VMEM((wrows, tq), I32)]
    return pl.pallas_call(
        functools.partial(_attn_body, mode=mode, p0=p0, t_valid=t_valid, l_valid=l_valid, tq=tq, tk=tk,
                          dk=dk, topk=topk, nh=nh),
        grid_spec=pltpu.PrefetchScalarGridSpec(
            num_scalar_prefetch=2,
            grid=(b, len(pairs)),
            in_specs=in_specs,
            out_specs=pl.BlockSpec((1, tq, hd), lambda i, s, qtt, ktt: (i, qtt[s], 0)),
            scratch_shapes=scratch),
        out_shape=jax.ShapeDtypeStruct((b, tpad, hd), BF16),
        compiler_params=_cparams(("parallel", "arbitrary")),
        name="attn_" + mode,
    )(qt_tab, kt_tab, qT, k, vT, *extra)


PACK_TILE = 256


def _pack_body(cache_ref, new_ref, o_ref, *, n_cache, transpose, nh):
    j = pl.program_id(1)

    def emit(head):
        for h in range(nh):
            sl = slice(h * LANES, (h + 1) * LANES)
            x = head(h, sl)
            if transpose:
                r0 = h * VT_ROWS
                o_ref[0, r0:r0 + LANES, :] = x.T.astype(o_ref.dtype)
                o_ref[0, r0 + LANES:r0 + VT_ROWS, :] = jnp.ones((VT_ONES, x.shape[0]), o_ref.dtype)
            else:
                o_ref[0, :, sl] = x.astype(o_ref.dtype)

    @pl.when(j < n_cache)
    def _():
        emit(lambda h, sl: cache_ref[0, 0, :, h, :])

    @pl.when(j >= n_cache)
    def _():
        emit(lambda h, sl: new_ref[0, :, sl])


def _pack_keys(cache, layer, new, lpad, *, transpose):
    _, b, past, nh, dh = cache.shape
    tp = 2 * PACK_TILE if (past % (2 * PACK_TILE) == 0 and lpad % (2 * PACK_TILE) == 0) else PACK_TILE
    assert dh == LANES and past % tp == 0 and lpad % tp == 0
    n_cache = past // tp
    hd = nh * dh
    new = _pad_axis(new, 1, lpad - past)
    out_shape = (b, nh * VT_ROWS, lpad) if transpose else (b, lpad, hd)
    out_spec = (pl.BlockSpec((1, nh * VT_ROWS, tp), lambda i, j: (i, 0, j)) if transpose
                else pl.BlockSpec((1, tp, hd), lambda i, j: (i, j, 0)))
    return pl.pallas_call(
        functools.partial(_pack_body, n_cache=n_cache, transpose=transpose, nh=nh),
        grid=(b, lpad // tp),
        in_specs=[pl.BlockSpec((1, 1, tp, nh, dh),
                               lambda i, j: (layer, i, jnp.minimum(j, n_cache - 1), 0, 0)),
                  pl.BlockSpec((1, tp, hd), lambda i, j: (i, jnp.maximum(j - n_cache, 0), 0))],
        out_specs=out_spec,
        out_shape=jax.ShapeDtypeStruct(out_shape, BF16),
        compiler_params=_cparams(("parallel", "parallel")),
        name="pack_keys",
    )(cache, new)


def _split_hi_lo(x):
    hi = x.astype(BF16)
    lo = (x - hi.astype(F32)).astype(BF16)
    return hi, lo


def _pad_axis(x, axis, size):
    if x.shape[axis] == size:
        return x
    pad = [(0, 0)] * x.ndim
    pad[axis] = (0, size - x.shape[axis])
    return jnp.pad(x, pad)


def _vt_ones(v):
    b, l, hd = v.shape
    vt = jnp.swapaxes(v, 1, 2).reshape(b, hd // LANES, LANES, l)
    ones = jnp.ones((b, hd // LANES, VT_ONES, l), v.dtype)
    return jnp.concatenate([vt, ones], axis=2).reshape(b, -1, l)


def _key_tile(l_valid):
    tk = 512 if l_valid > 1024 else 256
    return -(-l_valid // tk) * tk, tk


def _prep_weights(W, d_model):
    f = {}
    wd = A_WIDTH
    n_even = W["ev_w_in"].shape[0]
    n_odd = W["od_w_in"].shape[0]
    f["even"] = []
    for e in range(n_even):
        wi = W["ev_w_in"][e]
        w_in = jnp.concatenate([wi[:, :3 * wd], wi[:, A_COLS:], wi[:, 3 * wd:A_COLS],
                                jnp.zeros((d_model, LORA_PAD - A_LORA), F32)], axis=1).astype(BF16)
        mu = W["rwkv_mu"][e]
        w2 = jnp.zeros((LANES, wd), F32).at[:A_DECAY_LORA].set(W["rwkv_w2"][e])
        a2 = jnp.zeros((LANES, wd), F32).at[A_DECAY_LORA:A_DECAY_LORA + A_ICLR_LORA].set(W["rwkv_a2"][e])
        g2 = jnp.zeros((LORA_PAD - LANES, wd), F32).at[:A_GATE_LORA].set(W["rwkv_g2"][e])
        hm = lambda v: v.reshape(A_HEADS, 1, A_HEAD_DIM)
        rw = dict(mu_r=mu[:3 * wd].reshape(1, -1),
                  mu_l=_pad_axis(mu[3 * wd:], 0, LORA_PAD).reshape(1, -1),
                  w0=W["rwkv_w0"][e].reshape(1, -1), w2=w2.astype(BF16),
                  a0=W["rwkv_a0"][e].reshape(1, -1), a2=a2.astype(BF16), g2=g2.astype(BF16),
                  k_k=W["rwkv_k_k"][e].reshape(1, -1), k_a=W["rwkv_k_a"][e].reshape(1, -1),
                  r_k=hm(W["rwkv_r_k"][e]), ln_w=hm(W["rwkv_ln_w"][e]), ln_b=hm(W["rwkv_ln_b"][e]))
        eye = jnp.eye(B_BLOCKS, dtype=F32)
        blockdiag = lambda w: (eye[:, None, :, None] * w[:, :, None, :]).reshape(B_WIDTH, B_WIDTH)
        wgate = jnp.concatenate([blockdiag(W["lru_wa"][e]), blockdiag(W["lru_wx"][e])], axis=1).astype(BF16)
        bgate = jnp.concatenate([W["lru_ba"][e], W["lru_bx"][e]])
        wo = W["ev_w_out"][e].astype(BF16)
        f["even"].append(dict(w_in=w_in, rw=rw, wgate=wgate, bgate=bgate, wo_a=wo[:wd], wo_b=wo[wd:],
                              cw=W["lru_conv_w"][e], cb=W["lru_conv_b"][e], lam=W["lru_lambda"][e]))
    f["odd"] = []
    cw_ = C_WIDTH
    qi_w = C_IDX_HEADS * C_IDX_DIM
    for o in range(n_odd):
        wi = W["od_w_in"][o]
        offs = [0]
        for s in (cw_, cw_, cw_, qi_w, C_IDX_DIM, C_IDX_HEADS, D_Q_RANK, D_KV_RANK, D_ROPE):
            offs.append(offs[-1] + s)
        q, k, v, qi, ki, wi_, qd, kvd, kr = [wi[:, offs[i]:offs[i + 1]] for i in range(9)]
        small = jnp.concatenate([ki, kr, wi_], axis=1)
        w_in = jnp.concatenate([q, k, v, qi, qd, kvd, _pad_axis(small, 1, 512)], axis=1).astype(BF16)
        wuq = W["mla_w_uq"][o].reshape(D_Q_RANK, D_HEADS, D_NOPE + D_ROPE)
        wuq = jnp.concatenate([wuq[:, :, :D_NOPE].reshape(D_Q_RANK, -1),
                               wuq[:, :, D_NOPE:].reshape(D_Q_RANK, -1)], axis=1).astype(BF16)
        wukv = W["mla_w_ukv"][o].reshape(D_KV_RANK, D_HEADS, D_NOPE + D_V)
        wuk = _pad_axis(wukv[:, :, :D_NOPE], 2, MLA_DK).reshape(D_KV_RANK, -1).astype(BF16)
        wuv = wukv[:, :, D_NOPE:].reshape(D_KV_RANK, -1).astype(BF16)
        wo = W["od_w_out"][o].astype(BF16)
        f["odd"].append(dict(w_in=w_in, wuq=wuq, wuk=wuk, wuv=wuv, wo_c=wo[:cw_], wo_d=wo[cw_:],
                             gq=W["mla_q_norm"][o], gkv=W["mla_kv_norm"][o]))
    f["ffn"] = dict(wg=W["ffn_w_gate"].astype(BF16), wu=W["ffn_w_up"].astype(BF16),
                    wd=W["ffn_w_down"].astype(BF16))
    f["tabs"] = _bias_tables(W["rel_bias"])
    nb = N_BUCKETS // 2
    f["far"] = _pad_axis(W["rel_bias"][nb - 1:nb, :] * LOG2E, 1, LANES)
    return f


def _mixer_even(x2, b, t, p0, shift, s0, h0, cbuf, fe, norm_g):
    wd = A_WIDTH
    p = _norm_matmul(x2, norm_g, fe["w_in"])
    p3 = p.reshape(b, t, -1)
    lora_col = 3 * wd + 2 * B_WIDTH
    shift_r = shift[:, None, :3 * wd]
    shift_l = _pad_axis(shift[:, None, 3 * wd:], 2, LORA_PAD)
    ya, s_new = _rwkv(p3, lora_col // LORA_PAD, shift_r, shift_l, s0, fe["rw"])
    yb, h_new, c_new = _rglru(p3, 3, 4, fe["cw"], fe["cb"], fe["wgate"], fe["bgate"], fe["lam"],
                              h0, cbuf, p0=p0)
    x2 = _matmul([ya.reshape(b * t, wd), yb.reshape(b * t, B_WIDTH)], [fe["wo_a"], fe["wo_b"]], res=x2)
    last = p3[:, t - 1]
    new_shift = jnp.concatenate([last[:, :3 * wd], last[:, lora_col:lora_col + A_LORA]], axis=-1)
    return x2, new_shift, s_new, h_new[:, 0], c_new


def _mixer_odd(x2, b, t, p0, o, dsa_k, dsa_v, cik, clat, ckr, fo, f, norm_g, cos, sin):
    cw_ = C_WIDTH
    p = _norm_matmul(x2, norm_g, fo["w_in"])
    p3 = p.reshape(b, t, -1)
    past = dsa_k.shape[2]
    l_valid = past + t
    lpad, tk = _key_tile(l_valid)
    tpad = -(-t // LANES) * LANES
    q = p3[..., :cw_]
    k_new = p3[..., cw_:2 * cw_]
    v_new = p3[..., 2 * cw_:3 * cw_]
    qi = p3[..., 3 * cw_:3 * cw_ + 512]
    small = p3[..., 3 * cw_ + 1536:]
    ki_new = small[..., :C_IDX_DIM]
    wi = small[..., 2 * C_IDX_DIM:2 * C_IDX_DIM + C_IDX_HEADS]

    def keys(cache, new):
        allk = jnp.concatenate([cache.reshape(b, past, -1), new], axis=1) if past else new
        return _pad_axis(allk, 1, lpad)

    tq_ = lambda z: _pad_axis(jnp.swapaxes(z, 1, 2), 2, tpad)
    if past and past % PACK_TILE == 0:
        k_all = _pack_keys(dsa_k, o, k_new, lpad, transpose=False)
        vT_all = _pack_keys(dsa_v, o, v_new, lpad, transpose=True)
    else:
        k_all = keys(dsa_k[o], k_new).astype(BF16)
        vT_all = _vt_ones(keys(dsa_v[o], v_new).astype(BF16))
    kih, kil = _split_hi_lo(keys(cik, ki_new))
    ki3 = jnp.concatenate([kih, kih, kil], axis=-1)
    qih, qil = _split_hi_lo(tq_(qi).reshape(b, C_IDX_HEADS, C_IDX_DIM, tpad))
    q3 = jnp.concatenate([qih, qil, qih], axis=2).reshape(b, 3 * C_IDX_HEADS * C_IDX_DIM, tpad)
    yc = _attention("dsa", tq_(q * (C_HEAD_DIM ** -0.5 * LOG2E)).astype(BF16), k_all, vT_all,
                    (q3, ki3, tq_(wi), f["tabs"], f["far"]),
                    p0=p0, t_valid=t, l_valid=l_valid, tk=tk)[:, :t]
    qn, qr, lat, krope = _mla_prep(p3, 7, 8, 9, fo["gq"], fo["gkv"], fo["wuq"], cos, sin)
    lat_all = keys(clat, lat).reshape(b * lpad, -1)
    kr_all = keys(ckr, krope).reshape(b * lpad, -1)
    eye = jnp.eye(D_ROPE, dtype=BF16)
    ident = jnp.tile(jnp.pad(eye, ((0, 0), (D_NOPE, MLA_DK - D_NOPE - D_ROPE))), (1, D_HEADS))
    k_full = _matmul([lat_all, kr_all], [fo["wuk"], ident], out_dtype=BF16).reshape(b, lpad, -1)
    v_all = _matmul([lat_all], [fo["wuv"]], out_dtype=BF16).reshape(b, lpad, -1)
    q_full = jnp.concatenate([qn.reshape(b, t, D_HEADS, D_NOPE), qr.reshape(b, t, D_HEADS, D_ROPE),
                              jnp.zeros((b, t, D_HEADS, MLA_DK - D_NOPE - D_ROPE), BF16)], axis=-1)
    yd = _attention("mla", tq_(q_full.reshape(b, t, -1)), k_full, _vt_ones(v_all), (),
                    p0=p0, t_valid=t, l_valid=l_valid, tk=tk)[:, :t]
    x2 = _matmul([yc.reshape(b * t, cw_), yd.reshape(b * t, -1)], [fo["wo_c"], fo["wo_d"]], res=x2)
    return (x2, k_new.reshape(b, t, C_HEADS, C_HEAD_DIM), v_new.reshape(b, t, C_HEADS, C_HEAD_DIM),
            ki_new, lat, krope)


def _trunk(x, p0, shift, rwkv_s, lru_h, lru_conv, dsa_k, dsa_v, dsa_ik, mla_lat, mla_kr, W, f):
    b, t, d = x.shape
    depth = W["norm_mix"].shape[0]
    pos = (p0 + jnp.arange(t)).astype(F32)
    inv = ROPE_BASE ** (-jnp.arange(0, D_ROPE, 2, dtype=F32) / D_ROPE)
    ang = pos[:, None] * inv[None, :]
    cos = jnp.tile(jnp.cos(ang), (1, 2 * D_HEADS))
    sin = jnp.tile(jnp.sin(ang), (1, 2 * D_HEADS))
    x2 = x.reshape(b * t, d)
    ev = [[] for _ in range(4)]
    od = [[] for _ in range(5)]
    for layer in range(depth):
        if layer % 2 == 0:
            e = layer // 2
            x2, *outs = _mixer_even(x2, b, t, p0, shift[e], rwkv_s[e], lru_h[e], lru_conv[e],
                                    f["even"][e], W["norm_mix"][layer])
            for lst, o_ in zip(ev, outs):
                lst.append(o_)
        else:
            o = layer // 2
            x2, *outs = _mixer_odd(x2, b, t, p0, o, dsa_k, dsa_v, dsa_ik[o], mla_lat[o], mla_kr[o],
                                   f["odd"][o], f, W["norm_mix"][layer], cos, sin)
            for lst, o_ in zip(od, outs):
                lst.append(o_)
        ff = f["ffn"]
        x2 = _ffn(x2, W["norm_ffn"][layer], ff["wg"], ff["wu"], ff["wd"], layer, W["final_norm"],
                  final_norm=layer == depth - 1)
    return (x2.reshape(b, t, d),) + tuple(jnp.stack(v) for v in ev) + tuple(jnp.stack(v) for v in od)


def kernel(x_prompt, x_sample, state_rwkv_shift, state_rwkv, state_lru, state_lru_conv, cache_dsa_k, cache_dsa_v, cache_dsa_idx_k, cache_mla_latent, cache_mla_krope, rel_bias, final_norm, norm_mix, norm_ffn, ffn_w_gate, ffn_w_up, ffn_w_down, ev_w_in, ev_w_out, rwkv_mu, rwkv_w0, rwkv_w2, rwkv_a0, rwkv_a2, rwkv_g2, rwkv_k_k, rwkv_k_a, rwkv_r_k, rwkv_ln_w, rwkv_ln_b, lru_conv_w, lru_conv_b, lru_wa, lru_ba, lru_wx, lru_bx, lru_lambda, od_w_in, od_w_out, mla_q_norm, mla_w_uq, mla_kv_norm, mla_w_ukv):
    W = dict(rel_bias=rel_bias, final_norm=final_norm, norm_mix=norm_mix, norm_ffn=norm_ffn,
             ffn_w_gate=ffn_w_gate, ffn_w_up=ffn_w_up, ffn_w_down=ffn_w_down,
             ev_w_in=ev_w_in, ev_w_out=ev_w_out, rwkv_mu=rwkv_mu, rwkv_w0=rwkv_w0,
             rwkv_w2=rwkv_w2, rwkv_a0=rwkv_a0, rwkv_a2=rwkv_a2, rwkv_g2=rwkv_g2,
             rwkv_k_k=rwkv_k_k, rwkv_k_a=rwkv_k_a, rwkv_r_k=rwkv_r_k, rwkv_ln_w=rwkv_ln_w,
             rwkv_ln_b=rwkv_ln_b, lru_conv_w=lru_conv_w, lru_conv_b=lru_conv_b, lru_wa=lru_wa,
             lru_ba=lru_ba, lru_wx=lru_wx, lru_bx=lru_bx, lru_lambda=lru_lambda,
             od_w_in=od_w_in, od_w_out=od_w_out, mla_q_norm=mla_q_norm, mla_w_uq=mla_w_uq,
             mla_kv_norm=mla_kv_norm, mla_w_ukv=mla_w_ukv)
    d_model = x_prompt.shape[-1]
    f = _prep_weights(W, d_model)
    bp = x_prompt.shape[0]
    n_even, n_odd = ev_w_in.shape[0], od_w_in.shape[0]
    dt = x_prompt.dtype
    z = lambda *shape: jnp.zeros(shape, dt)
    outs_p = _trunk(
        x_prompt, 0,
        z(n_even, bp, A_COLS), z(n_even, bp, A_HEADS, A_HEAD_DIM, A_HEAD_DIM),
        z(n_even, bp, B_WIDTH), z(n_even, bp, B_CONV - 1, B_WIDTH),
        z(n_odd, bp, 0, C_HEADS, C_HEAD_DIM), z(n_odd, bp, 0, C_HEADS, C_HEAD_DIM),
        z(n_odd, bp, 0, C_IDX_DIM), z(n_odd, bp, 0, D_KV_RANK), z(n_odd, bp, 0, D_ROPE), W, f)
    past = cache_dsa_k.shape[2]
    outs_s = _trunk(x_sample, past, state_rwkv_shift, state_rwkv, state_lru, state_lru_conv,
                    cache_dsa_k, cache_dsa_v, cache_dsa_idx_k, cache_mla_latent, cache_mla_krope, W, f)
    return (outs_p[0], outs_s[0]) + tuple(outs_p[1:]) + tuple(outs_s[1:])
```

```python
import functools
import math

import jax
import jax.numpy as jnp
from jax import lax
from jax.experimental import pallas as pl
from jax.experimental.pallas import tpu as pltpu

F32 = jnp.float32
BF16 = jnp.bfloat16
I32 = jnp.int32

CHUNK = 64
NORM_EPS = 1e-6
A_HEADS = 16
A_HEAD_DIM = 64
A_WIDTH = A_HEADS * A_HEAD_DIM
A_DECAY_LORA = 64
A_ICLR_LORA = 64
A_GATE_LORA = 160
A_LORA = A_DECAY_LORA + A_ICLR_LORA + A_GATE_LORA
A_COLS = 3 * A_WIDTH + A_LORA
A_LN_EPS = 64e-5
B_WIDTH = 1024
B_BLOCKS = 16
B_CONV = 4
B_C = 8.0
C_HEADS = 8
C_HEAD_DIM = 128
C_WIDTH = C_HEADS * C_HEAD_DIM
C_IDX_HEADS = 8
C_IDX_DIM = 64
TOPK_MAX = 256
D_HEADS = 8
D_NOPE = 128
D_ROPE = 64
D_V = 128
D_Q_RANK = 512
D_KV_RANK = 512
ROPE_BASE = 10000.0
N_BUCKETS = 32
MAX_DISTANCE = 128

LANES = 128
SUBLANES = 8
VMEM_LIMIT = 56 * 1024 * 1024
LORA_PAD = 512
NEG_BIG = -1e30
INT_MIN = -2147483648
RC = 64
SEARCH_TILES = 4
VT_ONES = 16
VT_ROWS = LANES + VT_ONES
LOG2E = 1.4426950408889634
MLA_DK = 256

NN = (((1,), (0,)), ((), ()))
NT = (((1,), (1,)), ((), ()))
BNN = (((2,), (1,)), ((0,), (0,)))
BNT = (((2,), (2,)), ((0,), (0,)))
BTN = (((1,), (1,)), ((0,), (0,)))


def _cparams(sem):
    return pltpu.CompilerParams(dimension_semantics=sem, vmem_limit_bytes=VMEM_LIMIT)


def _rms(x, g, eps=NORM_EPS):
    ms = jnp.mean(x * x, axis=-1, keepdims=True)
    return x * lax.rsqrt(ms + eps) * g


def _softplus(x):
    return jnp.maximum(x, 0.0) + jnp.log1p(jnp.exp(-jnp.abs(x)))


def _dotp(a, b, dims, passes):
    if passes == 6:
        return lax.dot_general(a, b, dims, precision=lax.Precision.HIGHEST, preferred_element_type=F32)
    ah = a.astype(BF16)
    bh = b.astype(BF16)
    out = lax.dot_general(ah, bh, dims, preferred_element_type=F32)
    if passes == 3:
        al = (a - ah.astype(F32)).astype(BF16)
        bl = (b - bh.astype(F32)).astype(BF16)
        out = out + lax.dot_general(ah, bl, dims, preferred_element_type=F32)
        out = out + lax.dot_general(al, bh, dims, preferred_element_type=F32)
    return out


def _norm_matmul_body(x_ref, g_ref, w_ref, o_ref, xn_ref):
    @pl.when(pl.program_id(1) == 0)
    def _():
        xn_ref[...] = _rms(x_ref[...], g_ref[...]).astype(BF16)

    o_ref[...] = jnp.dot(xn_ref[...], w_ref[...], preferred_element_type=F32)


def _row_tile(m, cap):
    tm = cap
    while m % tm:
        tm //= 2
    return tm


def _norm_matmul(x, g, w, *, tn=512):
    m, k = x.shape
    n = w.shape[1]
    tm = _row_tile(m, 1024)
    return pl.pallas_call(
        _norm_matmul_body,
        grid=(m // tm, n // tn),
        in_specs=[pl.BlockSpec((tm, k), lambda i, j: (i, 0)),
                  pl.BlockSpec((1, k), lambda i, j: (0, 0)),
                  pl.BlockSpec((k, tn), lambda i, j: (0, j))],
        out_specs=pl.BlockSpec((tm, tn), lambda i, j: (i, j)),
        out_shape=jax.ShapeDtypeStruct((m, n), F32),
        scratch_shapes=[pltpu.VMEM((tm, k), BF16)],
        compiler_params=_cparams(("parallel", "arbitrary")),
        name="norm_matmul",
    )(x, g.reshape(1, k), w)


def _mm_body(*refs, n_lhs, has_res):
    o_ref = refs[-1]
    acc = refs[2 * n_lhs][...] if has_res else None
    for a_ref, w_ref in zip(refs[:n_lhs], refs[n_lhs:2 * n_lhs]):
        d = jnp.dot(a_ref[...].astype(BF16), w_ref[...], preferred_element_type=F32)
        acc = d if acc is None else acc + d
    o_ref[...] = acc.astype(o_ref.dtype)


def _matmul(lhs_list, w_list, res=None, *, tn=512, out_dtype=F32):
    m = lhs_list[0].shape[0]
    n = w_list[0].shape[1]
    tm = _row_tile(m, 1024)
    in_specs = [pl.BlockSpec((tm, a.shape[1]), lambda i, j: (i, 0)) for a in lhs_list]
    in_specs += [pl.BlockSpec((w.shape[0], tn), lambda i, j: (0, j)) for w in w_list]
    args = list(lhs_list) + list(w_list)
    if res is not None:
        in_specs.append(pl.BlockSpec((tm, tn), lambda i, j: (i, j)))
        args.append(res)
    return pl.pallas_call(
        functools.partial(_mm_body, n_lhs=len(lhs_list), has_res=res is not None),
        grid=(m // tm, n // tn),
        in_specs=in_specs,
        out_specs=pl.BlockSpec((tm, tn), lambda i, j: (i, j)),
        out_shape=jax.ShapeDtypeStruct((m, n), out_dtype),
        compiler_params=_cparams(("parallel", "arbitrary")),
        name="matmul",
    )(*args)


def _ffn_body(x_ref, ya_ref, yb_ref, woa_ref, wob_ref, g_ref, wg_ref, wu_ref, wd_ref, gf_ref, o_ref,
              xn_ref, acc_ref, *, final_norm):
    f = pl.program_id(1)

    @pl.when(f == 0)
    def _():
        x = x_ref[...] + jnp.dot(ya_ref[...], woa_ref[...], preferred_element_type=F32)
        x = x + jnp.dot(yb_ref[...], wob_ref[...], preferred_element_type=F32)
        xn_ref[...] = _rms(x, g_ref[...]).astype(BF16)
        acc_ref[...] = x

    xn = xn_ref[...]
    hg = jnp.dot(xn, wg_ref[0], preferred_element_type=F32)
    hu = jnp.dot(xn, wu_ref[0], preferred_element_type=F32)
    h = hg * jax.nn.sigmoid(hg) * hu
    acc_ref[...] += jnp.dot(h.astype(BF16), wd_ref[0], preferred_element_type=F32)

    @pl.when(f == pl.num_programs(1) - 1)
    def _():
        y = acc_ref[...]
        if final_norm:
            y = _rms(y, gf_ref[...])
        o_ref[...] = y


def _ffn(x, ya, yb, woa, wob, g, wg, wu, wd, layer, gf, *, final_norm, tf=512):
    m, k = x.shape
    dff = wg.shape[2]
    tm = _row_tile(m, 512)
    once = pl.Buffered(1)
    return pl.pallas_call(
        functools.partial(_ffn_body, final_norm=final_norm),
        grid=(m // tm, dff // tf),
        in_specs=[pl.BlockSpec((tm, k), lambda i, f: (i, 0), pipeline_mode=once),
                  pl.BlockSpec((tm, ya.shape[1]), lambda i, f: (i, 0), pipeline_mode=once),
                  pl.BlockSpec((tm, yb.shape[1]), lambda i, f: (i, 0), pipeline_mode=once),
                  pl.BlockSpec(woa.shape, lambda i, f: (0, 0), pipeline_mode=once),
                  pl.BlockSpec(wob.shape, lambda i, f: (0, 0), pipeline_mode=once),
                  pl.BlockSpec((1, k), lambda i, f: (0, 0)),
                  pl.BlockSpec((1, k, tf), lambda i, f: (layer, 0, f)),
                  pl.BlockSpec((1, k, tf), lambda i, f: (layer, 0, f)),
                  pl.BlockSpec((1, tf, k), lambda i, f: (layer, f, 0)),
                  pl.BlockSpec((1, k), lambda i, f: (0, 0))],
        out_specs=pl.BlockSpec((tm, k), lambda i, f: (i, 0)),
        out_shape=jax.ShapeDtypeStruct((m, k), F32),
        scratch_shapes=[pltpu.VMEM((tm, k), BF16), pltpu.VMEM((tm, k), F32)],
        compiler_params=_cparams(("parallel", "arbitrary")),
        name="ffn",
    )(x, ya, yb, woa, wob, g.reshape(1, k), wg, wu, wd, gf.reshape(1, k))


def _lru_body(gate_ref, xb_ref, cw_ref, cb_ref, wg_ref, bg_ref, lam_ref, h0_ref, cbuf_ref,
              y_ref, hout_ref, cout_ref, xbuf, hcar, *, tt, p0):
    t = pl.program_id(1)
    w = B_WIDTH
    halo = SUBLANES

    @pl.when(t == 0)
    def _():
        xbuf[0:halo, :] = jnp.zeros((halo, w), F32)
        xbuf[halo - (B_CONV - 1):halo, :] = cbuf_ref[0]
        hcar[...] = h0_ref[0]

    xb = xb_ref[0]
    xbuf[halo:halo + tt, :] = xb
    xc = cb_ref[...] + cw_ref[B_CONV - 1:B_CONV, :] * xb
    for j in range(B_CONV - 1):
        off = halo - (B_CONV - 1) + j
        xc = xc + cw_ref[j:j + 1, :] * xbuf[off:off + tt, :]
    tail = xbuf[tt + halo - (B_CONV - 1):tt + halo, :]
    xbuf[halo - (B_CONV - 1):halo, :] = tail

    pre = jnp.dot(xc.astype(BF16), wg_ref[...], preferred_element_type=F32) + bg_ref[...]
    rg = jax.nn.sigmoid(pre[:, :w])
    ig = jax.nn.sigmoid(pre[:, w:])
    log_a = (-B_C) * rg * _softplus(-lam_ref[...])
    a = jnp.exp(log_a)
    row = lax.broadcasted_iota(I32, (tt, w), 0)
    th = jnp.tanh(log_a)
    mult = jnp.sqrt(-2.0 * th / (1.0 - th))
    mult = jnp.where(row + (p0 + t * tt) == 0, 1.0, mult)
    u = mult * (ig * xc)

    d = 1
    while d < tt:
        keep = row >= d
        a_sh = pltpu.roll(a, d, 0)
        u_sh = pltpu.roll(u, d, 0)
        u = u + jnp.where(keep, a * u_sh, 0.0)
        a = jnp.where(keep, a * a_sh, a)
        d *= 2
    h = u + a * hcar[...]
    hcar[...] = h[tt - 1:tt, :]
    y_ref[0] = (h * jax.nn.gelu(gate_ref[0])).astype(y_ref.dtype)

    @pl.when(t == pl.num_programs(1) - 1)
    def _():
        hout_ref[0] = h[tt - 1:tt, :]
        cout_ref[0] = tail


def _rglru(p3, gate_blk, xb_blk, cw, cb, wgate, bgate, lam, h0, cbuf, *, p0):
    b, t, _ = p3.shape
    w = B_WIDTH
    tt = min(256, t)
    row = lambda v: v.reshape(1, -1)
    return pl.pallas_call(
        functools.partial(_lru_body, tt=tt, p0=p0),
        grid=(b, t // tt),
        in_specs=[pl.BlockSpec((1, tt, w), lambda i, j: (i, j, gate_blk)),
                  pl.BlockSpec((1, tt, w), lambda i, j: (i, j, xb_blk)),
                  pl.BlockSpec((B_CONV, w), lambda i, j: (0, 0)),
                  pl.BlockSpec((1, w), lambda i, j: (0, 0)),
                  pl.BlockSpec((w, 2 * w), lambda i, j: (0, 0)),
                  pl.BlockSpec((1, 2 * w), lambda i, j: (0, 0)),
                  pl.BlockSpec((1, w), lambda i, j: (0, 0)),
                  pl.BlockSpec((1, 1, w), lambda i, j: (i, 0, 0)),
                  pl.BlockSpec((1, B_CONV - 1, w), lambda i, j: (i, 0, 0))],
        out_specs=[pl.BlockSpec((1, tt, w), lambda i, j: (i, j, 0)),
                   pl.BlockSpec((1, 1, w), lambda i, j: (i, 0, 0)),
                   pl.BlockSpec((1, B_CONV - 1, w), lambda i, j: (i, 0, 0))],
        out_shape=[jax.ShapeDtypeStruct((b, t, w), BF16),
                   jax.ShapeDtypeStruct((b, 1, w), F32),
                   jax.ShapeDtypeStruct((b, B_CONV - 1, w), F32)],
        scratch_shapes=[pltpu.VMEM((tt + SUBLANES, w), F32), pltpu.VMEM((1, w), F32)],
        compiler_params=_cparams(("parallel", "arbitrary")),
        name="rglru",
    )(p3, p3, cw, row(cb), wgate, row(bgate), row(lam), h0.reshape(b, 1, w), cbuf)


RWKV_PASSES = 1
RWKV_PASSES_SOLVE = 3


def _rwkv_body(rkv_ref, lora_ref, shr_ref, shl_ref, s0_ref, mur_ref, mul_ref, w0_ref, w2_ref, a0_ref,
               a2_ref, g2_ref, kk_ref, ka_ref, rk_ref, lnw_ref, lnb_ref, tri_ref,
               y_ref, sout_ref, buf_r, buf_l, s_scr, st_r, st_k, st_v, st_q, st_a, st_l, st_d, *, c):
    ci = pl.program_id(1)
    halo = SUBLANES
    hn, n, wd = A_HEADS, A_HEAD_DIM, A_WIDTH
    mm = functools.partial(_dotp, passes=RWKV_PASSES)

    @pl.when(ci == 0)
    def _():
        buf_r[halo - 1:halo, :] = shr_ref[0]
        buf_l[halo - 1:halo, :] = shl_ref[0]
        s_scr[...] = s0_ref[0]

    cur_r = rkv_ref[0]
    cur_l = lora_ref[0]
    buf_r[halo:halo + c, :] = cur_r
    buf_l[halo:halo + c, :] = cur_l
    xm = cur_r + mur_ref[...] * (buf_r[halo - 1:halo - 1 + c, :] - cur_r)
    lo = cur_l + mul_ref[...] * (buf_l[halo - 1:halo - 1 + c, :] - cur_l)
    buf_r[halo - 1:halo, :] = cur_r[c - 1:c, :]
    buf_l[halo - 1:halo, :] = cur_l[c - 1:c, :]

    r = xm[:, :wd]
    k = xm[:, wd:2 * wd]
    v = xm[:, 2 * wd:]
    lo_a = lo[:, :LANES]
    w_pre = w0_ref[...] + jnp.dot(jnp.tanh(lo_a).astype(BF16), w2_ref[...], preferred_element_type=F32)
    w_log = -_softplus(-w_pre) - 0.5
    ld = -jnp.exp(w_log)
    a = jax.nn.sigmoid(a0_ref[...] + jnp.dot(lo_a.astype(BF16), a2_ref[...], preferred_element_type=F32))
    g = jnp.dot(jax.nn.sigmoid(lo[:, LANES:]).astype(BF16), g2_ref[...], preferred_element_type=F32)
    kq = k * kk_ref[...]
    k2 = k * (1.0 + (a - 1.0) * ka_ref[...])
    lc = lax.dot_general(tri_ref[...], ld, NN, precision=lax.Precision.HIGHEST,
                         preferred_element_type=F32)

    for h in range(hn):
        sl = slice(h * n, (h + 1) * n)
        st_r[h] = r[:, sl]
        st_k[h] = k2[:, sl]
        st_v[h] = v[:, sl]
        st_q[h] = kq[:, sl]
        st_a[h] = a[:, sl]
        st_l[h] = lc[:, sl]
        st_d[h] = ld[:, sl]

    rh_, k2h, vh, kqh, ah, lch, ldh = (st_r[...], st_k[...], st_v[...], st_q[...], st_a[...],
                                       st_l[...], st_d[...])
    nrm = jnp.sqrt(jnp.sum(kqh * kqh, axis=-1, keepdims=True))
    kk = kqh / jnp.maximum(nrm, 1e-12)
    kka = kk * ah
    e_neg = jnp.exp(-lch)
    am = jnp.exp(lch - ldh) * kk
    bm = kka * e_neg
    kh = k2h * e_neg
    rh = rh_ * jnp.exp(lch)
    l_end = lch[:, c - 1:c, :]
    e_c = jnp.exp(l_end - lch)
    bp = kka * e_c
    kp = k2h * e_c
    w_end = jnp.exp(l_end)

    x2 = jnp.concatenate([am, rh], axis=1)
    zb = _dotp(x2, bm, BNT, RWKV_PASSES_SOLVE)
    zk = mm(x2, kh, BNT)
    ti = lax.broadcasted_iota(I32, (hn, c, c), 1)
    si = lax.broadcasted_iota(I32, (hn, c, c), 2)
    strict = si < ti
    incl = si <= ti
    m1 = jnp.where(strict, zb[:, :c], 0.0)
    m4 = jnp.where(incl, zb[:, c:], 0.0)
    m2 = jnp.where(strict, zk[:, :c], 0.0)
    m3 = jnp.where(incl, zk[:, c:], 0.0)

    tm = jnp.where(si == ti, 1.0, 0.0) - m1
    npow = mm(m1, m1, BNN)
    span = 2
    while span < c:
        tm = tm + mm(tm, npow, BNN)
        span *= 2
        if span < c:
            npow = mm(npow, npow, BNN)

    s0 = s_scr[...]
    rhs = mm(am, s0, BNT) + mm(m2, vh, BNN)
    p = mm(tm, rhs, BNN)
    y = mm(rh, s0, BNT) + mm(m3, vh, BNN) - mm(m4, p, BNN)
    s_new = s0 * w_end + mm(vh, kp, BTN) - mm(p, bp, BTN)
    s_scr[...] = s_new

    mean = jnp.mean(y, axis=-1, keepdims=True)
    yc = y - mean
    var = jnp.mean(yc * yc, axis=-1, keepdims=True)
    yn = yc * lax.rsqrt(var + A_LN_EPS) * lnw_ref[...] + lnb_ref[...]
    bonus = jnp.sum(rh_ * k2h * rk_ref[...], axis=-1, keepdims=True) * vh
    yo = yn + bonus
    yo = jnp.concatenate([yo[h] for h in range(hn)], axis=-1)
    y_ref[0] = (yo * g).astype(y_ref.dtype)

    @pl.when(ci == pl.num_programs(1) - 1)
    def _():
        sout_ref[0] = s_new


def _rwkv(p3, lora_blk, shift_r, shift_l, s0, wts):
    b, t, _ = p3.shape
    c = min(CHUNK, t)
    hn, n, wd = A_HEADS, A_HEAD_DIM, A_WIDTH
    full = lambda shape: pl.BlockSpec(shape, lambda i, j: (0,) * len(shape))
    tri = (jnp.arange(c)[:, None] >= jnp.arange(c)[None, :]).astype(F32)
    st = pltpu.VMEM((hn, c, n), F32)
    return pl.pallas_call(
        functools.partial(_rwkv_body, c=c),
        grid=(b, t // c),
        in_specs=[pl.BlockSpec((1, c, 3 * wd), lambda i, j: (i, j, 0)),
                  pl.BlockSpec((1, c, LORA_PAD), lambda i, j: (i, j, lora_blk)),
                  pl.BlockSpec((1, 1, 3 * wd), lambda i, j: (i, 0, 0)),
                  pl.BlockSpec((1, 1, LORA_PAD), lambda i, j: (i, 0, 0)),
                  pl.BlockSpec((1, hn, n, n), lambda i, j: (i, 0, 0, 0)),
                  full((1, 3 * wd)), full((1, LORA_PAD)), full((1, wd)), full((LANES, wd)),
                  full((1, wd)), full((LANES, wd)), full((LORA_PAD - LANES, wd)),
                  full((1, wd)), full((1, wd)), full((hn, 1, n)), full((hn, 1, n)), full((hn, 1, n)),
                  full((c, c))],
        out_specs=[pl.BlockSpec((1, c, wd), lambda i, j: (i, j, 0)),
                   pl.BlockSpec((1, hn, n, n), lambda i, j: (i, 0, 0, 0))],
        out_shape=[jax.ShapeDtypeStruct((b, t, wd), BF16),
                   jax.ShapeDtypeStruct((b, hn, n, n), F32)],
        scratch_shapes=[pltpu.VMEM((c + SUBLANES, 3 * wd), F32), pltpu.VMEM((c + SUBLANES, LORA_PAD), F32),
                        pltpu.VMEM((hn, n, n), F32), st, st, st, st, st, st, st],
        compiler_params=_cparams(("parallel", "arbitrary")),
        name="rwkv7",
    )(p3, p3, shift_r, shift_l, s0, wts["mu_r"], wts["mu_l"], wts["w0"], wts["w2"], wts["a0"], wts["a2"],
      wts["g2"], wts["k_k"], wts["k_a"], wts["r_k"], wts["ln_w"], wts["ln_b"], tri)


def _rope(x, cos, sin):
    wdt = x.shape[-1]
    lane = lax.broadcasted_iota(I32, x.shape, 1)
    first = (lane % D_ROPE) < (D_ROPE // 2)
    rot = jnp.where(first, -pltpu.roll(x, wdt - D_ROPE // 2, 1), pltpu.roll(x, D_ROPE // 2, 1))
    return x * cos + rot * sin


def _mla_prep_body(qd_ref, kvd_ref, sm_ref, gq_ref, gkv_ref, wuq_ref, cos_ref, sin_ref,
                   qn_ref, qr_ref, lat_ref, kr_ref):
    nope_w = D_HEADS * D_NOPE
    qdn = _rms(qd_ref[0], gq_ref[...]).astype(BF16)
    qf = jnp.dot(qdn, wuq_ref[...], preferred_element_type=F32) * ((D_NOPE + D_ROPE) ** -0.5 * LOG2E)
    qn_ref[0] = qf[:, :nope_w].astype(qn_ref.dtype)
    cos = cos_ref[...]
    sin = sin_ref[...]
    qr_ref[0] = _rope(qf[:, nope_w:], cos, sin).astype(qr_ref.dtype)
    lat_ref[0] = _rms(kvd_ref[0], gkv_ref[...])
    sm = sm_ref[0][:, :LANES]
    kr = _rope(sm, cos[:, :LANES], sin[:, :LANES])
    kr_ref[0] = kr[:, C_IDX_DIM:C_IDX_DIM + D_ROPE]


def _mla_prep(p3, qd_blk, kvd_blk, sm_blk, gq, gkv, wuq, cos, sin):
    b, t, _ = p3.shape
    tt = min(256, t)
    rw = D_HEADS * D_ROPE
    full = lambda shape: pl.BlockSpec(shape, lambda i, j: (0,) * len(shape))
    return pl.pallas_call(
        _mla_prep_body,
        grid=(b, t // tt),
        in_specs=[pl.BlockSpec((1, tt, D_Q_RANK), lambda i, j: (i, j, qd_blk)),
                  pl.BlockSpec((1, tt, D_KV_RANK), lambda i, j: (i, j, kvd_blk)),
                  pl.BlockSpec((1, tt, 512), lambda i, j: (i, j, sm_blk)),
                  full((1, D_Q_RANK)), full((1, D_KV_RANK)), full(wuq.shape),
                  pl.BlockSpec((tt, rw), lambda i, j: (j, 0)),
                  pl.BlockSpec((tt, rw), lambda i, j: (j, 0))],
        out_specs=[pl.BlockSpec((1, tt, D_HEADS * D_NOPE), lambda i, j: (i, j, 0)),
                   pl.BlockSpec((1, tt, rw), lambda i, j: (i, j, 0)),
                   pl.BlockSpec((1, tt, D_KV_RANK), lambda i, j: (i, j, 0)),
                   pl.BlockSpec((1, tt, D_ROPE), lambda i, j: (i, j, 0))],
        out_shape=[jax.ShapeDtypeStruct((b, t, D_HEADS * D_NOPE), BF16),
                   jax.ShapeDtypeStruct((b, t, rw), BF16),
                   jax.ShapeDtypeStruct((b, t, D_KV_RANK), F32),
                   jax.ShapeDtypeStruct((b, t, D_ROPE), F32)],
        compiler_params=_cparams(("parallel", "parallel")),
        name="mla_prep",
    )(p3, p3, p3, gq.reshape(1, -1), gkv.reshape(1, -1), wuq, cos, sin)


def _bias_table_body(rb_ref, o_ref):
    d = pl.program_id(0)
    h = pl.program_id(1)
    nb = N_BUCKETS // 2
    max_exact = nb // 2
    s = lax.broadcasted_iota(I32, (LANES, LANES), 0)
    q = lax.broadcasted_iota(I32, (LANES, LANES), 1)
    rel = s - q - d * LANES
    n = jnp.abs(rel)
    big = jnp.maximum(n, max_exact).astype(F32)
    large = max_exact + (jnp.log(big / max_exact) / math.log(MAX_DISTANCE / max_exact)
                         * (nb - max_exact)).astype(I32)
    large = jnp.minimum(large, nb - 1)
    bucket = jnp.where(rel > 0, nb, 0) + jnp.where(n < max_exact, n, large)
    out = jnp.zeros((LANES, LANES), F32)
    for bk in range(N_BUCKETS):
        out = jnp.where(bucket == bk, rb_ref[bk, h], out)
    o_ref[0, 0] = out * LOG2E


def _bias_tables(rel_bias):
    return pl.pallas_call(
        _bias_table_body,
        grid=(2, C_HEADS),
        in_specs=[pl.BlockSpec(memory_space=pltpu.SMEM)],
        out_specs=pl.BlockSpec((1, 1, LANES, LANES), lambda d, h: (d, h, 0, 0)),
        out_shape=jax.ShapeDtypeStruct((2, C_HEADS, LANES, LANES), F32),
        name="bias_tables",
    )(rel_bias)


def _transpose32(x):
    x = list(x)
    for s, msk in ((16, 0x0000FFFF), (8, 0x00FF00FF), (4, 0x0F0F0F0F), (2, 0x33333333), (1, 0x55555555)):
        sh = jnp.full(x[0].shape, s, I32)
        for i in range(32):
            if i & s == 0:
                t = (lax.shift_right_logical(x[i], sh) ^ x[i + s]) & msk
                x[i + s] = x[i + s] ^ t
                x[i] = x[i] ^ lax.shift_left(t, sh)
    return x


def _attn_body(qt_tab, kt_tab, *refs, mode, p0, t_valid, l_valid, tq, tk, dk, topk, nh):
    dsa = mode == "dsa"
    dh = LANES
    if dsa:
        (qT_ref, k_ref, vT_ref, q3_ref, ki3_ref, wiT_ref, tab_ref, far_ref,
         o_ref, m_scr, l_scr, acc_scr, s_scr, p_scr, mb_scr, skey_scr, thr_scr, planes_scr, e_scr) = refs
    else:
        (qT_ref, k_ref, vT_ref, o_ref, m_scr, l_scr, acc_scr, s_scr, p_scr, mb_scr) = refs
    qt = qt_tab[pl.program_id(1)]
    kt = kt_tab[pl.program_id(1)]
    q_lo = p0 + qt * tq
    q_hi = p0 + jnp.minimum(qt * tq + tq, t_valid) - 1
    n_allowed = jnp.minimum((q_hi // CHUNK + 1) * CHUNK, l_valid)
    last_kt = (n_allowed - 1) // tk
    qpos = q_lo + lax.broadcasted_iota(I32, (1, tq), 1)
    qchunk = qpos // CHUNK

    def allowed_mask(k0, rows):
        kidx = k0 + lax.broadcasted_iota(I32, (rows, tq), 0)
        return (kidx // CHUNK <= qchunk) & (kidx < l_valid)

    @pl.when(kt == 0)
    def _():
        m_scr[...] = jnp.full(m_scr.shape, NEG_BIG, F32)
        l_scr[...] = jnp.zeros(l_scr.shape, F32)
        acc_scr[...] = jnp.zeros(acc_scr.shape, F32)

    if dsa:
        iscale = (C_IDX_HEADS * C_IDX_DIM) ** -0.5

        @pl.when(kt == 0)
        def _():
            wpt = tk // 32
            kw = 3 * C_IDX_DIM

            def score_tile(j, carry):
                k0 = pl.multiple_of(j * tk, tk)
                ki3 = ki3_ref[0, pl.ds(k0, tk), :]
                s = jnp.zeros((tk, tq), F32)
                for h in range(C_IDX_HEADS):
                    d = jnp.dot(ki3, q3_ref[0, h * kw:(h + 1) * kw, :], preferred_element_type=F32)
                    s = s + jnp.maximum(d, 0.0) * wiT_ref[0, h:h + 1, :]
                s = s * iscale + 0.0
                bits = pltpu.bitcast(s, I32)
                key = bits ^ ((bits >> 31) & 0x7FFFFFFF)
                key = jnp.where(allowed_mask(k0, tk), key, INT_MIN)
                skey_scr[pl.ds(k0, tk), :] = key
                ukey = key ^ INT_MIN
                w0 = pl.multiple_of(j * wpt, SUBLANES)
                for g in range(tk // 256):
                    rows = [ukey[g * 256 + 8 * i:g * 256 + 8 * i + 8, :] for i in range(32)]
                    for bi, plane in enumerate(_transpose32(rows)):
                        planes_scr[bi, pl.ds(w0 + g * SUBLANES, SUBLANES), :] = plane
                ones = jnp.full((wpt, tq), -1, I32)
                planes_scr[32, pl.ds(w0, wpt), :] = ones
                e_scr[pl.ds(w0, wpt), :] = ones
                return carry

            def blank_tile(j, carry):
                w0 = pl.multiple_of(j * wpt, SUBLANES)
                for bi in range(33):
                    planes_scr[bi, pl.ds(w0, wpt), :] = jnp.zeros((wpt, tq), I32)
                e_scr[pl.ds(w0, wpt), :] = jnp.zeros((wpt, tq), I32)
                return carry

            ntile = last_kt + 1
            lax.fori_loop(0, ntile, score_tile, 0)
            nblk = (ntile + SEARCH_TILES - 1) // SEARCH_TILES
            lax.fori_loop(ntile, nblk * SEARCH_TILES, blank_tile, 0)
            wpb = SEARCH_TILES * wpt

            def lanesum(acc):
                return jnp.sum(acc, axis=0, keepdims=True)

            def fold(pc):
                return jnp.sum(pc.reshape(wpb // SUBLANES, SUBLANES, tq), axis=0)

            def settle(e, plane, take):
                t = e & plane
                return jnp.where(take != 0, t, e ^ t)

            def bit_step(i, st):
                c_gt, th, take_prev = st
                bi = 31 - i

                def body(j, acc):
                    w0 = pl.multiple_of(j * wpb, SUBLANES)
                    e = settle(e_scr[pl.ds(w0, wpb), :], planes_scr[bi + 1, pl.ds(w0, wpb), :], take_prev)
                    e_scr[pl.ds(w0, wpb), :] = e
                    return acc + fold(lax.population_count(e & planes_scr[bi, pl.ds(w0, wpb), :]))

                c1 = lanesum(lax.fori_loop(0, nblk, body, jnp.zeros((SUBLANES, tq), I32)))
                take = (c_gt + c1) >= topk
                return (jnp.where(take, c_gt, c_gt + c1),
                        jnp.where(take, th | lax.shift_left(jnp.int32(1), bi), th), jnp.where(take, 1, 0))

            zero = jnp.zeros((1, tq), I32)
            c_gt, th_u, take0 = lax.fori_loop(0, 32, bit_step, (zero, zero, zero + 1))

            def last_body(j, acc):
                w0 = pl.multiple_of(j * wpb, SUBLANES)
                e = settle(e_scr[pl.ds(w0, wpb), :], planes_scr[0, pl.ds(w0, wpb), :], take0)
                e_scr[pl.ds(w0, wpb), :] = e
                return acc + fold(lax.population_count(e))

            n_eq = lanesum(lax.fori_loop(0, nblk, last_body, jnp.zeros((SUBLANES, tq), I32)))
            need = topk - c_gt
            nbits = max(1, (l_valid - 1).bit_length())
            thr_scr[0:1, :] = th_u ^ INT_MIN
            thr_scr[1:2, :] = jnp.full((1, tq), (1 << nbits) - 1, I32)

            @pl.when(jnp.max((n_eq - need).astype(F32)) > 0.0)
            def _():
                def idx_step(i, jb):
                    cand = jb - lax.shift_left(jnp.int32(1), nbits - 1 - i)

                    def body(j, acc):
                        w0 = pl.multiple_of(j * wpb, SUBLANES)
                        wr = w0 + lax.broadcasted_iota(I32, (wpb, tq), 0)
                        base = (wr >> 3) * 256 + (wr & 7)
                        mx = (cand - base) >> 3
                        low = jnp.left_shift(2, jnp.clip(mx, 0, 30)) - 1
                        msk = jnp.where(mx < 0, 0, jnp.where(mx >= 31, -1, low))
                        return acc + fold(lax.population_count(e_scr[pl.ds(w0, wpb), :] & msk))

                    cnt = lanesum(lax.fori_loop(0, nblk, body, jnp.zeros((SUBLANES, tq), I32)))
                    return jnp.where(cnt >= need, cand, jb)

                thr_scr[1:2, :] = lax.fori_loop(0, nbits, idx_step, jnp.full((1, tq), (1 << nbits) - 1, I32))

    def tile(near):
        k0 = pl.multiple_of(kt * tk, tk)
        nchunk = tk // RC
        masked = dsa or near
        if masked:
            for c in range(nchunk):
                r0 = k0 + c * RC
                mask = allowed_mask(r0, RC) if near else None
                if dsa:
                    x = skey_scr[pl.ds(r0, RC), :]
                    kidx = r0 + lax.broadcasted_iota(I32, (RC, tq), 0)
                    sel = (x > thr_scr[0:1, :]) | ((x == thr_scr[0:1, :]) & (kidx <= thr_scr[1:2, :]))
                    mask = sel & mask if near else sel
                mb_scr[c * RC:(c + 1) * RC, :] = jnp.where(mask, 0.0, NEG_BIG)

        def chunk_bias(h, c):
            far = far_ref[0:1, h:h + 1]
            if not near:
                return far
            sb, off = (c * RC) // LANES, (c * RC) % LANES
            cols = []
            for qb in range(tq // LANES):
                delta = k0 + sb * LANES - (q_lo + qb * LANES)
                cols.append(jnp.where(delta == 0, tab_ref[0, h, off:off + RC, :],
                                      jnp.where(delta == -LANES, tab_ref[1, h, off:off + RC, :], far)))
            return jnp.concatenate(cols, axis=1) if len(cols) > 1 else cols[0]

        for h in range(nh):
            ks = slice(h * dk, (h + 1) * dk)
            s_scr[h] = jnp.dot(k_ref[0, :, ks], qT_ref[0, ks, :], preferred_element_type=F32)
        m_news, alphas = [], []
        for h in range(nh):
            mx = jnp.full((SUBLANES, tq), NEG_BIG, F32)
            for c in range(nchunk):
                rows = slice(c * RC, (c + 1) * RC)
                blk = s_scr[h, rows, :]
                if masked:
                    blk = blk + mb_scr[rows, :]
                    if dsa:
                        blk = blk + chunk_bias(h, c)
                    s_scr[h, rows, :] = blk
                mx = jnp.maximum(mx, jnp.max(blk.reshape(RC // SUBLANES, SUBLANES, tq), axis=0))
            m_prev = m_scr[h:h + 1, :]
            m_new = jnp.maximum(m_prev, jnp.max(mx, axis=0, keepdims=True))
            m_scr[h:h + 1, :] = m_new
            m_news.append(m_new)
            alphas.append(jnp.exp2(m_prev - m_new))
        for h in range(nh):
            for c in range(nchunk):
                rows = slice(c * RC, (c + 1) * RC)
                p_scr[h, rows, :] = jnp.exp2(s_scr[h, rows, :] - m_news[h]).astype(BF16)
        for h in range(nh):
            sl = slice(h * dh, (h + 1) * dh)
            pv = jnp.dot(vT_ref[0, h * VT_ROWS:(h + 1) * VT_ROWS, :], p_scr[h], preferred_element_type=F32)
            acc_scr[sl, :] = alphas[h] * acc_scr[sl, :] + pv[:dh]
            l_scr[h:h + 1, :] = alphas[h] * l_scr[h:h + 1, :] + pv[dh:dh + 1]

    is_far = kt * tk + tk - 1 <= q_lo - LANES

    @pl.when(is_far)
    def _():
        tile(False)

    @pl.when(jnp.logical_not(is_far))
    def _():
        tile(True)

    @pl.when(kt == last_kt)
    def _():
        for h in range(nh):
            sl = slice(h * dh, (h + 1) * dh)
            o = acc_scr[sl, :] / l_scr[h:h + 1, :]
            o_ref[0, :, sl] = o.T.astype(o_ref.dtype)


def _attention(mode, qT, k, vT, extra, *, p0, t_valid, l_valid, tk):
    b, hdk, tpad = qT.shape
    lpad = k.shape[1]
    nh = vT.shape[1] // VT_ROWS
    hd = nh * LANES
    dk = hdk // nh
    tq = 2 * LANES if tpad % (2 * LANES) == 0 else LANES
    nq, nk = tpad // tq, lpad // tk
    topk = min(TOPK_MAX, l_valid // 4)
    assert p0 % LANES == 0 and lpad % tk == 0 and tk % 256 == 0 and l_valid >= topk >= 1 and tk >= topk

    pairs = []
    for qt in range(nq):
        q_hi = p0 + min(qt * tq + tq, t_valid) - 1
        last = (min((q_hi // CHUNK + 1) * CHUNK, l_valid) - 1) // tk
        pairs += [(qt, kt) for kt in range(last + 1)]
    qt_tab = jnp.asarray([p[0] for p in pairs], I32)
    kt_tab = jnp.asarray([p[1] for p in pairs], I32)

    kmap = lambda i, s, qtt, ktt: (i, ktt[s], 0)
    vmap_ = lambda i, s, qtt, ktt: (i, 0, ktt[s])
    qmap = lambda i, s, qtt, ktt: (i, 0, qtt[s])
    cmap = lambda n: (lambda i, s, qtt, ktt: (0,) * n)
    in_specs = [pl.BlockSpec((1, hdk, tq), qmap),
                pl.BlockSpec((1, tk, hdk), kmap),
                pl.BlockSpec((1, nh * VT_ROWS, tk), vmap_)]
    scratch = [pltpu.VMEM((nh, tq), F32), pltpu.VMEM((nh, tq), F32), pltpu.VMEM((hd, tq), F32),
               pltpu.VMEM((nh, tk, tq), F32), pltpu.VMEM((nh, tk, tq), BF16), pltpu.VMEM((tk, tq), F32)]
    if mode == "dsa":
        q3, ki3, wiT, tabs, far = extra
        wrows = -(-nk // SEARCH_TILES) * SEARCH_TILES * (tk // 32)
        in_specs += [pl.BlockSpec((1, q3.shape[1], tq), qmap),
                     pl.BlockSpec((1, lpad, ki3.shape[2]), lambda i, s, qtt, ktt: (i, 0, 0)),
                     pl.BlockSpec((1, C_IDX_HEADS, tq), qmap),
                     pl.BlockSpec(tabs.shape, cmap(4)),
                     pl.BlockSpec(far.shape, cmap(2))]
        scratch += [pltpu.VMEM((lpad, tq), I32), pltpu.VMEM((SUBLANES, tq), I32),
                    pltpu.VMEM((33, wrows, tq), I32), pltpu.VMEM((wrows, tq), I32)]
    return pl.pallas_call(
        functools.partial(_attn_body, mode=mode, p0=p0, t_valid=t_valid, l_valid=l_valid, tq=tq, tk=tk,
                          dk=dk, topk=topk, nh=nh),
        grid_spec=pltpu.PrefetchScalarGridSpec(
            num_scalar_prefetch=2,
            grid=(b, len(pairs)),
            in_specs=in_specs,
            out_specs=pl.BlockSpec((1, tq, hd), lambda i, s, qtt, ktt: (i, qtt[s], 0)),
            scratch_shapes=scratch),
        out_shape=jax.ShapeDtypeStruct((b, tpad, hd), BF16),
        compiler_params=_cparams(("parallel", "arbitrary")),
        name="attn_" + mode,
    )(qt_tab, kt_tab, qT, k, vT, *extra)


PACK_TILE = 256


def _pack_body(cache_ref, new_ref, o_ref, *, n_cache, transpose, nh):
    j = pl.program_id(1)

    def emit(head):
        for h in range(nh):
            sl = slice(h * LANES, (h + 1) * LANES)
            x = head(h, sl)
            if transpose:
                r0 = h * VT_ROWS
                o_ref[0, r0:r0 + LANES, :] = x.T.astype(o_ref.dtype)
                o_ref[0, r0 + LANES:r0 + VT_ROWS, :] = jnp.ones((VT_ONES, x.shape[0]), o_ref.dtype)
            else:
                o_ref[0, :, sl] = x.astype(o_ref.dtype)

    @pl.when(j < n_cache)
    def _():
        emit(lambda h, sl: cache_ref[0, 0, :, h, :])

    @pl.when(j >= n_cache)
    def _():
        emit(lambda h, sl: new_ref[0, :, sl])


def _pack_keys(cache, layer, new, lpad, *, transpose):
    _, b, past, nh, dh = cache.shape
    tp = 2 * PACK_TILE if (past % (2 * PACK_TILE) == 0 and lpad % (2 * PACK_TILE) == 0) else PACK_TILE
    assert dh == LANES and past % tp == 0 and lpad % tp == 0
    n_cache = past // tp
    hd = nh * dh
    new = _pad_axis(new, 1, lpad - past)
    out_shape = (b, nh * VT_ROWS, lpad) if transpose else (b, lpad, hd)
    out_spec = (pl.BlockSpec((1, nh * VT_ROWS, tp), lambda i, j: (i, 0, j)) if transpose
                else pl.BlockSpec((1, tp, hd), lambda i, j: (i, j, 0)))
    return pl.pallas_call(
        functools.partial(_pack_body, n_cache=n_cache, transpose=transpose, nh=nh),
        grid=(b, lpad // tp),
        in_specs=[pl.BlockSpec((1, 1, tp, nh, dh),
                               lambda i, j: (layer, i, jnp.minimum(j, n_cache - 1), 0, 0)),
                  pl.BlockSpec((1, tp, hd), lambda i, j: (i, jnp.maximum(j - n_cache, 0), 0))],
        out_specs=out_spec,
        out_shape=jax.ShapeDtypeStruct(out_shape, BF16),
        compiler_params=_cparams(("parallel", "parallel")),
        name="pack_keys",
    )(cache, new)


def _split_hi_lo(x):
    hi = x.astype(BF16)
    lo = (x - hi.astype(F32)).astype(BF16)
    return hi, lo


def _pad_axis(x, axis, size):
    if x.shape[axis] == size:
        return x
    pad = [(0, 0)] * x.ndim
    pad[axis] = (0, size - x.shape[axis])
    return jnp.pad(x, pad)


def _vt_ones(v):
    b, l, hd = v.shape
    vt = jnp.swapaxes(v, 1, 2).reshape(b, hd // LANES, LANES, l)
    ones = jnp.ones((b, hd // LANES, VT_ONES, l), v.dtype)
    return jnp.concatenate([vt, ones], axis=2).reshape(b, -1, l)


def _key_tile(l_valid):
    tk = 512 if l_valid > 1024 else 256
    return -(-l_valid // tk) * tk, tk


def _prep_weights(W, d_model):
    f = {}
    wd = A_WIDTH
    n_even = W["ev_w_in"].shape[0]
    n_odd = W["od_w_in"].shape[0]
    f["even"] = []
    for e in range(n_even):
        wi = W["ev_w_in"][e]
        w_in = jnp.concatenate([wi[:, :3 * wd], wi[:, A_COLS:], wi[:, 3 * wd:A_COLS],
                                jnp.zeros((d_model, LORA_PAD - A_LORA), F32)], axis=1).astype(BF16)
        mu = W["rwkv_mu"][e]
        w2 = jnp.zeros((LANES, wd), F32).at[:A_DECAY_LORA].set(W["rwkv_w2"][e])
        a2 = jnp.zeros((LANES, wd), F32).at[A_DECAY_LORA:A_DECAY_LORA + A_ICLR_LORA].set(W["rwkv_a2"][e])
        g2 = jnp.zeros((LORA_PAD - LANES, wd), F32).at[:A_GATE_LORA].set(W["rwkv_g2"][e])
        hm = lambda v: v.reshape(A_HEADS, 1, A_HEAD_DIM)
        rw = dict(mu_r=mu[:3 * wd].reshape(1, -1),
                  mu_l=_pad_axis(mu[3 * wd:], 0, LORA_PAD).reshape(1, -1),
                  w0=W["rwkv_w0"][e].reshape(1, -1), w2=w2.astype(BF16),
                  a0=W["rwkv_a0"][e].reshape(1, -1), a2=a2.astype(BF16), g2=g2.astype(BF16),
                  k_k=W["rwkv_k_k"][e].reshape(1, -1), k_a=W["rwkv_k_a"][e].reshape(1, -1),
                  r_k=hm(W["rwkv_r_k"][e]), ln_w=hm(W["rwkv_ln_w"][e]), ln_b=hm(W["rwkv_ln_b"][e]))
        eye = jnp.eye(B_BLOCKS, dtype=F32)
        blockdiag = lambda w: (eye[:, None, :, None] * w[:, :, None, :]).reshape(B_WIDTH, B_WIDTH)
        wgate = jnp.concatenate([blockdiag(W["lru_wa"][e]), blockdiag(W["lru_wx"][e])], axis=1).astype(BF16)
        bgate = jnp.concatenate([W["lru_ba"][e], W["lru_bx"][e]])
        wo = W["ev_w_out"][e].astype(BF16)
        f["even"].append(dict(w_in=w_in, rw=rw, wgate=wgate, bgate=bgate, wo_a=wo[:wd], wo_b=wo[wd:],
                              cw=W["lru_conv_w"][e], cb=W["lru_conv_b"][e], lam=W["lru_lambda"][e]))
    f["odd"] = []
    cw_ = C_WIDTH
    qi_w = C_IDX_HEADS * C_IDX_DIM
    for o in range(n_odd):
        wi = W["od_w_in"][o]
        offs = [0]
        for s in (cw_, cw_, cw_, qi_w, C_IDX_DIM, C_IDX_HEADS, D_Q_RANK, D_KV_RANK, D_ROPE):
            offs.append(offs[-1] + s)
        q, k, v, qi, ki, wi_, qd, kvd, kr = [wi[:, offs[i]:offs[i + 1]] for i in range(9)]
        small = jnp.concatenate([ki, kr, wi_], axis=1)
        w_in = jnp.concatenate([q, k, v, qi, qd, kvd, _pad_axis(small, 1, 512)], axis=1).astype(BF16)
        wuq = W["mla_w_uq"][o].reshape(D_Q_RANK, D_HEADS, D_NOPE + D_ROPE)
        wuq = jnp.concatenate([wuq[:, :, :D_NOPE].reshape(D_Q_RANK, -1),
                               wuq[:, :, D_NOPE:].reshape(D_Q_RANK, -1)], axis=1).astype(BF16)
        wukv = W["mla_w_ukv"][o].reshape(D_KV_RANK, D_HEADS, D_NOPE + D_V)
        wuk = _pad_axis(wukv[:, :, :D_NOPE], 2, MLA_DK).reshape(D_KV_RANK, -1).astype(BF16)
        wuv = wukv[:, :, D_NOPE:].reshape(D_KV_RANK, -1).astype(BF16)
        wo = W["od_w_out"][o].astype(BF16)
        f["odd"].append(dict(w_in=w_in, wuq=wuq, wuk=wuk, wuv=wuv, wo_c=wo[:cw_], wo_d=wo[cw_:],
                             gq=W["mla_q_norm"][o], gkv=W["mla_kv_norm"][o]))
    f["ffn"] = dict(wg=W["ffn_w_gate"].astype(BF16), wu=W["ffn_w_up"].astype(BF16),
                    wd=W["ffn_w_down"].astype(BF16))
    f["tabs"] = _bias_tables(W["rel_bias"])
    nb = N_BUCKETS // 2
    f["far"] = _pad_axis(W["rel_bias"][nb - 1:nb, :] * LOG2E, 1, LANES)
    return f


def _mixer_even(x2, b, t, p0, shift, s0, h0, cbuf, fe, norm_g):
    wd = A_WIDTH
    p = _norm_matmul(x2, norm_g, fe["w_in"])
    p3 = p.reshape(b, t, -1)
    lora_col = 3 * wd + 2 * B_WIDTH
    shift_r = shift[:, None, :3 * wd]
    shift_l = _pad_axis(shift[:, None, 3 * wd:], 2, LORA_PAD)
    ya, s_new = _rwkv(p3, lora_col // LORA_PAD, shift_r, shift_l, s0, fe["rw"])
    yb, h_new, c_new = _rglru(p3, 3, 4, fe["cw"], fe["cb"], fe["wgate"], fe["bgate"], fe["lam"],
                              h0, cbuf, p0=p0)
    proj = (ya.reshape(b * t, wd), yb.reshape(b * t, B_WIDTH), fe["wo_a"], fe["wo_b"])
    last = p3[:, t - 1]
    new_shift = jnp.concatenate([last[:, :3 * wd], last[:, lora_col:lora_col + A_LORA]], axis=-1)
    return proj, new_shift, s_new, h_new[:, 0], c_new


def _mixer_odd(x2, b, t, p0, o, dsa_k, dsa_v, cik, clat, ckr, fo, f, norm_g, cos, sin):
    cw_ = C_WIDTH
    p = _norm_matmul(x2, norm_g, fo["w_in"])
    p3 = p.reshape(b, t, -1)
    past = dsa_k.shape[2]
    l_valid = past + t
    lpad, tk = _key_tile(l_valid)
    tpad = -(-t // LANES) * LANES
    q = p3[..., :cw_]
    k_new = p3[..., cw_:2 * cw_]
    v_new = p3[..., 2 * cw_:3 * cw_]
    qi = p3[..., 3 * cw_:3 * cw_ + 512]
    small = p3[..., 3 * cw_ + 1536:]
    ki_new = small[..., :C_IDX_DIM]
    wi = small[..., 2 * C_IDX_DIM:2 * C_IDX_DIM + C_IDX_HEADS]

    def keys(cache, new):
        allk = jnp.concatenate([cache.reshape(b, past, -1), new], axis=1) if past else new
        return _pad_axis(allk, 1, lpad)

    tq_ = lambda z: _pad_axis(jnp.swapaxes(z, 1, 2), 2, tpad)
    if past and past % PACK_TILE == 0:
        k_all = _pack_keys(dsa_k, o, k_new, lpad, transpose=False)
        vT_all = _pack_keys(dsa_v, o, v_new, lpad, transpose=True)
    else:
        k_all = keys(dsa_k[o], k_new).astype(BF16)
        vT_all = _vt_ones(keys(dsa_v[o], v_new).astype(BF16))
    kih, kil = _split_hi_lo(keys(cik, ki_new))
    ki3 = jnp.concatenate([kih, kih, kil], axis=-1)
    qih, qil = _split_hi_lo(tq_(qi).reshape(b, C_IDX_HEADS, C_IDX_DIM, tpad))
    q3 = jnp.concatenate([qih, qil, qih], axis=2).reshape(b, 3 * C_IDX_HEADS * C_IDX_DIM, tpad)
    yc = _attention("dsa", tq_(q * (C_HEAD_DIM ** -0.5 * LOG2E)).astype(BF16), k_all, vT_all,
                    (q3, ki3, tq_(wi), f["tabs"], f["far"]),
                    p0=p0, t_valid=t, l_valid=l_valid, tk=tk)[:, :t]
    qn, qr, lat, krope = _mla_prep(p3, 7, 8, 9, fo["gq"], fo["gkv"], fo["wuq"], cos, sin)
    lat_all = keys(clat, lat).reshape(b * lpad, -1)
    kr_all = keys(ckr, krope).reshape(b * lpad, -1)
    eye = jnp.eye(D_ROPE, dtype=BF16)
    ident = jnp.tile(jnp.pad(eye, ((0, 0), (D_NOPE, MLA_DK - D_NOPE - D_ROPE))), (1, D_HEADS))
    k_full = _matmul([lat_all, kr_all], [fo["wuk"], ident], out_dtype=BF16).reshape(b, lpad, -1)
    v_all = _matmul([lat_all], [fo["wuv"]], out_dtype=BF16).reshape(b, lpad, -1)
    q_full = jnp.concatenate([qn.reshape(b, t, D_HEADS, D_NOPE), qr.reshape(b, t, D_HEADS, D_ROPE),
                              jnp.zeros((b, t, D_HEADS, MLA_DK - D_NOPE - D_ROPE), BF16)], axis=-1)
    yd = _attention("mla", tq_(q_full.reshape(b, t, -1)), k_full, _vt_ones(v_all), (),
                    p0=p0, t_valid=t, l_valid=l_valid, tk=tk)[:, :t]
    proj = (yc.reshape(b * t, cw_), yd.reshape(b * t, -1), fo["wo_c"], fo["wo_d"])
    return (proj, k_new.reshape(b, t, C_HEADS, C_HEAD_DIM), v_new.reshape(b, t, C_HEADS, C_HEAD_DIM),
            ki_new, lat, krope)


def _trunk(x, p0, shift, rwkv_s, lru_h, lru_conv, dsa_k, dsa_v, dsa_ik, mla_lat, mla_kr, W, f):
    b, t, d = x.shape
    depth = W["norm_mix"].shape[0]
    pos = (p0 + jnp.arange(t)).astype(F32)
    inv = ROPE_BASE ** (-jnp.arange(0, D_ROPE, 2, dtype=F32) / D_ROPE)
    ang = pos[:, None] * inv[None, :]
    cos = jnp.tile(jnp.cos(ang), (1, 2 * D_HEADS))
    sin = jnp.tile(jnp.sin(ang), (1, 2 * D_HEADS))
    x2 = x.reshape(b * t, d)
    ev = [[] for _ in range(4)]
    od = [[] for _ in range(5)]
    for layer in range(depth):
        if layer % 2 == 0:
            e = layer // 2
            proj, *outs = _mixer_even(x2, b, t, p0, shift[e], rwkv_s[e], lru_h[e], lru_conv[e],
                                      f["even"][e], W["norm_mix"][layer])
            for lst, o_ in zip(ev, outs):
                lst.append(o_)
        else:
            o = layer // 2
            proj, *outs = _mixer_odd(x2, b, t, p0, o, dsa_k, dsa_v, dsa_ik[o], mla_lat[o], mla_kr[o],
                                     f["odd"][o], f, W["norm_mix"][layer], cos, sin)
            for lst, o_ in zip(od, outs):
                lst.append(o_)
        ff = f["ffn"]
        x2 = _ffn(x2, *proj, W["norm_ffn"][layer], ff["wg"], ff["wu"], ff["wd"], layer, W["final_norm"],
                  final_norm=layer == depth - 1)
    return (x2.reshape(b, t, d),) + tuple(jnp.stack(v) for v in ev) + tuple(jnp.stack(v) for v in od)


def kernel(x_prompt, x_sample, state_rwkv_shift, state_rwkv, state_lru, state_lru_conv, cache_dsa_k, cache_dsa_v, cache_dsa_idx_k, cache_mla_latent, cache_mla_krope, rel_bias, final_norm, norm_mix, norm_ffn, ffn_w_gate, ffn_w_up, ffn_w_down, ev_w_in, ev_w_out, rwkv_mu, rwkv_w0, rwkv_w2, rwkv_a0, rwkv_a2, rwkv_g2, rwkv_k_k, rwkv_k_a, rwkv_r_k, rwkv_ln_w, rwkv_ln_b, lru_conv_w, lru_conv_b, lru_wa, lru_ba, lru_wx, lru_bx, lru_lambda, od_w_in, od_w_out, mla_q_norm, mla_w_uq, mla_kv_norm, mla_w_ukv):
    W = dict(rel_bias=rel_bias, final_norm=final_norm, norm_mix=norm_mix, norm_ffn=norm_ffn,
             ffn_w_gate=ffn_w_gate, ffn_w_up=ffn_w_up, ffn_w_down=ffn_w_down,
             ev_w_in=ev_w_in, ev_w_out=ev_w_out, rwkv_mu=rwkv_mu, rwkv_w0=rwkv_w0,
             rwkv_w2=rwkv_w2, rwkv_a0=rwkv_a0, rwkv_a2=rwkv_a2, rwkv_g2=rwkv_g2,
             rwkv_k_k=rwkv_k_k, rwkv_k_a=rwkv_k_a, rwkv_r_k=rwkv_r_k, rwkv_ln_w=rwkv_ln_w,
             rwkv_ln_b=rwkv_ln_b, lru_conv_w=lru_conv_w, lru_conv_b=lru_conv_b, lru_wa=lru_wa,
             lru_ba=lru_ba, lru_wx=lru_wx, lru_bx=lru_bx, lru_lambda=lru_lambda,
             od_w_in=od_w_in, od_w_out=od_w_out, mla_q_norm=mla_q_norm, mla_w_uq=mla_w_uq,
             mla_kv_norm=mla_kv_norm, mla_w_ukv=mla_w_ukv)
    d_model = x_prompt.shape[-1]
    f = _prep_weights(W, d_model)
    bp = x_prompt.shape[0]
    n_even, n_odd = ev_w_in.shape[0], od_w_in.shape[0]
    dt = x_prompt.dtype
    z = lambda *shape: jnp.zeros(shape, dt)
    outs_p = _trunk(
        x_prompt, 0,
        z(n_even, bp, A_COLS), z(n_even, bp, A_HEADS, A_HEAD_DIM, A_HEAD_DIM),
        z(n_even, bp, B_WIDTH), z(n_even, bp, B_CONV - 1, B_WIDTH),
        z(n_odd, bp, 0, C_HEADS, C_HEAD_DIM), z(n_odd, bp, 0, C_HEADS, C_HEAD_DIM),
        z(n_odd, bp, 0, C_IDX_DIM), z(n_odd, bp, 0, D_KV_RANK), z(n_odd, bp, 0, D_ROPE), W, f)
    past = cache_dsa_k.shape[2]
    outs_s = _trunk(x_sample, past, state_rwkv_shift, state_rwkv, state_lru, state_lru_conv,
                    cache_dsa_k, cache_dsa_v, cache_dsa_idx_k, cache_mla_latent, cache_mla_krope, W, f)
    return (outs_p[0], outs_s[0]) + tuple(outs_p[1:]) + tuple(outs_s[1:])
```

```python
import functools
import math

import jax
import jax.numpy as jnp
from jax import lax
from jax.experimental import pallas as pl
from jax.experimental.pallas import tpu as pltpu

F32 = jnp.float32
BF16 = jnp.bfloat16
I32 = jnp.int32

CHUNK = 64
NORM_EPS = 1e-6
A_HEADS = 16
A_HEAD_DIM = 64
A_WIDTH = A_HEADS * A_HEAD_DIM
A_DECAY_LORA = 64
A_ICLR_LORA = 64
A_GATE_LORA = 160
A_LORA = A_DECAY_LORA + A_ICLR_LORA + A_GATE_LORA
A_COLS = 3 * A_WIDTH + A_LORA
A_LN_EPS = 64e-5
B_WIDTH = 1024
B_BLOCKS = 16
B_CONV = 4
B_C = 8.0
C_HEADS = 8
C_HEAD_DIM = 128
C_WIDTH = C_HEADS * C_HEAD_DIM
C_IDX_HEADS = 8
C_IDX_DIM = 64
TOPK_MAX = 256
D_HEADS = 8
D_NOPE = 128
D_ROPE = 64
D_V = 128
D_Q_RANK = 512
D_KV_RANK = 512
ROPE_BASE = 10000.0
N_BUCKETS = 32
MAX_DISTANCE = 128

LANES = 128
SUBLANES = 8
VMEM_LIMIT = 56 * 1024 * 1024
LORA_PAD = 512
NEG_BIG = -1e30
INT_MIN = -2147483648
RC = 64
SEARCH_TILES = 4
VT_ONES = 16
VT_ROWS = LANES + VT_ONES
LOG2E = 1.4426950408889634
MLA_DK = 256

NN = (((1,), (0,)), ((), ()))
NT = (((1,), (1,)), ((), ()))
BNN = (((2,), (1,)), ((0,), (0,)))
BNT = (((2,), (2,)), ((0,), (0,)))
BTN = (((1,), (1,)), ((0,), (0,)))


def _cparams(sem):
    return pltpu.CompilerParams(dimension_semantics=sem, vmem_limit_bytes=VMEM_LIMIT)


def _rms(x, g, eps=NORM_EPS):
    ms = jnp.mean(x * x, axis=-1, keepdims=True)
    return x * lax.rsqrt(ms + eps) * g


def _softplus(x):
    return jnp.maximum(x, 0.0) + jnp.log1p(jnp.exp(-jnp.abs(x)))


def _dotp(a, b, dims, passes):
    if passes == 6:
        return lax.dot_general(a, b, dims, precision=lax.Precision.HIGHEST, preferred_element_type=F32)
    ah = a.astype(BF16)
    bh = b.astype(BF16)
    out = lax.dot_general(ah, bh, dims, preferred_element_type=F32)
    if passes == 3:
        al = (a - ah.astype(F32)).astype(BF16)
        bl = (b - bh.astype(F32)).astype(BF16)
        out = out + lax.dot_general(ah, bl, dims, preferred_element_type=F32)
        out = out + lax.dot_general(al, bh, dims, preferred_element_type=F32)
    return out


def _norm_matmul_body(x_ref, g_ref, w_ref, *refs, segs):
    outs, xn_ref = refs[:-1], refs[-1]
    j = pl.program_id(1)

    @pl.when(j == 0)
    def _():
        xn_ref[...] = _rms(x_ref[...], g_ref[...]).astype(BF16)

    acc = jnp.dot(xn_ref[...], w_ref[...], preferred_element_type=F32)
    for o_ref, (lo, hi, _, scale) in zip(outs, segs):
        @pl.when((j >= lo) & (j < hi))
        def _(o_ref=o_ref, scale=scale):
            o_ref[...] = (acc if scale == 1.0 else acc * scale).astype(o_ref.dtype)


def _row_tile(m, cap):
    tm = cap
    while m % tm:
        tm //= 2
    return tm


def _norm_matmul(x, g, w, segs=None, *, tn=512):
    m, k = x.shape
    n = w.shape[1]
    tm = _row_tile(m, 1024)
    segs = segs or [(0, n // tn, F32, 1.0)]
    out_specs = [pl.BlockSpec((tm, tn), lambda i, j, lo=lo, nb=hi - lo: (i, jnp.clip(j - lo, 0, nb - 1)))
                 for lo, hi, _, _ in segs]
    outs = pl.pallas_call(
        functools.partial(_norm_matmul_body, segs=tuple(segs)),
        grid=(m // tm, n // tn),
        in_specs=[pl.BlockSpec((tm, k), lambda i, j: (i, 0)),
                  pl.BlockSpec((1, k), lambda i, j: (0, 0)),
                  pl.BlockSpec((k, tn), lambda i, j: (0, j))],
        out_specs=out_specs,
        out_shape=[jax.ShapeDtypeStruct((m, (hi - lo) * tn), dt) for lo, hi, dt, _ in segs],
        scratch_shapes=[pltpu.VMEM((tm, k), BF16)],
        compiler_params=_cparams(("parallel", "arbitrary")),
        name="norm_matmul",
    )(x, g.reshape(1, k), w)
    return outs if len(segs) > 1 else outs[0]


def _mm_body(*refs, n_lhs, has_res):
    o_ref = refs[-1]
    acc = refs[2 * n_lhs][...] if has_res else None
    for a_ref, w_ref in zip(refs[:n_lhs], refs[n_lhs:2 * n_lhs]):
        d = jnp.dot(a_ref[...].astype(BF16), w_ref[...], preferred_element_type=F32)
        acc = d if acc is None else acc + d
    o_ref[...] = acc.astype(o_ref.dtype)


def _matmul(lhs_list, w_list, res=None, *, tn=512, out_dtype=F32):
    m = lhs_list[0].shape[0]
    n = w_list[0].shape[1]
    tm = _row_tile(m, 1024)
    in_specs = [pl.BlockSpec((tm, a.shape[1]), lambda i, j: (i, 0)) for a in lhs_list]
    in_specs += [pl.BlockSpec((w.shape[0], tn), lambda i, j: (0, j)) for w in w_list]
    args = list(lhs_list) + list(w_list)
    if res is not None:
        in_specs.append(pl.BlockSpec((tm, tn), lambda i, j: (i, j)))
        args.append(res)
    return pl.pallas_call(
        functools.partial(_mm_body, n_lhs=len(lhs_list), has_res=res is not None),
        grid=(m // tm, n // tn),
        in_specs=in_specs,
        out_specs=pl.BlockSpec((tm, tn), lambda i, j: (i, j)),
        out_shape=jax.ShapeDtypeStruct((m, n), out_dtype),
        compiler_params=_cparams(("parallel", "arbitrary")),
        name="matmul",
    )(*args)


def _ffn_body(x_ref, g_ref, wg_ref, wu_ref, wd_ref, gf_ref, o_ref, xn_ref, acc_ref, *, final_norm):
    f = pl.program_id(1)

    @pl.when(f == 0)
    def _():
        xn_ref[...] = _rms(x_ref[...], g_ref[...]).astype(BF16)
        acc_ref[...] = jnp.zeros_like(acc_ref)

    xn = xn_ref[...]
    hg = jnp.dot(xn, wg_ref[0], preferred_element_type=F32)
    hu = jnp.dot(xn, wu_ref[0], preferred_element_type=F32)
    h = hg * jax.nn.sigmoid(hg) * hu
    acc_ref[...] += jnp.dot(h.astype(BF16), wd_ref[0], preferred_element_type=F32)

    @pl.when(f == pl.num_programs(1) - 1)
    def _():
        y = x_ref[...] + acc_ref[...]
        if final_norm:
            y = _rms(y, gf_ref[...])
        o_ref[...] = y


def _ffn(x, g, wg, wu, wd, layer, gf, *, final_norm, tf=512):
    m, k = x.shape
    dff = wg.shape[2]
    tm = _row_tile(m, 512)
    return pl.pallas_call(
        functools.partial(_ffn_body, final_norm=final_norm),
        grid=(m // tm, dff // tf),
        in_specs=[pl.BlockSpec((tm, k), lambda i, f: (i, 0)),
                  pl.BlockSpec((1, k), lambda i, f: (0, 0)),
                  pl.BlockSpec((1, k, tf), lambda i, f: (layer, 0, f)),
                  pl.BlockSpec((1, k, tf), lambda i, f: (layer, 0, f)),
                  pl.BlockSpec((1, tf, k), lambda i, f: (layer, f, 0)),
                  pl.BlockSpec((1, k), lambda i, f: (0, 0))],
        out_specs=pl.BlockSpec((tm, k), lambda i, f: (i, 0)),
        out_shape=jax.ShapeDtypeStruct((m, k), F32),
        scratch_shapes=[pltpu.VMEM((tm, k), BF16), pltpu.VMEM((tm, k), F32)],
        compiler_params=_cparams(("parallel", "arbitrary")),
        name="ffn",
    )(x, g.reshape(1, k), wg, wu, wd, gf.reshape(1, k))


def _lru_body(gate_ref, xb_ref, cw_ref, cb_ref, wg_ref, bg_ref, lam_ref, h0_ref, cbuf_ref,
              y_ref, hout_ref, cout_ref, xbuf, hcar, *, tt, p0):
    t = pl.program_id(1)
    w = B_WIDTH
    halo = SUBLANES

    @pl.when(t == 0)
    def _():
        xbuf[0:halo, :] = jnp.zeros((halo, w), F32)
        xbuf[halo - (B_CONV - 1):halo, :] = cbuf_ref[0]
        hcar[...] = h0_ref[0]

    xb = xb_ref[0]
    xbuf[halo:halo + tt, :] = xb
    xc = cb_ref[...] + cw_ref[B_CONV - 1:B_CONV, :] * xb
    for j in range(B_CONV - 1):
        off = halo - (B_CONV - 1) + j
        xc = xc + cw_ref[j:j + 1, :] * xbuf[off:off + tt, :]
    tail = xbuf[tt + halo - (B_CONV - 1):tt + halo, :]
    xbuf[halo - (B_CONV - 1):halo, :] = tail

    pre = jnp.dot(xc.astype(BF16), wg_ref[...], preferred_element_type=F32) + bg_ref[...]
    rg = jax.nn.sigmoid(pre[:, :w])
    ig = jax.nn.sigmoid(pre[:, w:])
    log_a = (-B_C) * rg * _softplus(-lam_ref[...])
    a = jnp.exp(log_a)
    row = lax.broadcasted_iota(I32, (tt, w), 0)
    th = jnp.tanh(log_a)
    mult = jnp.sqrt(-2.0 * th / (1.0 - th))
    mult = jnp.where(row + (p0 + t * tt) == 0, 1.0, mult)
    u = mult * (ig * xc)

    d = 1
    while d < tt:
        keep = row >= d
        a_sh = pltpu.roll(a, d, 0)
        u_sh = pltpu.roll(u, d, 0)
        u = u + jnp.where(keep, a * u_sh, 0.0)
        a = jnp.where(keep, a * a_sh, a)
        d *= 2
    h = u + a * hcar[...]
    hcar[...] = h[tt - 1:tt, :]
    y_ref[0] = (h * jax.nn.gelu(gate_ref[0])).astype(y_ref.dtype)

    @pl.when(t == pl.num_programs(1) - 1)
    def _():
        hout_ref[0] = h[tt - 1:tt, :]
        cout_ref[0] = tail


def _rglru(p3, gate_blk, xb_blk, cw, cb, wgate, bgate, lam, h0, cbuf, *, p0):
    b, t, _ = p3.shape
    w = B_WIDTH
    tt = min(256, t)
    row = lambda v: v.reshape(1, -1)
    return pl.pallas_call(
        functools.partial(_lru_body, tt=tt, p0=p0),
        grid=(b, t // tt),
        in_specs=[pl.BlockSpec((1, tt, w), lambda i, j: (i, j, gate_blk)),
                  pl.BlockSpec((1, tt, w), lambda i, j: (i, j, xb_blk)),
                  pl.BlockSpec((B_CONV, w), lambda i, j: (0, 0)),
                  pl.BlockSpec((1, w), lambda i, j: (0, 0)),
                  pl.BlockSpec((w, 2 * w), lambda i, j: (0, 0)),
                  pl.BlockSpec((1, 2 * w), lambda i, j: (0, 0)),
                  pl.BlockSpec((1, w), lambda i, j: (0, 0)),
                  pl.BlockSpec((1, 1, w), lambda i, j: (i, 0, 0)),
                  pl.BlockSpec((1, B_CONV - 1, w), lambda i, j: (i, 0, 0))],
        out_specs=[pl.BlockSpec((1, tt, w), lambda i, j: (i, j, 0)),
                   pl.BlockSpec((1, 1, w), lambda i, j: (i, 0, 0)),
                   pl.BlockSpec((1, B_CONV - 1, w), lambda i, j: (i, 0, 0))],
        out_shape=[jax.ShapeDtypeStruct((b, t, w), BF16),
                   jax.ShapeDtypeStruct((b, 1, w), F32),
                   jax.ShapeDtypeStruct((b, B_CONV - 1, w), F32)],
        scratch_shapes=[pltpu.VMEM((tt + SUBLANES, w), F32), pltpu.VMEM((1, w), F32)],
        compiler_params=_cparams(("parallel", "arbitrary")),
        name="rglru",
    )(p3, p3, cw, row(cb), wgate, row(bgate), row(lam), h0.reshape(b, 1, w), cbuf)


RWKV_PASSES = 1
RWKV_PASSES_SOLVE = 3


def _rwkv_body(rkv_ref, lora_ref, shr_ref, shl_ref, s0_ref, mur_ref, mul_ref, w0_ref, w2_ref, a0_ref,
               a2_ref, g2_ref, kk_ref, ka_ref, rk_ref, lnw_ref, lnb_ref, tri_ref,
               y_ref, sout_ref, buf_r, buf_l, s_scr, st_r, st_k, st_v, st_q, st_a, st_l, st_d, *, c):
    ci = pl.program_id(1)
    halo = SUBLANES
    hn, n, wd = A_HEADS, A_HEAD_DIM, A_WIDTH
    mm = functools.partial(_dotp, passes=RWKV_PASSES)

    @pl.when(ci == 0)
    def _():
        buf_r[halo - 1:halo, :] = shr_ref[0]
        buf_l[halo - 1:halo, :] = shl_ref[0]
        s_scr[...] = s0_ref[0]

    cur_r = rkv_ref[0]
    cur_l = lora_ref[0]
    buf_r[halo:halo + c, :] = cur_r
    buf_l[halo:halo + c, :] = cur_l
    xm = cur_r + mur_ref[...] * (buf_r[halo - 1:halo - 1 + c, :] - cur_r)
    lo = cur_l + mul_ref[...] * (buf_l[halo - 1:halo - 1 + c, :] - cur_l)
    buf_r[halo - 1:halo, :] = cur_r[c - 1:c, :]
    buf_l[halo - 1:halo, :] = cur_l[c - 1:c, :]

    r = xm[:, :wd]
    k = xm[:, wd:2 * wd]
    v = xm[:, 2 * wd:]
    lo_a = lo[:, :LANES]
    w_pre = w0_ref[...] + jnp.dot(jnp.tanh(lo_a).astype(BF16), w2_ref[...], preferred_element_type=F32)
    w_log = -_softplus(-w_pre) - 0.5
    ld = -jnp.exp(w_log)
    a = jax.nn.sigmoid(a0_ref[...] + jnp.dot(lo_a.astype(BF16), a2_ref[...], preferred_element_type=F32))
    g = jnp.dot(jax.nn.sigmoid(lo[:, LANES:]).astype(BF16), g2_ref[...], preferred_element_type=F32)
    kq = k * kk_ref[...]
    k2 = k * (1.0 + (a - 1.0) * ka_ref[...])
    lc = lax.dot_general(tri_ref[...], ld, NN, precision=lax.Precision.HIGHEST,
                         preferred_element_type=F32)

    for h in range(hn):
        sl = slice(h * n, (h + 1) * n)
        st_r[h] = r[:, sl]
        st_k[h] = k2[:, sl]
        st_v[h] = v[:, sl]
        st_q[h] = kq[:, sl]
        st_a[h] = a[:, sl]
        st_l[h] = lc[:, sl]
        st_d[h] = ld[:, sl]

    rh_, k2h, vh, kqh, ah, lch, ldh = (st_r[...], st_k[...], st_v[...], st_q[...], st_a[...],
                                       st_l[...], st_d[...])
    nrm = jnp.sqrt(jnp.sum(kqh * kqh, axis=-1, keepdims=True))
    kk = kqh / jnp.maximum(nrm, 1e-12)
    kka = kk * ah
    e_neg = jnp.exp(-lch)
    am = jnp.exp(lch - ldh) * kk
    bm = kka * e_neg
    kh = k2h * e_neg
    rh = rh_ * jnp.exp(lch)
    l_end = lch[:, c - 1:c, :]
    e_c = jnp.exp(l_end - lch)
    bp = kka * e_c
    kp = k2h * e_c
    w_end = jnp.exp(l_end)

    x2 = jnp.concatenate([am, rh], axis=1)
    zb = _dotp(x2, bm, BNT, RWKV_PASSES_SOLVE)
    zk = mm(x2, kh, BNT)
    ti = lax.broadcasted_iota(I32, (hn, c, c), 1)
    si = lax.broadcasted_iota(I32, (hn, c, c), 2)
    strict = si < ti
    incl = si <= ti
    m1 = jnp.where(strict, zb[:, :c], 0.0)
    m4 = jnp.where(incl, zb[:, c:], 0.0)
    m2 = jnp.where(strict, zk[:, :c], 0.0)
    m3 = jnp.where(incl, zk[:, c:], 0.0)

    tm = jnp.where(si == ti, 1.0, 0.0) - m1
    npow = mm(m1, m1, BNN)
    span = 2
    while span < c:
        tm = tm + mm(tm, npow, BNN)
        span *= 2
        if span < c:
            npow = mm(npow, npow, BNN)

    s0 = s_scr[...]
    rhs = mm(am, s0, BNT) + mm(m2, vh, BNN)
    p = mm(tm, rhs, BNN)
    y = mm(rh, s0, BNT) + mm(m3, vh, BNN) - mm(m4, p, BNN)
    s_new = s0 * w_end + mm(vh, kp, BTN) - mm(p, bp, BTN)
    s_scr[...] = s_new

    mean = jnp.mean(y, axis=-1, keepdims=True)
    yc = y - mean
    var = jnp.mean(yc * yc, axis=-1, keepdims=True)
    yn = yc * lax.rsqrt(var + A_LN_EPS) * lnw_ref[...] + lnb_ref[...]
    bonus = jnp.sum(rh_ * k2h * rk_ref[...], axis=-1, keepdims=True) * vh
    yo = yn + bonus
    yo = jnp.concatenate([yo[h] for h in range(hn)], axis=-1)
    y_ref[0] = (yo * g).astype(y_ref.dtype)

    @pl.when(ci == pl.num_programs(1) - 1)
    def _():
        sout_ref[0] = s_new


def _rwkv(p3, lora_blk, shift_r, shift_l, s0, wts):
    b, t, _ = p3.shape
    c = min(CHUNK, t)
    hn, n, wd = A_HEADS, A_HEAD_DIM, A_WIDTH
    full = lambda shape: pl.BlockSpec(shape, lambda i, j: (0,) * len(shape))
    tri = (jnp.arange(c)[:, None] >= jnp.arange(c)[None, :]).astype(F32)
    st = pltpu.VMEM((hn, c, n), F32)
    return pl.pallas_call(
        functools.partial(_rwkv_body, c=c),
        grid=(b, t // c),
        in_specs=[pl.BlockSpec((1, c, 3 * wd), lambda i, j: (i, j, 0)),
                  pl.BlockSpec((1, c, LORA_PAD), lambda i, j: (i, j, lora_blk)),
                  pl.BlockSpec((1, 1, 3 * wd), lambda i, j: (i, 0, 0)),
                  pl.BlockSpec((1, 1, LORA_PAD), lambda i, j: (i, 0, 0)),
                  pl.BlockSpec((1, hn, n, n), lambda i, j: (i, 0, 0, 0)),
                  full((1, 3 * wd)), full((1, LORA_PAD)), full((1, wd)), full((LANES, wd)),
                  full((1, wd)), full((LANES, wd)), full((LORA_PAD - LANES, wd)),
                  full((1, wd)), full((1, wd)), full((hn, 1, n)), full((hn, 1, n)), full((hn, 1, n)),
                  full((c, c))],
        out_specs=[pl.BlockSpec((1, c, wd), lambda i, j: (i, j, 0)),
                   pl.BlockSpec((1, hn, n, n), lambda i, j: (i, 0, 0, 0))],
        out_shape=[jax.ShapeDtypeStruct((b, t, wd), BF16),
                   jax.ShapeDtypeStruct((b, hn, n, n), F32)],
        scratch_shapes=[pltpu.VMEM((c + SUBLANES, 3 * wd), F32), pltpu.VMEM((c + SUBLANES, LORA_PAD), F32),
                        pltpu.VMEM((hn, n, n), F32), st, st, st, st, st, st, st],
        compiler_params=_cparams(("parallel", "arbitrary")),
        name="rwkv7",
    )(p3, p3, shift_r, shift_l, s0, wts["mu_r"], wts["mu_l"], wts["w0"], wts["w2"], wts["a0"], wts["a2"],
      wts["g2"], wts["k_k"], wts["k_a"], wts["r_k"], wts["ln_w"], wts["ln_b"], tri)


def _rope(x, cos, sin):
    wdt = x.shape[-1]
    lane = lax.broadcasted_iota(I32, x.shape, 1)
    first = (lane % D_ROPE) < (D_ROPE // 2)
    rot = jnp.where(first, -pltpu.roll(x, wdt - D_ROPE // 2, 1), pltpu.roll(x, D_ROPE // 2, 1))
    return x * cos + rot * sin


def _mla_prep_body(qd_ref, kvd_ref, sm_ref, gq_ref, gkv_ref, wuq_ref, cos_ref, sin_ref,
                   qn_ref, qr_ref, lat_ref, kr_ref):
    nope_w = D_HEADS * D_NOPE
    qdn = _rms(qd_ref[0], gq_ref[...]).astype(BF16)
    qf = jnp.dot(qdn, wuq_ref[...], preferred_element_type=F32) * ((D_NOPE + D_ROPE) ** -0.5 * LOG2E)
    qn_ref[0] = qf[:, :nope_w].astype(qn_ref.dtype)
    cos = cos_ref[...]
    sin = sin_ref[...]
    qr_ref[0] = _rope(qf[:, nope_w:], cos, sin).astype(qr_ref.dtype)
    lat_ref[0] = _rms(kvd_ref[0], gkv_ref[...])
    sm = sm_ref[0][:, :LANES]
    kr = _rope(sm, cos[:, :LANES], sin[:, :LANES])
    kr_ref[0] = kr[:, C_IDX_DIM:C_IDX_DIM + D_ROPE]


def _mla_prep(p3, qd_blk, kvd_blk, sm_blk, gq, gkv, wuq, cos, sin):
    b, t, _ = p3.shape
    tt = min(256, t)
    rw = D_HEADS * D_ROPE
    full = lambda shape: pl.BlockSpec(shape, lambda i, j: (0,) * len(shape))
    return pl.pallas_call(
        _mla_prep_body,
        grid=(b, t // tt),
        in_specs=[pl.BlockSpec((1, tt, D_Q_RANK), lambda i, j: (i, j, qd_blk)),
                  pl.BlockSpec((1, tt, D_KV_RANK), lambda i, j: (i, j, kvd_blk)),
                  pl.BlockSpec((1, tt, 512), lambda i, j: (i, j, sm_blk)),
                  full((1, D_Q_RANK)), full((1, D_KV_RANK)), full(wuq.shape),
                  pl.BlockSpec((tt, rw), lambda i, j: (j, 0)),
                  pl.BlockSpec((tt, rw), lambda i, j: (j, 0))],
        out_specs=[pl.BlockSpec((1, tt, D_HEADS * D_NOPE), lambda i, j: (i, j, 0)),
                   pl.BlockSpec((1, tt, rw), lambda i, j: (i, j, 0)),
                   pl.BlockSpec((1, tt, D_KV_RANK), lambda i, j: (i, j, 0)),
                   pl.BlockSpec((1, tt, D_ROPE), lambda i, j: (i, j, 0))],
        out_shape=[jax.ShapeDtypeStruct((b, t, D_HEADS * D_NOPE), BF16),
                   jax.ShapeDtypeStruct((b, t, rw), BF16),
                   jax.ShapeDtypeStruct((b, t, D_KV_RANK), F32),
                   jax.ShapeDtypeStruct((b, t, D_ROPE), F32)],
        compiler_params=_cparams(("parallel", "parallel")),
        name="mla_prep",
    )(p3, p3, p3, gq.reshape(1, -1), gkv.reshape(1, -1), wuq, cos, sin)


def _bias_table_body(rb_ref, o_ref):
    d = pl.program_id(0)
    h = pl.program_id(1)
    nb = N_BUCKETS // 2
    max_exact = nb // 2
    s = lax.broadcasted_iota(I32, (LANES, LANES), 0)
    q = lax.broadcasted_iota(I32, (LANES, LANES), 1)
    rel = s - q - d * LANES
    n = jnp.abs(rel)
    big = jnp.maximum(n, max_exact).astype(F32)
    large = max_exact + (jnp.log(big / max_exact) / math.log(MAX_DISTANCE / max_exact)
                         * (nb - max_exact)).astype(I32)
    large = jnp.minimum(large, nb - 1)
    bucket = jnp.where(rel > 0, nb, 0) + jnp.where(n < max_exact, n, large)
    out = jnp.zeros((LANES, LANES), F32)
    for bk in range(N_BUCKETS):
        out = jnp.where(bucket == bk, rb_ref[bk, h], out)
    o_ref[0, 0] = out * LOG2E


def _bias_tables(rel_bias):
    return pl.pallas_call(
        _bias_table_body,
        grid=(2, C_HEADS),
        in_specs=[pl.BlockSpec(memory_space=pltpu.SMEM)],
        out_specs=pl.BlockSpec((1, 1, LANES, LANES), lambda d, h: (d, h, 0, 0)),
        out_shape=jax.ShapeDtypeStruct((2, C_HEADS, LANES, LANES), F32),
        name="bias_tables",
    )(rel_bias)


def _transpose32(x):
    x = list(x)
    for s, msk in ((16, 0x0000FFFF), (8, 0x00FF00FF), (4, 0x0F0F0F0F), (2, 0x33333333), (1, 0x55555555)):
        sh = jnp.full(x[0].shape, s, I32)
        for i in range(32):
            if i & s == 0:
                t = (lax.shift_right_logical(x[i], sh) ^ x[i + s]) & msk
                x[i + s] = x[i + s] ^ t
                x[i] = x[i] ^ lax.shift_left(t, sh)
    return x


def _attn_body(qt_tab, kt_tab, *refs, mode, p0, t_valid, l_valid, tq, tk, dk, topk, nh):
    dsa = mode == "dsa"
    dh = LANES
    if dsa:
        (qT_ref, k_ref, vT_ref, q3_ref, ki3_ref, wiT_ref, tab_ref, far_ref,
         o_ref, m_scr, l_scr, acc_scr, s_scr, p_scr, mb_scr, skey_scr, thr_scr, planes_scr, e_scr) = refs
    else:
        (qT_ref, k_ref, vT_ref, o_ref, m_scr, l_scr, acc_scr, s_scr, p_scr, mb_scr) = refs
    qt = qt_tab[pl.program_id(1)]
    kt = kt_tab[pl.program_id(1)]
    q_lo = p0 + qt * tq
    q_hi = p0 + jnp.minimum(qt * tq + tq, t_valid) - 1
    n_allowed = jnp.minimum((q_hi // CHUNK + 1) * CHUNK, l_valid)
    last_kt = (n_allowed - 1) // tk
    qpos = q_lo + lax.broadcasted_iota(I32, (1, tq), 1)
    qchunk = qpos // CHUNK

    def allowed_mask(k0, rows):
        kidx = k0 + lax.broadcasted_iota(I32, (rows, tq), 0)
        return (kidx // CHUNK <= qchunk) & (kidx < l_valid)

    @pl.when(kt == 0)
    def _():
        m_scr[...] = jnp.full(m_scr.shape, NEG_BIG, F32)
        l_scr[...] = jnp.zeros(l_scr.shape, F32)
        acc_scr[...] = jnp.zeros(acc_scr.shape, F32)

    if dsa:
        iscale = (C_IDX_HEADS * C_IDX_DIM) ** -0.5

        @pl.when(kt == 0)
        def _():
            wpt = tk // 32
            kw = 3 * C_IDX_DIM

            def score_tile(j, carry):
                k0 = pl.multiple_of(j * tk, tk)
                ki3 = ki3_ref[0, pl.ds(k0, tk), :]
                s = jnp.zeros((tk, tq), F32)
                for h in range(C_IDX_HEADS):
                    d = jnp.dot(ki3, q3_ref[0, h * kw:(h + 1) * kw, :], preferred_element_type=F32)
                    s = s + jnp.maximum(d, 0.0) * wiT_ref[0, h:h + 1, :]
                s = s * iscale + 0.0
                bits = pltpu.bitcast(s, I32)
                key = bits ^ ((bits >> 31) & 0x7FFFFFFF)
                key = jnp.where(allowed_mask(k0, tk), key, INT_MIN)
                skey_scr[pl.ds(k0, tk), :] = key
                ukey = key ^ INT_MIN
                w0 = pl.multiple_of(j * wpt, SUBLANES)
                for g in range(tk // 256):
                    rows = [ukey[g * 256 + 8 * i:g * 256 + 8 * i + 8, :] for i in range(32)]
                    for bi, plane in enumerate(_transpose32(rows)):
                        planes_scr[bi, pl.ds(w0 + g * SUBLANES, SUBLANES), :] = plane
                ones = jnp.full((wpt, tq), -1, I32)
                planes_scr[32, pl.ds(w0, wpt), :] = ones
                e_scr[pl.ds(w0, wpt), :] = ones
                return carry

            def blank_tile(j, carry):
                w0 = pl.multiple_of(j * wpt, SUBLANES)
                for bi in range(33):
                    planes_scr[bi, pl.ds(w0, wpt), :] = jnp.zeros((wpt, tq), I32)
                e_scr[pl.ds(w0, wpt), :] = jnp.zeros((wpt, tq), I32)
                return carry

            ntile = last_kt + 1
            lax.fori_loop(0, ntile, score_tile, 0)
            nblk = (ntile + SEARCH_TILES - 1) // SEARCH_TILES
            lax.fori_loop(ntile, nblk * SEARCH_TILES, blank_tile, 0)
            wpb = SEARCH_TILES * wpt

            def lanesum(acc):
                return jnp.sum(acc, axis=0, keepdims=True)

            def fold(pc):
                return jnp.sum(pc.reshape(wpb // SUBLANES, SUBLANES, tq), axis=0)

            def settle(e, plane, take):
                t = e & plane
                return jnp.where(take != 0, t, e ^ t)

            def bit_step(i, st):
                c_gt, th, take_prev = st
                bi = 31 - i

                def body(j, acc):
                    w0 = pl.multiple_of(j * wpb, SUBLANES)
                    e = settle(e_scr[pl.ds(w0, wpb), :], planes_scr[bi + 1, pl.ds(w0, wpb), :], take_prev)
                    e_scr[pl.ds(w0, wpb), :] = e
                    return acc + fold(lax.population_count(e & planes_scr[bi, pl.ds(w0, wpb), :]))

                c1 = lanesum(lax.fori_loop(0, nblk, body, jnp.zeros((SUBLANES, tq), I32)))
                take = (c_gt + c1) >= topk
                return (jnp.where(take, c_gt, c_gt + c1),
                        jnp.where(take, th | lax.shift_left(jnp.int32(1), bi), th), jnp.where(take, 1, 0))

            zero = jnp.zeros((1, tq), I32)
            c_gt, th_u, take0 = lax.fori_loop(0, 32, bit_step, (zero, zero, zero + 1))

            def last_body(j, acc):
                w0 = pl.multiple_of(j * wpb, SUBLANES)
                e = settle(e_scr[pl.ds(w0, wpb), :], planes_scr[0, pl.ds(w0, wpb), :], take0)
                e_scr[pl.ds(w0, wpb), :] = e
                return acc + fold(lax.population_count(e))

            n_eq = lanesum(lax.fori_loop(0, nblk, last_body, jnp.zeros((SUBLANES, tq), I32)))
            need = topk - c_gt
            nbits = max(1, (l_valid - 1).bit_length())
            thr_scr[0:1, :] = th_u ^ INT_MIN
            thr_scr[1:2, :] = jnp.full((1, tq), (1 << nbits) - 1, I32)

            @pl.when(jnp.max((n_eq - need).astype(F32)) > 0.0)
            def _():
                def idx_step(i, jb):
                    cand = jb - lax.shift_left(jnp.int32(1), nbits - 1 - i)

                    def body(j, acc):
                        w0 = pl.multiple_of(j * wpb, SUBLANES)
                        wr = w0 + lax.broadcasted_iota(I32, (wpb, tq), 0)
                        base = (wr >> 3) * 256 + (wr & 7)
                        mx = (cand - base) >> 3
                        low = jnp.left_shift(2, jnp.clip(mx, 0, 30)) - 1
                        msk = jnp.where(mx < 0, 0, jnp.where(mx >= 31, -1, low))
                        return acc + fold(lax.population_count(e_scr[pl.ds(w0, wpb), :] & msk))

                    cnt = lanesum(lax.fori_loop(0, nblk, body, jnp.zeros((SUBLANES, tq), I32)))
                    return jnp.where(cnt >= need, cand, jb)

                thr_scr[1:2, :] = lax.fori_loop(0, nbits, idx_step, jnp.full((1, tq), (1 << nbits) - 1, I32))

    def tile(near):
        k0 = pl.multiple_of(kt * tk, tk)
        nchunk = tk // RC
        masked = dsa or near
        if masked:
            for c in range(nchunk):
                r0 = k0 + c * RC
                mask = allowed_mask(r0, RC) if near else None
                if dsa:
                    x = skey_scr[pl.ds(r0, RC), :]
                    kidx = r0 + lax.broadcasted_iota(I32, (RC, tq), 0)
                    sel = (x > thr_scr[0:1, :]) | ((x == thr_scr[0:1, :]) & (kidx <= thr_scr[1:2, :]))
                    mask = sel & mask if near else sel
                mb_scr[c * RC:(c + 1) * RC, :] = jnp.where(mask, 0.0, NEG_BIG)

        def chunk_bias(h, c):
            far = far_ref[0:1, h:h + 1]
            if not near:
                return far
            sb, off = (c * RC) // LANES, (c * RC) % LANES
            cols = []
            for qb in range(tq // LANES):
                delta = k0 + sb * LANES - (q_lo + qb * LANES)
                cols.append(jnp.where(delta == 0, tab_ref[0, h, off:off + RC, :],
                                      jnp.where(delta == -LANES, tab_ref[1, h, off:off + RC, :], far)))
            return jnp.concatenate(cols, axis=1) if len(cols) > 1 else cols[0]

        for h in range(nh):
            ks = slice(h * dk, (h + 1) * dk)
            s_scr[h] = jnp.dot(k_ref[0, :, ks], qT_ref[0, ks, :], preferred_element_type=F32)
        m_news, alphas = [], []
        for h in range(nh):
            mx = jnp.full((SUBLANES, tq), NEG_BIG, F32)
            for c in range(nchunk):
                rows = slice(c * RC, (c + 1) * RC)
                blk = s_scr[h, rows, :]
                if masked:
                    blk = blk + mb_scr[rows, :]
                    if dsa:
                        blk = blk + chunk_bias(h, c)
                    s_scr[h, rows, :] = blk
                mx = jnp.maximum(mx, jnp.max(blk.reshape(RC // SUBLANES, SUBLANES, tq), axis=0))
            m_prev = m_scr[h:h + 1, :]
            m_new = jnp.maximum(m_prev, jnp.max(mx, axis=0, keepdims=True))
            m_scr[h:h + 1, :] = m_new
            m_news.append(m_new)
            alphas.append(jnp.exp2(m_prev - m_new))
        for h in range(nh):
            for c in range(nchunk):
                rows = slice(c * RC, (c + 1) * RC)
                p_scr[h, rows, :] = jnp.exp2(s_scr[h, rows, :] - m_news[h]).astype(BF16)
        for h in range(nh):
            sl = slice(h * dh, (h + 1) * dh)
            pv = jnp.dot(vT_ref[0, h * VT_ROWS:(h + 1) * VT_ROWS, :], p_scr[h], preferred_element_type=F32)
            acc_scr[sl, :] = alphas[h] * acc_scr[sl, :] + pv[:dh]
            l_scr[h:h + 1, :] = alphas[h] * l_scr[h:h + 1, :] + pv[dh:dh + 1]

    is_far = kt * tk + tk - 1 <= q_lo - LANES

    @pl.when(is_far)
    def _():
        tile(False)

    @pl.when(jnp.logical_not(is_far))
    def _():
        tile(True)

    @pl.when(kt == last_kt)
    def _():
        for h in range(nh):
            sl = slice(h * dh, (h + 1) * dh)
            o = acc_scr[sl, :] / l_scr[h:h + 1, :]
            o_ref[0, :, sl] = o.T.astype(o_ref.dtype)


def _attention(mode, qT, k, vT, extra, *, p0, t_valid, l_valid, tk):
    b, hdk, tpad = qT.shape
    lpad = k.shape[1]
    nh = vT.shape[1] // VT_ROWS
    hd = nh * LANES
    dk = hdk // nh
    tq = 2 * LANES if tpad % (2 * LANES) == 0 else LANES
    nq, nk = tpad // tq, lpad // tk
    topk = min(TOPK_MAX, l_valid // 4)
    assert p0 % LANES == 0 and lpad % tk == 0 and tk % 256 == 0 and l_valid >= topk >= 1 and tk >= topk

    pairs = []
    for qt in range(nq):
        q_hi = p0 + min(qt * tq + tq, t_valid) - 1
        last = (min((q_hi // CHUNK + 1) * CHUNK, l_valid) - 1) // tk
        pairs += [(qt, kt) for kt in range(last + 1)]
    qt_tab = jnp.asarray([p[0] for p in pairs], I32)
    kt_tab = jnp.asarray([p[1] for p in pairs], I32)

    kmap = lambda i, s, qtt, ktt: (i, ktt[s], 0)
    vmap_ = lambda i, s, qtt, ktt: (i, 0, ktt[s])
    qmap = lambda i, s, qtt, ktt: (i, 0, qtt[s])
    cmap = lambda n: (lambda i, s, qtt, ktt: (0,) * n)
    in_specs = [pl.BlockSpec((1, hdk, tq), qmap),
                pl.BlockSpec((1, tk, hdk), kmap),
                pl.BlockSpec((1, nh * VT_ROWS, tk), vmap_)]
    scratch = [pltpu.VMEM((nh, tq), F32), pltpu.VMEM((nh, tq), F32), pltpu.VMEM((hd, tq), F32),
               pltpu.VMEM((nh, tk, tq), F32), pltpu.VMEM((nh, tk, tq), BF16), pltpu.VMEM((tk, tq), F32)]
    if mode == "dsa":
        q3, ki3, wiT, tabs, far = extra
        wrows = -(-nk // SEARCH_TILES) * SEARCH_TILES * (tk // 32)
        in_specs += [pl.BlockSpec((1, q3.shape[1], tq), qmap),
                     pl.BlockSpec((1, lpad, ki3.shape[2]), lambda i, s, qtt, ktt: (i, 0, 0)),
                     pl.BlockSpec((1, C_IDX_HEADS, tq), qmap),
                     pl.BlockSpec(tabs.shape, cmap(4)),
                     pl.BlockSpec(far.shape, cmap(2))]
        scratch += [pltpu.VMEM((lpad, tq), I32), pltpu.VMEM((SUBLANES, tq), I32),
                    pltpu.VMEM((33, wrows, tq), I32), pltpu.VMEM((wrows, tq), I32)]
    return pl.pallas_call(
        functools.partial(_attn_body, mode=mode, p0=p0, t_valid=t_valid, l_valid=l_valid, tq=tq, tk=tk,
                          dk=dk, topk=topk, nh=nh),
        grid_spec=pltpu.PrefetchScalarGridSpec(
            num_scalar_prefetch=2,
            grid=(b, len(pairs)),
            in_specs=in_specs,
            out_specs=pl.BlockSpec((1, tq, hd), lambda i, s, qtt, ktt: (i, qtt[s], 0)),
            scratch_shapes=scratch),
        out_shape=jax.ShapeDtypeStruct((b, tpad, hd), BF16),
        compiler_params=_cparams(("parallel", "arbitrary")),
        name="attn_" + mode,
    )(qt_tab, kt_tab, qT, k, vT, *extra)


PACK_TILE = 256


def _pack_body(cache_ref, new_ref, o_ref, *, n_cache, transpose, nh):
    j = pl.program_id(1)

    def emit(head):
        for h in range(nh):
            sl = slice(h * LANES, (h + 1) * LANES)
            x = head(h, sl)
            if transpose:
                r0 = h * VT_ROWS
                o_ref[0, r0:r0 + LANES, :] = x.T.astype(o_ref.dtype)
                o_ref[0, r0 + LANES:r0 + VT_ROWS, :] = jnp.ones((VT_ONES, x.shape[0]), o_ref.dtype)
            else:
                o_ref[0, :, sl] = x.astype(o_ref.dtype)

    @pl.when(j < n_cache)
    def _():
        emit(lambda h, sl: cache_ref[0, 0, :, h, :])

    @pl.when(j >= n_cache)
    def _():
        emit(lambda h, sl: new_ref[0, :, sl])


def _pack_keys(cache, layer, new, lpad, *, transpose):
    _, b, past, nh, dh = cache.shape
    tp = 2 * PACK_TILE if (past % (2 * PACK_TILE) == 0 and lpad % (2 * PACK_TILE) == 0) else PACK_TILE
    assert dh == LANES and past % tp == 0 and lpad % tp == 0
    n_cache = past // tp
    hd = nh * dh
    new = _pad_axis(new, 1, lpad - past)
    out_shape = (b, nh * VT_ROWS, lpad) if transpose else (b, lpad, hd)
    out_spec = (pl.BlockSpec((1, nh * VT_ROWS, tp), lambda i, j: (i, 0, j)) if transpose
                else pl.BlockSpec((1, tp, hd), lambda i, j: (i, j, 0)))
    return pl.pallas_call(
        functools.partial(_pack_body, n_cache=n_cache, transpose=transpose, nh=nh),
        grid=(b, lpad // tp),
        in_specs=[pl.BlockSpec((1, 1, tp, nh, dh),
                               lambda i, j: (layer, i, jnp.minimum(j, n_cache - 1), 0, 0)),
                  pl.BlockSpec((1, tp, hd), lambda i, j: (i, jnp.maximum(j - n_cache, 0), 0))],
        out_specs=out_spec,
        out_shape=jax.ShapeDtypeStruct(out_shape, BF16),
        compiler_params=_cparams(("parallel", "parallel")),
        name="pack_keys",
    )(cache, new)


def _split_hi_lo(x):
    hi = x.astype(BF16)
    lo = (x - hi.astype(F32)).astype(BF16)
    return hi, lo


def _pad_axis(x, axis, size):
    if x.shape[axis] == size:
        return x
    pad = [(0, 0)] * x.ndim
    pad[axis] = (0, size - x.shape[axis])
    return jnp.pad(x, pad)


def _vt_ones(v):
    b, l, hd = v.shape
    vt = jnp.swapaxes(v, 1, 2).reshape(b, hd // LANES, LANES, l)
    ones = jnp.ones((b, hd // LANES, VT_ONES, l), v.dtype)
    return jnp.concatenate([vt, ones], axis=2).reshape(b, -1, l)


def _key_tile(l_valid):
    tk = 512 if l_valid > 1024 else 256
    return -(-l_valid // tk) * tk, tk


def _prep_weights(W, d_model):
    f = {}
    wd = A_WIDTH
    n_even = W["ev_w_in"].shape[0]
    n_odd = W["od_w_in"].shape[0]
    f["even"] = []
    for e in range(n_even):
        wi = W["ev_w_in"][e]
        w_in = jnp.concatenate([wi[:, :3 * wd], wi[:, A_COLS:], wi[:, 3 * wd:A_COLS],
                                jnp.zeros((d_model, LORA_PAD - A_LORA), F32)], axis=1).astype(BF16)
        mu = W["rwkv_mu"][e]
        w2 = jnp.zeros((LANES, wd), F32).at[:A_DECAY_LORA].set(W["rwkv_w2"][e])
        a2 = jnp.zeros((LANES, wd), F32).at[A_DECAY_LORA:A_DECAY_LORA + A_ICLR_LORA].set(W["rwkv_a2"][e])
        g2 = jnp.zeros((LORA_PAD - LANES, wd), F32).at[:A_GATE_LORA].set(W["rwkv_g2"][e])
        hm = lambda v: v.reshape(A_HEADS, 1, A_HEAD_DIM)
        rw = dict(mu_r=mu[:3 * wd].reshape(1, -1),
                  mu_l=_pad_axis(mu[3 * wd:], 0, LORA_PAD).reshape(1, -1),
                  w0=W["rwkv_w0"][e].reshape(1, -1), w2=w2.astype(BF16),
                  a0=W["rwkv_a0"][e].reshape(1, -1), a2=a2.astype(BF16), g2=g2.astype(BF16),
                  k_k=W["rwkv_k_k"][e].reshape(1, -1), k_a=W["rwkv_k_a"][e].reshape(1, -1),
                  r_k=hm(W["rwkv_r_k"][e]), ln_w=hm(W["rwkv_ln_w"][e]), ln_b=hm(W["rwkv_ln_b"][e]))
        eye = jnp.eye(B_BLOCKS, dtype=F32)
        blockdiag = lambda w: (eye[:, None, :, None] * w[:, :, None, :]).reshape(B_WIDTH, B_WIDTH)
        wgate = jnp.concatenate([blockdiag(W["lru_wa"][e]), blockdiag(W["lru_wx"][e])], axis=1).astype(BF16)
        bgate = jnp.concatenate([W["lru_ba"][e], W["lru_bx"][e]])
        wo = W["ev_w_out"][e].astype(BF16)
        f["even"].append(dict(w_in=w_in, rw=rw, wgate=wgate, bgate=bgate, wo_a=wo[:wd], wo_b=wo[wd:],
                              cw=W["lru_conv_w"][e], cb=W["lru_conv_b"][e], lam=W["lru_lambda"][e]))
    f["odd"] = []
    cw_ = C_WIDTH
    qi_w = C_IDX_HEADS * C_IDX_DIM
    for o in range(n_odd):
        wi = W["od_w_in"][o]
        offs = [0]
        for s in (cw_, cw_, cw_, qi_w, C_IDX_DIM, C_IDX_HEADS, D_Q_RANK, D_KV_RANK, D_ROPE):
            offs.append(offs[-1] + s)
        q, k, v, qi, ki, wi_, qd, kvd, kr = [wi[:, offs[i]:offs[i + 1]] for i in range(9)]
        small = jnp.concatenate([ki, kr, wi_], axis=1)
        w_in = jnp.concatenate([q, k, v, qi, qd, kvd, _pad_axis(small, 1, 512)], axis=1).astype(BF16)
        wuq = W["mla_w_uq"][o].reshape(D_Q_RANK, D_HEADS, D_NOPE + D_ROPE)
        wuq = jnp.concatenate([wuq[:, :, :D_NOPE].reshape(D_Q_RANK, -1),
                               wuq[:, :, D_NOPE:].reshape(D_Q_RANK, -1)], axis=1).astype(BF16)
        wukv = W["mla_w_ukv"][o].reshape(D_KV_RANK, D_HEADS, D_NOPE + D_V)
        wuk = _pad_axis(wukv[:, :, :D_NOPE], 2, MLA_DK).reshape(D_KV_RANK, -1).astype(BF16)
        wuv = wukv[:, :, D_NOPE:].reshape(D_KV_RANK, -1).astype(BF16)
        wo = W["od_w_out"][o].astype(BF16)
        f["odd"].append(dict(w_in=w_in, wuq=wuq, wuk=wuk, wuv=wuv, wo_c=wo[:cw_], wo_d=wo[cw_:],
                             gq=W["mla_q_norm"][o], gkv=W["mla_kv_norm"][o]))
    f["ffn"] = dict(wg=W["ffn_w_gate"].astype(BF16), wu=W["ffn_w_up"].astype(BF16),
                    wd=W["ffn_w_down"].astype(BF16))
    f["tabs"] = _bias_tables(W["rel_bias"])
    nb = N_BUCKETS // 2
    f["far"] = _pad_axis(W["rel_bias"][nb - 1:nb, :] * LOG2E, 1, LANES)
    return f


def _mixer_even(x2, b, t, p0, shift, s0, h0, cbuf, fe, norm_g):
    wd = A_WIDTH
    p = _norm_matmul(x2, norm_g, fe["w_in"])
    p3 = p.reshape(b, t, -1)
    lora_col = 3 * wd + 2 * B_WIDTH
    shift_r = shift[:, None, :3 * wd]
    shift_l = _pad_axis(shift[:, None, 3 * wd:], 2, LORA_PAD)
    ya, s_new = _rwkv(p3, lora_col // LORA_PAD, shift_r, shift_l, s0, fe["rw"])
    yb, h_new, c_new = _rglru(p3, 3, 4, fe["cw"], fe["cb"], fe["wgate"], fe["bgate"], fe["lam"],
                              h0, cbuf, p0=p0)
    x2 = _matmul([ya.reshape(b * t, wd), yb.reshape(b * t, B_WIDTH)], [fe["wo_a"], fe["wo_b"]], res=x2)
    last = p3[:, t - 1]
    new_shift = jnp.concatenate([last[:, :3 * wd], last[:, lora_col:lora_col + A_LORA]], axis=-1)
    return x2, new_shift, s_new, h_new[:, 0], c_new


def _mixer_odd(x2, b, t, p0, o, dsa_k, dsa_v, cik, clat, ckr, fo, f, norm_g, cos, sin):
    cw_ = C_WIDTH
    nt = cw_ // 512
    segs = [(0, nt, BF16, C_HEAD_DIM ** -0.5 * LOG2E), (nt, 2 * nt, F32, 1.0), (nt, 2 * nt, BF16, 1.0),
            (2 * nt, 3 * nt, F32, 1.0), (2 * nt, 3 * nt, BF16, 1.0), (3 * nt, 3 * nt + 4, F32, 1.0)]
    q_bf, k_f, k_bf, v_f, v_bf, rest = [z.reshape(b, t, -1) for z in
                                        _norm_matmul(x2, norm_g, fo["w_in"], segs)]
    past = dsa_k.shape[2]
    l_valid = past + t
    lpad, tk = _key_tile(l_valid)
    tpad = -(-t // LANES) * LANES
    k_new, v_new = k_f, v_f
    qi = rest[..., :512]
    small = rest[..., 1536:]
    ki_new = small[..., :C_IDX_DIM]
    wi = small[..., 2 * C_IDX_DIM:2 * C_IDX_DIM + C_IDX_HEADS]

    def keys(cache, new):
        allk = jnp.concatenate([cache.reshape(b, past, -1), new], axis=1) if past else new
        return _pad_axis(allk, 1, lpad)

    tq_ = lambda z: _pad_axis(jnp.swapaxes(z, 1, 2), 2, tpad)
    if past and past % PACK_TILE == 0:
        k_all = _pack_keys(dsa_k, o, k_new, lpad, transpose=False)
        vT_all = _pack_keys(dsa_v, o, v_new, lpad, transpose=True)
    elif past:
        k_all = keys(dsa_k[o], k_new).astype(BF16)
        vT_all = _vt_ones(keys(dsa_v[o], v_new).astype(BF16))
    else:
        k_all = _pad_axis(k_bf, 1, lpad)
        vT_all = _vt_ones(_pad_axis(v_bf, 1, lpad))
    kih, kil = _split_hi_lo(keys(cik, ki_new))
    ki3 = jnp.concatenate([kih, kih, kil], axis=-1)
    qih, qil = _split_hi_lo(tq_(qi).reshape(b, C_IDX_HEADS, C_IDX_DIM, tpad))
    q3 = jnp.concatenate([qih, qil, qih], axis=2).reshape(b, 3 * C_IDX_HEADS * C_IDX_DIM, tpad)
    yc = _attention("dsa", tq_(q_bf), k_all, vT_all,
                    (q3, ki3, tq_(wi), f["tabs"], f["far"]),
                    p0=p0, t_valid=t, l_valid=l_valid, tk=tk)[:, :t]
    qn, qr, lat, krope = _mla_prep(rest, 1, 2, 3, fo["gq"], fo["gkv"], fo["wuq"], cos, sin)
    lat_all = keys(clat, lat).reshape(b * lpad, -1)
    kr_all = keys(ckr, krope).reshape(b * lpad, -1)
    eye = jnp.eye(D_ROPE, dtype=BF16)
    ident = jnp.tile(jnp.pad(eye, ((0, 0), (D_NOPE, MLA_DK - D_NOPE - D_ROPE))), (1, D_HEADS))
    k_full = _matmul([lat_all, kr_all], [fo["wuk"], ident], out_dtype=BF16).reshape(b, lpad, -1)
    v_all = _matmul([lat_all], [fo["wuv"]], out_dtype=BF16).reshape(b, lpad, -1)
    q_full = jnp.concatenate([qn.reshape(b, t, D_HEADS, D_NOPE), qr.reshape(b, t, D_HEADS, D_ROPE),
                              jnp.zeros((b, t, D_HEADS, MLA_DK - D_NOPE - D_ROPE), BF16)], axis=-1)
    yd = _attention("mla", tq_(q_full.reshape(b, t, -1)), k_full, _vt_ones(v_all), (),
                    p0=p0, t_valid=t, l_valid=l_valid, tk=tk)[:, :t]
    x2 = _matmul([yc.reshape(b * t, cw_), yd.reshape(b * t, -1)], [fo["wo_c"], fo["wo_d"]], res=x2)
    return (x2, k_new.reshape(b, t, C_HEADS, C_HEAD_DIM), v_new.reshape(b, t, C_HEADS, C_HEAD_DIM),
            ki_new, lat, krope)


def _trunk(x, p0, shift, rwkv_s, lru_h, lru_conv, dsa_k, dsa_v, dsa_ik, mla_lat, mla_kr, W, f):
    b, t, d = x.shape
    depth = W["norm_mix"].shape[0]
    pos = (p0 + jnp.arange(t)).astype(F32)
    inv = ROPE_BASE ** (-jnp.arange(0, D_ROPE, 2, dtype=F32) / D_ROPE)
    ang = pos[:, None] * inv[None, :]
    cos = jnp.tile(jnp.cos(ang), (1, 2 * D_HEADS))
    sin = jnp.tile(jnp.sin(ang), (1, 2 * D_HEADS))
    x2 = x.reshape(b * t, d)
    ev = [[] for _ in range(4)]
    od = [[] for _ in range(5)]
    for layer in range(depth):
        if layer % 2 == 0:
            e = layer // 2
            x2, *outs = _mixer_even(x2, b, t, p0, shift[e], rwkv_s[e], lru_h[e], lru_conv[e],
                                    f["even"][e], W["norm_mix"][layer])
            for lst, o_ in zip(ev, outs):
                lst.append(o_)
        else:
            o = layer // 2
            x2, *outs = _mixer_odd(x2, b, t, p0, o, dsa_k, dsa_v, dsa_ik[o], mla_lat[o], mla_kr[o],
                                   f["odd"][o], f, W["norm_mix"][layer], cos, sin)
            for lst, o_ in zip(od, outs):
                lst.append(o_)
        ff = f["ffn"]
        x2 = _ffn(x2, W["norm_ffn"][layer], ff["wg"], ff["wu"], ff["wd"], layer, W["final_norm"],
                  final_norm=layer == depth - 1)
    return (x2.reshape(b, t, d),) + tuple(jnp.stack(v) for v in ev) + tuple(jnp.stack(v) for v in od)


def kernel(x_prompt, x_sample, state_rwkv_shift, state_rwkv, state_lru, state_lru_conv, cache_dsa_k, cache_dsa_v, cache_dsa_idx_k, cache_mla_latent, cache_mla_krope, rel_bias, final_norm, norm_mix, norm_ffn, ffn_w_gate, ffn_w_up, ffn_w_down, ev_w_in, ev_w_out, rwkv_mu, rwkv_w0, rwkv_w2, rwkv_a0, rwkv_a2, rwkv_g2, rwkv_k_k, rwkv_k_a, rwkv_r_k, rwkv_ln_w, rwkv_ln_b, lru_conv_w, lru_conv_b, lru_wa, lru_ba, lru_wx, lru_bx, lru_lambda, od_w_in, od_w_out, mla_q_norm, mla_w_uq, mla_kv_norm, mla_w_ukv):
    W = dict(rel_bias=rel_bias, final_norm=final_norm, norm_mix=norm_mix, norm_ffn=norm_ffn,
             ffn_w_gate=ffn_w_gate, ffn_w_up=ffn_w_up, ffn_w_down=ffn_w_down,
             ev_w_in=ev_w_in, ev_w_out=ev_w_out, rwkv_mu=rwkv_mu, rwkv_w0=rwkv_w0,
             rwkv_w2=rwkv_w2, rwkv_a0=rwkv_a0, rwkv_a2=rwkv_a2, rwkv_g2=rwkv_g2,
             rwkv_k_k=rwkv_k_k, rwkv_k_a=rwkv_k_a, rwkv_r_k=rwkv_r_k, rwkv_ln_w=rwkv_ln_w,
             rwkv_ln_b=rwkv_ln_b, lru_conv_w=lru_conv_w, lru_conv_b=lru_conv_b, lru_wa=lru_wa,
             lru_ba=lru_ba, lru_wx=lru_wx, lru_bx=lru_bx, lru_lambda=lru_lambda,
             od_w_in=od_w_in, od_w_out=od_w_out, mla_q_norm=mla_q_norm, mla_w_uq=mla_w_uq,
             mla_kv_norm=mla_kv_norm, mla_w_ukv=mla_w_ukv)
    d_model = x_prompt.shape[-1]
    f = _prep_weights(W, d_model)
    bp = x_prompt.shape[0]
    n_even, n_odd = ev_w_in.shape[0], od_w_in.shape[0]
    dt = x_prompt.dtype
    z = lambda *shape: jnp.zeros(shape, dt)
    outs_p = _trunk(
        x_prompt, 0,
        z(n_even, bp, A_COLS), z(n_even, bp, A_HEADS, A_HEAD_DIM, A_HEAD_DIM),
        z(n_even, bp, B_WIDTH), z(n_even, bp, B_CONV - 1, B_WIDTH),
        z(n_odd, bp, 0, C_HEADS, C_HEAD_DIM), z(n_odd, bp, 0, C_HEADS, C_HEAD_DIM),
        z(n_odd, bp, 0, C_IDX_DIM), z(n_odd, bp, 0, D_KV_RANK), z(n_odd, bp, 0, D_ROPE), W, f)
    past = cache_dsa_k.shape[2]
    outs_s = _trunk(x_sample, past, state_rwkv_shift, state_rwkv, state_lru, state_lru_conv,
                    cache_dsa_k, cache_dsa_v, cache_dsa_idx_k, cache_mla_latent, cache_mla_krope, W, f)
    return (outs_p[0], outs_s[0]) + tuple(outs_p[1:]) + tuple(outs_s[1:])
```

```python
import functools
import math

import jax
import jax.numpy as jnp
from jax import lax
from jax.experimental import pallas as pl
from jax.experimental.pallas import tpu as pltpu

F32 = jnp.float32
BF16 = jnp.bfloat16
I32 = jnp.int32

CHUNK = 64
NORM_EPS = 1e-6
A_HEADS = 16
A_HEAD_DIM = 64
A_WIDTH = A_HEADS * A_HEAD_DIM
A_DECAY_LORA = 64
A_ICLR_LORA = 64
A_GATE_LORA = 160
A_LORA = A_DECAY_LORA + A_ICLR_LORA + A_GATE_LORA
A_COLS = 3 * A_WIDTH + A_LORA
A_LN_EPS = 64e-5
B_WIDTH = 1024
B_BLOCKS = 16
B_CONV = 4
B_C = 8.0
C_HEADS = 8
C_HEAD_DIM = 128
C_WIDTH = C_HEADS * C_HEAD_DIM
C_IDX_HEADS = 8
C_IDX_DIM = 64
TOPK_MAX = 256
D_HEADS = 8
D_NOPE = 128
D_ROPE = 64
D_V = 128
D_Q_RANK = 512
D_KV_RANK = 512
ROPE_BASE = 10000.0
N_BUCKETS = 32
MAX_DISTANCE = 128

LANES = 128
SUBLANES = 8
VMEM_LIMIT = 56 * 1024 * 1024
LORA_PAD = 512
NEG_BIG = -1e30
INT_MIN = -2147483648
RC = 64
SEARCH_TILES = 4
VT_ONES = 16
VT_ROWS = LANES + VT_ONES
LOG2E = 1.4426950408889634
MLA_DK = 256

NN = (((1,), (0,)), ((), ()))
NT = (((1,), (1,)), ((), ()))
BNN = (((2,), (1,)), ((0,), (0,)))
BNT = (((2,), (2,)), ((0,), (0,)))
BTN = (((1,), (1,)), ((0,), (0,)))


def _cparams(sem):
    return pltpu.CompilerParams(dimension_semantics=sem, vmem_limit_bytes=VMEM_LIMIT)


def _rms(x, g, eps=NORM_EPS):
    ms = jnp.mean(x * x, axis=-1, keepdims=True)
    return x * lax.rsqrt(ms + eps) * g


def _softplus(x):
    return jnp.maximum(x, 0.0) + jnp.log1p(jnp.exp(-jnp.abs(x)))


def _dotp(a, b, dims, passes):
    if passes == 6:
        return lax.dot_general(a, b, dims, precision=lax.Precision.HIGHEST, preferred_element_type=F32)
    ah = a.astype(BF16)
    bh = b.astype(BF16)
    out = lax.dot_general(ah, bh, dims, preferred_element_type=F32)
    if passes == 3:
        al = (a - ah.astype(F32)).astype(BF16)
        bl = (b - bh.astype(F32)).astype(BF16)
        out = out + lax.dot_general(ah, bl, dims, preferred_element_type=F32)
        out = out + lax.dot_general(al, bh, dims, preferred_element_type=F32)
    return out


def _norm_matmul_body(x_ref, g_ref, w_ref, *refs, segs):
    outs, xn_ref = refs[:-1], refs[-1]
    j = pl.program_id(1)

    @pl.when(j == 0)
    def _():
        xn_ref[...] = _rms(x_ref[...], g_ref[...]).astype(BF16)

    acc = jnp.dot(xn_ref[...], w_ref[...], preferred_element_type=F32)
    for o_ref, (lo, hi, _, scale) in zip(outs, segs):
        @pl.when((j >= lo) & (j < hi))
        def _(o_ref=o_ref, scale=scale):
            o_ref[...] = (acc if scale == 1.0 else acc * scale).astype(o_ref.dtype)


def _row_tile(m, cap):
    tm = cap
    while m % tm:
        tm //= 2
    return tm


def _norm_matmul(x, g, w, segs=None, *, tn=512):
    m, k = x.shape
    n = w.shape[1]
    tm = _row_tile(m, 1024)
    segs = segs or [(0, n // tn, F32, 1.0)]
    out_specs = [pl.BlockSpec((tm, tn), lambda i, j, lo=lo, nb=hi - lo: (i, jnp.clip(j - lo, 0, nb - 1)))
                 for lo, hi, _, _ in segs]
    outs = pl.pallas_call(
        functools.partial(_norm_matmul_body, segs=tuple(segs)),
        grid=(m // tm, n // tn),
        in_specs=[pl.BlockSpec((tm, k), lambda i, j: (i, 0)),
                  pl.BlockSpec((1, k), lambda i, j: (0, 0)),
                  pl.BlockSpec((k, tn), lambda i, j: (0, j))],
        out_specs=out_specs,
        out_shape=[jax.ShapeDtypeStruct((m, (hi - lo) * tn), dt) for lo, hi, dt, _ in segs],
        scratch_shapes=[pltpu.VMEM((tm, k), BF16)],
        compiler_params=_cparams(("parallel", "arbitrary")),
        name="norm_matmul",
    )(x, g.reshape(1, k), w)
    return outs if len(segs) > 1 else outs[0]


def _mm_body(*refs, n_lhs, has_res):
    o_ref = refs[-1]
    acc = refs[2 * n_lhs][...] if has_res else None
    for a_ref, w_ref in zip(refs[:n_lhs], refs[n_lhs:2 * n_lhs]):
        d = jnp.dot(a_ref[...].astype(BF16), w_ref[...], preferred_element_type=F32)
        acc = d if acc is None else acc + d
    o_ref[...] = acc.astype(o_ref.dtype)


def _matmul(lhs_list, w_list, res=None, *, tn=512, out_dtype=F32):
    m = lhs_list[0].shape[0]
    n = w_list[0].shape[1]
    tm = _row_tile(m, 1024)
    in_specs = [pl.BlockSpec((tm, a.shape[1]), lambda i, j: (i, 0)) for a in lhs_list]
    in_specs += [pl.BlockSpec((w.shape[0], tn), lambda i, j: (0, j)) for w in w_list]
    args = list(lhs_list) + list(w_list)
    if res is not None:
        in_specs.append(pl.BlockSpec((tm, tn), lambda i, j: (i, j)))
        args.append(res)
    return pl.pallas_call(
        functools.partial(_mm_body, n_lhs=len(lhs_list), has_res=res is not None),
        grid=(m // tm, n // tn),
        in_specs=in_specs,
        out_specs=pl.BlockSpec((tm, tn), lambda i, j: (i, j)),
        out_shape=jax.ShapeDtypeStruct((m, n), out_dtype),
        compiler_params=_cparams(("parallel", "arbitrary")),
        name="matmul",
    )(*args)


def _ffn_body(x_ref, g_ref, wg_ref, wu_ref, wd_ref, gf_ref, o_ref, xn_ref, acc_ref, *, final_norm):
    f = pl.program_id(1)

    @pl.when(f == 0)
    def _():
        xn_ref[...] = _rms(x_ref[...], g_ref[...]).astype(BF16)
        acc_ref[...] = jnp.zeros_like(acc_ref)

    xn = xn_ref[...]
    hg = jnp.dot(xn, wg_ref[0], preferred_element_type=F32)
    hu = jnp.dot(xn, wu_ref[0], preferred_element_type=F32)
    h = hg * jax.nn.sigmoid(hg) * hu
    acc_ref[...] += jnp.dot(h.astype(BF16), wd_ref[0], preferred_element_type=F32)

    @pl.when(f == pl.num_programs(1) - 1)
    def _():
        y = x_ref[...] + acc_ref[...]
        if final_norm:
            y = _rms(y, gf_ref[...])
        o_ref[...] = y


def _ffn(x, g, wg, wu, wd, layer, gf, *, final_norm, tf=512):
    m, k = x.shape
    dff = wg.shape[2]
    tm = _row_tile(m, 512)
    return pl.pallas_call(
        functools.partial(_ffn_body, final_norm=final_norm),
        grid=(m // tm, dff // tf),
        in_specs=[pl.BlockSpec((tm, k), lambda i, f: (i, 0)),
                  pl.BlockSpec((1, k), lambda i, f: (0, 0)),
                  pl.BlockSpec((1, k, tf), lambda i, f: (layer, 0, f)),
                  pl.BlockSpec((1, k, tf), lambda i, f: (layer, 0, f)),
                  pl.BlockSpec((1, tf, k), lambda i, f: (layer, f, 0)),
                  pl.BlockSpec((1, k), lambda i, f: (0, 0))],
        out_specs=pl.BlockSpec((tm, k), lambda i, f: (i, 0)),
        out_shape=jax.ShapeDtypeStruct((m, k), F32),
        scratch_shapes=[pltpu.VMEM((tm, k), BF16), pltpu.VMEM((tm, k), F32)],
        compiler_params=_cparams(("parallel", "arbitrary")),
        name="ffn",
    )(x, g.reshape(1, k), wg, wu, wd, gf.reshape(1, k))


def _lru_body(gate_ref, xb_ref, cw_ref, cb_ref, wg_ref, bg_ref, lam_ref, h0_ref, cbuf_ref,
              y_ref, hout_ref, cout_ref, xbuf, hcar, *, tt, p0):
    t = pl.program_id(1)
    w = B_WIDTH
    halo = SUBLANES

    @pl.when(t == 0)
    def _():
        xbuf[0:halo, :] = jnp.zeros((halo, w), F32)
        xbuf[halo - (B_CONV - 1):halo, :] = cbuf_ref[0]
        hcar[...] = h0_ref[0]

    xb = xb_ref[0]
    xbuf[halo:halo + tt, :] = xb
    xc = cb_ref[...] + cw_ref[B_CONV - 1:B_CONV, :] * xb
    for j in range(B_CONV - 1):
        off = halo - (B_CONV - 1) + j
        xc = xc + cw_ref[j:j + 1, :] * xbuf[off:off + tt, :]
    tail = xbuf[tt + halo - (B_CONV - 1):tt + halo, :]
    xbuf[halo - (B_CONV - 1):halo, :] = tail

    pre = jnp.dot(xc.astype(BF16), wg_ref[...], preferred_element_type=F32) + bg_ref[...]
    rg = jax.nn.sigmoid(pre[:, :w])
    ig = jax.nn.sigmoid(pre[:, w:])
    log_a = (-B_C) * rg * _softplus(-lam_ref[...])
    a = jnp.exp(log_a)
    row = lax.broadcasted_iota(I32, (tt, w), 0)
    th = jnp.tanh(log_a)
    mult = jnp.sqrt(-2.0 * th / (1.0 - th))
    mult = jnp.where(row + (p0 + t * tt) == 0, 1.0, mult)
    u = mult * (ig * xc)

    d = 1
    while d < tt:
        keep = row >= d
        a_sh = pltpu.roll(a, d, 0)
        u_sh = pltpu.roll(u, d, 0)
        u = u + jnp.where(keep, a * u_sh, 0.0)
        a = jnp.where(keep, a * a_sh, a)
        d *= 2
    h = u + a * hcar[...]
    hcar[...] = h[tt - 1:tt, :]
    y_ref[0] = (h * jax.nn.gelu(gate_ref[0])).astype(y_ref.dtype)

    @pl.when(t == pl.num_programs(1) - 1)
    def _():
        hout_ref[0] = h[tt - 1:tt, :]
        cout_ref[0] = tail


def _rglru(p3, gate_blk, xb_blk, cw, cb, wgate, bgate, lam, h0, cbuf, *, p0):
    b, t, _ = p3.shape
    w = B_WIDTH
    tt = min(256, t)
    row = lambda v: v.reshape(1, -1)
    return pl.pallas_call(
        functools.partial(_lru_body, tt=tt, p0=p0),
        grid=(b, t // tt),
        in_specs=[pl.BlockSpec((1, tt, w), lambda i, j: (i, j, gate_blk)),
                  pl.BlockSpec((1, tt, w), lambda i, j: (i, j, xb_blk)),
                  pl.BlockSpec((B_CONV, w), lambda i, j: (0, 0)),
                  pl.BlockSpec((1, w), lambda i, j: (0, 0)),
                  pl.BlockSpec((w, 2 * w), lambda i, j: (0, 0)),
                  pl.BlockSpec((1, 2 * w), lambda i, j: (0, 0)),
                  pl.BlockSpec((1, w), lambda i, j: (0, 0)),
                  pl.BlockSpec((1, 1, w), lambda i, j: (i, 0, 0)),
                  pl.BlockSpec((1, B_CONV - 1, w), lambda i, j: (i, 0, 0))],
        out_specs=[pl.BlockSpec((1, tt, w), lambda i, j: (i, j, 0)),
                   pl.BlockSpec((1, 1, w), lambda i, j: (i, 0, 0)),
                   pl.BlockSpec((1, B_CONV - 1, w), lambda i, j: (i, 0, 0))],
        out_shape=[jax.ShapeDtypeStruct((b, t, w), BF16),
                   jax.ShapeDtypeStruct((b, 1, w), F32),
                   jax.ShapeDtypeStruct((b, B_CONV - 1, w), F32)],
        scratch_shapes=[pltpu.VMEM((tt + SUBLANES, w), F32), pltpu.VMEM((1, w), F32)],
        compiler_params=_cparams(("parallel", "arbitrary")),
        name="rglru",
    )(p3, p3, cw, row(cb), wgate, row(bgate), row(lam), h0.reshape(b, 1, w), cbuf)


RWKV_PASSES = 1
RWKV_PASSES_SOLVE = 3


def _rwkv_body(rkv_ref, lora_ref, shr_ref, shl_ref, s0_ref, mur_ref, mul_ref, w0_ref, w2_ref, a0_ref,
               a2_ref, g2_ref, kk_ref, ka_ref, rk_ref, lnw_ref, lnb_ref, tri_ref,
               y_ref, sout_ref, buf_r, buf_l, s_scr, st_r, st_k, st_v, st_q, st_a, st_l, st_d, *, c):
    ci = pl.program_id(1)
    halo = SUBLANES
    hn, n, wd = A_HEADS, A_HEAD_DIM, A_WIDTH
    mm = functools.partial(_dotp, passes=RWKV_PASSES)

    @pl.when(ci == 0)
    def _():
        buf_r[halo - 1:halo, :] = shr_ref[0]
        buf_l[halo - 1:halo, :] = shl_ref[0]
        s_scr[...] = s0_ref[0]

    cur_r = rkv_ref[0]
    cur_l = lora_ref[0]
    buf_r[halo:halo + c, :] = cur_r
    buf_l[halo:halo + c, :] = cur_l
    xm = cur_r + mur_ref[...] * (buf_r[halo - 1:halo - 1 + c, :] - cur_r)
    lo = cur_l + mul_ref[...] * (buf_l[halo - 1:halo - 1 + c, :] - cur_l)
    buf_r[halo - 1:halo, :] = cur_r[c - 1:c, :]
    buf_l[halo - 1:halo, :] = cur_l[c - 1:c, :]

    r = xm[:, :wd]
    k = xm[:, wd:2 * wd]
    v = xm[:, 2 * wd:]
    lo_a = lo[:, :LANES]
    w_pre = w0_ref[...] + jnp.dot(jnp.tanh(lo_a).astype(BF16), w2_ref[...], preferred_element_type=F32)
    w_log = -_softplus(-w_pre) - 0.5
    ld = -jnp.exp(w_log)
    a = jax.nn.sigmoid(a0_ref[...] + jnp.dot(lo_a.astype(BF16), a2_ref[...], preferred_element_type=F32))
    g = jnp.dot(jax.nn.sigmoid(lo[:, LANES:]).astype(BF16), g2_ref[...], preferred_element_type=F32)
    kq = k * kk_ref[...]
    k2 = k * (1.0 + (a - 1.0) * ka_ref[...])
    lc = lax.dot_general(tri_ref[...], ld, NN, precision=lax.Precision.HIGHEST,
                         preferred_element_type=F32)

    for h in range(hn):
        sl = slice(h * n, (h + 1) * n)
        st_r[h] = r[:, sl]
        st_k[h] = k2[:, sl]
        st_v[h] = v[:, sl]
        st_q[h] = kq[:, sl]
        st_a[h] = a[:, sl]
        st_l[h] = lc[:, sl]
        st_d[h] = ld[:, sl]

    rh_, k2h, vh, kqh, ah, lch, ldh = (st_r[...], st_k[...], st_v[...], st_q[...], st_a[...],
                                       st_l[...], st_d[...])
    nrm = jnp.sqrt(jnp.sum(kqh * kqh, axis=-1, keepdims=True))
    kk = kqh / jnp.maximum(nrm, 1e-12)
    kka = kk * ah
    e_neg = jnp.exp(-lch)
    am = jnp.exp(lch - ldh) * kk
    bm = kka * e_neg
    kh = k2h * e_neg
    rh = rh_ * jnp.exp(lch)
    l_end = lch[:, c - 1:c, :]
    e_c = jnp.exp(l_end - lch)
    bp = kka * e_c
    kp = k2h * e_c
    w_end = jnp.exp(l_end)

    x2 = jnp.concatenate([am, rh], axis=1)
    zb = _dotp(x2, bm, BNT, RWKV_PASSES_SOLVE)
    zk = mm(x2, kh, BNT)
    ti = lax.broadcasted_iota(I32, (hn, c, c), 1)
    si = lax.broadcasted_iota(I32, (hn, c, c), 2)
    strict = si < ti
    incl = si <= ti
    m1 = jnp.where(strict, zb[:, :c], 0.0)
    m4 = jnp.where(incl, zb[:, c:], 0.0)
    m2 = jnp.where(strict, zk[:, :c], 0.0)
    m3 = jnp.where(incl, zk[:, c:], 0.0)

    tm = jnp.where(si == ti, 1.0, 0.0) - m1
    npow = mm(m1, m1, BNN)
    span = 2
    while span < c:
        tm = tm + mm(tm, npow, BNN)
        span *= 2
        if span < c:
            npow = mm(npow, npow, BNN)

    s0 = s_scr[...]
    rhs = mm(am, s0, BNT) + mm(m2, vh, BNN)
    p = mm(tm, rhs, BNN)
    y = mm(rh, s0, BNT) + mm(m3, vh, BNN) - mm(m4, p, BNN)
    s_new = s0 * w_end + mm(vh, kp, BTN) - mm(p, bp, BTN)
    s_scr[...] = s_new

    mean = jnp.mean(y, axis=-1, keepdims=True)
    yc = y - mean
    var = jnp.mean(yc * yc, axis=-1, keepdims=True)
    yn = yc * lax.rsqrt(var + A_LN_EPS) * lnw_ref[...] + lnb_ref[...]
    bonus = jnp.sum(rh_ * k2h * rk_ref[...], axis=-1, keepdims=True) * vh
    yo = yn + bonus
    yo = jnp.concatenate([yo[h] for h in range(hn)], axis=-1)
    y_ref[0] = (yo * g).astype(y_ref.dtype)

    @pl.when(ci == pl.num_programs(1) - 1)
    def _():
        sout_ref[0] = s_new


def _rwkv(p3, lora_blk, shift_r, shift_l, s0, wts):
    b, t, _ = p3.shape
    c = min(CHUNK, t)
    hn, n, wd = A_HEADS, A_HEAD_DIM, A_WIDTH
    full = lambda shape: pl.BlockSpec(shape, lambda i, j: (0,) * len(shape))
    tri = (jnp.arange(c)[:, None] >= jnp.arange(c)[None, :]).astype(F32)
    st = pltpu.VMEM((hn, c, n), F32)
    return pl.pallas_call(
        functools.partial(_rwkv_body, c=c),
        grid=(b, t // c),
        in_specs=[pl.BlockSpec((1, c, 3 * wd), lambda i, j: (i, j, 0)),
                  pl.BlockSpec((1, c, LORA_PAD), lambda i, j: (i, j, lora_blk)),
                  pl.BlockSpec((1, 1, 3 * wd), lambda i, j: (i, 0, 0)),
                  pl.BlockSpec((1, 1, LORA_PAD), lambda i, j: (i, 0, 0)),
                  pl.BlockSpec((1, hn, n, n), lambda i, j: (i, 0, 0, 0)),
                  full((1, 3 * wd)), full((1, LORA_PAD)), full((1, wd)), full((LANES, wd)),
                  full((1, wd)), full((LANES, wd)), full((LORA_PAD - LANES, wd)),
                  full((1, wd)), full((1, wd)), full((hn, 1, n)), full((hn, 1, n)), full((hn, 1, n)),
                  full((c, c))],
        out_specs=[pl.BlockSpec((1, c, wd), lambda i, j: (i, j, 0)),
                   pl.BlockSpec((1, hn, n, n), lambda i, j: (i, 0, 0, 0))],
        out_shape=[jax.ShapeDtypeStruct((b, t, wd), BF16),
                   jax.ShapeDtypeStruct((b, hn, n, n), F32)],
        scratch_shapes=[pltpu.VMEM((c + SUBLANES, 3 * wd), F32), pltpu.VMEM((c + SUBLANES, LORA_PAD), F32),
                        pltpu.VMEM((hn, n, n), F32), st, st, st, st, st, st, st],
        compiler_params=_cparams(("parallel", "arbitrary")),
        name="rwkv7",
    )(p3, p3, shift_r, shift_l, s0, wts["mu_r"], wts["mu_l"], wts["w0"], wts["w2"], wts["a0"], wts["a2"],
      wts["g2"], wts["k_k"], wts["k_a"], wts["r_k"], wts["ln_w"], wts["ln_b"], tri)


def _rope(x, cos, sin):
    wdt = x.shape[-1]
    lane = lax.broadcasted_iota(I32, x.shape, 1)
    first = (lane % D_ROPE) < (D_ROPE // 2)
    rot = jnp.where(first, -pltpu.roll(x, wdt - D_ROPE // 2, 1), pltpu.roll(x, D_ROPE // 2, 1))
    return x * cos + rot * sin


def _mla_prep_body(qd_ref, kvd_ref, sm_ref, gq_ref, gkv_ref, wuq_ref, cos_ref, sin_ref,
                   qn_ref, qr_ref, lat_ref, kr_ref):
    nope_w = D_HEADS * D_NOPE
    qdn = _rms(qd_ref[0], gq_ref[...]).astype(BF16)
    qf = jnp.dot(qdn, wuq_ref[...], preferred_element_type=F32) * ((D_NOPE + D_ROPE) ** -0.5 * LOG2E)
    qn_ref[0] = qf[:, :nope_w].astype(qn_ref.dtype)
    cos = cos_ref[...]
    sin = sin_ref[...]
    qr_ref[0] = _rope(qf[:, nope_w:], cos, sin).astype(qr_ref.dtype)
    lat_ref[0] = _rms(kvd_ref[0], gkv_ref[...])
    sm = sm_ref[0][:, :LANES]
    kr = _rope(sm, cos[:, :LANES], sin[:, :LANES])
    kr_ref[0] = kr[:, C_IDX_DIM:C_IDX_DIM + D_ROPE]


def _mla_prep(p3, qd_blk, kvd_blk, sm_blk, gq, gkv, wuq, cos, sin):
    b, t, _ = p3.shape
    tt = min(256, t)
    rw = D_HEADS * D_ROPE
    full = lambda shape: pl.BlockSpec(shape, lambda i, j: (0,) * len(shape))
    return pl.pallas_call(
        _mla_prep_body,
        grid=(b, t // tt),
        in_specs=[pl.BlockSpec((1, tt, D_Q_RANK), lambda i, j: (i, j, qd_blk)),
                  pl.BlockSpec((1, tt, D_KV_RANK), lambda i, j: (i, j, kvd_blk)),
                  pl.BlockSpec((1, tt, 512), lambda i, j: (i, j, sm_blk)),
                  full((1, D_Q_RANK)), full((1, D_KV_RANK)), full(wuq.shape),
                  pl.BlockSpec((tt, rw), lambda i, j: (j, 0)),
                  pl.BlockSpec((tt, rw), lambda i, j: (j, 0))],
        out_specs=[pl.BlockSpec((1, tt, D_HEADS * D_NOPE), lambda i, j: (i, j, 0)),
                   pl.BlockSpec((1, tt, rw), lambda i, j: (i, j, 0)),
                   pl.BlockSpec((1, tt, D_KV_RANK), lambda i, j: (i, j, 0)),
                   pl.BlockSpec((1, tt, D_ROPE), lambda i, j: (i, j, 0))],
        out_shape=[jax.ShapeDtypeStruct((b, t, D_HEADS * D_NOPE), BF16),
                   jax.ShapeDtypeStruct((b, t, rw), BF16),
                   jax.ShapeDtypeStruct((b, t, D_KV_RANK), F32),
                   jax.ShapeDtypeStruct((b, t, D_ROPE), F32)],
        compiler_params=_cparams(("parallel", "parallel")),
        name="mla_prep",
    )(p3, p3, p3, gq.reshape(1, -1), gkv.reshape(1, -1), wuq, cos, sin)


def _bias_table_body(rb_ref, o_ref):
    d = pl.program_id(0)
    h = pl.program_id(1)
    nb = N_BUCKETS // 2
    max_exact = nb // 2
    s = lax.broadcasted_iota(I32, (LANES, LANES), 0)
    q = lax.broadcasted_iota(I32, (LANES, LANES), 1)
    rel = s - q - d * LANES
    n = jnp.abs(rel)
    big = jnp.maximum(n, max_exact).astype(F32)
    large = max_exact + (jnp.log(big / max_exact) / math.log(MAX_DISTANCE / max_exact)
                         * (nb - max_exact)).astype(I32)
    large = jnp.minimum(large, nb - 1)
    bucket = jnp.where(rel > 0, nb, 0) + jnp.where(n < max_exact, n, large)
    out = jnp.zeros((LANES, LANES), F32)
    for bk in range(N_BUCKETS):
        out = jnp.where(bucket == bk, rb_ref[bk, h], out)
    o_ref[0, 0] = out * LOG2E


def _bias_tables(rel_bias):
    return pl.pallas_call(
        _bias_table_body,
        grid=(2, C_HEADS),
        in_specs=[pl.BlockSpec(memory_space=pltpu.SMEM)],
        out_specs=pl.BlockSpec((1, 1, LANES, LANES), lambda d, h: (d, h, 0, 0)),
        out_shape=jax.ShapeDtypeStruct((2, C_HEADS, LANES, LANES), F32),
        name="bias_tables",
    )(rel_bias)


def _transpose32(x):
    x = list(x)
    for s, msk in ((16, 0x0000FFFF), (8, 0x00FF00FF), (4, 0x0F0F0F0F), (2, 0x33333333), (1, 0x55555555)):
        sh = jnp.full(x[0].shape, s, I32)
        for i in range(32):
            if i & s == 0:
                t = (lax.shift_right_logical(x[i], sh) ^ x[i + s]) & msk
                x[i + s] = x[i + s] ^ t
                x[i] = x[i] ^ lax.shift_left(t, sh)
    return x


def _attn_body(qt_tab, kt_tab, *refs, mode, p0, t_valid, l_valid, tq, tk, dk, topk, nh):
    dsa = mode == "dsa"
    dh = LANES
    if dsa:
        (qT_ref, k_ref, vT_ref, q3_ref, ki3_ref, wiT_ref, tab_ref, far_ref,
         o_ref, m_scr, l_scr, acc_scr, s_scr, p_scr, mb_scr, skey_scr, thr_scr, planes_scr, e_scr) = refs
    else:
        (qT_ref, k_ref, vT_ref, o_ref, m_scr, l_scr, acc_scr, s_scr, p_scr, mb_scr) = refs
    qt = qt_tab[pl.program_id(1)]
    kt = kt_tab[pl.program_id(1)]
    q_lo = p0 + qt * tq
    q_hi = p0 + jnp.minimum(qt * tq + tq, t_valid) - 1
    n_allowed = jnp.minimum((q_hi // CHUNK + 1) * CHUNK, l_valid)
    last_kt = (n_allowed - 1) // tk
    qpos = q_lo + lax.broadcasted_iota(I32, (1, tq), 1)
    qchunk = qpos // CHUNK

    def allowed_mask(k0, rows):
        kidx = k0 + lax.broadcasted_iota(I32, (rows, tq), 0)
        return (kidx // CHUNK <= qchunk) & (kidx < l_valid)

    @pl.when(kt == 0)
    def _():
        m_scr[...] = jnp.full(m_scr.shape, NEG_BIG, F32)
        l_scr[...] = jnp.zeros(l_scr.shape, F32)
        acc_scr[...] = jnp.zeros(acc_scr.shape, F32)

    if dsa:
        iscale = (C_IDX_HEADS * C_IDX_DIM) ** -0.5

        @pl.when(kt == 0)
        def _():
            wpt = tk // 32
            kw = 3 * C_IDX_DIM

            def score_tile(j, carry):
                k0 = pl.multiple_of(j * tk, tk)
                ki3 = ki3_ref[0, pl.ds(k0, tk), :]
                s = jnp.zeros((tk, tq), F32)
                for h in range(C_IDX_HEADS):
                    d = jnp.dot(ki3, q3_ref[0, h * kw:(h + 1) * kw, :], preferred_element_type=F32)
                    s = s + jnp.maximum(d, 0.0) * wiT_ref[0, h:h + 1, :]
                s = s * iscale + 0.0
                bits = pltpu.bitcast(s, I32)
                key = bits ^ ((bits >> 31) & 0x7FFFFFFF)
                key = jnp.where(allowed_mask(k0, tk), key, INT_MIN)
                skey_scr[pl.ds(k0, tk), :] = key
                ukey = key ^ INT_MIN
                w0 = pl.multiple_of(j * wpt, SUBLANES)
                for g in range(tk // 256):
                    rows = [ukey[g * 256 + 8 * i:g * 256 + 8 * i + 8, :] for i in range(32)]
                    for bi, plane in enumerate(_transpose32(rows)):
                        planes_scr[bi, pl.ds(w0 + g * SUBLANES, SUBLANES), :] = plane
                ones = jnp.full((wpt, tq), -1, I32)
                planes_scr[32, pl.ds(w0, wpt), :] = ones
                e_scr[pl.ds(w0, wpt), :] = ones
                return carry

            def blank_tile(j, carry):
                w0 = pl.multiple_of(j * wpt, SUBLANES)
                for bi in range(33):
                    planes_scr[bi, pl.ds(w0, wpt), :] = jnp.zeros((wpt, tq), I32)
                e_scr[pl.ds(w0, wpt), :] = jnp.zeros((wpt, tq), I32)
                return carry

            ntile = last_kt + 1
            lax.fori_loop(0, ntile, score_tile, 0)
            nblk = (ntile + SEARCH_TILES - 1) // SEARCH_TILES
            lax.fori_loop(ntile, nblk * SEARCH_TILES, blank_tile, 0)
            wpb = SEARCH_TILES * wpt

            def lanesum(acc):
                return jnp.sum(acc, axis=0, keepdims=True)

            def fold(pc):
                return jnp.sum(pc.reshape(wpb // SUBLANES, SUBLANES, tq), axis=0)

            def settle(e, plane, take):
                t = e & plane
                return jnp.where(take != 0, t, e ^ t)

            def bit_step(i, st):
                c_gt, th, take_prev = st
                bi = 31 - i

                def body(j, acc):
                    w0 = pl.multiple_of(j * wpb, SUBLANES)
                    e = settle(e_scr[pl.ds(w0, wpb), :], planes_scr[bi + 1, pl.ds(w0, wpb), :], take_prev)
                    e_scr[pl.ds(w0, wpb), :] = e
                    return acc + fold(lax.population_count(e & planes_scr[bi, pl.ds(w0, wpb), :]))

                c1 = lanesum(lax.fori_loop(0, nblk, body, jnp.zeros((SUBLANES, tq), I32)))
                take = (c_gt + c1) >= topk
                return (jnp.where(take, c_gt, c_gt + c1),
                        jnp.where(take, th | lax.shift_left(jnp.int32(1), bi), th), jnp.where(take, 1, 0))

            zero = jnp.zeros((1, tq), I32)
            c_gt, th_u, take0 = lax.fori_loop(0, 32, bit_step, (zero, zero, zero + 1))

            def last_body(j, acc):
                w0 = pl.multiple_of(j * wpb, SUBLANES)
                e = settle(e_scr[pl.ds(w0, wpb), :], planes_scr[0, pl.ds(w0, wpb), :], take0)
                e_scr[pl.ds(w0, wpb), :] = e
                return acc + fold(lax.population_count(e))

            n_eq = lanesum(lax.fori_loop(0, nblk, last_body, jnp.zeros((SUBLANES, tq), I32)))
            need = topk - c_gt
            nbits = max(1, (l_valid - 1).bit_length())
            thr_scr[0:1, :] = th_u ^ INT_MIN
            thr_scr[1:2, :] = jnp.full((1, tq), (1 << nbits) - 1, I32)

            @pl.when(jnp.max((n_eq - need).astype(F32)) > 0.0)
            def _():
                def idx_step(i, jb):
                    cand = jb - lax.shift_left(jnp.int32(1), nbits - 1 - i)

                    def body(j, acc):
                        w0 = pl.multiple_of(j * wpb, SUBLANES)
                        wr = w0 + lax.broadcasted_iota(I32, (wpb, tq), 0)
                        base = (wr >> 3) * 256 + (wr & 7)
                        mx = (cand - base) >> 3
                        low = jnp.left_shift(2, jnp.clip(mx, 0, 30)) - 1
                        msk = jnp.where(mx < 0, 0, jnp.where(mx >= 31, -1, low))
                        return acc + fold(lax.population_count(e_scr[pl.ds(w0, wpb), :] & msk))

                    cnt = lanesum(lax.fori_loop(0, nblk, body, jnp.zeros((SUBLANES, tq), I32)))
                    return jnp.where(cnt >= need, cand, jb)

                thr_scr[1:2, :] = lax.fori_loop(0, nbits, idx_step, jnp.full((1, tq), (1 << nbits) - 1, I32))

    def tile(near):
        k0 = pl.multiple_of(kt * tk, tk)
        nchunk = tk // RC
        masked = dsa or near
        if masked:
            for c in range(nchunk):
                r0 = k0 + c * RC
                mask = allowed_mask(r0, RC) if near else None
                if dsa:
                    x = skey_scr[pl.ds(r0, RC), :]
                    kidx = r0 + lax.broadcasted_iota(I32, (RC, tq), 0)
                    sel = (x > thr_scr[0:1, :]) | ((x == thr_scr[0:1, :]) & (kidx <= thr_scr[1:2, :]))
                    mask = sel & mask if near else sel
                mb_scr[c * RC:(c + 1) * RC, :] = jnp.where(mask, 0.0, NEG_BIG)

        def chunk_bias(h, c):
            far = far_ref[0:1, h:h + 1]
            if not near:
                return far
            sb, off = (c * RC) // LANES, (c * RC) % LANES
            cols = []
            for qb in range(tq // LANES):
                delta = k0 + sb * LANES - (q_lo + qb * LANES)
                cols.append(jnp.where(delta == 0, tab_ref[0, h, off:off + RC, :],
                                      jnp.where(delta == -LANES, tab_ref[1, h, off:off + RC, :], far)))
            return jnp.concatenate(cols, axis=1) if len(cols) > 1 else cols[0]

        for h in range(nh):
            ks = slice(h * dk, (h + 1) * dk)
            s_scr[h] = jnp.dot(k_ref[0, :, ks], qT_ref[0, ks, :], preferred_element_type=F32)
        m_news, alphas = [], []
        for h in range(nh):
            mx = jnp.full((SUBLANES, tq), NEG_BIG, F32)
            for c in range(nchunk):
                rows = slice(c * RC, (c + 1) * RC)
                blk = s_scr[h, rows, :]
                if masked:
                    blk = blk + mb_scr[rows, :]
                    if dsa:
                        blk = blk + chunk_bias(h, c)
                    s_scr[h, rows, :] = blk
                mx = jnp.maximum(mx, jnp.max(blk.reshape(RC // SUBLANES, SUBLANES, tq), axis=0))
            m_prev = m_scr[h:h + 1, :]
            m_new = jnp.maximum(m_prev, jnp.max(mx, axis=0, keepdims=True))
            m_scr[h:h + 1, :] = m_new
            m_news.append(m_new)
            alphas.append(jnp.exp2(m_prev - m_new))
        for h in range(nh):
            for c in range(nchunk):
                rows = slice(c * RC, (c + 1) * RC)
                p_scr[h, rows, :] = jnp.exp2(s_scr[h, rows, :] - m_news[h]).astype(BF16)
        for h in range(nh):
            sl = slice(h * dh, (h + 1) * dh)
            pv = jnp.dot(vT_ref[0, h * VT_ROWS:(h + 1) * VT_ROWS, :], p_scr[h], preferred_element_type=F32)
            acc_scr[sl, :] = alphas[h] * acc_scr[sl, :] + pv[:dh]
            l_scr[h:h + 1, :] = alphas[h] * l_scr[h:h + 1, :] + pv[dh:dh + 1]

    is_far = kt * tk + tk - 1 <= q_lo - LANES

    @pl.when(is_far)
    def _():
        tile(False)

    @pl.when(jnp.logical_not(is_far))
    def _():
        tile(True)

    @pl.when(kt == last_kt)
    def _():
        for h in range(nh):
            sl = slice(h * dh, (h + 1) * dh)
            o = acc_scr[sl, :] / l_scr[h:h + 1, :]
            o_ref[0, :, sl] = o.T.astype(o_ref.dtype)


def _attention(mode, qT, k, vT, extra, *, p0, t_valid, l_valid, tk):
    b, hdk, tpad = qT.shape
    lpad = k.shape[1]
    nh = vT.shape[1] // VT_ROWS
    hd = nh * LANES
    dk = hdk // nh
    tq = 2 * LANES if tpad % (2 * LANES) == 0 else LANES
    nq, nk = tpad // tq, lpad // tk
    topk = min(TOPK_MAX, l_valid // 4)
    assert p0 % LANES == 0 and lpad % tk == 0 and tk % 256 == 0 and l_valid >= topk >= 1 and tk >= topk

    pairs = []
    for qt in range(nq):
        q_hi = p0 + min(qt * tq + tq, t_valid) - 1
        last = (min((q_hi // CHUNK + 1) * CHUNK, l_valid) - 1) // tk
        pairs += [(qt, kt) for kt in range(last + 1)]
    qt_tab = jnp.asarray([p[0] for p in pairs], I32)
    kt_tab = jnp.asarray([p[1] for p in pairs], I32)

    kmap = lambda i, s, qtt, ktt: (i, ktt[s], 0)
    vmap_ = lambda i, s, qtt, ktt: (i, 0, ktt[s])
    qmap = lambda i, s, qtt, ktt: (i, 0, qtt[s])
    cmap = lambda n: (lambda i, s, qtt, ktt: (0,) * n)
    in_specs = [pl.BlockSpec((1, hdk, tq), qmap),
                pl.BlockSpec((1, tk, hdk), kmap),
                pl.BlockSpec((1, nh * VT_ROWS, tk), vmap_)]
    scratch = [pltpu.VMEM((nh, tq), F32), pltpu.VMEM((nh, tq), F32), pltpu.VMEM((hd, tq), F32),
               pltpu.VMEM((nh, tk, tq), F32), pltpu.VMEM((nh, tk, tq), BF16), pltpu.VMEM((tk, tq), F32)]
    if mode == "dsa":
        q3, ki3, wiT, tabs, far = extra
        wrows = -(-nk // SEARCH_TILES) * SEARCH_TILES * (tk // 32)
        in_specs += [pl.BlockSpec((1, q3.shape[1], tq), qmap),
                     pl.BlockSpec((1, lpad, ki3.shape[2]), lambda i, s, qtt, ktt: (i, 0, 0)),
                     pl.BlockSpec((1, C_IDX_HEADS, tq), qmap),
                     pl.BlockSpec(tabs.shape, cmap(4)),
                     pl.BlockSpec(far.shape, cmap(2))]
        scratch += [pltpu.VMEM((lpad, tq), I32), pltpu.VMEM((SUBLANES, tq), I32),
                    pltpu.VMEM((33, wrows, tq), I32), pltpu.VMEM((wrows, tq), I32)]
    return pl.pallas_call(
        functools.partial(_attn_body, mode=mode, p0=p0, t_valid=t_valid, l_valid=l_valid, tq=tq, tk=tk,
                          dk=dk, topk=topk, nh=nh),
        grid_spec=pltpu.PrefetchScalarGridSpec(
            num_scalar_prefetch=2,
            grid=(b, len(pairs)),
            in_specs=in_specs,
            out_specs=pl.BlockSpec((1, tq, hd), lambda i, s, qtt, ktt: (i, qtt[s], 0)),
            scratch_shapes=scratch),
        out_shape=jax.ShapeDtypeStruct((b, tpad, hd), BF16),
        compiler_params=_cparams(("parallel", "arbitrary")),
        name="attn_" + mode,
    )(qt_tab, kt_tab, qT, k, vT, *extra)


PACK_TILE = 256


def _pack_body(cache_ref, new_ref, o_ref, *, n_cache, transpose, nh):
    j = pl.program_id(1)

    def emit(head):
        for h in range(nh):
            sl = slice(h * LANES, (h + 1) * LANES)
            x = head(h, sl)
            if transpose:
                r0 = h * VT_ROWS
                o_ref[0, r0:r0 + LANES, :] = x.T.astype(o_ref.dtype)
                o_ref[0, r0 + LANES:r0 + VT_ROWS, :] = jnp.ones((VT_ONES, x.shape[0]), o_ref.dtype)
            else:
                o_ref[0, :, sl] = x.astype(o_ref.dtype)

    @pl.when(j < n_cache)
    def _():
        tp = o_ref.shape[2] if transpose else o_ref.shape[1]
        emit(lambda h, sl: cache_ref[0, 0, pl.ds(h, tp, stride=nh), :])

    @pl.when(j >= n_cache)
    def _():
        emit(lambda h, sl: new_ref[0, :, sl])


def _pack_keys(cache, layer, new, lpad, *, transpose):
    _, b, past, nh, dh = cache.shape
    tp = 2 * PACK_TILE if (past % (2 * PACK_TILE) == 0 and lpad % (2 * PACK_TILE) == 0) else PACK_TILE
    assert dh == LANES and past % tp == 0 and lpad % tp == 0
    n_cache = past // tp
    hd = nh * dh
    new = _pad_axis(new, 1, lpad - past)
    out_shape = (b, nh * VT_ROWS, lpad) if transpose else (b, lpad, hd)
    out_spec = (pl.BlockSpec((1, nh * VT_ROWS, tp), lambda i, j: (i, 0, j)) if transpose
                else pl.BlockSpec((1, tp, hd), lambda i, j: (i, j, 0)))
    return pl.pallas_call(
        functools.partial(_pack_body, n_cache=n_cache, transpose=transpose, nh=nh),
        grid=(b, lpad // tp),
        in_specs=[pl.BlockSpec((1, 1, tp * nh, dh),
                               lambda i, j: (layer, i, jnp.minimum(j, n_cache - 1), 0)),
                  pl.BlockSpec((1, tp, hd), lambda i, j: (i, jnp.maximum(j - n_cache, 0), 0))],
        out_specs=out_spec,
        out_shape=jax.ShapeDtypeStruct(out_shape, BF16),
        compiler_params=_cparams(("parallel", "parallel")),
        name="pack_keys",
    )(cache.reshape(cache.shape[0], b, past * nh, dh), new)


def _split_hi_lo(x):
    hi = x.astype(BF16)
    lo = (x - hi.astype(F32)).astype(BF16)
    return hi, lo


def _pad_axis(x, axis, size):
    if x.shape[axis] == size:
        return x
    pad = [(0, 0)] * x.ndim
    pad[axis] = (0, size - x.shape[axis])
    return jnp.pad(x, pad)


def _vt_ones(v):
    b, l, hd = v.shape
    vt = jnp.swapaxes(v, 1, 2).reshape(b, hd // LANES, LANES, l)
    ones = jnp.ones((b, hd // LANES, VT_ONES, l), v.dtype)
    return jnp.concatenate([vt, ones], axis=2).reshape(b, -1, l)


def _key_tile(l_valid):
    tk = 512 if l_valid > 1024 else 256
    return -(-l_valid // tk) * tk, tk


def _prep_weights(W, d_model):
    f = {}
    wd = A_WIDTH
    n_even = W["ev_w_in"].shape[0]
    n_odd = W["od_w_in"].shape[0]
    f["even"] = []
    for e in range(n_even):
        wi = W["ev_w_in"][e]
        w_in = jnp.concatenate([wi[:, :3 * wd], wi[:, A_COLS:], wi[:, 3 * wd:A_COLS],
                                jnp.zeros((d_model, LORA_PAD - A_LORA), F32)], axis=1).astype(BF16)
        mu = W["rwkv_mu"][e]
        w2 = jnp.zeros((LANES, wd), F32).at[:A_DECAY_LORA].set(W["rwkv_w2"][e])
        a2 = jnp.zeros((LANES, wd), F32).at[A_DECAY_LORA:A_DECAY_LORA + A_ICLR_LORA].set(W["rwkv_a2"][e])
        g2 = jnp.zeros((LORA_PAD - LANES, wd), F32).at[:A_GATE_LORA].set(W["rwkv_g2"][e])
        hm = lambda v: v.reshape(A_HEADS, 1, A_HEAD_DIM)
        rw = dict(mu_r=mu[:3 * wd].reshape(1, -1),
                  mu_l=_pad_axis(mu[3 * wd:], 0, LORA_PAD).reshape(1, -1),
                  w0=W["rwkv_w0"][e].reshape(1, -1), w2=w2.astype(BF16),
                  a0=W["rwkv_a0"][e].reshape(1, -1), a2=a2.astype(BF16), g2=g2.astype(BF16),
                  k_k=W["rwkv_k_k"][e].reshape(1, -1), k_a=W["rwkv_k_a"][e].reshape(1, -1),
                  r_k=hm(W["rwkv_r_k"][e]), ln_w=hm(W["rwkv_ln_w"][e]), ln_b=hm(W["rwkv_ln_b"][e]))
        eye = jnp.eye(B_BLOCKS, dtype=F32)
        blockdiag = lambda w: (eye[:, None, :, None] * w[:, :, None, :]).reshape(B_WIDTH, B_WIDTH)
        wgate = jnp.concatenate([blockdiag(W["lru_wa"][e]), blockdiag(W["lru_wx"][e])], axis=1).astype(BF16)
        bgate = jnp.concatenate([W["lru_ba"][e], W["lru_bx"][e]])
        wo = W["ev_w_out"][e].astype(BF16)
        f["even"].append(dict(w_in=w_in, rw=rw, wgate=wgate, bgate=bgate, wo_a=wo[:wd], wo_b=wo[wd:],
                              cw=W["lru_conv_w"][e], cb=W["lru_conv_b"][e], lam=W["lru_lambda"][e]))
    f["odd"] = []
    cw_ = C_WIDTH
    qi_w = C_IDX_HEADS * C_IDX_DIM
    for o in range(n_odd):
        wi = W["od_w_in"][o]
        offs = [0]
        for s in (cw_, cw_, cw_, qi_w, C_IDX_DIM, C_IDX_HEADS, D_Q_RANK, D_KV_RANK, D_ROPE):
            offs.append(offs[-1] + s)
        q, k, v, qi, ki, wi_, qd, kvd, kr = [wi[:, offs[i]:offs[i + 1]] for i in range(9)]
        small = jnp.concatenate([ki, kr, wi_], axis=1)
        w_in = jnp.concatenate([q, k, v, qi, qd, kvd, _pad_axis(small, 1, 512)], axis=1).astype(BF16)
        wuq = W["mla_w_uq"][o].reshape(D_Q_RANK, D_HEADS, D_NOPE + D_ROPE)
        wuq = jnp.concatenate([wuq[:, :, :D_NOPE].reshape(D_Q_RANK, -1),
                               wuq[:, :, D_NOPE:].reshape(D_Q_RANK, -1)], axis=1).astype(BF16)
        wukv = W["mla_w_ukv"][o].reshape(D_KV_RANK, D_HEADS, D_NOPE + D_V)
        wuk = _pad_axis(wukv[:, :, :D_NOPE], 2, MLA_DK).reshape(D_KV_RANK, -1).astype(BF16)
        wuv = wukv[:, :, D_NOPE:].reshape(D_KV_RANK, -1).astype(BF16)
        wo = W["od_w_out"][o].astype(BF16)
        f["odd"].append(dict(w_in=w_in, wuq=wuq, wuk=wuk, wuv=wuv, wo_c=wo[:cw_], wo_d=wo[cw_:],
                             gq=W["mla_q_norm"][o], gkv=W["mla_kv_norm"][o]))
    f["ffn"] = dict(wg=W["ffn_w_gate"].astype(BF16), wu=W["ffn_w_up"].astype(BF16),
                    wd=W["ffn_w_down"].astype(BF16))
    f["tabs"] = _bias_tables(W["rel_bias"])
    nb = N_BUCKETS // 2
    f["far"] = _pad_axis(W["rel_bias"][nb - 1:nb, :] * LOG2E, 1, LANES)
    return f


def _mixer_even(x2, b, t, p0, shift, s0, h0, cbuf, fe, norm_g):
    wd = A_WIDTH
    p = _norm_matmul(x2, norm_g, fe["w_in"])
    p3 = p.reshape(b, t, -1)
    lora_col = 3 * wd + 2 * B_WIDTH
    shift_r = shift[:, None, :3 * wd]
    shift_l = _pad_axis(shift[:, None, 3 * wd:], 2, LORA_PAD)
    ya, s_new = _rwkv(p3, lora_col // LORA_PAD, shift_r, shift_l, s0, fe["rw"])
    yb, h_new, c_new = _rglru(p3, 3, 4, fe["cw"], fe["cb"], fe["wgate"], fe["bgate"], fe["lam"],
                              h0, cbuf, p0=p0)
    x2 = _matmul([ya.reshape(b * t, wd), yb.reshape(b * t, B_WIDTH)], [fe["wo_a"], fe["wo_b"]], res=x2)
    last = p3[:, t - 1]
    new_shift = jnp.concatenate([last[:, :3 * wd], last[:, lora_col:lora_col + A_LORA]], axis=-1)
    return x2, new_shift, s_new, h_new[:, 0], c_new


def _mixer_odd(x2, b, t, p0, o, dsa_k, dsa_v, cik, clat, ckr, fo, f, norm_g, cos, sin):
    cw_ = C_WIDTH
    nt = cw_ // 512
    segs = [(0, nt, BF16, C_HEAD_DIM ** -0.5 * LOG2E), (nt, 2 * nt, F32, 1.0), (nt, 2 * nt, BF16, 1.0),
            (2 * nt, 3 * nt, F32, 1.0), (2 * nt, 3 * nt, BF16, 1.0), (3 * nt, 3 * nt + 4, F32, 1.0)]
    q_bf, k_f, k_bf, v_f, v_bf, rest = [z.reshape(b, t, -1) for z in
                                        _norm_matmul(x2, norm_g, fo["w_in"], segs)]
    past = dsa_k.shape[2]
    l_valid = past + t
    lpad, tk = _key_tile(l_valid)
    tpad = -(-t // LANES) * LANES
    k_new, v_new = k_f, v_f
    qi = rest[..., :512]
    small = rest[..., 1536:]
    ki_new = small[..., :C_IDX_DIM]
    wi = small[..., 2 * C_IDX_DIM:2 * C_IDX_DIM + C_IDX_HEADS]

    def keys(cache, new):
        allk = jnp.concatenate([cache.reshape(b, past, -1), new], axis=1) if past else new
        return _pad_axis(allk, 1, lpad)

    tq_ = lambda z: _pad_axis(jnp.swapaxes(z, 1, 2), 2, tpad)
    if past and past % PACK_TILE == 0:
        k_all = _pack_keys(dsa_k, o, k_new, lpad, transpose=False)
        vT_all = _pack_keys(dsa_v, o, v_new, lpad, transpose=True)
    elif past:
        k_all = keys(dsa_k[o], k_new).astype(BF16)
        vT_all = _vt_ones(keys(dsa_v[o], v_new).astype(BF16))
    else:
        k_all = _pad_axis(k_bf, 1, lpad)
        vT_all = _vt_ones(_pad_axis(v_bf, 1, lpad))
    kih, kil = _split_hi_lo(keys(cik, ki_new))
    ki3 = jnp.concatenate([kih, kih, kil], axis=-1)
    qih, qil = _split_hi_lo(tq_(qi).reshape(b, C_IDX_HEADS, C_IDX_DIM, tpad))
    q3 = jnp.concatenate([qih, qil, qih], axis=2).reshape(b, 3 * C_IDX_HEADS * C_IDX_DIM, tpad)
    yc = _attention("dsa", tq_(q_bf), k_all, vT_all,
                    (q3, ki3, tq_(wi), f["tabs"], f["far"]),
                    p0=p0, t_valid=t, l_valid=l_valid, tk=tk)[:, :t]
    qn, qr, lat, krope = _mla_prep(rest, 1, 2, 3, fo["gq"], fo["gkv"], fo["wuq"], cos, sin)
    lat_all = keys(clat, lat).reshape(b * lpad, -1)
    kr_all = keys(ckr, krope).reshape(b * lpad, -1)
    eye = jnp.eye(D_ROPE, dtype=BF16)
    ident = jnp.tile(jnp.pad(eye, ((0, 0), (D_NOPE, MLA_DK - D_NOPE - D_ROPE))), (1, D_HEADS))
    k_full = _matmul([lat_all, kr_all], [fo["wuk"], ident], out_dtype=BF16).reshape(b, lpad, -1)
    v_all = _matmul([lat_all], [fo["wuv"]], out_dtype=BF16).reshape(b, lpad, -1)
    q_full = jnp.concatenate([qn.reshape(b, t, D_HEADS, D_NOPE), qr.reshape(b, t, D_HEADS, D_ROPE),
                              jnp.zeros((b, t, D_HEADS, MLA_DK - D_NOPE - D_ROPE), BF16)], axis=-1)
    yd = _attention("mla", tq_(q_full.reshape(b, t, -1)), k_full, _vt_ones(v_all), (),
                    p0=p0, t_valid=t, l_valid=l_valid, tk=tk)[:, :t]
    x2 = _matmul([yc.reshape(b * t, cw_), yd.reshape(b * t, -1)], [fo["wo_c"], fo["wo_d"]], res=x2)
    return (x2, k_new.reshape(b, t, C_HEADS, C_HEAD_DIM), v_new.reshape(b, t, C_HEADS, C_HEAD_DIM),
            ki_new, lat, krope)


def _trunk(x, p0, shift, rwkv_s, lru_h, lru_conv, dsa_k, dsa_v, dsa_ik, mla_lat, mla_kr, W, f):
    b, t, d = x.shape
    depth = W["norm_mix"].shape[0]
    pos = (p0 + jnp.arange(t)).astype(F32)
    inv = ROPE_BASE ** (-jnp.arange(0, D_ROPE, 2, dtype=F32) / D_ROPE)
    ang = pos[:, None] * inv[None, :]
    cos = jnp.tile(jnp.cos(ang), (1, 2 * D_HEADS))
    sin = jnp.tile(jnp.sin(ang), (1, 2 * D_HEADS))
    x2 = x.reshape(b * t, d)
    ev = [[] for _ in range(4)]
    od = [[] for _ in range(5)]
    for layer in range(depth):
        if layer % 2 == 0:
            e = layer // 2
            x2, *outs = _mixer_even(x2, b, t, p0, shift[e], rwkv_s[e], lru_h[e], lru_conv[e],
                                    f["even"][e], W["norm_mix"][layer])
            for lst, o_ in zip(ev, outs):
                lst.append(o_)
        else:
            o = layer // 2
            x2, *outs = _mixer_odd(x2, b, t, p0, o, dsa_k, dsa_v, dsa_ik[o], mla_lat[o], mla_kr[o],
                                   f["odd"][o], f, W["norm_mix"][layer], cos, sin)
            for lst, o_ in zip(od, outs):
                lst.append(o_)
        ff = f["ffn"]
        x2 = _ffn(x2, W["norm_ffn"][layer], ff["wg"], ff["wu"], ff["wd"], layer, W["final_norm"],
                  final_norm=layer == depth - 1)
    return (x2.reshape(b, t, d),) + tuple(jnp.stack(v) for v in ev) + tuple(jnp.stack(v) for v in od)


def kernel(x_prompt, x_sample, state_rwkv_shift, state_rwkv, state_lru, state_lru_conv, cache_dsa_k, cache_dsa_v, cache_dsa_idx_k, cache_mla_latent, cache_mla_krope, rel_bias, final_norm, norm_mix, norm_ffn, ffn_w_gate, ffn_w_up, ffn_w_down, ev_w_in, ev_w_out, rwkv_mu, rwkv_w0, rwkv_w2, rwkv_a0, rwkv_a2, rwkv_g2, rwkv_k_k, rwkv_k_a, rwkv_r_k, rwkv_ln_w, rwkv_ln_b, lru_conv_w, lru_conv_b, lru_wa, lru_ba, lru_wx, lru_bx, lru_lambda, od_w_in, od_w_out, mla_q_norm, mla_w_uq, mla_kv_norm, mla_w_ukv):
    W = dict(rel_bias=rel_bias, final_norm=final_norm, norm_mix=norm_mix, norm_ffn=norm_ffn,
             ffn_w_gate=ffn_w_gate, ffn_w_up=ffn_w_up, ffn_w_down=ffn_w_down,
             ev_w_in=ev_w_in, ev_w_out=ev_w_out, rwkv_mu=rwkv_mu, rwkv_w0=rwkv_w0,
             rwkv_w2=rwkv_w2, rwkv_a0=rwkv_a0, rwkv_a2=rwkv_a2, rwkv_g2=rwkv_g2,
             rwkv_k_k=rwkv_k_k, rwkv_k_a=rwkv_k_a, rwkv_r_k=rwkv_r_k, rwkv_ln_w=rwkv_ln_w,
             rwkv_ln_b=rwkv_ln_b, lru_conv_w=lru_conv_w, lru_conv_b=lru_conv_b, lru_wa=lru_wa,
             lru_ba=lru_ba, lru_wx=lru_wx, lru_bx=lru_bx, lru_lambda=lru_lambda,
             od_w_in=od_w_in, od_w_out=od_w_out, mla_q_norm=mla_q_norm, mla_w_uq=mla_w_uq,
             mla_kv_norm=mla_kv_norm, mla_w_ukv=mla_w_ukv)
    d_model = x_prompt.shape[-1]
    f = _prep_weights(W, d_model)
    bp = x_prompt.shape[0]
    n_even, n_odd = ev_w_in.shape[0], od_w_in.shape[0]
    dt = x_prompt.dtype
    z = lambda *shape: jnp.zeros(shape, dt)
    outs_p = _trunk(
        x_prompt, 0,
        z(n_even, bp, A_COLS), z(n_even, bp, A_HEADS, A_HEAD_DIM, A_HEAD_DIM),
        z(n_even, bp, B_WIDTH), z(n_even, bp, B_CONV - 1, B_WIDTH),
        z(n_odd, bp, 0, C_HEADS, C_HEAD_DIM), z(n_odd, bp, 0, C_HEADS, C_HEAD_DIM),
        z(n_odd, bp, 0, C_IDX_DIM), z(n_odd, bp, 0, D_KV_RANK), z(n_odd, bp, 0, D_ROPE), W, f)
    past = cache_dsa_k.shape[2]
    outs_s = _trunk(x_sample, past, state_rwkv_shift, state_rwkv, state_lru, state_lru_conv,
                    cache_dsa_k, cache_dsa_v, cache_dsa_idx_k, cache_mla_latent, cache_mla_krope, W, f)
    return (outs_p[0], outs_s[0]) + tuple(outs_p[1:]) + tuple(outs_s[1:])
```

```python
import functools
import math

import jax
import jax.numpy as jnp
from jax import lax
from jax.experimental import pallas as pl
from jax.experimental.pallas import tpu as pltpu

F32 = jnp.float32
BF16 = jnp.bfloat16
I32 = jnp.int32

CHUNK = 64
NORM_EPS = 1e-6
A_HEADS = 16
A_HEAD_DIM = 64
A_WIDTH = A_HEADS * A_HEAD_DIM
A_DECAY_LORA = 64
A_ICLR_LORA = 64
A_GATE_LORA = 160
A_LORA = A_DECAY_LORA + A_ICLR_LORA + A_GATE_LORA
A_COLS = 3 * A_WIDTH + A_LORA
A_LN_EPS = 64e-5
B_WIDTH = 1024
B_BLOCKS = 16
B_CONV = 4
B_C = 8.0
C_HEADS = 8
C_HEAD_DIM = 128
C_WIDTH = C_HEADS * C_HEAD_DIM
C_IDX_HEADS = 8
C_IDX_DIM = 64
TOPK_MAX = 256
D_HEADS = 8
D_NOPE = 128
D_ROPE = 64
D_V = 128
D_Q_RANK = 512
D_KV_RANK = 512
ROPE_BASE = 10000.0
N_BUCKETS = 32
MAX_DISTANCE = 128

LANES = 128
SUBLANES = 8
VMEM_LIMIT = 56 * 1024 * 1024
LORA_PAD = 512
NEG_BIG = -1e30
INT_MIN = -2147483648
RC = 64
SEARCH_TILES = 4
VT_ONES = 16
VT_ROWS = LANES + VT_ONES
LOG2E = 1.4426950408889634
MLA_DK = 256

NN = (((1,), (0,)), ((), ()))
NT = (((1,), (1,)), ((), ()))
BNN = (((2,), (1,)), ((0,), (0,)))
BNT = (((2,), (2,)), ((0,), (0,)))
BTN = (((1,), (1,)), ((0,), (0,)))


def _cparams(sem):
    return pltpu.CompilerParams(dimension_semantics=sem, vmem_limit_bytes=VMEM_LIMIT)


def _rms(x, g, eps=NORM_EPS):
    ms = jnp.mean(x * x, axis=-1, keepdims=True)
    return x * lax.rsqrt(ms + eps) * g


def _softplus(x):
    return jnp.maximum(x, 0.0) + jnp.log1p(jnp.exp(-jnp.abs(x)))


def _dotp(a, b, dims, passes):
    if passes == 6:
        return lax.dot_general(a, b, dims, precision=lax.Precision.HIGHEST, preferred_element_type=F32)
    ah = a.astype(BF16)
    bh = b.astype(BF16)
    out = lax.dot_general(ah, bh, dims, preferred_element_type=F32)
    if passes == 3:
        al = (a - ah.astype(F32)).astype(BF16)
        bl = (b - bh.astype(F32)).astype(BF16)
        out = out + lax.dot_general(ah, bl, dims, preferred_element_type=F32)
        out = out + lax.dot_general(al, bh, dims, preferred_element_type=F32)
    return out


def _norm_matmul_body(x_ref, g_ref, w_ref, *refs, segs):
    outs, xn_ref = refs[:-1], refs[-1]
    j = pl.program_id(1)

    @pl.when(j == 0)
    def _():
        xn_ref[...] = _rms(x_ref[...], g_ref[...]).astype(BF16)

    acc = jnp.dot(xn_ref[...], w_ref[...], preferred_element_type=F32)
    for o_ref, (lo, hi, _, scale) in zip(outs, segs):
        @pl.when((j >= lo) & (j < hi))
        def _(o_ref=o_ref, scale=scale):
            o_ref[...] = (acc if scale == 1.0 else acc * scale).astype(o_ref.dtype)


def _row_tile(m, cap):
    tm = cap
    while m % tm:
        tm //= 2
    return tm


def _norm_matmul(x, g, w, segs=None, *, tn=512):
    m, k = x.shape
    n = w.shape[1]
    tm = _row_tile(m, 1024)
    segs = segs or [(0, n // tn, F32, 1.0)]
    out_specs = [pl.BlockSpec((tm, tn), lambda i, j, lo=lo, nb=hi - lo: (i, jnp.clip(j - lo, 0, nb - 1)))
                 for lo, hi, _, _ in segs]
    outs = pl.pallas_call(
        functools.partial(_norm_matmul_body, segs=tuple(segs)),
        grid=(m // tm, n // tn),
        in_specs=[pl.BlockSpec((tm, k), lambda i, j: (i, 0)),
                  pl.BlockSpec((1, k), lambda i, j: (0, 0)),
                  pl.BlockSpec((k, tn), lambda i, j: (0, j))],
        out_specs=out_specs,
        out_shape=[jax.ShapeDtypeStruct((m, (hi - lo) * tn), dt) for lo, hi, dt, _ in segs],
        scratch_shapes=[pltpu.VMEM((tm, k), BF16)],
        compiler_params=_cparams(("parallel", "arbitrary")),
        name="norm_matmul",
    )(x, g.reshape(1, k), w)
    return outs if len(segs) > 1 else outs[0]


def _mm_body(*refs, n_lhs, has_res):
    o_ref = refs[-1]
    acc = refs[2 * n_lhs][...] if has_res else None
    for a_ref, w_ref in zip(refs[:n_lhs], refs[n_lhs:2 * n_lhs]):
        d = jnp.dot(a_ref[...].astype(BF16), w_ref[...], preferred_element_type=F32)
        acc = d if acc is None else acc + d
    o_ref[...] = acc.astype(o_ref.dtype)


def _matmul(lhs_list, w_list, res=None, *, out_dtype=F32):
    m = lhs_list[0].shape[0]
    n = w_list[0].shape[1]
    tm = _row_tile(m, 1024)
    tn = 1024 if n % 1024 == 0 else 512
    in_specs = [pl.BlockSpec((tm, a.shape[1]), lambda i, j: (i, 0)) for a in lhs_list]
    in_specs += [pl.BlockSpec((w.shape[0], tn), lambda i, j: (0, j)) for w in w_list]
    args = list(lhs_list) + list(w_list)
    if res is not None:
        in_specs.append(pl.BlockSpec((tm, tn), lambda i, j: (i, j)))
        args.append(res)
    return pl.pallas_call(
        functools.partial(_mm_body, n_lhs=len(lhs_list), has_res=res is not None),
        grid=(m // tm, n // tn),
        in_specs=in_specs,
        out_specs=pl.BlockSpec((tm, tn), lambda i, j: (i, j)),
        out_shape=jax.ShapeDtypeStruct((m, n), out_dtype),
        compiler_params=_cparams(("parallel", "arbitrary")),
        name="matmul",
    )(*args)


def _ffn_body(x_ref, g_ref, wg_ref, wu_ref, wd_ref, gf_ref, o_ref, xn_ref, acc_ref, *, final_norm):
    f = pl.program_id(1)

    @pl.when(f == 0)
    def _():
        xn_ref[...] = _rms(x_ref[...], g_ref[...]).astype(BF16)
        acc_ref[...] = jnp.zeros_like(acc_ref)

    xn = xn_ref[...]
    hg = jnp.dot(xn, wg_ref[0], preferred_element_type=F32)
    hu = jnp.dot(xn, wu_ref[0], preferred_element_type=F32)
    h = hg * jax.nn.sigmoid(hg) * hu
    acc_ref[...] += jnp.dot(h.astype(BF16), wd_ref[0], preferred_element_type=F32)

    @pl.when(f == pl.num_programs(1) - 1)
    def _():
        y = x_ref[...] + acc_ref[...]
        if final_norm:
            y = _rms(y, gf_ref[...])
        o_ref[...] = y


def _ffn(x, g, wg, wu, wd, layer, gf, *, final_norm, tf=512):
    m, k = x.shape
    dff = wg.shape[2]
    tm = _row_tile(m, 512)
    return pl.pallas_call(
        functools.partial(_ffn_body, final_norm=final_norm),
        grid=(m // tm, dff // tf),
        in_specs=[pl.BlockSpec((tm, k), lambda i, f: (i, 0)),
                  pl.BlockSpec((1, k), lambda i, f: (0, 0)),
                  pl.BlockSpec((1, k, tf), lambda i, f: (layer, 0, f)),
                  pl.BlockSpec((1, k, tf), lambda i, f: (layer, 0, f)),
                  pl.BlockSpec((1, tf, k), lambda i, f: (layer, f, 0)),
                  pl.BlockSpec((1, k), lambda i, f: (0, 0))],
        out_specs=pl.BlockSpec((tm, k), lambda i, f: (i, 0)),
        out_shape=jax.ShapeDtypeStruct((m, k), F32),
        scratch_shapes=[pltpu.VMEM((tm, k), BF16), pltpu.VMEM((tm, k), F32)],
        compiler_params=_cparams(("parallel", "arbitrary")),
        name="ffn",
    )(x, g.reshape(1, k), wg, wu, wd, gf.reshape(1, k))


def _lru_body(gate_ref, xb_ref, cw_ref, cb_ref, wg_ref, bg_ref, lam_ref, h0_ref, cbuf_ref,
              y_ref, hout_ref, cout_ref, xbuf, hcar, *, tt, p0):
    t = pl.program_id(1)
    w = B_WIDTH
    halo = SUBLANES

    @pl.when(t == 0)
    def _():
        xbuf[0:halo, :] = jnp.zeros((halo, w), F32)
        xbuf[halo - (B_CONV - 1):halo, :] = cbuf_ref[0]
        hcar[...] = h0_ref[0]

    xb = xb_ref[0]
    xbuf[halo:halo + tt, :] = xb
    xc = cb_ref[...] + cw_ref[B_CONV - 1:B_CONV, :] * xb
    for j in range(B_CONV - 1):
        off = halo - (B_CONV - 1) + j
        xc = xc + cw_ref[j:j + 1, :] * xbuf[off:off + tt, :]
    tail = xbuf[tt + halo - (B_CONV - 1):tt + halo, :]
    xbuf[halo - (B_CONV - 1):halo, :] = tail

    pre = jnp.dot(xc.astype(BF16), wg_ref[...], preferred_element_type=F32) + bg_ref[...]
    rg = jax.nn.sigmoid(pre[:, :w])
    ig = jax.nn.sigmoid(pre[:, w:])
    log_a = (-B_C) * rg * _softplus(-lam_ref[...])
    a = jnp.exp(log_a)
    row = lax.broadcasted_iota(I32, (tt, w), 0)
    th = jnp.tanh(log_a)
    mult = jnp.sqrt(-2.0 * th / (1.0 - th))
    mult = jnp.where(row + (p0 + t * tt) == 0, 1.0, mult)
    u = mult * (ig * xc)

    d = 1
    while d < tt:
        keep = row >= d
        a_sh = pltpu.roll(a, d, 0)
        u_sh = pltpu.roll(u, d, 0)
        u = u + jnp.where(keep, a * u_sh, 0.0)
        a = jnp.where(keep, a * a_sh, a)
        d *= 2
    h = u + a * hcar[...]
    hcar[...] = h[tt - 1:tt, :]
    y_ref[0] = (h * jax.nn.gelu(gate_ref[0])).astype(y_ref.dtype)

    @pl.when(t == pl.num_programs(1) - 1)
    def _():
        hout_ref[0] = h[tt - 1:tt, :]
        cout_ref[0] = tail


def _rglru(p3, gate_blk, xb_blk, cw, cb, wgate, bgate, lam, h0, cbuf, *, p0):
    b, t, _ = p3.shape
    w = B_WIDTH
    tt = min(256, t)
    row = lambda v: v.reshape(1, -1)
    return pl.pallas_call(
        functools.partial(_lru_body, tt=tt, p0=p0),
        grid=(b, t // tt),
        in_specs=[pl.BlockSpec((1, tt, w), lambda i, j: (i, j, gate_blk)),
                  pl.BlockSpec((1, tt, w), lambda i, j: (i, j, xb_blk)),
                  pl.BlockSpec((B_CONV, w), lambda i, j: (0, 0)),
                  pl.BlockSpec((1, w), lambda i, j: (0, 0)),
                  pl.BlockSpec((w, 2 * w), lambda i, j: (0, 0)),
                  pl.BlockSpec((1, 2 * w), lambda i, j: (0, 0)),
                  pl.BlockSpec((1, w), lambda i, j: (0, 0)),
                  pl.BlockSpec((1, 1, w), lambda i, j: (i, 0, 0)),
                  pl.BlockSpec((1, B_CONV - 1, w), lambda i, j: (i, 0, 0))],
        out_specs=[pl.BlockSpec((1, tt, w), lambda i, j: (i, j, 0)),
                   pl.BlockSpec((1, 1, w), lambda i, j: (i, 0, 0)),
                   pl.BlockSpec((1, B_CONV - 1, w), lambda i, j: (i, 0, 0))],
        out_shape=[jax.ShapeDtypeStruct((b, t, w), BF16),
                   jax.ShapeDtypeStruct((b, 1, w), F32),
                   jax.ShapeDtypeStruct((b, B_CONV - 1, w), F32)],
        scratch_shapes=[pltpu.VMEM((tt + SUBLANES, w), F32), pltpu.VMEM((1, w), F32)],
        compiler_params=_cparams(("parallel", "arbitrary")),
        name="rglru",
    )(p3, p3, cw, row(cb), wgate, row(bgate), row(lam), h0.reshape(b, 1, w), cbuf)


RWKV_PASSES = 1
RWKV_PASSES_SOLVE = 3


def _rwkv_body(rkv_ref, lora_ref, shr_ref, shl_ref, s0_ref, mur_ref, mul_ref, w0_ref, w2_ref, a0_ref,
               a2_ref, g2_ref, kk_ref, ka_ref, rk_ref, lnw_ref, lnb_ref, tri_ref,
               y_ref, sout_ref, buf_r, buf_l, s_scr, st_r, st_k, st_v, st_q, st_a, st_l, st_d, *, c):
    ci = pl.program_id(1)
    halo = SUBLANES
    hn, n, wd = A_HEADS, A_HEAD_DIM, A_WIDTH
    mm = functools.partial(_dotp, passes=RWKV_PASSES)

    @pl.when(ci == 0)
    def _():
        buf_r[halo - 1:halo, :] = shr_ref[0]
        buf_l[halo - 1:halo, :] = shl_ref[0]
        s_scr[...] = s0_ref[0]

    cur_r = rkv_ref[0]
    cur_l = lora_ref[0]
    buf_r[halo:halo + c, :] = cur_r
    buf_l[halo:halo + c, :] = cur_l
    xm = cur_r + mur_ref[...] * (buf_r[halo - 1:halo - 1 + c, :] - cur_r)
    lo = cur_l + mul_ref[...] * (buf_l[halo - 1:halo - 1 + c, :] - cur_l)
    buf_r[halo - 1:halo, :] = cur_r[c - 1:c, :]
    buf_l[halo - 1:halo, :] = cur_l[c - 1:c, :]

    r = xm[:, :wd]
    k = xm[:, wd:2 * wd]
    v = xm[:, 2 * wd:]
    lo_a = lo[:, :LANES]
    w_pre = w0_ref[...] + jnp.dot(jnp.tanh(lo_a).astype(BF16), w2_ref[...], preferred_element_type=F32)
    w_log = -_softplus(-w_pre) - 0.5
    ld = -jnp.exp(w_log)
    a = jax.nn.sigmoid(a0_ref[...] + jnp.dot(lo_a.astype(BF16), a2_ref[...], preferred_element_type=F32))
    g = jnp.dot(jax.nn.sigmoid(lo[:, LANES:]).astype(BF16), g2_ref[...], preferred_element_type=F32)
    kq = k * kk_ref[...]
    k2 = k * (1.0 + (a - 1.0) * ka_ref[...])
    lc = lax.dot_general(tri_ref[...], ld, NN, precision=lax.Precision.HIGHEST,
                         preferred_element_type=F32)

    for h in range(hn):
        sl = slice(h * n, (h + 1) * n)
        st_r[h] = r[:, sl]
        st_k[h] = k2[:, sl]
        st_v[h] = v[:, sl]
        st_q[h] = kq[:, sl]
        st_a[h] = a[:, sl]
        st_l[h] = lc[:, sl]
        st_d[h] = ld[:, sl]

    rh_, k2h, vh, kqh, ah, lch, ldh = (st_r[...], st_k[...], st_v[...], st_q[...], st_a[...],
                                       st_l[...], st_d[...])
    nrm = jnp.sqrt(jnp.sum(kqh * kqh, axis=-1, keepdims=True))
    kk = kqh / jnp.maximum(nrm, 1e-12)
    kka = kk * ah
    e_neg = jnp.exp(-lch)
    am = jnp.exp(lch - ldh) * kk
    bm = kka * e_neg
    kh = k2h * e_neg
    rh = rh_ * jnp.exp(lch)
    l_end = lch[:, c - 1:c, :]
    e_c = jnp.exp(l_end - lch)
    bp = kka * e_c
    kp = k2h * e_c
    w_end = jnp.exp(l_end)

    x2 = jnp.concatenate([am, rh], axis=1)
    zb = _dotp(x2, bm, BNT, RWKV_PASSES_SOLVE)
    zk = mm(x2, kh, BNT)
    ti = lax.broadcasted_iota(I32, (hn, c, c), 1)
    si = lax.broadcasted_iota(I32, (hn, c, c), 2)
    strict = si < ti
    incl = si <= ti
    m1 = jnp.where(strict, zb[:, :c], 0.0)
    m4 = jnp.where(incl, zb[:, c:], 0.0)
    m2 = jnp.where(strict, zk[:, :c], 0.0)
    m3 = jnp.where(incl, zk[:, c:], 0.0)

    tm = jnp.where(si == ti, 1.0, 0.0) - m1
    npow = mm(m1, m1, BNN)
    span = 2
    while span < c:
        tm = tm + mm(tm, npow, BNN)
        span *= 2
        if span < c:
            npow = mm(npow, npow, BNN)

    s0 = s_scr[...]
    rhs = mm(am, s0, BNT) + mm(m2, vh, BNN)
    p = mm(tm, rhs, BNN)
    y = mm(rh, s0, BNT) + mm(m3, vh, BNN) - mm(m4, p, BNN)
    s_new = s0 * w_end + mm(vh, kp, BTN) - mm(p, bp, BTN)
    s_scr[...] = s_new

    mean = jnp.mean(y, axis=-1, keepdims=True)
    yc = y - mean
    var = jnp.mean(yc * yc, axis=-1, keepdims=True)
    yn = yc * lax.rsqrt(var + A_LN_EPS) * lnw_ref[...] + lnb_ref[...]
    bonus = jnp.sum(rh_ * k2h * rk_ref[...], axis=-1, keepdims=True) * vh
    yo = yn + bonus
    yo = jnp.concatenate([yo[h] for h in range(hn)], axis=-1)
    y_ref[0] = (yo * g).astype(y_ref.dtype)

    @pl.when(ci == pl.num_programs(1) - 1)
    def _():
        sout_ref[0] = s_new


def _rwkv(p3, lora_blk, shift_r, shift_l, s0, wts):
    b, t, _ = p3.shape
    c = min(CHUNK, t)
    hn, n, wd = A_HEADS, A_HEAD_DIM, A_WIDTH
    full = lambda shape: pl.BlockSpec(shape, lambda i, j: (0,) * len(shape))
    tri = (jnp.arange(c)[:, None] >= jnp.arange(c)[None, :]).astype(F32)
    st = pltpu.VMEM((hn, c, n), F32)
    return pl.pallas_call(
        functools.partial(_rwkv_body, c=c),
        grid=(b, t // c),
        in_specs=[pl.BlockSpec((1, c, 3 * wd), lambda i, j: (i, j, 0)),
                  pl.BlockSpec((1, c, LORA_PAD), lambda i, j: (i, j, lora_blk)),
                  pl.BlockSpec((1, 1, 3 * wd), lambda i, j: (i, 0, 0)),
                  pl.BlockSpec((1, 1, LORA_PAD), lambda i, j: (i, 0, 0)),
                  pl.BlockSpec((1, hn, n, n), lambda i, j: (i, 0, 0, 0)),
                  full((1, 3 * wd)), full((1, LORA_PAD)), full((1, wd)), full((LANES, wd)),
                  full((1, wd)), full((LANES, wd)), full((LORA_PAD - LANES, wd)),
                  full((1, wd)), full((1, wd)), full((hn, 1, n)), full((hn, 1, n)), full((hn, 1, n)),
                  full((c, c))],
        out_specs=[pl.BlockSpec((1, c, wd), lambda i, j: (i, j, 0)),
                   pl.BlockSpec((1, hn, n, n), lambda i, j: (i, 0, 0, 0))],
        out_shape=[jax.ShapeDtypeStruct((b, t, wd), BF16),
                   jax.ShapeDtypeStruct((b, hn, n, n), F32)],
        scratch_shapes=[pltpu.VMEM((c + SUBLANES, 3 * wd), F32), pltpu.VMEM((c + SUBLANES, LORA_PAD), F32),
                        pltpu.VMEM((hn, n, n), F32), st, st, st, st, st, st, st],
        compiler_params=_cparams(("parallel", "arbitrary")),
        name="rwkv7",
    )(p3, p3, shift_r, shift_l, s0, wts["mu_r"], wts["mu_l"], wts["w0"], wts["w2"], wts["a0"], wts["a2"],
      wts["g2"], wts["k_k"], wts["k_a"], wts["r_k"], wts["ln_w"], wts["ln_b"], tri)


def _rope(x, cos, sin):
    wdt = x.shape[-1]
    lane = lax.broadcasted_iota(I32, x.shape, 1)
    first = (lane % D_ROPE) < (D_ROPE // 2)
    rot = jnp.where(first, -pltpu.roll(x, wdt - D_ROPE // 2, 1), pltpu.roll(x, D_ROPE // 2, 1))
    return x * cos + rot * sin


def _mla_prep_body(qd_ref, kvd_ref, sm_ref, gq_ref, gkv_ref, wuq_ref, cos_ref, sin_ref,
                   qn_ref, qr_ref, lat_ref, kr_ref):
    nope_w = D_HEADS * D_NOPE
    qdn = _rms(qd_ref[0], gq_ref[...]).astype(BF16)
    qf = jnp.dot(qdn, wuq_ref[...], preferred_element_type=F32) * ((D_NOPE + D_ROPE) ** -0.5 * LOG2E)
    qn_ref[0] = qf[:, :nope_w].astype(qn_ref.dtype)
    cos = cos_ref[...]
    sin = sin_ref[...]
    qr_ref[0] = _rope(qf[:, nope_w:], cos, sin).astype(qr_ref.dtype)
    lat_ref[0] = _rms(kvd_ref[0], gkv_ref[...])
    sm = sm_ref[0][:, :LANES]
    kr = _rope(sm, cos[:, :LANES], sin[:, :LANES])
    kr_ref[0] = kr[:, C_IDX_DIM:C_IDX_DIM + D_ROPE]


def _mla_prep(p3, qd_blk, kvd_blk, sm_blk, gq, gkv, wuq, cos, sin):
    b, t, _ = p3.shape
    tt = min(256, t)
    rw = D_HEADS * D_ROPE
    full = lambda shape: pl.BlockSpec(shape, lambda i, j: (0,) * len(shape))
    return pl.pallas_call(
        _mla_prep_body,
        grid=(b, t // tt),
        in_specs=[pl.BlockSpec((1, tt, D_Q_RANK), lambda i, j: (i, j, qd_blk)),
                  pl.BlockSpec((1, tt, D_KV_RANK), lambda i, j: (i, j, kvd_blk)),
                  pl.BlockSpec((1, tt, 512), lambda i, j: (i, j, sm_blk)),
                  full((1, D_Q_RANK)), full((1, D_KV_RANK)), full(wuq.shape),
                  pl.BlockSpec((tt, rw), lambda i, j: (j, 0)),
                  pl.BlockSpec((tt, rw), lambda i, j: (j, 0))],
        out_specs=[pl.BlockSpec((1, tt, D_HEADS * D_NOPE), lambda i, j: (i, j, 0)),
                   pl.BlockSpec((1, tt, rw), lambda i, j: (i, j, 0)),
                   pl.BlockSpec((1, tt, D_KV_RANK), lambda i, j: (i, j, 0)),
                   pl.BlockSpec((1, tt, D_ROPE), lambda i, j: (i, j, 0))],
        out_shape=[jax.ShapeDtypeStruct((b, t, D_HEADS * D_NOPE), BF16),
                   jax.ShapeDtypeStruct((b, t, rw), BF16),
                   jax.ShapeDtypeStruct((b, t, D_KV_RANK), F32),
                   jax.ShapeDtypeStruct((b, t, D_ROPE), F32)],
        compiler_params=_cparams(("parallel", "parallel")),
        name="mla_prep",
    )(p3, p3, p3, gq.reshape(1, -1), gkv.reshape(1, -1), wuq, cos, sin)


def _bias_table_body(rb_ref, o_ref):
    d = pl.program_id(0)
    h = pl.program_id(1)
    nb = N_BUCKETS // 2
    max_exact = nb // 2
    s = lax.broadcasted_iota(I32, (LANES, LANES), 0)
    q = lax.broadcasted_iota(I32, (LANES, LANES), 1)
    rel = s - q - d * LANES
    n = jnp.abs(rel)
    big = jnp.maximum(n, max_exact).astype(F32)
    large = max_exact + (jnp.log(big / max_exact) / math.log(MAX_DISTANCE / max_exact)
                         * (nb - max_exact)).astype(I32)
    large = jnp.minimum(large, nb - 1)
    bucket = jnp.where(rel > 0, nb, 0) + jnp.where(n < max_exact, n, large)
    out = jnp.zeros((LANES, LANES), F32)
    for bk in range(N_BUCKETS):
        out = jnp.where(bucket == bk, rb_ref[bk, h], out)
    o_ref[0, 0] = out * LOG2E


def _bias_tables(rel_bias):
    return pl.pallas_call(
        _bias_table_body,
        grid=(2, C_HEADS),
        in_specs=[pl.BlockSpec(memory_space=pltpu.SMEM)],
        out_specs=pl.BlockSpec((1, 1, LANES, LANES), lambda d, h: (d, h, 0, 0)),
        out_shape=jax.ShapeDtypeStruct((2, C_HEADS, LANES, LANES), F32),
        name="bias_tables",
    )(rel_bias)


def _transpose32(x):
    x = list(x)
    for s, msk in ((16, 0x0000FFFF), (8, 0x00FF00FF), (4, 0x0F0F0F0F), (2, 0x33333333), (1, 0x55555555)):
        sh = jnp.full(x[0].shape, s, I32)
        for i in range(32):
            if i & s == 0:
                t = (lax.shift_right_logical(x[i], sh) ^ x[i + s]) & msk
                x[i + s] = x[i + s] ^ t
                x[i] = x[i] ^ lax.shift_left(t, sh)
    return x


def _attn_body(qt_tab, kt_tab, *refs, mode, p0, t_valid, l_valid, tq, tk, dk, topk, nh):
    dsa = mode == "dsa"
    dh = LANES
    if dsa:
        (qT_ref, k_ref, vT_ref, q3_ref, ki3_ref, wiT_ref, tab_ref, far_ref,
         o_ref, m_scr, l_scr, acc_scr, s_scr, p_scr, mb_scr, skey_scr, thr_scr, planes_scr, e_scr) = refs
    else:
        (qT_ref, k_ref, vT_ref, o_ref, m_scr, l_scr, acc_scr, s_scr, p_scr, mb_scr) = refs
    qt = qt_tab[pl.program_id(1)]
    kt = kt_tab[pl.program_id(1)]
    q_lo = p0 + qt * tq
    q_hi = p0 + jnp.minimum(qt * tq + tq, t_valid) - 1
    n_allowed = jnp.minimum((q_hi // CHUNK + 1) * CHUNK, l_valid)
    last_kt = (n_allowed - 1) // tk
    qpos = q_lo + lax.broadcasted_iota(I32, (1, tq), 1)
    qchunk = qpos // CHUNK

    def allowed_mask(k0, rows):
        kidx = k0 + lax.broadcasted_iota(I32, (rows, tq), 0)
        return (kidx // CHUNK <= qchunk) & (kidx < l_valid)

    @pl.when(kt == 0)
    def _():
        m_scr[...] = jnp.full(m_scr.shape, NEG_BIG, F32)
        l_scr[...] = jnp.zeros(l_scr.shape, F32)
        acc_scr[...] = jnp.zeros(acc_scr.shape, F32)

    if dsa:
        iscale = (C_IDX_HEADS * C_IDX_DIM) ** -0.5

        @pl.when(kt == 0)
        def _():
            wpt = tk // 32
            kw = 3 * C_IDX_DIM

            def score_tile(j, carry):
                k0 = pl.multiple_of(j * tk, tk)
                ki3 = ki3_ref[0, pl.ds(k0, tk), :]
                s = jnp.zeros((tk, tq), F32)
                for h in range(C_IDX_HEADS):
                    d = jnp.dot(ki3, q3_ref[0, h * kw:(h + 1) * kw, :], preferred_element_type=F32)
                    s = s + jnp.maximum(d, 0.0) * wiT_ref[0, h:h + 1, :]
                s = s * iscale + 0.0
                bits = pltpu.bitcast(s, I32)
                key = bits ^ ((bits >> 31) & 0x7FFFFFFF)
                key = jnp.where(allowed_mask(k0, tk), key, INT_MIN)
                skey_scr[pl.ds(k0, tk), :] = key
                ukey = key ^ INT_MIN
                w0 = pl.multiple_of(j * wpt, SUBLANES)
                for g in range(tk // 256):
                    rows = [ukey[g * 256 + 8 * i:g * 256 + 8 * i + 8, :] for i in range(32)]
                    for bi, plane in enumerate(_transpose32(rows)):
                        planes_scr[bi, pl.ds(w0 + g * SUBLANES, SUBLANES), :] = plane
                ones = jnp.full((wpt, tq), -1, I32)
                planes_scr[32, pl.ds(w0, wpt), :] = ones
                e_scr[pl.ds(w0, wpt), :] = ones
                return carry

            def blank_tile(j, carry):
                w0 = pl.multiple_of(j * wpt, SUBLANES)
                for bi in range(33):
                    planes_scr[bi, pl.ds(w0, wpt), :] = jnp.zeros((wpt, tq), I32)
                e_scr[pl.ds(w0, wpt), :] = jnp.zeros((wpt, tq), I32)
                return carry

            ntile = last_kt + 1
            lax.fori_loop(0, ntile, score_tile, 0)
            nblk = (ntile + SEARCH_TILES - 1) // SEARCH_TILES
            lax.fori_loop(ntile, nblk * SEARCH_TILES, blank_tile, 0)
            wpb = SEARCH_TILES * wpt

            def lanesum(acc):
                return jnp.sum(acc, axis=0, keepdims=True)

            def fold(pc):
                return jnp.sum(pc.reshape(wpb // SUBLANES, SUBLANES, tq), axis=0)

            def settle(e, plane, take):
                t = e & plane
                return jnp.where(take != 0, t, e ^ t)

            def bit_step(i, st):
                c_gt, th, take_prev = st
                bi = 31 - i

                def body(j, acc):
                    w0 = pl.multiple_of(j * wpb, SUBLANES)
                    e = settle(e_scr[pl.ds(w0, wpb), :], planes_scr[bi + 1, pl.ds(w0, wpb), :], take_prev)
                    e_scr[pl.ds(w0, wpb), :] = e
                    return acc + fold(lax.population_count(e & planes_scr[bi, pl.ds(w0, wpb), :]))

                c1 = lanesum(lax.fori_loop(0, nblk, body, jnp.zeros((SUBLANES, tq), I32)))
                take = (c_gt + c1) >= topk
                return (jnp.where(take, c_gt, c_gt + c1),
                        jnp.where(take, th | lax.shift_left(jnp.int32(1), bi), th), jnp.where(take, 1, 0))

            zero = jnp.zeros((1, tq), I32)
            c_gt, th_u, take0 = lax.fori_loop(0, 32, bit_step, (zero, zero, zero + 1))

            def last_body(j, acc):
                w0 = pl.multiple_of(j * wpb, SUBLANES)
                e = settle(e_scr[pl.ds(w0, wpb), :], planes_scr[0, pl.ds(w0, wpb), :], take0)
                e_scr[pl.ds(w0, wpb), :] = e
                return acc + fold(lax.population_count(e))

            n_eq = lanesum(lax.fori_loop(0, nblk, last_body, jnp.zeros((SUBLANES, tq), I32)))
            need = topk - c_gt
            nbits = max(1, (l_valid - 1).bit_length())
            thr_scr[0:1, :] = th_u ^ INT_MIN
            thr_scr[1:2, :] = jnp.full((1, tq), (1 << nbits) - 1, I32)

            @pl.when(jnp.max((n_eq - need).astype(F32)) > 0.0)
            def _():
                def idx_step(i, jb):
                    cand = jb - lax.shift_left(jnp.int32(1), nbits - 1 - i)

                    def body(j, acc):
                        w0 = pl.multiple_of(j * wpb, SUBLANES)
                        wr = w0 + lax.broadcasted_iota(I32, (wpb, tq), 0)
                        base = (wr >> 3) * 256 + (wr & 7)
                        mx = (cand - base) >> 3
                        low = jnp.left_shift(2, jnp.clip(mx, 0, 30)) - 1
                        msk = jnp.where(mx < 0, 0, jnp.where(mx >= 31, -1, low))
                        return acc + fold(lax.population_count(e_scr[pl.ds(w0, wpb), :] & msk))

                    cnt = lanesum(lax.fori_loop(0, nblk, body, jnp.zeros((SUBLANES, tq), I32)))
                    return jnp.where(cnt >= need, cand, jb)

                thr_scr[1:2, :] = lax.fori_loop(0, nbits, idx_step, jnp.full((1, tq), (1 << nbits) - 1, I32))

    def tile(near):
        k0 = pl.multiple_of(kt * tk, tk)
        nchunk = tk // RC
        masked = dsa or near
        if masked:
            for c in range(nchunk):
                r0 = k0 + c * RC
                mask = allowed_mask(r0, RC) if near else None
                if dsa:
                    x = skey_scr[pl.ds(r0, RC), :]
                    kidx = r0 + lax.broadcasted_iota(I32, (RC, tq), 0)
                    sel = (x > thr_scr[0:1, :]) | ((x == thr_scr[0:1, :]) & (kidx <= thr_scr[1:2, :]))
                    mask = sel & mask if near else sel
                mb_scr[c * RC:(c + 1) * RC, :] = jnp.where(mask, 0.0, NEG_BIG)

        def chunk_bias(h, c):
            far = far_ref[0:1, h:h + 1]
            if not near:
                return far
            sb, off = (c * RC) // LANES, (c * RC) % LANES
            cols = []
            for qb in range(tq // LANES):
                delta = k0 + sb * LANES - (q_lo + qb * LANES)
                cols.append(jnp.where(delta == 0, tab_ref[0, h, off:off + RC, :],
                                      jnp.where(delta == -LANES, tab_ref[1, h, off:off + RC, :], far)))
            return jnp.concatenate(cols, axis=1) if len(cols) > 1 else cols[0]

        for h in range(nh):
            ks = slice(h * dk, (h + 1) * dk)
            s_scr[h] = jnp.dot(k_ref[0, :, ks], qT_ref[0, ks, :], preferred_element_type=F32)
        m_news, alphas = [], []
        for h in range(nh):
            mx = jnp.full((SUBLANES, tq), NEG_BIG, F32)
            for c in range(nchunk):
                rows = slice(c * RC, (c + 1) * RC)
                blk = s_scr[h, rows, :]
                if masked:
                    blk = blk + mb_scr[rows, :]
                    if dsa:
                        blk = blk + chunk_bias(h, c)
                    s_scr[h, rows, :] = blk
                mx = jnp.maximum(mx, jnp.max(blk.reshape(RC // SUBLANES, SUBLANES, tq), axis=0))
            m_prev = m_scr[h:h + 1, :]
            m_new = jnp.maximum(m_prev, jnp.max(mx, axis=0, keepdims=True))
            m_scr[h:h + 1, :] = m_new
            m_news.append(m_new)
            alphas.append(jnp.exp2(m_prev - m_new))
        for h in range(nh):
            for c in range(nchunk):
                rows = slice(c * RC, (c + 1) * RC)
                p_scr[h, rows, :] = jnp.exp2(s_scr[h, rows, :] - m_news[h]).astype(BF16)
        for h in range(nh):
            sl = slice(h * dh, (h + 1) * dh)
            pv = jnp.dot(vT_ref[0, h * VT_ROWS:(h + 1) * VT_ROWS, :], p_scr[h], preferred_element_type=F32)
            acc_scr[sl, :] = alphas[h] * acc_scr[sl, :] + pv[:dh]
            l_scr[h:h + 1, :] = alphas[h] * l_scr[h:h + 1, :] + pv[dh:dh + 1]

    is_far = kt * tk + tk - 1 <= q_lo - LANES

    @pl.when(is_far)
    def _():
        tile(False)

    @pl.when(jnp.logical_not(is_far))
    def _():
        tile(True)

    @pl.when(kt == last_kt)
    def _():
        for h in range(nh):
            sl = slice(h * dh, (h + 1) * dh)
            o = acc_scr[sl, :] / l_scr[h:h + 1, :]
            o_ref[0, :, sl] = o.T.astype(o_ref.dtype)


def _attention(mode, qT, k, vT, extra, *, p0, t_valid, l_valid, tk):
    b, hdk, tpad = qT.shape
    lpad = k.shape[1]
    nh = vT.shape[1] // VT_ROWS
    hd = nh * LANES
    dk = hdk // nh
    tq = 2 * LANES if tpad % (2 * LANES) == 0 else LANES
    nq, nk = tpad // tq, lpad // tk
    topk = min(TOPK_MAX, l_valid // 4)
    assert p0 % LANES == 0 and lpad % tk == 0 and tk % 256 == 0 and l_valid >= topk >= 1 and tk >= topk

    pairs = []
    for qt in range(nq):
        q_hi = p0 + min(qt * tq + tq, t_valid) - 1
        last = (min((q_hi // CHUNK + 1) * CHUNK, l_valid) - 1) // tk
        pairs += [(qt, kt) for kt in range(last + 1)]
    qt_tab = jnp.asarray([p[0] for p in pairs], I32)
    kt_tab = jnp.asarray([p[1] for p in pairs], I32)

    kmap = lambda i, s, qtt, ktt: (i, ktt[s], 0)
    vmap_ = lambda i, s, qtt, ktt: (i, 0, ktt[s])
    qmap = lambda i, s, qtt, ktt: (i, 0, qtt[s])
    cmap = lambda n: (lambda i, s, qtt, ktt: (0,) * n)
    in_specs = [pl.BlockSpec((1, hdk, tq), qmap),
                pl.BlockSpec((1, tk, hdk), kmap),
                pl.BlockSpec((1, nh * VT_ROWS, tk), vmap_)]
    scratch = [pltpu.VMEM((nh, tq), F32), pltpu.VMEM((nh, tq), F32), pltpu.VMEM((hd, tq), F32),
               pltpu.VMEM((nh, tk, tq), F32), pltpu.VMEM((nh, tk, tq), BF16), pltpu.VMEM((tk, tq), F32)]
    if mode == "dsa":
        q3, ki3, wiT, tabs, far = extra
        wrows = -(-nk // SEARCH_TILES) * SEARCH_TILES * (tk // 32)
        in_specs += [pl.BlockSpec((1, q3.shape[1], tq), qmap),
                     pl.BlockSpec((1, lpad, ki3.shape[2]), lambda i, s, qtt, ktt: (i, 0, 0)),
                     pl.BlockSpec((1, C_IDX_HEADS, tq), qmap),
                     pl.BlockSpec(tabs.shape, cmap(4)),
                     pl.BlockSpec(far.shape, cmap(2))]
        scratch += [pltpu.VMEM((lpad, tq), I32), pltpu.VMEM((SUBLANES, tq), I32),
                    pltpu.VMEM((33, wrows, tq), I32), pltpu.VMEM((wrows, tq), I32)]
    return pl.pallas_call(
        functools.partial(_attn_body, mode=mode, p0=p0, t_valid=t_valid, l_valid=l_valid, tq=tq, tk=tk,
                          dk=dk, topk=topk, nh=nh),
        grid_spec=pltpu.PrefetchScalarGridSpec(
            num_scalar_prefetch=2,
            grid=(b, len(pairs)),
            in_specs=in_specs,
            out_specs=pl.BlockSpec((1, tq, hd), lambda i, s, qtt, ktt: (i, qtt[s], 0)),
            scratch_shapes=scratch),
        out_shape=jax.ShapeDtypeStruct((b, tpad, hd), BF16),
        compiler_params=_cparams(("parallel", "arbitrary")),
        name="attn_" + mode,
    )(qt_tab, kt_tab, qT, k, vT, *extra)


PACK_TILE = 256


def _pack_body(cache_ref, new_ref, o_ref, *, n_cache, transpose, nh):
    j = pl.program_id(1)

    def emit(head):
        for h in range(nh):
            sl = slice(h * LANES, (h + 1) * LANES)
            x = head(h, sl)
            if transpose:
                r0 = h * VT_ROWS
                o_ref[0, r0:r0 + LANES, :] = x.T.astype(o_ref.dtype)
                o_ref[0, r0 + LANES:r0 + VT_ROWS, :] = jnp.ones((VT_ONES, x.shape[0]), o_ref.dtype)
            else:
                o_ref[0, :, sl] = x.astype(o_ref.dtype)

    @pl.when(j < n_cache)
    def _():
        tp = o_ref.shape[2] if transpose else o_ref.shape[1]
        emit(lambda h, sl: cache_ref[0, 0, pl.ds(h, tp, stride=nh), :])

    @pl.when(j >= n_cache)
    def _():
        emit(lambda h, sl: new_ref[0, :, sl])


def _pack_keys(cache, layer, new, lpad, *, transpose):
    _, b, past, nh, dh = cache.shape
    tp = 2 * PACK_TILE if (past % (2 * PACK_TILE) == 0 and lpad % (2 * PACK_TILE) == 0) else PACK_TILE
    assert dh == LANES and past % tp == 0 and lpad % tp == 0
    n_cache = past // tp
    hd = nh * dh
    new = _pad_axis(new, 1, lpad - past)
    out_shape = (b, nh * VT_ROWS, lpad) if transpose else (b, lpad, hd)
    out_spec = (pl.BlockSpec((1, nh * VT_ROWS, tp), lambda i, j: (i, 0, j)) if transpose
                else pl.BlockSpec((1, tp, hd), lambda i, j: (i, j, 0)))
    return pl.pallas_call(
        functools.partial(_pack_body, n_cache=n_cache, transpose=transpose, nh=nh),
        grid=(b, lpad // tp),
        in_specs=[pl.BlockSpec((1, 1, tp * nh, dh),
                               lambda i, j: (layer, i, jnp.minimum(j, n_cache - 1), 0)),
                  pl.BlockSpec((1, tp, hd), lambda i, j: (i, jnp.maximum(j - n_cache, 0), 0))],
        out_specs=out_spec,
        out_shape=jax.ShapeDtypeStruct(out_shape, BF16),
        compiler_params=_cparams(("parallel", "parallel")),
        name="pack_keys",
    )(cache.reshape(cache.shape[0], b, past * nh, dh), new)


def _split_hi_lo(x):
    hi = x.astype(BF16)
    lo = (x - hi.astype(F32)).astype(BF16)
    return hi, lo


def _pad_axis(x, axis, size):
    if x.shape[axis] == size:
        return x
    pad = [(0, 0)] * x.ndim
    pad[axis] = (0, size - x.shape[axis])
    return jnp.pad(x, pad)


def _vt_ones(v):
    b, l, hd = v.shape
    vt = jnp.swapaxes(v, 1, 2).reshape(b, hd // LANES, LANES, l)
    ones = jnp.ones((b, hd // LANES, VT_ONES, l), v.dtype)
    return jnp.concatenate([vt, ones], axis=2).reshape(b, -1, l)


def _key_tile(l_valid):
    tk = 512 if l_valid > 1024 else 256
    return -(-l_valid // tk) * tk, tk


def _prep_weights(W, d_model):
    f = {}
    wd = A_WIDTH
    n_even = W["ev_w_in"].shape[0]
    n_odd = W["od_w_in"].shape[0]
    f["even"] = []
    for e in range(n_even):
        wi = W["ev_w_in"][e]
        w_in = jnp.concatenate([wi[:, :3 * wd], wi[:, A_COLS:], wi[:, 3 * wd:A_COLS],
                                jnp.zeros((d_model, LORA_PAD - A_LORA), F32)], axis=1).astype(BF16)
        mu = W["rwkv_mu"][e]
        w2 = jnp.zeros((LANES, wd), F32).at[:A_DECAY_LORA].set(W["rwkv_w2"][e])
        a2 = jnp.zeros((LANES, wd), F32).at[A_DECAY_LORA:A_DECAY_LORA + A_ICLR_LORA].set(W["rwkv_a2"][e])
        g2 = jnp.zeros((LORA_PAD - LANES, wd), F32).at[:A_GATE_LORA].set(W["rwkv_g2"][e])
        hm = lambda v: v.reshape(A_HEADS, 1, A_HEAD_DIM)
        rw = dict(mu_r=mu[:3 * wd].reshape(1, -1),
                  mu_l=_pad_axis(mu[3 * wd:], 0, LORA_PAD).reshape(1, -1),
                  w0=W["rwkv_w0"][e].reshape(1, -1), w2=w2.astype(BF16),
                  a0=W["rwkv_a0"][e].reshape(1, -1), a2=a2.astype(BF16), g2=g2.astype(BF16),
                  k_k=W["rwkv_k_k"][e].reshape(1, -1), k_a=W["rwkv_k_a"][e].reshape(1, -1),
                  r_k=hm(W["rwkv_r_k"][e]), ln_w=hm(W["rwkv_ln_w"][e]), ln_b=hm(W["rwkv_ln_b"][e]))
        eye = jnp.eye(B_BLOCKS, dtype=F32)
        blockdiag = lambda w: (eye[:, None, :, None] * w[:, :, None, :]).reshape(B_WIDTH, B_WIDTH)
        wgate = jnp.concatenate([blockdiag(W["lru_wa"][e]), blockdiag(W["lru_wx"][e])], axis=1).astype(BF16)
        bgate = jnp.concatenate([W["lru_ba"][e], W["lru_bx"][e]])
        wo = W["ev_w_out"][e].astype(BF16)
        f["even"].append(dict(w_in=w_in, rw=rw, wgate=wgate, bgate=bgate, wo_a=wo[:wd], wo_b=wo[wd:],
                              cw=W["lru_conv_w"][e], cb=W["lru_conv_b"][e], lam=W["lru_lambda"][e]))
    f["odd"] = []
    cw_ = C_WIDTH
    qi_w = C_IDX_HEADS * C_IDX_DIM
    for o in range(n_odd):
        wi = W["od_w_in"][o]
        offs = [0]
        for s in (cw_, cw_, cw_, qi_w, C_IDX_DIM, C_IDX_HEADS, D_Q_RANK, D_KV_RANK, D_ROPE):
            offs.append(offs[-1] + s)
        q, k, v, qi, ki, wi_, qd, kvd, kr = [wi[:, offs[i]:offs[i + 1]] for i in range(9)]
        small = jnp.concatenate([ki, kr, wi_], axis=1)
        w_in = jnp.concatenate([q, k, v, qi, qd, kvd, _pad_axis(small, 1, 512)], axis=1).astype(BF16)
        wuq = W["mla_w_uq"][o].reshape(D_Q_RANK, D_HEADS, D_NOPE + D_ROPE)
        wuq = jnp.concatenate([wuq[:, :, :D_NOPE].reshape(D_Q_RANK, -1),
                               wuq[:, :, D_NOPE:].reshape(D_Q_RANK, -1)], axis=1).astype(BF16)
        wukv = W["mla_w_ukv"][o].reshape(D_KV_RANK, D_HEADS, D_NOPE + D_V)
        wuk = _pad_axis(wukv[:, :, :D_NOPE], 2, MLA_DK).reshape(D_KV_RANK, -1).astype(BF16)
        wuv = wukv[:, :, D_NOPE:].reshape(D_KV_RANK, -1).astype(BF16)
        wo = W["od_w_out"][o].astype(BF16)
        f["odd"].append(dict(w_in=w_in, wuq=wuq, wuk=wuk, wuv=wuv, wo_c=wo[:cw_], wo_d=wo[cw_:],
                             gq=W["mla_q_norm"][o], gkv=W["mla_kv_norm"][o]))
    f["ffn"] = dict(wg=W["ffn_w_gate"].astype(BF16), wu=W["ffn_w_up"].astype(BF16),
                    wd=W["ffn_w_down"].astype(BF16))
    f["tabs"] = _bias_tables(W["rel_bias"])
    nb = N_BUCKETS // 2
    f["far"] = _pad_axis(W["rel_bias"][nb - 1:nb, :] * LOG2E, 1, LANES)
    return f


def _mixer_even(x2, b, t, p0, shift, s0, h0, cbuf, fe, norm_g):
    wd = A_WIDTH
    p = _norm_matmul(x2, norm_g, fe["w_in"])
    p3 = p.reshape(b, t, -1)
    lora_col = 3 * wd + 2 * B_WIDTH
    shift_r = shift[:, None, :3 * wd]
    shift_l = _pad_axis(shift[:, None, 3 * wd:], 2, LORA_PAD)
    ya, s_new = _rwkv(p3, lora_col // LORA_PAD, shift_r, shift_l, s0, fe["rw"])
    yb, h_new, c_new = _rglru(p3, 3, 4, fe["cw"], fe["cb"], fe["wgate"], fe["bgate"], fe["lam"],
                              h0, cbuf, p0=p0)
    x2 = _matmul([ya.reshape(b * t, wd), yb.reshape(b * t, B_WIDTH)], [fe["wo_a"], fe["wo_b"]], res=x2)
    last = p3[:, t - 1]
    new_shift = jnp.concatenate([last[:, :3 * wd], last[:, lora_col:lora_col + A_LORA]], axis=-1)
    return x2, new_shift, s_new, h_new[:, 0], c_new


def _mixer_odd(x2, b, t, p0, o, dsa_k, dsa_v, cik, clat, ckr, fo, f, norm_g, cos, sin):
    cw_ = C_WIDTH
    nt = cw_ // 512
    segs = [(0, nt, BF16, C_HEAD_DIM ** -0.5 * LOG2E), (nt, 2 * nt, F32, 1.0), (nt, 2 * nt, BF16, 1.0),
            (2 * nt, 3 * nt, F32, 1.0), (2 * nt, 3 * nt, BF16, 1.0), (3 * nt, 3 * nt + 4, F32, 1.0)]
    q_bf, k_f, k_bf, v_f, v_bf, rest = [z.reshape(b, t, -1) for z in
                                        _norm_matmul(x2, norm_g, fo["w_in"], segs)]
    past = dsa_k.shape[2]
    l_valid = past + t
    lpad, tk = _key_tile(l_valid)
    tpad = -(-t // LANES) * LANES
    k_new, v_new = k_f, v_f
    qi = rest[..., :512]
    small = rest[..., 1536:]
    ki_new = small[..., :C_IDX_DIM]
    wi = small[..., 2 * C_IDX_DIM:2 * C_IDX_DIM + C_IDX_HEADS]

    def keys(cache, new):
        allk = jnp.concatenate([cache.reshape(b, past, -1), new], axis=1) if past else new
        return _pad_axis(allk, 1, lpad)

    tq_ = lambda z: _pad_axis(jnp.swapaxes(z, 1, 2), 2, tpad)
    if past and past % PACK_TILE == 0:
        k_all = _pack_keys(dsa_k, o, k_new, lpad, transpose=False)
        vT_all = _pack_keys(dsa_v, o, v_new, lpad, transpose=True)
    elif past:
        k_all = keys(dsa_k[o], k_new).astype(BF16)
        vT_all = _vt_ones(keys(dsa_v[o], v_new).astype(BF16))
    else:
        k_all = _pad_axis(k_bf, 1, lpad)
        vT_all = _vt_ones(_pad_axis(v_bf, 1, lpad))
    kih, kil = _split_hi_lo(keys(cik, ki_new))
    ki3 = jnp.concatenate([kih, kih, kil], axis=-1)
    qih, qil = _split_hi_lo(tq_(qi).reshape(b, C_IDX_HEADS, C_IDX_DIM, tpad))
    q3 = jnp.concatenate([qih, qil, qih], axis=2).reshape(b, 3 * C_IDX_HEADS * C_IDX_DIM, tpad)
    yc = _attention("dsa", tq_(q_bf), k_all, vT_all,
                    (q3, ki3, tq_(wi), f["tabs"], f["far"]),
                    p0=p0, t_valid=t, l_valid=l_valid, tk=tk)[:, :t]
    qn, qr, lat, krope = _mla_prep(rest, 1, 2, 3, fo["gq"], fo["gkv"], fo["wuq"], cos, sin)
    lat_all = keys(clat, lat).reshape(b * lpad, -1)
    kr_all = keys(ckr, krope).reshape(b * lpad, -1)
    eye = jnp.eye(D_ROPE, dtype=BF16)
    ident = jnp.tile(jnp.pad(eye, ((0, 0), (D_NOPE, MLA_DK - D_NOPE - D_ROPE))), (1, D_HEADS))
    k_full = _matmul([lat_all, kr_all], [fo["wuk"], ident], out_dtype=BF16).reshape(b, lpad, -1)
    v_all = _matmul([lat_all], [fo["wuv"]], out_dtype=BF16).reshape(b, lpad, -1)
    q_full = jnp.concatenate([qn.reshape(b, t, D_HEADS, D_NOPE), qr.reshape(b, t, D_HEADS, D_ROPE),
                              jnp.zeros((b, t, D_HEADS, MLA_DK - D_NOPE - D_ROPE), BF16)], axis=-1)
    yd = _attention("mla", tq_(q_full.reshape(b, t, -1)), k_full, _vt_ones(v_all), (),
                    p0=p0, t_valid=t, l_valid=l_valid, tk=tk)[:, :t]
    x2 = _matmul([yc.reshape(b * t, cw_), yd.reshape(b * t, -1)], [fo["wo_c"], fo["wo_d"]], res=x2)
    return (x2, k_new.reshape(b, t, C_HEADS, C_HEAD_DIM), v_new.reshape(b, t, C_HEADS, C_HEAD_DIM),
            ki_new, lat, krope)


def _trunk(x, p0, shift, rwkv_s, lru_h, lru_conv, dsa_k, dsa_v, dsa_ik, mla_lat, mla_kr, W, f):
    b, t, d = x.shape
    depth = W["norm_mix"].shape[0]
    pos = (p0 + jnp.arange(t)).astype(F32)
    inv = ROPE_BASE ** (-jnp.arange(0, D_ROPE, 2, dtype=F32) / D_ROPE)
    ang = pos[:, None] * inv[None, :]
    cos = jnp.tile(jnp.cos(ang), (1, 2 * D_HEADS))
    sin = jnp.tile(jnp.sin(ang), (1, 2 * D_HEADS))
    x2 = x.reshape(b * t, d)
    ev = [[] for _ in range(4)]
    od = [[] for _ in range(5)]
    for layer in range(depth):
        if layer % 2 == 0:
            e = layer // 2
            x2, *outs = _mixer_even(x2, b, t, p0, shift[e], rwkv_s[e], lru_h[e], lru_conv[e],
                                    f["even"][e], W["norm_mix"][layer])
            for lst, o_ in zip(ev, outs):
                lst.append(o_)
        else:
            o = layer // 2
            x2, *outs = _mixer_odd(x2, b, t, p0, o, dsa_k, dsa_v, dsa_ik[o], mla_lat[o], mla_kr[o],
                                   f["odd"][o], f, W["norm_mix"][layer], cos, sin)
            for lst, o_ in zip(od, outs):
                lst.append(o_)
        ff = f["ffn"]
        x2 = _ffn(x2, W["norm_ffn"][layer], ff["wg"], ff["wu"], ff["wd"], layer, W["final_norm"],
                  final_norm=layer == depth - 1)
    return (x2.reshape(b, t, d),) + tuple(jnp.stack(v) for v in ev) + tuple(jnp.stack(v) for v in od)


def kernel(x_prompt, x_sample, state_rwkv_shift, state_rwkv, state_lru, state_lru_conv, cache_dsa_k, cache_dsa_v, cache_dsa_idx_k, cache_mla_latent, cache_mla_krope, rel_bias, final_norm, norm_mix, norm_ffn, ffn_w_gate, ffn_w_up, ffn_w_down, ev_w_in, ev_w_out, rwkv_mu, rwkv_w0, rwkv_w2, rwkv_a0, rwkv_a2, rwkv_g2, rwkv_k_k, rwkv_k_a, rwkv_r_k, rwkv_ln_w, rwkv_ln_b, lru_conv_w, lru_conv_b, lru_wa, lru_ba, lru_wx, lru_bx, lru_lambda, od_w_in, od_w_out, mla_q_norm, mla_w_uq, mla_kv_norm, mla_w_ukv):
    W = dict(rel_bias=rel_bias, final_norm=final_norm, norm_mix=norm_mix, norm_ffn=norm_ffn,
             ffn_w_gate=ffn_w_gate, ffn_w_up=ffn_w_up, ffn_w_down=ffn_w_down,
             ev_w_in=ev_w_in, ev_w_out=ev_w_out, rwkv_mu=rwkv_mu, rwkv_w0=rwkv_w0,
             rwkv_w2=rwkv_w2, rwkv_a0=rwkv_a0, rwkv_a2=rwkv_a2, rwkv_g2=rwkv_g2,
             rwkv_k_k=rwkv_k_k, rwkv_k_a=rwkv_k_a, rwkv_r_k=rwkv_r_k, rwkv_ln_w=rwkv_ln_w,
             rwkv_ln_b=rwkv_ln_b, lru_conv_w=lru_conv_w, lru_conv_b=lru_conv_b, lru_wa=lru_wa,
             lru_ba=lru_ba, lru_wx=lru_wx, lru_bx=lru_bx, lru_lambda=lru_lambda,
             od_w_in=od_w_in, od_w_out=od_w_out, mla_q_norm=mla_q_norm, mla_w_uq=mla_w_uq,
             mla_kv_norm=mla_kv_norm, mla_w_ukv=mla_w_ukv)
    d_model = x_prompt.shape[-1]
    f = _prep_weights(W, d_model)
    bp = x_prompt.shape[0]
    n_even, n_odd = ev_w_in.shape[0], od_w_in.shape[0]
    dt = x_prompt.dtype
    z = lambda *shape: jnp.zeros(shape, dt)
    outs_p = _trunk(
        x_prompt, 0,
        z(n_even, bp, A_COLS), z(n_even, bp, A_HEADS, A_HEAD_DIM, A_HEAD_DIM),
        z(n_even, bp, B_WIDTH), z(n_even, bp, B_CONV - 1, B_WIDTH),
        z(n_odd, bp, 0, C_HEADS, C_HEAD_DIM), z(n_odd, bp, 0, C_HEADS, C_HEAD_DIM),
        z(n_odd, bp, 0, C_IDX_DIM), z(n_odd, bp, 0, D_KV_RANK), z(n_odd, bp, 0, D_ROPE), W, f)
    past = cache_dsa_k.shape[2]
    outs_s = _trunk(x_sample, past, state_rwkv_shift, state_rwkv, state_lru, state_lru_conv,
                    cache_dsa_k, cache_dsa_v, cache_dsa_idx_k, cache_mla_latent, cache_mla_krope, W, f)
    return (outs_p[0], outs_s[0]) + tuple(outs_p[1:]) + tuple(outs_s[1:])
```

```python
import functools
import math

import jax
import jax.numpy as jnp
from jax import lax
from jax.experimental import pallas as pl
from jax.experimental.pallas import tpu as pltpu

F32 = jnp.float32
BF16 = jnp.bfloat16
I32 = jnp.int32

CHUNK = 64
NORM_EPS = 1e-6
A_HEADS = 16
A_HEAD_DIM = 64
A_WIDTH = A_HEADS * A_HEAD_DIM
A_DECAY_LORA = 64
A_ICLR_LORA = 64
A_GATE_LORA = 160
A_LORA = A_DECAY_LORA + A_ICLR_LORA + A_GATE_LORA
A_COLS = 3 * A_WIDTH + A_LORA
A_LN_EPS = 64e-5
B_WIDTH = 1024
B_BLOCKS = 16
B_CONV = 4
B_C = 8.0
C_HEADS = 8
C_HEAD_DIM = 128
C_WIDTH = C_HEADS * C_HEAD_DIM
C_IDX_HEADS = 8
C_IDX_DIM = 64
TOPK_MAX = 256
D_HEADS = 8
D_NOPE = 128
D_ROPE = 64
D_V = 128
D_Q_RANK = 512
D_KV_RANK = 512
ROPE_BASE = 10000.0
N_BUCKETS = 32
MAX_DISTANCE = 128

LANES = 128
SUBLANES = 8
VMEM_LIMIT = 56 * 1024 * 1024
LORA_PAD = 512
NEG_BIG = -1e30
INT_MIN = -2147483648
RC = 64
SEARCH_TILES = 4
VT_ONES = 16
VT_ROWS = LANES + VT_ONES
LOG2E = 1.4426950408889634
MLA_DK = 256

NN = (((1,), (0,)), ((), ()))
NT = (((1,), (1,)), ((), ()))
BNN = (((2,), (1,)), ((0,), (0,)))
BNT = (((2,), (2,)), ((0,), (0,)))
BTN = (((1,), (1,)), ((0,), (0,)))


def _cparams(sem):
    return pltpu.CompilerParams(dimension_semantics=sem, vmem_limit_bytes=VMEM_LIMIT)


def _rms(x, g, eps=NORM_EPS):
    ms = jnp.mean(x * x, axis=-1, keepdims=True)
    return x * lax.rsqrt(ms + eps) * g


def _softplus(x):
    return jnp.maximum(x, 0.0) + jnp.log1p(jnp.exp(-jnp.abs(x)))


def _dotp(a, b, dims, passes):
    if passes == 6:
        return lax.dot_general(a, b, dims, precision=lax.Precision.HIGHEST, preferred_element_type=F32)
    ah = a.astype(BF16)
    bh = b.astype(BF16)
    out = lax.dot_general(ah, bh, dims, preferred_element_type=F32)
    if passes == 3:
        al = (a - ah.astype(F32)).astype(BF16)
        bl = (b - bh.astype(F32)).astype(BF16)
        out = out + lax.dot_general(ah, bl, dims, preferred_element_type=F32)
        out = out + lax.dot_general(al, bh, dims, preferred_element_type=F32)
    return out


def _norm_matmul_body(x_ref, g_ref, w_ref, *refs, segs):
    outs, xn_ref = refs[:-1], refs[-1]
    j = pl.program_id(1)

    @pl.when(j == 0)
    def _():
        xn_ref[...] = _rms(x_ref[...], g_ref[...]).astype(BF16)

    acc = jnp.dot(xn_ref[...], w_ref[...], preferred_element_type=F32)
    for o_ref, (lo, hi, _, scale) in zip(outs, segs):
        @pl.when((j >= lo) & (j < hi))
        def _(o_ref=o_ref, scale=scale):
            o_ref[...] = (acc if scale == 1.0 else acc * scale).astype(o_ref.dtype)


def _row_tile(m, cap):
    tm = cap
    while m % tm:
        tm //= 2
    return tm


def _norm_matmul(x, g, w, segs=None, *, tn=512):
    m, k = x.shape
    n = w.shape[1]
    tm = _row_tile(m, 1024)
    if segs is None and n % (11 * LANES) == 0:
        tn = 11 * LANES
    segs = segs or [(0, n // tn, F32, 1.0)]
    out_specs = [pl.BlockSpec((tm, tn), lambda i, j, lo=lo, nb=hi - lo: (i, jnp.clip(j - lo, 0, nb - 1)))
                 for lo, hi, _, _ in segs]
    outs = pl.pallas_call(
        functools.partial(_norm_matmul_body, segs=tuple(segs)),
        grid=(m // tm, n // tn),
        in_specs=[pl.BlockSpec((tm, k), lambda i, j: (i, 0)),
                  pl.BlockSpec((1, k), lambda i, j: (0, 0)),
                  pl.BlockSpec((k, tn), lambda i, j: (0, j))],
        out_specs=out_specs,
        out_shape=[jax.ShapeDtypeStruct((m, (hi - lo) * tn), dt) for lo, hi, dt, _ in segs],
        scratch_shapes=[pltpu.VMEM((tm, k), BF16)],
        compiler_params=_cparams(("parallel", "arbitrary")),
        name="norm_matmul",
    )(x, g.reshape(1, k), w)
    return outs if len(segs) > 1 else outs[0]


def _mm_body(*refs, n_lhs, has_res):
    o_ref = refs[-1]
    acc = refs[2 * n_lhs][...] if has_res else None
    for a_ref, w_ref in zip(refs[:n_lhs], refs[n_lhs:2 * n_lhs]):
        d = jnp.dot(a_ref[...].astype(BF16), w_ref[...], preferred_element_type=F32)
        acc = d if acc is None else acc + d
    o_ref[...] = acc.astype(o_ref.dtype)


def _matmul(lhs_list, w_list, res=None, *, out_dtype=F32):
    m = lhs_list[0].shape[0]
    n = w_list[0].shape[1]
    tm = _row_tile(m, 1024)
    tn = 1024 if n % 1024 == 0 else 512
    in_specs = [pl.BlockSpec((tm, a.shape[1]), lambda i, j: (i, 0)) for a in lhs_list]
    in_specs += [pl.BlockSpec((w.shape[0], tn), lambda i, j: (0, j)) for w in w_list]
    args = list(lhs_list) + list(w_list)
    if res is not None:
        in_specs.append(pl.BlockSpec((tm, tn), lambda i, j: (i, j)))
        args.append(res)
    return pl.pallas_call(
        functools.partial(_mm_body, n_lhs=len(lhs_list), has_res=res is not None),
        grid=(m // tm, n // tn),
        in_specs=in_specs,
        out_specs=pl.BlockSpec((tm, tn), lambda i, j: (i, j)),
        out_shape=jax.ShapeDtypeStruct((m, n), out_dtype),
        compiler_params=_cparams(("parallel", "arbitrary")),
        name="matmul",
    )(*args)


def _ffn_body(x_ref, g_ref, wg_ref, wu_ref, wd_ref, gf_ref, o_ref, xn_ref, acc_ref, *, final_norm):
    f = pl.program_id(1)

    @pl.when(f == 0)
    def _():
        xn_ref[...] = _rms(x_ref[...], g_ref[...]).astype(BF16)
        acc_ref[...] = jnp.zeros_like(acc_ref)

    xn = xn_ref[...]
    hg = jnp.dot(xn, wg_ref[0], preferred_element_type=F32)
    hu = jnp.dot(xn, wu_ref[0], preferred_element_type=F32)
    h = hg * jax.nn.sigmoid(hg) * hu
    acc_ref[...] += jnp.dot(h.astype(BF16), wd_ref[0], preferred_element_type=F32)

    @pl.when(f == pl.num_programs(1) - 1)
    def _():
        y = x_ref[...] + acc_ref[...]
        if final_norm:
            y = _rms(y, gf_ref[...])
        o_ref[...] = y


def _ffn(x, g, wg, wu, wd, layer, gf, *, final_norm, tf=512):
    m, k = x.shape
    dff = wg.shape[2]
    tm = _row_tile(m, 512)
    return pl.pallas_call(
        functools.partial(_ffn_body, final_norm=final_norm),
        grid=(m // tm, dff // tf),
        in_specs=[pl.BlockSpec((tm, k), lambda i, f: (i, 0)),
                  pl.BlockSpec((1, k), lambda i, f: (0, 0)),
                  pl.BlockSpec((1, k, tf), lambda i, f: (layer, 0, f)),
                  pl.BlockSpec((1, k, tf), lambda i, f: (layer, 0, f)),
                  pl.BlockSpec((1, tf, k), lambda i, f: (layer, f, 0)),
                  pl.BlockSpec((1, k), lambda i, f: (0, 0))],
        out_specs=pl.BlockSpec((tm, k), lambda i, f: (i, 0)),
        out_shape=jax.ShapeDtypeStruct((m, k), F32),
        scratch_shapes=[pltpu.VMEM((tm, k), BF16), pltpu.VMEM((tm, k), F32)],
        compiler_params=_cparams(("parallel", "arbitrary")),
        name="ffn",
    )(x, g.reshape(1, k), wg, wu, wd, gf.reshape(1, k))


def _lru_body(gate_ref, xb_ref, cw_ref, cb_ref, wg_ref, bg_ref, lam_ref, h0_ref, cbuf_ref,
              y_ref, hout_ref, cout_ref, xbuf, hcar, *, tt, p0):
    t = pl.program_id(1)
    w = B_WIDTH
    halo = SUBLANES

    @pl.when(t == 0)
    def _():
        xbuf[0:halo, :] = jnp.zeros((halo, w), F32)
        xbuf[halo - (B_CONV - 1):halo, :] = cbuf_ref[0]
        hcar[...] = h0_ref[0]

    xb = xb_ref[0]
    xbuf[halo:halo + tt, :] = xb
    xc = cb_ref[...] + cw_ref[B_CONV - 1:B_CONV, :] * xb
    for j in range(B_CONV - 1):
        off = halo - (B_CONV - 1) + j
        xc = xc + cw_ref[j:j + 1, :] * xbuf[off:off + tt, :]
    tail = xbuf[tt + halo - (B_CONV - 1):tt + halo, :]
    xbuf[halo - (B_CONV - 1):halo, :] = tail

    pre = jnp.dot(xc.astype(BF16), wg_ref[...], preferred_element_type=F32) + bg_ref[...]
    rg = jax.nn.sigmoid(pre[:, :w])
    ig = jax.nn.sigmoid(pre[:, w:])
    log_a = (-B_C) * rg * _softplus(-lam_ref[...])
    a = jnp.exp(log_a)
    row = lax.broadcasted_iota(I32, (tt, w), 0)
    th = jnp.tanh(log_a)
    mult = jnp.sqrt(-2.0 * th / (1.0 - th))
    mult = jnp.where(row + (p0 + t * tt) == 0, 1.0, mult)
    u = mult * (ig * xc)

    d = 1
    while d < tt:
        keep = row >= d
        a_sh = pltpu.roll(a, d, 0)
        u_sh = pltpu.roll(u, d, 0)
        u = u + jnp.where(keep, a * u_sh, 0.0)
        a = jnp.where(keep, a * a_sh, a)
        d *= 2
    h = u + a * hcar[...]
    hcar[...] = h[tt - 1:tt, :]
    y_ref[0] = (h * jax.nn.gelu(gate_ref[0])).astype(y_ref.dtype)

    @pl.when(t == pl.num_programs(1) - 1)
    def _():
        hout_ref[0] = h[tt - 1:tt, :]
        cout_ref[0] = tail


def _rglru(p3, gate_blk, xb_blk, cw, cb, wgate, bgate, lam, h0, cbuf, *, p0):
    b, t, _ = p3.shape
    w = B_WIDTH
    tt = min(256, t)
    row = lambda v: v.reshape(1, -1)
    return pl.pallas_call(
        functools.partial(_lru_body, tt=tt, p0=p0),
        grid=(b, t // tt),
        in_specs=[pl.BlockSpec((1, tt, w), lambda i, j: (i, j, gate_blk)),
                  pl.BlockSpec((1, tt, w), lambda i, j: (i, j, xb_blk)),
                  pl.BlockSpec((B_CONV, w), lambda i, j: (0, 0)),
                  pl.BlockSpec((1, w), lambda i, j: (0, 0)),
                  pl.BlockSpec((w, 2 * w), lambda i, j: (0, 0)),
                  pl.BlockSpec((1, 2 * w), lambda i, j: (0, 0)),
                  pl.BlockSpec((1, w), lambda i, j: (0, 0)),
                  pl.BlockSpec((1, 1, w), lambda i, j: (i, 0, 0)),
                  pl.BlockSpec((1, B_CONV - 1, w), lambda i, j: (i, 0, 0))],
        out_specs=[pl.BlockSpec((1, tt, w), lambda i, j: (i, j, 0)),
                   pl.BlockSpec((1, 1, w), lambda i, j: (i, 0, 0)),
                   pl.BlockSpec((1, B_CONV - 1, w), lambda i, j: (i, 0, 0))],
        out_shape=[jax.ShapeDtypeStruct((b, t, w), BF16),
                   jax.ShapeDtypeStruct((b, 1, w), F32),
                   jax.ShapeDtypeStruct((b, B_CONV - 1, w), F32)],
        scratch_shapes=[pltpu.VMEM((tt + SUBLANES, w), F32), pltpu.VMEM((1, w), F32)],
        compiler_params=_cparams(("parallel", "arbitrary")),
        name="rglru",
    )(p3, p3, cw, row(cb), wgate, row(bgate), row(lam), h0.reshape(b, 1, w), cbuf)


RWKV_PASSES = 1
RWKV_PASSES_SOLVE = 3


def _rwkv_body(rkv_ref, lora_ref, shr_ref, shl_ref, s0_ref, mur_ref, mul_ref, w0_ref, w2_ref, a0_ref,
               a2_ref, g2_ref, kk_ref, ka_ref, rk_ref, lnw_ref, lnb_ref, tri_ref,
               y_ref, sout_ref, buf_r, buf_l, s_scr, st_r, st_k, st_v, st_q, st_a, st_l, st_d, *, c):
    ci = pl.program_id(1)
    halo = SUBLANES
    hn, n, wd = A_HEADS, A_HEAD_DIM, A_WIDTH
    mm = functools.partial(_dotp, passes=RWKV_PASSES)

    @pl.when(ci == 0)
    def _():
        buf_r[halo - 1:halo, :] = shr_ref[0]
        buf_l[halo - 1:halo, :] = shl_ref[0]
        s_scr[...] = s0_ref[0]

    cur_r = rkv_ref[0]
    cur_l = lora_ref[0]
    buf_r[halo:halo + c, :] = cur_r
    buf_l[halo:halo + c, :] = cur_l
    xm = cur_r + mur_ref[...] * (buf_r[halo - 1:halo - 1 + c, :] - cur_r)
    lo = cur_l + mul_ref[...] * (buf_l[halo - 1:halo - 1 + c, :] - cur_l)
    buf_r[halo - 1:halo, :] = cur_r[c - 1:c, :]
    buf_l[halo - 1:halo, :] = cur_l[c - 1:c, :]

    r = xm[:, :wd]
    k = xm[:, wd:2 * wd]
    v = xm[:, 2 * wd:]
    lo_a = lo[:, :LANES]
    w_pre = w0_ref[...] + jnp.dot(jnp.tanh(lo_a).astype(BF16), w2_ref[...], preferred_element_type=F32)
    w_log = -_softplus(-w_pre) - 0.5
    ld = -jnp.exp(w_log)
    a = jax.nn.sigmoid(a0_ref[...] + jnp.dot(lo_a.astype(BF16), a2_ref[...], preferred_element_type=F32))
    g = jnp.dot(jax.nn.sigmoid(lo[:, LANES:]).astype(BF16), g2_ref[...], preferred_element_type=F32)
    kq = k * kk_ref[...]
    k2 = k * (1.0 + (a - 1.0) * ka_ref[...])
    lc = lax.dot_general(tri_ref[...], ld, NN, precision=lax.Precision.HIGHEST,
                         preferred_element_type=F32)

    for h in range(hn):
        sl = slice(h * n, (h + 1) * n)
        st_r[h] = r[:, sl]
        st_k[h] = k2[:, sl]
        st_v[h] = v[:, sl]
        st_q[h] = kq[:, sl]
        st_a[h] = a[:, sl]
        st_l[h] = lc[:, sl]
        st_d[h] = ld[:, sl]

    rh_, k2h, vh, kqh, ah, lch, ldh = (st_r[...], st_k[...], st_v[...], st_q[...], st_a[...],
                                       st_l[...], st_d[...])
    nrm = jnp.sqrt(jnp.sum(kqh * kqh, axis=-1, keepdims=True))
    kk = kqh / jnp.maximum(nrm, 1e-12)
    kka = kk * ah
    e_neg = jnp.exp(-lch)
    am = jnp.exp(lch - ldh) * kk
    bm = kka * e_neg
    kh = k2h * e_neg
    rh = rh_ * jnp.exp(lch)
    l_end = lch[:, c - 1:c, :]
    e_c = jnp.exp(l_end - lch)
    bp = kka * e_c
    kp = k2h * e_c
    w_end = jnp.exp(l_end)

    x2 = jnp.concatenate([am, rh], axis=1)
    zb = _dotp(x2, bm, BNT, RWKV_PASSES_SOLVE)
    zk = mm(x2, kh, BNT)
    ti = lax.broadcasted_iota(I32, (hn, c, c), 1)
    si = lax.broadcasted_iota(I32, (hn, c, c), 2)
    strict = si < ti
    incl = si <= ti
    m1 = jnp.where(strict, zb[:, :c], 0.0)
    m4 = jnp.where(incl, zb[:, c:], 0.0)
    m2 = jnp.where(strict, zk[:, :c], 0.0)
    m3 = jnp.where(incl, zk[:, c:], 0.0)

    tm = jnp.where(si == ti, 1.0, 0.0) - m1
    npow = mm(m1, m1, BNN)
    span = 2
    while span < c:
        tm = tm + mm(tm, npow, BNN)
        span *= 2
        if span < c:
            npow = mm(npow, npow, BNN)

    s0 = s_scr[...]
    rhs = mm(am, s0, BNT) + mm(m2, vh, BNN)
    p = mm(tm, rhs, BNN)
    y = mm(rh, s0, BNT) + mm(m3, vh, BNN) - mm(m4, p, BNN)
    s_new = s0 * w_end + mm(vh, kp, BTN) - mm(p, bp, BTN)
    s_scr[...] = s_new

    mean = jnp.mean(y, axis=-1, keepdims=True)
    yc = y - mean
    var = jnp.mean(yc * yc, axis=-1, keepdims=True)
    yn = yc * lax.rsqrt(var + A_LN_EPS) * lnw_ref[...] + lnb_ref[...]
    bonus = jnp.sum(rh_ * k2h * rk_ref[...], axis=-1, keepdims=True) * vh
    yo = yn + bonus
    yo = jnp.concatenate([yo[h] for h in range(hn)], axis=-1)
    y_ref[0] = (yo * g).astype(y_ref.dtype)

    @pl.when(ci == pl.num_programs(1) - 1)
    def _():
        sout_ref[0] = s_new


def _rwkv(p3, lora_blk, shift_r, shift_l, s0, wts):
    b, t, _ = p3.shape
    c = min(CHUNK, t)
    hn, n, wd = A_HEADS, A_HEAD_DIM, A_WIDTH
    full = lambda shape: pl.BlockSpec(shape, lambda i, j: (0,) * len(shape))
    tri = (jnp.arange(c)[:, None] >= jnp.arange(c)[None, :]).astype(F32)
    st = pltpu.VMEM((hn, c, n), F32)
    return pl.pallas_call(
        functools.partial(_rwkv_body, c=c),
        grid=(b, t // c),
        in_specs=[pl.BlockSpec((1, c, 3 * wd), lambda i, j: (i, j, 0)),
                  pl.BlockSpec((1, c, LORA_PAD), lambda i, j: (i, j, lora_blk)),
                  pl.BlockSpec((1, 1, 3 * wd), lambda i, j: (i, 0, 0)),
                  pl.BlockSpec((1, 1, LORA_PAD), lambda i, j: (i, 0, 0)),
                  pl.BlockSpec((1, hn, n, n), lambda i, j: (i, 0, 0, 0)),
                  full((1, 3 * wd)), full((1, LORA_PAD)), full((1, wd)), full((LANES, wd)),
                  full((1, wd)), full((LANES, wd)), full((LORA_PAD - LANES, wd)),
                  full((1, wd)), full((1, wd)), full((hn, 1, n)), full((hn, 1, n)), full((hn, 1, n)),
                  full((c, c))],
        out_specs=[pl.BlockSpec((1, c, wd), lambda i, j: (i, j, 0)),
                   pl.BlockSpec((1, hn, n, n), lambda i, j: (i, 0, 0, 0))],
        out_shape=[jax.ShapeDtypeStruct((b, t, wd), BF16),
                   jax.ShapeDtypeStruct((b, hn, n, n), F32)],
        scratch_shapes=[pltpu.VMEM((c + SUBLANES, 3 * wd), F32), pltpu.VMEM((c + SUBLANES, LORA_PAD), F32),
                        pltpu.VMEM((hn, n, n), F32), st, st, st, st, st, st, st],
        compiler_params=_cparams(("parallel", "arbitrary")),
        name="rwkv7",
    )(p3, p3, shift_r, shift_l, s0, wts["mu_r"], wts["mu_l"], wts["w0"], wts["w2"], wts["a0"], wts["a2"],
      wts["g2"], wts["k_k"], wts["k_a"], wts["r_k"], wts["ln_w"], wts["ln_b"], tri)


def _rope(x, cos, sin):
    wdt = x.shape[-1]
    lane = lax.broadcasted_iota(I32, x.shape, 1)
    first = (lane % D_ROPE) < (D_ROPE // 2)
    rot = jnp.where(first, -pltpu.roll(x, wdt - D_ROPE // 2, 1), pltpu.roll(x, D_ROPE // 2, 1))
    return x * cos + rot * sin


def _mla_prep_body(qd_ref, kvd_ref, sm_ref, gq_ref, gkv_ref, wuq_ref, cos_ref, sin_ref,
                   qn_ref, qr_ref, lat_ref, kr_ref):
    nope_w = D_HEADS * D_NOPE
    qdn = _rms(qd_ref[0], gq_ref[...]).astype(BF16)
    qf = jnp.dot(qdn, wuq_ref[...], preferred_element_type=F32) * ((D_NOPE + D_ROPE) ** -0.5 * LOG2E)
    qn_ref[0] = qf[:, :nope_w].astype(qn_ref.dtype)
    cos = cos_ref[...]
    sin = sin_ref[...]
    qr_ref[0] = _rope(qf[:, nope_w:], cos, sin).astype(qr_ref.dtype)
    lat_ref[0] = _rms(kvd_ref[0], gkv_ref[...])
    sm = sm_ref[0][:, :LANES]
    kr = _rope(sm, cos[:, :LANES], sin[:, :LANES])
    kr_ref[0] = kr[:, C_IDX_DIM:C_IDX_DIM + D_ROPE]


def _mla_prep(p3, qd_blk, kvd_blk, sm_blk, gq, gkv, wuq, cos, sin):
    b, t, _ = p3.shape
    tt = min(256, t)
    rw = D_HEADS * D_ROPE
    full = lambda shape: pl.BlockSpec(shape, lambda i, j: (0,) * len(shape))
    return pl.pallas_call(
        _mla_prep_body,
        grid=(b, t // tt),
        in_specs=[pl.BlockSpec((1, tt, D_Q_RANK), lambda i, j: (i, j, qd_blk)),
                  pl.BlockSpec((1, tt, D_KV_RANK), lambda i, j: (i, j, kvd_blk)),
                  pl.BlockSpec((1, tt, 512), lambda i, j: (i, j, sm_blk)),
                  full((1, D_Q_RANK)), full((1, D_KV_RANK)), full(wuq.shape),
                  pl.BlockSpec((tt, rw), lambda i, j: (j, 0)),
                  pl.BlockSpec((tt, rw), lambda i, j: (j, 0))],
        out_specs=[pl.BlockSpec((1, tt, D_HEADS * D_NOPE), lambda i, j: (i, j, 0)),
                   pl.BlockSpec((1, tt, rw), lambda i, j: (i, j, 0)),
                   pl.BlockSpec((1, tt, D_KV_RANK), lambda i, j: (i, j, 0)),
                   pl.BlockSpec((1, tt, D_ROPE), lambda i, j: (i, j, 0))],
        out_shape=[jax.ShapeDtypeStruct((b, t, D_HEADS * D_NOPE), BF16),
                   jax.ShapeDtypeStruct((b, t, rw), BF16),
                   jax.ShapeDtypeStruct((b, t, D_KV_RANK), F32),
                   jax.ShapeDtypeStruct((b, t, D_ROPE), F32)],
        compiler_params=_cparams(("parallel", "parallel")),
        name="mla_prep",
    )(p3, p3, p3, gq.reshape(1, -1), gkv.reshape(1, -1), wuq, cos, sin)


def _bias_table_body(rb_ref, o_ref):
    d = pl.program_id(0)
    h = pl.program_id(1)
    nb = N_BUCKETS // 2
    max_exact = nb // 2
    s = lax.broadcasted_iota(I32, (LANES, LANES), 0)
    q = lax.broadcasted_iota(I32, (LANES, LANES), 1)
    rel = s - q - d * LANES
    n = jnp.abs(rel)
    big = jnp.maximum(n, max_exact).astype(F32)
    large = max_exact + (jnp.log(big / max_exact) / math.log(MAX_DISTANCE / max_exact)
                         * (nb - max_exact)).astype(I32)
    large = jnp.minimum(large, nb - 1)
    bucket = jnp.where(rel > 0, nb, 0) + jnp.where(n < max_exact, n, large)
    out = jnp.zeros((LANES, LANES), F32)
    for bk in range(N_BUCKETS):
        out = jnp.where(bucket == bk, rb_ref[bk, h], out)
    o_ref[0, 0] = out * LOG2E


def _bias_tables(rel_bias):
    return pl.pallas_call(
        _bias_table_body,
        grid=(2, C_HEADS),
        in_specs=[pl.BlockSpec(memory_space=pltpu.SMEM)],
        out_specs=pl.BlockSpec((1, 1, LANES, LANES), lambda d, h: (d, h, 0, 0)),
        out_shape=jax.ShapeDtypeStruct((2, C_HEADS, LANES, LANES), F32),
        name="bias_tables",
    )(rel_bias)


def _transpose32(x):
    x = list(x)
    for s, msk in ((16, 0x0000FFFF), (8, 0x00FF00FF), (4, 0x0F0F0F0F), (2, 0x33333333), (1, 0x55555555)):
        sh = jnp.full(x[0].shape, s, I32)
        for i in range(32):
            if i & s == 0:
                t = (lax.shift_right_logical(x[i], sh) ^ x[i + s]) & msk
                x[i + s] = x[i + s] ^ t
                x[i] = x[i] ^ lax.shift_left(t, sh)
    return x


def _attn_body(qt_tab, kt_tab, *refs, mode, p0, t_valid, l_valid, tq, tk, dk, topk, nh):
    dsa = mode == "dsa"
    dh = LANES
    if dsa:
        (qT_ref, k_ref, vT_ref, q3_ref, ki3_ref, wiT_ref, tab_ref, far_ref,
         o_ref, m_scr, l_scr, acc_scr, s_scr, p_scr, mb_scr, skey_scr, thr_scr, planes_scr, e_scr) = refs
    else:
        (qT_ref, k_ref, vT_ref, o_ref, m_scr, l_scr, acc_scr, s_scr, p_scr, mb_scr) = refs
    qt = qt_tab[pl.program_id(1)]
    kt = kt_tab[pl.program_id(1)]
    q_lo = p0 + qt * tq
    q_hi = p0 + jnp.minimum(qt * tq + tq, t_valid) - 1
    n_allowed = jnp.minimum((q_hi // CHUNK + 1) * CHUNK, l_valid)
    last_kt = (n_allowed - 1) // tk
    qpos = q_lo + lax.broadcasted_iota(I32, (1, tq), 1)
    qchunk = qpos // CHUNK

    def allowed_mask(k0, rows):
        kidx = k0 + lax.broadcasted_iota(I32, (rows, tq), 0)
        return (kidx // CHUNK <= qchunk) & (kidx < l_valid)

    @pl.when(kt == 0)
    def _():
        m_scr[...] = jnp.full(m_scr.shape, NEG_BIG, F32)
        l_scr[...] = jnp.zeros(l_scr.shape, F32)
        acc_scr[...] = jnp.zeros(acc_scr.shape, F32)

    if dsa:
        iscale = (C_IDX_HEADS * C_IDX_DIM) ** -0.5

        @pl.when(kt == 0)
        def _():
            wpt = tk // 32
            kw = 3 * C_IDX_DIM

            def score_tile(j, carry):
                k0 = pl.multiple_of(j * tk, tk)
                ki3 = ki3_ref[0, pl.ds(k0, tk), :]
                s = jnp.zeros((tk, tq), F32)
                for h in range(C_IDX_HEADS):
                    d = jnp.dot(ki3, q3_ref[0, h * kw:(h + 1) * kw, :], preferred_element_type=F32)
                    s = s + jnp.maximum(d, 0.0) * wiT_ref[0, h:h + 1, :]
                s = s * iscale + 0.0
                bits = pltpu.bitcast(s, I32)
                key = bits ^ ((bits >> 31) & 0x7FFFFFFF)
                key = jnp.where(allowed_mask(k0, tk), key, INT_MIN)
                skey_scr[pl.ds(k0, tk), :] = key
                ukey = key ^ INT_MIN
                w0 = pl.multiple_of(j * wpt, SUBLANES)
                for g in range(tk // 256):
                    rows = [ukey[g * 256 + 8 * i:g * 256 + 8 * i + 8, :] for i in range(32)]
                    for bi, plane in enumerate(_transpose32(rows)):
                        planes_scr[bi, pl.ds(w0 + g * SUBLANES, SUBLANES), :] = plane
                ones = jnp.full((wpt, tq), -1, I32)
                planes_scr[32, pl.ds(w0, wpt), :] = ones
                e_scr[pl.ds(w0, wpt), :] = ones
                return carry

            def blank_tile(j, carry):
                w0 = pl.multiple_of(j * wpt, SUBLANES)
                for bi in range(33):
                    planes_scr[bi, pl.ds(w0, wpt), :] = jnp.zeros((wpt, tq), I32)
                e_scr[pl.ds(w0, wpt), :] = jnp.zeros((wpt, tq), I32)
                return carry

            ntile = last_kt + 1
            lax.fori_loop(0, ntile, score_tile, 0)
            nblk = (ntile + SEARCH_TILES - 1) // SEARCH_TILES
            lax.fori_loop(ntile, nblk * SEARCH_TILES, blank_tile, 0)
            wpb = SEARCH_TILES * wpt

            def lanesum(acc):
                return jnp.sum(acc, axis=0, keepdims=True)

            def fold(pc):
                return jnp.sum(pc.reshape(wpb // SUBLANES, SUBLANES, tq), axis=0)

            def settle(e, plane, take):
                t = e & plane
                return jnp.where(take != 0, t, e ^ t)

            def bit_step(i, st):
                c_gt, th, take_prev = st
                bi = 31 - i

                def body(j, acc):
                    w0 = pl.multiple_of(j * wpb, SUBLANES)
                    e = settle(e_scr[pl.ds(w0, wpb), :], planes_scr[bi + 1, pl.ds(w0, wpb), :], take_prev)
                    e_scr[pl.ds(w0, wpb), :] = e
                    return acc + fold(lax.population_count(e & planes_scr[bi, pl.ds(w0, wpb), :]))

                c1 = lanesum(lax.fori_loop(0, nblk, body, jnp.zeros((SUBLANES, tq), I32)))
                take = (c_gt + c1) >= topk
                return (jnp.where(take, c_gt, c_gt + c1),
                        jnp.where(take, th | lax.shift_left(jnp.int32(1), bi), th), jnp.where(take, 1, 0))

            zero = jnp.zeros((1, tq), I32)
            c_gt, th_u, take0 = lax.fori_loop(0, 32, bit_step, (zero, zero, zero + 1))

            def last_body(j, acc):
                w0 = pl.multiple_of(j * wpb, SUBLANES)
                e = settle(e_scr[pl.ds(w0, wpb), :], planes_scr[0, pl.ds(w0, wpb), :], take0)
                e_scr[pl.ds(w0, wpb), :] = e
                return acc + fold(lax.population_count(e))

            n_eq = lanesum(lax.fori_loop(0, nblk, last_body, jnp.zeros((SUBLANES, tq), I32)))
            need = topk - c_gt
            nbits = max(1, (l_valid - 1).bit_length())
            thr_scr[0:1, :] = th_u ^ INT_MIN
            thr_scr[1:2, :] = jnp.full((1, tq), (1 << nbits) - 1, I32)

            @pl.when(jnp.max((n_eq - need).astype(F32)) > 0.0)
            def _():
                def idx_step(i, jb):
                    cand = jb - lax.shift_left(jnp.int32(1), nbits - 1 - i)

                    def body(j, acc):
                        w0 = pl.multiple_of(j * wpb, SUBLANES)
                        wr = w0 + lax.broadcasted_iota(I32, (wpb, tq), 0)
                        base = (wr >> 3) * 256 + (wr & 7)
                        mx = (cand - base) >> 3
                        low = jnp.left_shift(2, jnp.clip(mx, 0, 30)) - 1
                        msk = jnp.where(mx < 0, 0, jnp.where(mx >= 31, -1, low))
                        return acc + fold(lax.population_count(e_scr[pl.ds(w0, wpb), :] & msk))

                    cnt = lanesum(lax.fori_loop(0, nblk, body, jnp.zeros((SUBLANES, tq), I32)))
                    return jnp.where(cnt >= need, cand, jb)

                thr_scr[1:2, :] = lax.fori_loop(0, nbits, idx_step, jnp.full((1, tq), (1 << nbits) - 1, I32))

    def tile(near):
        k0 = pl.multiple_of(kt * tk, tk)
        nchunk = tk // RC
        masked = dsa or near
        if masked:
            for c in range(nchunk):
                r0 = k0 + c * RC
                mask = allowed_mask(r0, RC) if near else None
                if dsa:
                    x = skey_scr[pl.ds(r0, RC), :]
                    kidx = r0 + lax.broadcasted_iota(I32, (RC, tq), 0)
                    sel = (x > thr_scr[0:1, :]) | ((x == thr_scr[0:1, :]) & (kidx <= thr_scr[1:2, :]))
                    mask = sel & mask if near else sel
                mb_scr[c * RC:(c + 1) * RC, :] = jnp.where(mask, 0.0, NEG_BIG)

        def chunk_bias(h, c):
            far = far_ref[0:1, h:h + 1]
            if not near:
                return far
            sb, off = (c * RC) // LANES, (c * RC) % LANES
            cols = []
            for qb in range(tq // LANES):
                delta = k0 + sb * LANES - (q_lo + qb * LANES)
                cols.append(jnp.where(delta == 0, tab_ref[0, h, off:off + RC, :],
                                      jnp.where(delta == -LANES, tab_ref[1, h, off:off + RC, :], far)))
            return jnp.concatenate(cols, axis=1) if len(cols) > 1 else cols[0]

        for h in range(nh):
            ks = slice(h * dk, (h + 1) * dk)
            s_scr[h] = jnp.dot(k_ref[0, :, ks], qT_ref[0, ks, :], preferred_element_type=F32)
        m_news, alphas = [], []
        for h in range(nh):
            mx = jnp.full((SUBLANES, tq), NEG_BIG, F32)
            for c in range(nchunk):
                rows = slice(c * RC, (c + 1) * RC)
                blk = s_scr[h, rows, :]
                if masked:
                    blk = blk + mb_scr[rows, :]
                    if dsa:
                        blk = blk + chunk_bias(h, c)
                    s_scr[h, rows, :] = blk
                mx = jnp.maximum(mx, jnp.max(blk.reshape(RC // SUBLANES, SUBLANES, tq), axis=0))
            m_prev = m_scr[h:h + 1, :]
            m_new = jnp.maximum(m_prev, jnp.max(mx, axis=0, keepdims=True))
            m_scr[h:h + 1, :] = m_new
            m_news.append(m_new)
            alphas.append(jnp.exp2(m_prev - m_new))
        for h in range(nh):
            for c in range(nchunk):
                rows = slice(c * RC, (c + 1) * RC)
                p_scr[h, rows, :] = jnp.exp2(s_scr[h, rows, :] - m_news[h]).astype(BF16)
        for h in range(nh):
            sl = slice(h * dh, (h + 1) * dh)
            pv = jnp.dot(vT_ref[0, h * VT_ROWS:(h + 1) * VT_ROWS, :], p_scr[h], preferred_element_type=F32)
            acc_scr[sl, :] = alphas[h] * acc_scr[sl, :] + pv[:dh]
            l_scr[h:h + 1, :] = alphas[h] * l_scr[h:h + 1, :] + pv[dh:dh + 1]

    is_far = kt * tk + tk - 1 <= q_lo - LANES

    @pl.when(is_far)
    def _():
        tile(False)

    @pl.when(jnp.logical_not(is_far))
    def _():
        tile(True)

    @pl.when(kt == last_kt)
    def _():
        for h in range(nh):
            sl = slice(h * dh, (h + 1) * dh)
            o = acc_scr[sl, :] / l_scr[h:h + 1, :]
            o_ref[0, :, sl] = o.T.astype(o_ref.dtype)


def _attention(mode, qT, k, vT, extra, *, p0, t_valid, l_valid, tk):
    b, hdk, tpad = qT.shape
    lpad = k.shape[1]
    nh = vT.shape[1] // VT_ROWS
    hd = nh * LANES
    dk = hdk // nh
    tq = 2 * LANES if tpad % (2 * LANES) == 0 else LANES
    nq, nk = tpad // tq, lpad // tk
    topk = min(TOPK_MAX, l_valid // 4)
    assert p0 % LANES == 0 and lpad % tk == 0 and tk % 256 == 0 and l_valid >= topk >= 1 and tk >= topk

    pairs = []
    for qt in range(nq):
        q_hi = p0 + min(qt * tq + tq, t_valid) - 1
        last = (min((q_hi // CHUNK + 1) * CHUNK, l_valid) - 1) // tk
        pairs += [(qt, kt) for kt in range(last + 1)]
    qt_tab = jnp.asarray([p[0] for p in pairs], I32)
    kt_tab = jnp.asarray([p[1] for p in pairs], I32)

    kmap = lambda i, s, qtt, ktt: (i, ktt[s], 0)
    vmap_ = lambda i, s, qtt, ktt: (i, 0, ktt[s])
    qmap = lambda i, s, qtt, ktt: (i, 0, qtt[s])
    cmap = lambda n: (lambda i, s, qtt, ktt: (0,) * n)
    in_specs = [pl.BlockSpec((1, hdk, tq), qmap),
                pl.BlockSpec((1, tk, hdk), kmap),
                pl.BlockSpec((1, nh * VT_ROWS, tk), vmap_)]
    scratch = [pltpu.VMEM((nh, tq), F32), pltpu.VMEM((nh, tq), F32), pltpu.VMEM((hd, tq), F32),
               pltpu.VMEM((nh, tk, tq), F32), pltpu.VMEM((nh, tk, tq), BF16), pltpu.VMEM((tk, tq), F32)]
    if mode == "dsa":
        q3, ki3, wiT, tabs, far = extra
        wrows = -(-nk // SEARCH_TILES) * SEARCH_TILES * (tk // 32)
        in_specs += [pl.BlockSpec((1, q3.shape[1], tq), qmap),
                     pl.BlockSpec((1, lpad, ki3.shape[2]), lambda i, s, qtt, ktt: (i, 0, 0)),
                     pl.BlockSpec((1, C_IDX_HEADS, tq), qmap),
                     pl.BlockSpec(tabs.shape, cmap(4)),
                     pl.BlockSpec(far.shape, cmap(2))]
        scratch += [pltpu.VMEM((lpad, tq), I32), pltpu.VMEM((SUBLANES, tq), I32),
                    pltpu.VMEM((33, wrows, tq), I32), pltpu.VMEM((wrows, tq), I32)]
    return pl.pallas_call(
        functools.partial(_attn_body, mode=mode, p0=p0, t_valid=t_valid, l_valid=l_valid, tq=tq, tk=tk,
                          dk=dk, topk=topk, nh=nh),
        grid_spec=pltpu.PrefetchScalarGridSpec(
            num_scalar_prefetch=2,
            grid=(b, len(pairs)),
            in_specs=in_specs,
            out_specs=pl.BlockSpec((1, tq, hd), lambda i, s, qtt, ktt: (i, qtt[s], 0)),
            scratch_shapes=scratch),
        out_shape=jax.ShapeDtypeStruct((b, tpad, hd), BF16),
        compiler_params=_cparams(("parallel", "arbitrary")),
        name="attn_" + mode,
    )(qt_tab, kt_tab, qT, k, vT, *extra)


PACK_TILE = 256


def _pack_body(cache_ref, new_ref, o_ref, *, n_cache, transpose, nh):
    j = pl.program_id(1)

    def emit(head):
        for h in range(nh):
            sl = slice(h * LANES, (h + 1) * LANES)
            x = head(h, sl)
            if transpose:
                r0 = h * VT_ROWS
                o_ref[0, r0:r0 + LANES, :] = x.T.astype(o_ref.dtype)
                o_ref[0, r0 + LANES:r0 + VT_ROWS, :] = jnp.ones((VT_ONES, x.shape[0]), o_ref.dtype)
            else:
                o_ref[0, :, sl] = x.astype(o_ref.dtype)

    @pl.when(j < n_cache)
    def _():
        tp = o_ref.shape[2] if transpose else o_ref.shape[1]
        emit(lambda h, sl: cache_ref[0, 0, pl.ds(h, tp, stride=nh), :])

    @pl.when(j >= n_cache)
    def _():
        emit(lambda h, sl: new_ref[0, :, sl])


def _pack_keys(cache, layer, new, lpad, *, transpose):
    _, b, past, nh, dh = cache.shape
    tp = 2 * PACK_TILE if (past % (2 * PACK_TILE) == 0 and lpad % (2 * PACK_TILE) == 0) else PACK_TILE
    assert dh == LANES and past % tp == 0 and lpad % tp == 0
    n_cache = past // tp
    hd = nh * dh
    new = _pad_axis(new, 1, lpad - past)
    out_shape = (b, nh * VT_ROWS, lpad) if transpose else (b, lpad, hd)
    out_spec = (pl.BlockSpec((1, nh * VT_ROWS, tp), lambda i, j: (i, 0, j)) if transpose
                else pl.BlockSpec((1, tp, hd), lambda i, j: (i, j, 0)))
    return pl.pallas_call(
        functools.partial(_pack_body, n_cache=n_cache, transpose=transpose, nh=nh),
        grid=(b, lpad // tp),
        in_specs=[pl.BlockSpec((1, 1, tp * nh, dh),
                               lambda i, j: (layer, i, jnp.minimum(j, n_cache - 1), 0)),
                  pl.BlockSpec((1, tp, hd), lambda i, j: (i, jnp.maximum(j - n_cache, 0), 0))],
        out_specs=out_spec,
        out_shape=jax.ShapeDtypeStruct(out_shape, BF16),
        compiler_params=_cparams(("parallel", "parallel")),
        name="pack_keys",
    )(cache.reshape(cache.shape[0], b, past * nh, dh), new)


def _split_hi_lo(x):
    hi = x.astype(BF16)
    lo = (x - hi.astype(F32)).astype(BF16)
    return hi, lo


def _pad_axis(x, axis, size):
    if x.shape[axis] == size:
        return x
    pad = [(0, 0)] * x.ndim
    pad[axis] = (0, size - x.shape[axis])
    return jnp.pad(x, pad)


def _vt_ones(v):
    b, l, hd = v.shape
    vt = jnp.swapaxes(v, 1, 2).reshape(b, hd // LANES, LANES, l)
    ones = jnp.ones((b, hd // LANES, VT_ONES, l), v.dtype)
    return jnp.concatenate([vt, ones], axis=2).reshape(b, -1, l)


def _key_tile(l_valid):
    tk = 512 if l_valid > 1024 else 256
    return -(-l_valid // tk) * tk, tk


def _prep_weights(W, d_model):
    f = {}
    wd = A_WIDTH
    n_even = W["ev_w_in"].shape[0]
    n_odd = W["od_w_in"].shape[0]
    f["even"] = []
    for e in range(n_even):
        wi = W["ev_w_in"][e]
        w_in = jnp.concatenate([wi[:, :3 * wd], wi[:, A_COLS:], wi[:, 3 * wd:A_COLS],
                                jnp.zeros((d_model, LORA_PAD - A_LORA), F32)], axis=1).astype(BF16)
        mu = W["rwkv_mu"][e]
        w2 = jnp.zeros((LANES, wd), F32).at[:A_DECAY_LORA].set(W["rwkv_w2"][e])
        a2 = jnp.zeros((LANES, wd), F32).at[A_DECAY_LORA:A_DECAY_LORA + A_ICLR_LORA].set(W["rwkv_a2"][e])
        g2 = jnp.zeros((LORA_PAD - LANES, wd), F32).at[:A_GATE_LORA].set(W["rwkv_g2"][e])
        hm = lambda v: v.reshape(A_HEADS, 1, A_HEAD_DIM)
        rw = dict(mu_r=mu[:3 * wd].reshape(1, -1),
                  mu_l=_pad_axis(mu[3 * wd:], 0, LORA_PAD).reshape(1, -1),
                  w0=W["rwkv_w0"][e].reshape(1, -1), w2=w2.astype(BF16),
                  a0=W["rwkv_a0"][e].reshape(1, -1), a2=a2.astype(BF16), g2=g2.astype(BF16),
                  k_k=W["rwkv_k_k"][e].reshape(1, -1), k_a=W["rwkv_k_a"][e].reshape(1, -1),
                  r_k=hm(W["rwkv_r_k"][e]), ln_w=hm(W["rwkv_ln_w"][e]), ln_b=hm(W["rwkv_ln_b"][e]))
        eye = jnp.eye(B_BLOCKS, dtype=F32)
        blockdiag = lambda w: (eye[:, None, :, None] * w[:, :, None, :]).reshape(B_WIDTH, B_WIDTH)
        wgate = jnp.concatenate([blockdiag(W["lru_wa"][e]), blockdiag(W["lru_wx"][e])], axis=1).astype(BF16)
        bgate = jnp.concatenate([W["lru_ba"][e], W["lru_bx"][e]])
        wo = W["ev_w_out"][e].astype(BF16)
        f["even"].append(dict(w_in=w_in, rw=rw, wgate=wgate, bgate=bgate, wo_a=wo[:wd], wo_b=wo[wd:],
                              cw=W["lru_conv_w"][e], cb=W["lru_conv_b"][e], lam=W["lru_lambda"][e]))
    f["odd"] = []
    cw_ = C_WIDTH
    qi_w = C_IDX_HEADS * C_IDX_DIM
    for o in range(n_odd):
        wi = W["od_w_in"][o]
        offs = [0]
        for s in (cw_, cw_, cw_, qi_w, C_IDX_DIM, C_IDX_HEADS, D_Q_RANK, D_KV_RANK, D_ROPE):
            offs.append(offs[-1] + s)
        q, k, v, qi, ki, wi_, qd, kvd, kr = [wi[:, offs[i]:offs[i + 1]] for i in range(9)]
        small = jnp.concatenate([ki, kr, wi_], axis=1)
        w_in = jnp.concatenate([q, k, v, qi, qd, kvd, _pad_axis(small, 1, 512)], axis=1).astype(BF16)
        wuq = W["mla_w_uq"][o].reshape(D_Q_RANK, D_HEADS, D_NOPE + D_ROPE)
        wuq = jnp.concatenate([wuq[:, :, :D_NOPE].reshape(D_Q_RANK, -1),
                               wuq[:, :, D_NOPE:].reshape(D_Q_RANK, -1)], axis=1).astype(BF16)
        wukv = W["mla_w_ukv"][o].reshape(D_KV_RANK, D_HEADS, D_NOPE + D_V)
        wuk = _pad_axis(wukv[:, :, :D_NOPE], 2, MLA_DK).reshape(D_KV_RANK, -1).astype(BF16)
        wuv = wukv[:, :, D_NOPE:].reshape(D_KV_RANK, -1).astype(BF16)
        wo = W["od_w_out"][o].astype(BF16)
        f["odd"].append(dict(w_in=w_in, wuq=wuq, wuk=wuk, wuv=wuv, wo_c=wo[:cw_], wo_d=wo[cw_:],
                             gq=W["mla_q_norm"][o], gkv=W["mla_kv_norm"][o]))
    f["ffn"] = dict(wg=W["ffn_w_gate"].astype(BF16), wu=W["ffn_w_up"].astype(BF16),
                    wd=W["ffn_w_down"].astype(BF16))
    f["tabs"] = _bias_tables(W["rel_bias"])
    nb = N_BUCKETS // 2
    f["far"] = _pad_axis(W["rel_bias"][nb - 1:nb, :] * LOG2E, 1, LANES)
    return f


def _mixer_even(x2, b, t, p0, shift, s0, h0, cbuf, fe, norm_g):
    wd = A_WIDTH
    p = _norm_matmul(x2, norm_g, fe["w_in"])
    p3 = p.reshape(b, t, -1)
    lora_col = 3 * wd + 2 * B_WIDTH
    shift_r = shift[:, None, :3 * wd]
    shift_l = _pad_axis(shift[:, None, 3 * wd:], 2, LORA_PAD)
    ya, s_new = _rwkv(p3, lora_col // LORA_PAD, shift_r, shift_l, s0, fe["rw"])
    yb, h_new, c_new = _rglru(p3, 3, 4, fe["cw"], fe["cb"], fe["wgate"], fe["bgate"], fe["lam"],
                              h0, cbuf, p0=p0)
    x2 = _matmul([ya.reshape(b * t, wd), yb.reshape(b * t, B_WIDTH)], [fe["wo_a"], fe["wo_b"]], res=x2)
    last = p3[:, t - 1]
    new_shift = jnp.concatenate([last[:, :3 * wd], last[:, lora_col:lora_col + A_LORA]], axis=-1)
    return x2, new_shift, s_new, h_new[:, 0], c_new


def _mixer_odd(x2, b, t, p0, o, dsa_k, dsa_v, cik, clat, ckr, fo, f, norm_g, cos, sin):
    cw_ = C_WIDTH
    nt = cw_ // 512
    segs = [(0, nt, BF16, C_HEAD_DIM ** -0.5 * LOG2E), (nt, 2 * nt, F32, 1.0), (nt, 2 * nt, BF16, 1.0),
            (2 * nt, 3 * nt, F32, 1.0), (2 * nt, 3 * nt, BF16, 1.0), (3 * nt, 3 * nt + 4, F32, 1.0)]
    q_bf, k_f, k_bf, v_f, v_bf, rest = [z.reshape(b, t, -1) for z in
                                        _norm_matmul(x2, norm_g, fo["w_in"], segs)]
    past = dsa_k.shape[2]
    l_valid = past + t
    lpad, tk = _key_tile(l_valid)
    tpad = -(-t // LANES) * LANES
    k_new, v_new = k_f, v_f
    qi = rest[..., :512]
    small = rest[..., 1536:]
    ki_new = small[..., :C_IDX_DIM]
    wi = small[..., 2 * C_IDX_DIM:2 * C_IDX_DIM + C_IDX_HEADS]

    def keys(cache, new):
        allk = jnp.concatenate([cache.reshape(b, past, -1), new], axis=1) if past else new
        return _pad_axis(allk, 1, lpad)

    tq_ = lambda z: _pad_axis(jnp.swapaxes(z, 1, 2), 2, tpad)
    if past and past % PACK_TILE == 0:
        k_all = _pack_keys(dsa_k, o, k_new, lpad, transpose=False)
        vT_all = _pack_keys(dsa_v, o, v_new, lpad, transpose=True)
    elif past:
        k_all = keys(dsa_k[o], k_new).astype(BF16)
        vT_all = _vt_ones(keys(dsa_v[o], v_new).astype(BF16))
    else:
        k_all = _pad_axis(k_bf, 1, lpad)
        vT_all = _vt_ones(_pad_axis(v_bf, 1, lpad))
    kih, kil = _split_hi_lo(keys(cik, ki_new))
    ki3 = jnp.concatenate([kih, kih, kil], axis=-1)
    qih, qil = _split_hi_lo(tq_(qi).reshape(b, C_IDX_HEADS, C_IDX_DIM, tpad))
    q3 = jnp.concatenate([qih, qil, qih], axis=2).reshape(b, 3 * C_IDX_HEADS * C_IDX_DIM, tpad)
    yc = _attention("dsa", tq_(q_bf), k_all, vT_all,
                    (q3, ki3, tq_(wi), f["tabs"], f["far"]),
                    p0=p0, t_valid=t, l_valid=l_valid, tk=tk)[:, :t]
    qn, qr, lat, krope = _mla_prep(rest, 1, 2, 3, fo["gq"], fo["gkv"], fo["wuq"], cos, sin)
    lat_all = keys(clat, lat).reshape(b * lpad, -1)
    kr_all = keys(ckr, krope).reshape(b * lpad, -1)
    eye = jnp.eye(D_ROPE, dtype=BF16)
    ident = jnp.tile(jnp.pad(eye, ((0, 0), (D_NOPE, MLA_DK - D_NOPE - D_ROPE))), (1, D_HEADS))
    k_full = _matmul([lat_all, kr_all], [fo["wuk"], ident], out_dtype=BF16).reshape(b, lpad, -1)
    v_all = _matmul([lat_all], [fo["wuv"]], out_dtype=BF16).reshape(b, lpad, -1)
    q_full = jnp.concatenate([qn.reshape(b, t, D_HEADS, D_NOPE), qr.reshape(b, t, D_HEADS, D_ROPE),
                              jnp.zeros((b, t, D_HEADS, MLA_DK - D_NOPE - D_ROPE), BF16)], axis=-1)
    yd = _attention("mla", tq_(q_full.reshape(b, t, -1)), k_full, _vt_ones(v_all), (),
                    p0=p0, t_valid=t, l_valid=l_valid, tk=tk)[:, :t]
    x2 = _matmul([yc.reshape(b * t, cw_), yd.reshape(b * t, -1)], [fo["wo_c"], fo["wo_d"]], res=x2)
    return (x2, k_new.reshape(b, t, C_HEADS, C_HEAD_DIM), v_new.reshape(b, t, C_HEADS, C_HEAD_DIM),
            ki_new, lat, krope)


def _trunk(x, p0, shift, rwkv_s, lru_h, lru_conv, dsa_k, dsa_v, dsa_ik, mla_lat, mla_kr, W, f):
    b, t, d = x.shape
    depth = W["norm_mix"].shape[0]
    pos = (p0 + jnp.arange(t)).astype(F32)
    inv = ROPE_BASE ** (-jnp.arange(0, D_ROPE, 2, dtype=F32) / D_ROPE)
    ang = pos[:, None] * inv[None, :]
    cos = jnp.tile(jnp.cos(ang), (1, 2 * D_HEADS))
    sin = jnp.tile(jnp.sin(ang), (1, 2 * D_HEADS))
    x2 = x.reshape(b * t, d)
    ev = [[] for _ in range(4)]
    od = [[] for _ in range(5)]
    for layer in range(depth):
        if layer % 2 == 0:
            e = layer // 2
            x2, *outs = _mixer_even(x2, b, t, p0, shift[e], rwkv_s[e], lru_h[e], lru_conv[e],
                                    f["even"][e], W["norm_mix"][layer])
            for lst, o_ in zip(ev, outs):
                lst.append(o_)
        else:
            o = layer // 2
            x2, *outs = _mixer_odd(x2, b, t, p0, o, dsa_k, dsa_v, dsa_ik[o], mla_lat[o], mla_kr[o],
                                   f["odd"][o], f, W["norm_mix"][layer], cos, sin)
            for lst, o_ in zip(od, outs):
                lst.append(o_)
        ff = f["ffn"]
        x2 = _ffn(x2, W["norm_ffn"][layer], ff["wg"], ff["wu"], ff["wd"], layer, W["final_norm"],
                  final_norm=layer == depth - 1)
    return (x2.reshape(b, t, d),) + tuple(jnp.stack(v) for v in ev) + tuple(jnp.stack(v) for v in od)


def kernel(x_prompt, x_sample, state_rwkv_shift, state_rwkv, state_lru, state_lru_conv, cache_dsa_k, cache_dsa_v, cache_dsa_idx_k, cache_mla_latent, cache_mla_krope, rel_bias, final_norm, norm_mix, norm_ffn, ffn_w_gate, ffn_w_up, ffn_w_down, ev_w_in, ev_w_out, rwkv_mu, rwkv_w0, rwkv_w2, rwkv_a0, rwkv_a2, rwkv_g2, rwkv_k_k, rwkv_k_a, rwkv_r_k, rwkv_ln_w, rwkv_ln_b, lru_conv_w, lru_conv_b, lru_wa, lru_ba, lru_wx, lru_bx, lru_lambda, od_w_in, od_w_out, mla_q_norm, mla_w_uq, mla_kv_norm, mla_w_ukv):
    W = dict(rel_bias=rel_bias, final_norm=final_norm, norm_mix=norm_mix, norm_ffn=norm_ffn,
             ffn_w_gate=ffn_w_gate, ffn_w_up=ffn_w_up, ffn_w_down=ffn_w_down,
             ev_w_in=ev_w_in, ev_w_out=ev_w_out, rwkv_mu=rwkv_mu, rwkv_w0=rwkv_w0,
             rwkv_w2=rwkv_w2, rwkv_a0=rwkv_a0, rwkv_a2=rwkv_a2, rwkv_g2=rwkv_g2,
             rwkv_k_k=rwkv_k_k, rwkv_k_a=rwkv_k_a, rwkv_r_k=rwkv_r_k, rwkv_ln_w=rwkv_ln_w,
             rwkv_ln_b=rwkv_ln_b, lru_conv_w=lru_conv_w, lru_conv_b=lru_conv_b, lru_wa=lru_wa,
             lru_ba=lru_ba, lru_wx=lru_wx, lru_bx=lru_bx, lru_lambda=lru_lambda,
             od_w_in=od_w_in, od_w_out=od_w_out, mla_q_norm=mla_q_norm, mla_w_uq=mla_w_uq,
             mla_kv_norm=mla_kv_norm, mla_w_ukv=mla_w_ukv)
    d_model = x_prompt.shape[-1]
    f = _prep_weights(W, d_model)
    bp = x_prompt.shape[0]
    n_even, n_odd = ev_w_in.shape[0], od_w_in.shape[0]
    dt = x_prompt.dtype
    z = lambda *shape: jnp.zeros(shape, dt)
    outs_p = _trunk(
        x_prompt, 0,
        z(n_even, bp, A_COLS), z(n_even, bp, A_HEADS, A_HEAD_DIM, A_HEAD_DIM),
        z(n_even, bp, B_WIDTH), z(n_even, bp, B_CONV - 1, B_WIDTH),
        z(n_odd, bp, 0, C_HEADS, C_HEAD_DIM), z(n_odd, bp, 0, C_HEADS, C_HEAD_DIM),
        z(n_odd, bp, 0, C_IDX_DIM), z(n_odd, bp, 0, D_KV_RANK), z(n_odd, bp, 0, D_ROPE), W, f)
    past = cache_dsa_k.shape[2]
    outs_s = _trunk(x_sample, past, state_rwkv_shift, state_rwkv, state_lru, state_lru_conv,
                    cache_dsa_k, cache_dsa_v, cache_dsa_idx_k, cache_mla_latent, cache_mla_krope, W, f)
    return (outs_p[0], outs_s[0]) + tuple(outs_p[1:]) + tuple(outs_s[1:])
```
